```python
import math
import jax, jax.numpy as jnp
from jax import lax
import numpy as np

D_MODEL = 2048
BATCH = 8
SEQ = 4096
DEPTH = 2

CHUNK = 64
N_MIXERS = 2
N_A = (DEPTH + 1) // 2
N_B = DEPTH // 2
S5_GROUP = 16
S5_GROUPS = D_MODEL // S5_GROUP
S5_STATE = 64
S5_DT_MIN = 0.001
S5_DT_MAX = 0.1
SB_HEAD_DIM = 128
SB_HEADS = D_MODEL // SB_HEAD_DIM
Q_BLOCK = 128
D_FF = 5632
FFN_RES = 0.5
EPS = 1e-6

kernel_name = "hybrid_s5_stickbreaking_macaron"


def rmsnorm(x, g):
    xf = x.astype(jnp.float32)
    y = xf * lax.rsqrt(jnp.mean(xf * xf, axis=-1, keepdims=True) + EPS)
    return (y * g.astype(jnp.float32)).astype(x.dtype)


def swiglu(h, w_gate, w_up, w_down):
    return (jax.nn.silu(h @ w_gate) * (h @ w_up)) @ w_down


def _complex_scan_combine(left, right):
    a1r, a1i, b1r, b1i = left
    a2r, a2i, b2r, b2i = right
    return (a2r * a1r - a2i * a1i,
            a2r * a1i + a2i * a1r,
            a2r * b1r - a2i * b1i + b2r,
            a2r * b1i + a2i * b1r + b2i)


def s5_mixer(h, w_in, lam_re, lam_im, log_dt, b_re, b_im, c_re, c_im, d_skip, w_glu, b_glu, w_out):
    f32 = jnp.float32
    bsz, seq, _ = h.shape
    n_chunks = seq // CHUNK
    u = (h @ w_in).astype(f32)
    dt = jnp.exp(log_dt.astype(f32))[:, None]
    lr = jnp.minimum(lam_re.astype(f32), -1e-4)
    li = lam_im.astype(f32)
    mag = jnp.exp(lr * dt)
    ab_re = mag * jnp.cos(li * dt)
    ab_im = mag * jnp.sin(li * dt)
    den = lr * lr + li * li
    n_re = ab_re - 1.0
    f_re = (n_re * lr + ab_im * li) / den
    f_im = (ab_im * lr - n_re * li) / den
    br = b_re.astype(f32)
    bi = b_im.astype(f32)
    bb_re = f_re[..., None] * br - f_im[..., None] * bi
    bb_im = f_re[..., None] * bi + f_im[..., None] * br
    cr = c_re.astype(f32)
    ci = c_im.astype(f32)

    uc = u.reshape(bsz, n_chunks, CHUNK, S5_GROUPS, S5_GROUP).transpose(1, 0, 2, 3, 4)
    a_re = jnp.broadcast_to(ab_re, (bsz, CHUNK, S5_GROUPS, S5_STATE))
    a_im = jnp.broadcast_to(ab_im, (bsz, CHUNK, S5_GROUPS, S5_STATE))

    def chunk_step(carry, u_chunk):
        s_re0, s_im0 = carry
        bu_re = jnp.einsum('bcgh,gph->bcgp', u_chunk, bb_re)
        bu_im = jnp.einsum('bcgh,gph->bcgp', u_chunk, bb_im)
        bu_re = bu_re.at[:, 0].add(ab_re * s_re0 - ab_im * s_im0)
        bu_im = bu_im.at[:, 0].add(ab_re * s_im0 + ab_im * s_re0)
        _, _, s_re, s_im = lax.associative_scan(_complex_scan_combine, (a_re, a_im, bu_re, bu_im), axis=1)
        y = jnp.einsum('bcgp,ghp->bcgh', s_re, cr) - jnp.einsum('bcgp,ghp->bcgh', s_im, ci)
        return (s_re[:, -1], s_im[:, -1]), y

    s0 = jnp.zeros((bsz, S5_GROUPS, S5_STATE), f32)
    _, yc = lax.scan(chunk_step, (s0, s0), uc)
    y = yc.transpose(1, 0, 2, 3, 4).reshape(bsz, seq, S5_GROUPS * S5_GROUP)
    y = y + d_skip.astype(f32) * u
    y = jax.nn.gelu(y).astype(h.dtype)
    y = y * jax.nn.sigmoid(y @ w_glu + b_glu)
    return y @ w_out


def stick_breaking_mixer(h, w_qkv, g_q, g_k, w_o):
    f32 = jnp.float32
    bsz, seq, _ = h.shape
    qkv = (h @ w_qkv).reshape(bsz, seq, 3, SB_HEADS, SB_HEAD_DIM)
    q = rmsnorm(qkv[:, :, 0], g_q).transpose(0, 2, 1, 3)
    k = rmsnorm(qkv[:, :, 1], g_k).transpose(0, 2, 1, 3)
    v = qkv[:, :, 2].transpose(0, 2, 1, 3)
    scale = 1.0 / math.sqrt(SB_HEAD_DIM)
    outs = []
    for blk in range(seq // Q_BLOCK):
        q0 = blk * Q_BLOCK
        kv_len = q0 + Q_BLOCK
        qb = q[:, :, q0:kv_len].astype(f32)
        kb = k[:, :, :kv_len].astype(f32)
        vb = v[:, :, :kv_len]
        z = jnp.einsum('bhqd,bhkd->bhqk', qb, kb) * scale
        t_idx = q0 + jnp.arange(Q_BLOCK)[:, None]
        s_idx = jnp.arange(kv_len)[None, :]
        past = s_idx < t_idx
        log_beta = jax.nn.log_sigmoid(z)
        log_keep = jnp.where(past, jax.nn.log_sigmoid(-z), 0.0)
        log_stick = lax.cumsum(log_keep, axis=3, reverse=True) - log_keep
        w = jnp.where(past, jnp.exp(log_beta + log_stick), 0.0)
        outs.append(jnp.einsum('bhqk,bhkd->bhqd', w.astype(v.dtype), vb))
    o = jnp.concatenate(outs, axis=2)
    o = o.transpose(0, 2, 1, 3).reshape(bsz, seq, SB_HEADS * SB_HEAD_DIM)
    return o @ w_o


def _fwd_setup_inputs(seed: int = 0) -> dict:
    key = jax.random.key(seed)
    ks = iter(jax.random.split(key, 40))
    f32 = jnp.float32
    D, F = D_MODEL, D_FF
    G, H, P = S5_GROUPS, S5_GROUP, S5_STATE

    def nrm(shape, scale):
        return jax.random.normal(next(ks), shape, f32) * scale

    def gain(shape):
        return jnp.ones(shape, f32) + 0.02 * jax.random.normal(next(ks), shape, f32)

    x = jax.random.normal(next(ks), (BATCH, SEQ, D), f32)
    norm_ffn1 = gain((DEPTH, D))
    ffn1_w_gate = nrm((DEPTH, D, F), D ** -0.5)
    ffn1_w_up = nrm((DEPTH, D, F), D ** -0.5)
    ffn1_w_down = nrm((DEPTH, F, D), F ** -0.5)
    norm_mix = gain((DEPTH, D))
    s5_w_in = nrm((N_A, D, G * H), D ** -0.5)
    s5_lam_re = -0.5 + 0.01 * jax.random.normal(next(ks), (N_A, G, P), f32)
    s5_lam_im = math.pi * jnp.arange(P, dtype=f32)[None, None, :] + 0.01 * jax.random.normal(next(ks), (N_A, G, P), f32)
    s5_log_dt = jax.random.uniform(next(ks), (N_A, G), f32, math.log(S5_DT_MIN), math.log(S5_DT_MAX))
    s5_b_re = nrm((N_A, G, P, H), (2 * H) ** -0.5)
    s5_b_im = nrm((N_A, G, P, H), (2 * H) ** -0.5)
    s5_c_re = nrm((N_A, G, H, P), P ** -0.5)
    s5_c_im = nrm((N_A, G, H, P), P ** -0.5)
    s5_d = nrm((N_A, G * H), 1.0)
    s5_w_glu = nrm((N_A, G * H, G * H), (G * H) ** -0.5)
    s5_b_glu = nrm((N_A, G * H), 0.01)
    s5_w_out = nrm((N_A, G * H, D), (G * H) ** -0.5)
    sb_w_qkv = nrm((N_B, D, 3 * SB_HEADS * SB_HEAD_DIM), D ** -0.5)
    sb_g_q = gain((N_B, SB_HEAD_DIM))
    sb_g_k = gain((N_B, SB_HEAD_DIM))
    sb_w_o = nrm((N_B, SB_HEADS * SB_HEAD_DIM, D), (SB_HEADS * SB_HEAD_DIM) ** -0.5)
    norm_ffn2 = gain((DEPTH, D))
    ffn2_w_gate = nrm((DEPTH, D, F), D ** -0.5)
    ffn2_w_up = nrm((DEPTH, D, F), D ** -0.5)
    ffn2_w_down = nrm((DEPTH, F, D), F ** -0.5)
    return {
        "x": x,
        "norm_ffn1": norm_ffn1, "ffn1_w_gate": ffn1_w_gate, "ffn1_w_up": ffn1_w_up, "ffn1_w_down": ffn1_w_down,
        "norm_mix": norm_mix,
        "s5_w_in": s5_w_in, "s5_lam_re": s5_lam_re, "s5_lam_im": s5_lam_im, "s5_log_dt": s5_log_dt,
        "s5_b_re": s5_b_re, "s5_b_im": s5_b_im, "s5_c_re": s5_c_re, "s5_c_im": s5_c_im, "s5_d": s5_d,
        "s5_w_glu": s5_w_glu, "s5_b_glu": s5_b_glu, "s5_w_out": s5_w_out,
        "sb_w_qkv": sb_w_qkv, "sb_g_q": sb_g_q, "sb_g_k": sb_g_k, "sb_w_o": sb_w_o,
        "norm_ffn2": norm_ffn2, "ffn2_w_gate": ffn2_w_gate, "ffn2_w_up": ffn2_w_up, "ffn2_w_down": ffn2_w_down,
    }


def _fwd_reference(x, norm_ffn1, ffn1_w_gate, ffn1_w_up, ffn1_w_down, norm_mix,
              s5_w_in, s5_lam_re, s5_lam_im, s5_log_dt, s5_b_re, s5_b_im, s5_c_re, s5_c_im, s5_d,
              s5_w_glu, s5_b_glu, s5_w_out,
              sb_w_qkv, sb_g_q, sb_g_k, sb_w_o,
              norm_ffn2, ffn2_w_gate, ffn2_w_up, ffn2_w_down):
    for i in range(DEPTH):
        h = rmsnorm(x, norm_ffn1[i])
        x = x + FFN_RES * swiglu(h, ffn1_w_gate[i], ffn1_w_up[i], ffn1_w_down[i])
        h = rmsnorm(x, norm_mix[i])
        j = i // N_MIXERS
        if i % N_MIXERS == 0:
            x = x + s5_mixer(h, s5_w_in[j], s5_lam_re[j], s5_lam_im[j], s5_log_dt[j],
                             s5_b_re[j], s5_b_im[j], s5_c_re[j], s5_c_im[j], s5_d[j],
                             s5_w_glu[j], s5_b_glu[j], s5_w_out[j])
        else:
            x = x + stick_breaking_mixer(h, sb_w_qkv[j], sb_g_q[j], sb_g_k[j], sb_w_o[j])
        h = rmsnorm(x, norm_ffn2[i])
        x = x + FFN_RES * swiglu(h, ffn2_w_gate[i], ffn2_w_up[i], ffn2_w_down[i])
    return x


import jax as _jax
import jax.numpy as _jnp

TWIN_FORMAT = 'train_step'
FWD_PARAMS = ['x', 'norm_ffn1', 'ffn1_w_gate', 'ffn1_w_up', 'ffn1_w_down', 'norm_mix', 's5_w_in', 's5_lam_re', 's5_lam_im', 's5_log_dt', 's5_b_re', 's5_b_im', 's5_c_re', 's5_c_im', 's5_d', 's5_w_glu', 's5_b_glu', 's5_w_out', 'sb_w_qkv', 'sb_g_q', 'sb_g_k', 'sb_w_o', 'norm_ffn2', 'ffn2_w_gate', 'ffn2_w_up', 'ffn2_w_down']
TWIN_WEIGHTS = ['norm_ffn1', 'ffn1_w_gate', 'ffn1_w_up', 'ffn1_w_down', 'norm_mix', 's5_w_in', 's5_lam_re', 's5_lam_im', 's5_log_dt', 's5_b_re', 's5_b_im', 's5_c_re', 's5_c_im', 's5_d', 's5_w_glu', 's5_b_glu', 's5_w_out', 'sb_w_qkv', 'sb_g_q', 'sb_g_k', 'sb_w_o', 'norm_ffn2', 'ffn2_w_gate', 'ffn2_w_up', 'ffn2_w_down']
TWIN_DIFF_INPUT = 'x'
TWIN_INPUTS = ['x', 'norm_ffn1', 'ffn1_w_gate', 'ffn1_w_up', 'ffn1_w_down', 'norm_mix', 's5_w_in', 's5_lam_re', 's5_lam_im', 's5_log_dt', 's5_b_re', 's5_b_im', 's5_c_re', 's5_c_im', 's5_d', 's5_w_glu', 's5_b_glu', 's5_w_out', 'sb_w_qkv', 'sb_g_q', 'sb_g_k', 'sb_w_o', 'norm_ffn2', 'ffn2_w_gate', 'ffn2_w_up', 'ffn2_w_down', 'loss_target', 'm_norm_ffn1', 'm_ffn1_w_gate', 'm_ffn1_w_up', 'm_ffn1_w_down', 'm_norm_mix', 'm_s5_w_in', 'm_s5_lam_re', 'm_s5_lam_im', 'm_s5_log_dt', 'm_s5_b_re', 'm_s5_b_im', 'm_s5_c_re', 'm_s5_c_im', 'm_s5_d', 'm_s5_w_glu', 'm_s5_b_glu', 'm_s5_w_out', 'm_sb_w_qkv', 'm_sb_g_q', 'm_sb_g_k', 'm_sb_w_o', 'm_norm_ffn2', 'm_ffn2_w_gate', 'm_ffn2_w_up', 'm_ffn2_w_down', 'v_norm_ffn1', 'v_ffn1_w_gate', 'v_ffn1_w_up', 'v_ffn1_w_down', 'v_norm_mix', 'v_s5_w_in', 'v_s5_lam_re', 'v_s5_lam_im', 'v_s5_log_dt', 'v_s5_b_re', 'v_s5_b_im', 'v_s5_c_re', 'v_s5_c_im', 'v_s5_d', 'v_s5_w_glu', 'v_s5_b_glu', 'v_s5_w_out', 'v_sb_w_qkv', 'v_sb_g_q', 'v_sb_g_k', 'v_sb_w_o', 'v_norm_ffn2', 'v_ffn2_w_gate', 'v_ffn2_w_up', 'v_ffn2_w_down']
TWIN_OUTPUTS = ['loss', 'grad_x', 'grad_norm_ffn1', 'grad_ffn1_w_gate', 'grad_ffn1_w_up', 'grad_ffn1_w_down', 'grad_norm_mix', 'grad_s5_w_in', 'grad_s5_lam_re', 'grad_s5_lam_im', 'grad_s5_log_dt', 'grad_s5_b_re', 'grad_s5_b_im', 'grad_s5_c_re', 'grad_s5_c_im', 'grad_s5_d', 'grad_s5_w_glu', 'grad_s5_b_glu', 'grad_s5_w_out', 'grad_sb_w_qkv', 'grad_sb_g_q', 'grad_sb_g_k', 'grad_sb_w_o', 'grad_norm_ffn2', 'grad_ffn2_w_gate', 'grad_ffn2_w_up', 'grad_ffn2_w_down', 'delta_norm_ffn1', 'delta_ffn1_w_gate', 'delta_ffn1_w_up', 'delta_ffn1_w_down', 'delta_norm_mix', 'delta_s5_w_in', 'delta_s5_lam_re', 'delta_s5_lam_im', 'delta_s5_log_dt', 'delta_s5_b_re', 'delta_s5_b_im', 'delta_s5_c_re', 'delta_s5_c_im', 'delta_s5_d', 'delta_s5_w_glu', 'delta_s5_b_glu', 'delta_s5_w_out', 'delta_sb_w_qkv', 'delta_sb_g_q', 'delta_sb_g_k', 'delta_sb_w_o', 'delta_norm_ffn2', 'delta_ffn2_w_gate', 'delta_ffn2_w_up', 'delta_ffn2_w_down', 'new_m_norm_ffn1', 'new_m_ffn1_w_gate', 'new_m_ffn1_w_up', 'new_m_ffn1_w_down', 'new_m_norm_mix', 'new_m_s5_w_in', 'new_m_s5_lam_re', 'new_m_s5_lam_im', 'new_m_s5_log_dt', 'new_m_s5_b_re', 'new_m_s5_b_im', 'new_m_s5_c_re', 'new_m_s5_c_im', 'new_m_s5_d', 'new_m_s5_w_glu', 'new_m_s5_b_glu', 'new_m_s5_w_out', 'new_m_sb_w_qkv', 'new_m_sb_g_q', 'new_m_sb_g_k', 'new_m_sb_w_o', 'new_m_norm_ffn2', 'new_m_ffn2_w_gate', 'new_m_ffn2_w_up', 'new_m_ffn2_w_down', 'new_v_norm_ffn1', 'new_v_ffn1_w_gate', 'new_v_ffn1_w_up', 'new_v_ffn1_w_down', 'new_v_norm_mix', 'new_v_s5_w_in', 'new_v_s5_lam_re', 'new_v_s5_lam_im', 'new_v_s5_log_dt', 'new_v_s5_b_re', 'new_v_s5_b_im', 'new_v_s5_c_re', 'new_v_s5_c_im', 'new_v_s5_d', 'new_v_s5_w_glu', 'new_v_s5_b_glu', 'new_v_s5_w_out', 'new_v_sb_w_qkv', 'new_v_sb_g_q', 'new_v_sb_g_k', 'new_v_sb_w_o', 'new_v_norm_ffn2', 'new_v_ffn2_w_gate', 'new_v_ffn2_w_up', 'new_v_ffn2_w_down']
TWIN_LEAF_KINDS = {'loss': 'loss', 'grad_x': 'grad_x', 'grad_norm_ffn1': 'grad_w', 'grad_ffn1_w_gate': 'grad_w', 'grad_ffn1_w_up': 'grad_w', 'grad_ffn1_w_down': 'grad_w', 'grad_norm_mix': 'grad_w', 'grad_s5_w_in': 'grad_w', 'grad_s5_lam_re': 'grad_w', 'grad_s5_lam_im': 'grad_w', 'grad_s5_log_dt': 'grad_w', 'grad_s5_b_re': 'grad_w', 'grad_s5_b_im': 'grad_w', 'grad_s5_c_re': 'grad_w', 'grad_s5_c_im': 'grad_w', 'grad_s5_d': 'grad_w', 'grad_s5_w_glu': 'grad_w', 'grad_s5_b_glu': 'grad_w', 'grad_s5_w_out': 'grad_w', 'grad_sb_w_qkv': 'grad_w', 'grad_sb_g_q': 'grad_w', 'grad_sb_g_k': 'grad_w', 'grad_sb_w_o': 'grad_w', 'grad_norm_ffn2': 'grad_w', 'grad_ffn2_w_gate': 'grad_w', 'grad_ffn2_w_up': 'grad_w', 'grad_ffn2_w_down': 'grad_w', 'delta_norm_ffn1': 'delta_w', 'delta_ffn1_w_gate': 'delta_w', 'delta_ffn1_w_up': 'delta_w', 'delta_ffn1_w_down': 'delta_w', 'delta_norm_mix': 'delta_w', 'delta_s5_w_in': 'delta_w', 'delta_s5_lam_re': 'delta_w', 'delta_s5_lam_im': 'delta_w', 'delta_s5_log_dt': 'delta_w', 'delta_s5_b_re': 'delta_w', 'delta_s5_b_im': 'delta_w', 'delta_s5_c_re': 'delta_w', 'delta_s5_c_im': 'delta_w', 'delta_s5_d': 'delta_w', 'delta_s5_w_glu': 'delta_w', 'delta_s5_b_glu': 'delta_w', 'delta_s5_w_out': 'delta_w', 'delta_sb_w_qkv': 'delta_w', 'delta_sb_g_q': 'delta_w', 'delta_sb_g_k': 'delta_w', 'delta_sb_w_o': 'delta_w', 'delta_norm_ffn2': 'delta_w', 'delta_ffn2_w_gate': 'delta_w', 'delta_ffn2_w_up': 'delta_w', 'delta_ffn2_w_down': 'delta_w', 'new_m_norm_ffn1': 'new_m', 'new_m_ffn1_w_gate': 'new_m', 'new_m_ffn1_w_up': 'new_m', 'new_m_ffn1_w_down': 'new_m', 'new_m_norm_mix': 'new_m', 'new_m_s5_w_in': 'new_m', 'new_m_s5_lam_re': 'new_m', 'new_m_s5_lam_im': 'new_m', 'new_m_s5_log_dt': 'new_m', 'new_m_s5_b_re': 'new_m', 'new_m_s5_b_im': 'new_m', 'new_m_s5_c_re': 'new_m', 'new_m_s5_c_im': 'new_m', 'new_m_s5_d': 'new_m', 'new_m_s5_w_glu': 'new_m', 'new_m_s5_b_glu': 'new_m', 'new_m_s5_w_out': 'new_m', 'new_m_sb_w_qkv': 'new_m', 'new_m_sb_g_q': 'new_m', 'new_m_sb_g_k': 'new_m', 'new_m_sb_w_o': 'new_m', 'new_m_norm_ffn2': 'new_m', 'new_m_ffn2_w_gate': 'new_m', 'new_m_ffn2_w_up': 'new_m', 'new_m_ffn2_w_down': 'new_m', 'new_v_norm_ffn1': 'new_v', 'new_v_ffn1_w_gate': 'new_v', 'new_v_ffn1_w_up': 'new_v', 'new_v_ffn1_w_down': 'new_v', 'new_v_norm_mix': 'new_v', 'new_v_s5_w_in': 'new_v', 'new_v_s5_lam_re': 'new_v', 'new_v_s5_lam_im': 'new_v', 'new_v_s5_log_dt': 'new_v', 'new_v_s5_b_re': 'new_v', 'new_v_s5_b_im': 'new_v', 'new_v_s5_c_re': 'new_v', 'new_v_s5_c_im': 'new_v', 'new_v_s5_d': 'new_v', 'new_v_s5_w_glu': 'new_v', 'new_v_s5_b_glu': 'new_v', 'new_v_s5_w_out': 'new_v', 'new_v_sb_w_qkv': 'new_v', 'new_v_sb_g_q': 'new_v', 'new_v_sb_g_k': 'new_v', 'new_v_sb_w_o': 'new_v', 'new_v_norm_ffn2': 'new_v', 'new_v_ffn2_w_gate': 'new_v', 'new_v_ffn2_w_up': 'new_v', 'new_v_ffn2_w_down': 'new_v'}


def _forward(args):
    return _fwd_reference(*[args[k] for k in FWD_PARAMS])


def _output_shape():
    def fwd():
        inp = _fwd_setup_inputs(0)
        return _fwd_reference(*[inp[k] for k in FWD_PARAMS])
    out = _jax.eval_shape(fwd)
    return out.shape, out.dtype

N_MICROBATCH = 1
ADAM_LR = 0.001
ADAM_B1 = 0.9
ADAM_B2 = 0.999
ADAM_EPS = 1e-08
ADAM_WD = 0.01
ADAM_STEP = 10
PER_EXAMPLE_BATCH_AXIS = {'x': 0, 'loss_target': 0}
SHARED_INPUTS = []
_WEIGHT_DTYPES = {'norm_ffn1': _jnp.float32, 'ffn1_w_gate': _jnp.float32, 'ffn1_w_up': _jnp.float32, 'ffn1_w_down': _jnp.float32, 'norm_mix': _jnp.float32, 's5_w_in': _jnp.float32, 's5_lam_re': _jnp.float32, 's5_lam_im': _jnp.float32, 's5_log_dt': _jnp.float32, 's5_b_re': _jnp.float32, 's5_b_im': _jnp.float32, 's5_c_re': _jnp.float32, 's5_c_im': _jnp.float32, 's5_d': _jnp.float32, 's5_w_glu': _jnp.float32, 's5_b_glu': _jnp.float32, 's5_w_out': _jnp.float32, 'sb_w_qkv': _jnp.float32, 'sb_g_q': _jnp.float32, 'sb_g_k': _jnp.float32, 'sb_w_o': _jnp.float32, 'norm_ffn2': _jnp.float32, 'ffn2_w_gate': _jnp.float32, 'ffn2_w_up': _jnp.float32, 'ffn2_w_down': _jnp.float32}
MOMENT_SCALE = {'norm_ffn1': 3.041964e+00, 'ffn1_w_gate': 6.021996e-02, 'ffn1_w_up': 6.230218e-02, 'ffn1_w_down': 1.020043e-01, 'norm_mix': 5.208875e+00, 's5_w_in': 1.608064e-01, 's5_lam_re': 1.232660e-02, 's5_lam_im': 9.061532e-03, 's5_log_dt': 3.533004e+00, 's5_b_re': 7.597158e-03, 's5_b_im': 7.620632e-03, 's5_c_re': 1.108578e-02, 's5_c_im': 1.085250e-02, 's5_d': 3.000602e+00, 's5_w_glu': 5.908295e-01, 's5_b_glu': 1.748848e+00, 's5_w_out': 1.095505e+00, 'sb_w_qkv': 3.972816e-01, 'sb_g_q': 1.560185e+01, 'sb_g_k': 1.556238e+01, 'sb_w_o': 4.983959e-01, 'norm_ffn2': 3.083939e+00, 'ffn2_w_gate': 6.669138e-02, 'ffn2_w_up': 6.281829e-02, 'ffn2_w_down': 1.021694e-01}


def _to_microbatches(a, axis):
    t = _jnp.moveaxis(a, axis, 0)
    t = t.reshape((N_MICROBATCH, t.shape[0] // N_MICROBATCH) + t.shape[1:])
    return _jnp.moveaxis(t, 1, axis + 1)


def setup_inputs(seed: int = 0) -> dict:
    inp = _fwd_setup_inputs(seed)
    key = _jax.random.fold_in(_jax.random.key(seed), 7919)
    shape, _ = _output_shape()
    out = dict(inp)
    out["loss_target"] = _jax.random.normal(_jax.random.fold_in(key, 0), shape, _jnp.float32)
    for i, name in enumerate(TWIN_WEIGHTS):
        w = inp[name].astype(_jnp.float32)
        if MOMENT_SCALE is None:
            s = _jnp.sqrt(_jnp.mean(_jnp.square(w)) + 1e-30)
        else:
            s = MOMENT_SCALE[name]
        km, kv = _jax.random.split(_jax.random.fold_in(key, i + 1))
        out[name] = w
        out["m_" + name] = s * _jax.random.normal(km, w.shape, _jnp.float32)
        out["v_" + name] = (s * s) * _jax.random.uniform(kv, w.shape, _jnp.float32, 0.5, 1.5)
    if N_MICROBATCH > 1:
        for name, axis in PER_EXAMPLE_BATCH_AXIS.items():
            out[name] = _to_microbatches(out[name], axis)
    return {'x': out['x'], 'norm_ffn1': out['norm_ffn1'], 'ffn1_w_gate': out['ffn1_w_gate'], 'ffn1_w_up': out['ffn1_w_up'], 'ffn1_w_down': out['ffn1_w_down'], 'norm_mix': out['norm_mix'], 's5_w_in': out['s5_w_in'], 's5_lam_re': out['s5_lam_re'], 's5_lam_im': out['s5_lam_im'], 's5_log_dt': out['s5_log_dt'], 's5_b_re': out['s5_b_re'], 's5_b_im': out['s5_b_im'], 's5_c_re': out['s5_c_re'], 's5_c_im': out['s5_c_im'], 's5_d': out['s5_d'], 's5_w_glu': out['s5_w_glu'], 's5_b_glu': out['s5_b_glu'], 's5_w_out': out['s5_w_out'], 'sb_w_qkv': out['sb_w_qkv'], 'sb_g_q': out['sb_g_q'], 'sb_g_k': out['sb_g_k'], 'sb_w_o': out['sb_w_o'], 'norm_ffn2': out['norm_ffn2'], 'ffn2_w_gate': out['ffn2_w_gate'], 'ffn2_w_up': out['ffn2_w_up'], 'ffn2_w_down': out['ffn2_w_down'], 'loss_target': out['loss_target'], 'm_norm_ffn1': out['m_norm_ffn1'], 'm_ffn1_w_gate': out['m_ffn1_w_gate'], 'm_ffn1_w_up': out['m_ffn1_w_up'], 'm_ffn1_w_down': out['m_ffn1_w_down'], 'm_norm_mix': out['m_norm_mix'], 'm_s5_w_in': out['m_s5_w_in'], 'm_s5_lam_re': out['m_s5_lam_re'], 'm_s5_lam_im': out['m_s5_lam_im'], 'm_s5_log_dt': out['m_s5_log_dt'], 'm_s5_b_re': out['m_s5_b_re'], 'm_s5_b_im': out['m_s5_b_im'], 'm_s5_c_re': out['m_s5_c_re'], 'm_s5_c_im': out['m_s5_c_im'], 'm_s5_d': out['m_s5_d'], 'm_s5_w_glu': out['m_s5_w_glu'], 'm_s5_b_glu': out['m_s5_b_glu'], 'm_s5_w_out': out['m_s5_w_out'], 'm_sb_w_qkv': out['m_sb_w_qkv'], 'm_sb_g_q': out['m_sb_g_q'], 'm_sb_g_k': out['m_sb_g_k'], 'm_sb_w_o': out['m_sb_w_o'], 'm_norm_ffn2': out['m_norm_ffn2'], 'm_ffn2_w_gate': out['m_ffn2_w_gate'], 'm_ffn2_w_up': out['m_ffn2_w_up'], 'm_ffn2_w_down': out['m_ffn2_w_down'], 'v_norm_ffn1': out['v_norm_ffn1'], 'v_ffn1_w_gate': out['v_ffn1_w_gate'], 'v_ffn1_w_up': out['v_ffn1_w_up'], 'v_ffn1_w_down': out['v_ffn1_w_down'], 'v_norm_mix': out['v_norm_mix'], 'v_s5_w_in': out['v_s5_w_in'], 'v_s5_lam_re': out['v_s5_lam_re'], 'v_s5_lam_im': out['v_s5_lam_im'], 'v_s5_log_dt': out['v_s5_log_dt'], 'v_s5_b_re': out['v_s5_b_re'], 'v_s5_b_im': out['v_s5_b_im'], 'v_s5_c_re': out['v_s5_c_re'], 'v_s5_c_im': out['v_s5_c_im'], 'v_s5_d': out['v_s5_d'], 'v_s5_w_glu': out['v_s5_w_glu'], 'v_s5_b_glu': out['v_s5_b_glu'], 'v_s5_w_out': out['v_s5_w_out'], 'v_sb_w_qkv': out['v_sb_w_qkv'], 'v_sb_g_q': out['v_sb_g_q'], 'v_sb_g_k': out['v_sb_g_k'], 'v_sb_w_o': out['v_sb_w_o'], 'v_norm_ffn2': out['v_norm_ffn2'], 'v_ffn2_w_gate': out['v_ffn2_w_gate'], 'v_ffn2_w_up': out['v_ffn2_w_up'], 'v_ffn2_w_down': out['v_ffn2_w_down']}


def _loss(weights, diff, rest, loss_target):
    with _jax.named_scope("forward"):
        args = {**rest, TWIN_DIFF_INPUT: diff, **{k: w.astype(_WEIGHT_DTYPES[k]) for k, w in weights.items()}}
        y = _forward(args)
    with _jax.named_scope("loss_head"):
        err = _jnp.square(y.astype(_jnp.float32) - loss_target)
        return 0.5 * _jnp.sum(_jnp.mean(err, axis=-1)) if err.ndim else 0.5 * err


def _adamw(w, g, m, v):
    m = ADAM_B1 * m + (1.0 - ADAM_B1) * g
    v = ADAM_B2 * v + (1.0 - ADAM_B2) * _jnp.square(g)
    m_hat = m / (1.0 - ADAM_B1 ** ADAM_STEP)
    v_hat = v / (1.0 - ADAM_B2 ** ADAM_STEP)
    delta = -ADAM_LR * (m_hat / (_jnp.sqrt(v_hat) + ADAM_EPS) + ADAM_WD * w)
    return delta, m, v


def reference(x, norm_ffn1, ffn1_w_gate, ffn1_w_up, ffn1_w_down, norm_mix, s5_w_in, s5_lam_re, s5_lam_im, s5_log_dt, s5_b_re, s5_b_im, s5_c_re, s5_c_im, s5_d, s5_w_glu, s5_b_glu, s5_w_out, sb_w_qkv, sb_g_q, sb_g_k, sb_w_o, norm_ffn2, ffn2_w_gate, ffn2_w_up, ffn2_w_down, loss_target, m_norm_ffn1, m_ffn1_w_gate, m_ffn1_w_up, m_ffn1_w_down, m_norm_mix, m_s5_w_in, m_s5_lam_re, m_s5_lam_im, m_s5_log_dt, m_s5_b_re, m_s5_b_im, m_s5_c_re, m_s5_c_im, m_s5_d, m_s5_w_glu, m_s5_b_glu, m_s5_w_out, m_sb_w_qkv, m_sb_g_q, m_sb_g_k, m_sb_w_o, m_norm_ffn2, m_ffn2_w_gate, m_ffn2_w_up, m_ffn2_w_down, v_norm_ffn1, v_ffn1_w_gate, v_ffn1_w_up, v_ffn1_w_down, v_norm_mix, v_s5_w_in, v_s5_lam_re, v_s5_lam_im, v_s5_log_dt, v_s5_b_re, v_s5_b_im, v_s5_c_re, v_s5_c_im, v_s5_d, v_s5_w_glu, v_s5_b_glu, v_s5_w_out, v_sb_w_qkv, v_sb_g_q, v_sb_g_k, v_sb_w_o, v_norm_ffn2, v_ffn2_w_gate, v_ffn2_w_up, v_ffn2_w_down):
    given = dict(x=x, norm_ffn1=norm_ffn1, ffn1_w_gate=ffn1_w_gate, ffn1_w_up=ffn1_w_up, ffn1_w_down=ffn1_w_down, norm_mix=norm_mix, s5_w_in=s5_w_in, s5_lam_re=s5_lam_re, s5_lam_im=s5_lam_im, s5_log_dt=s5_log_dt, s5_b_re=s5_b_re, s5_b_im=s5_b_im, s5_c_re=s5_c_re, s5_c_im=s5_c_im, s5_d=s5_d, s5_w_glu=s5_w_glu, s5_b_glu=s5_b_glu, s5_w_out=s5_w_out, sb_w_qkv=sb_w_qkv, sb_g_q=sb_g_q, sb_g_k=sb_g_k, sb_w_o=sb_w_o, norm_ffn2=norm_ffn2, ffn2_w_gate=ffn2_w_gate, ffn2_w_up=ffn2_w_up, ffn2_w_down=ffn2_w_down, loss_target=loss_target, m_norm_ffn1=m_norm_ffn1, m_ffn1_w_gate=m_ffn1_w_gate, m_ffn1_w_up=m_ffn1_w_up, m_ffn1_w_down=m_ffn1_w_down, m_norm_mix=m_norm_mix, m_s5_w_in=m_s5_w_in, m_s5_lam_re=m_s5_lam_re, m_s5_lam_im=m_s5_lam_im, m_s5_log_dt=m_s5_log_dt, m_s5_b_re=m_s5_b_re, m_s5_b_im=m_s5_b_im, m_s5_c_re=m_s5_c_re, m_s5_c_im=m_s5_c_im, m_s5_d=m_s5_d, m_s5_w_glu=m_s5_w_glu, m_s5_b_glu=m_s5_b_glu, m_s5_w_out=m_s5_w_out, m_sb_w_qkv=m_sb_w_qkv, m_sb_g_q=m_sb_g_q, m_sb_g_k=m_sb_g_k, m_sb_w_o=m_sb_w_o, m_norm_ffn2=m_norm_ffn2, m_ffn2_w_gate=m_ffn2_w_gate, m_ffn2_w_up=m_ffn2_w_up, m_ffn2_w_down=m_ffn2_w_down, v_norm_ffn1=v_norm_ffn1, v_ffn1_w_gate=v_ffn1_w_gate, v_ffn1_w_up=v_ffn1_w_up, v_ffn1_w_down=v_ffn1_w_down, v_norm_mix=v_norm_mix, v_s5_w_in=v_s5_w_in, v_s5_lam_re=v_s5_lam_re, v_s5_lam_im=v_s5_lam_im, v_s5_log_dt=v_s5_log_dt, v_s5_b_re=v_s5_b_re, v_s5_b_im=v_s5_b_im, v_s5_c_re=v_s5_c_re, v_s5_c_im=v_s5_c_im, v_s5_d=v_s5_d, v_s5_w_glu=v_s5_w_glu, v_s5_b_glu=v_s5_b_glu, v_s5_w_out=v_s5_w_out, v_sb_w_qkv=v_sb_w_qkv, v_sb_g_q=v_sb_g_q, v_sb_g_k=v_sb_g_k, v_sb_w_o=v_sb_w_o, v_norm_ffn2=v_norm_ffn2, v_ffn2_w_gate=v_ffn2_w_gate, v_ffn2_w_up=v_ffn2_w_up, v_ffn2_w_down=v_ffn2_w_down)
    weights = {n: given[n] for n in TWIN_WEIGHTS}
    shared = {n: given[n] for n in SHARED_INPUTS}
    per_example = {n: given[n] for n in ['x']}
    grad_fn = _jax.value_and_grad(_loss, argnums=(0, 1))

    def one_microbatch(ex, loss_target):
        ex = dict(ex)
        diff = ex.pop(TWIN_DIFF_INPUT)
        return grad_fn(weights, diff, {**shared, **ex}, loss_target)

    if N_MICROBATCH == 1:
        loss, (grad_w, grad_x) = one_microbatch(per_example, given["loss_target"])
    else:
        def body(carry, xs):
            loss_sum, grad_sum = carry
            l_k, (gw_k, gx_k) = one_microbatch(xs[0], xs[1])
            with _jax.named_scope("update"):
                return (loss_sum + l_k, _jax.tree.map(_jnp.add, grad_sum, gw_k)), gx_k

        init = (_jnp.zeros((), _jnp.float32), _jax.tree.map(_jnp.zeros_like, weights))
        (loss, grad_w), grad_x = _jax.lax.scan(body, init, (per_example, given["loss_target"]))
    with _jax.named_scope("update"):
        delta_w, new_m, new_v = {}, {}, {}
        for n in TWIN_WEIGHTS:
            delta_w[n], new_m[n], new_v[n] = _adamw(weights[n], grad_w[n], given["m_" + n], given["v_" + n])
    return (loss, grad_x, *[grad_w[n] for n in TWIN_WEIGHTS], *[delta_w[n] for n in TWIN_WEIGHTS],
            *[new_m[n] for n in TWIN_WEIGHTS], *[new_v[n] for n in TWIN_WEIGHTS])
```

```python
import math

import jax
import jax.numpy as jnp
from jax import lax
from jax.experimental import pallas as pl
from jax.experimental.pallas import tpu as pltpu

f32 = jnp.float32
bf16 = jnp.bfloat16

NDEV = 8
AXES = ("x", "y", "c")
EPS = 1e-6
HEAD = 128
S5H = 16
S5P = 64
GB = 8
FFN_RES = 0.5
ADAM_LR = 0.001
ADAM_B1 = 0.9
ADAM_B2 = 0.999
ADAM_EPS = 1e-08
ADAM_WD = 0.01
ADAM_STEP = 10
VMEM_LIMIT_V7X = 56 * 2 ** 20
TM, TN, TK = 512, 512, 512

NN = (((1,), (0,)), ((), ()))
NT = (((1,), (1,)), ((), ()))
TNd = (((0,), (0,)), ((), ()))


def _tile(n, target, align):
    if n <= target:
        return n
    t = (target // align) * align
    while t >= align:
        if n % t == 0:
            return t
        t -= align
    return n


def _params(sem):
    return pltpu.CompilerParams(dimension_semantics=sem, vmem_limit_bytes=VMEM_LIMIT_V7X)


def _mm(name, a_list, b_list, a_spec, b_spec, dims, grid, out_shapes, out_specs, acc_shape,
        epilogue=None, extra=(), extra_specs=()):
    n_p, n_e, n_o = len(a_list), len(extra), len(out_shapes)
    nk = grid[2]

    def body(*refs):
        a_refs = refs[:n_p]
        b_refs = refs[n_p:2 * n_p]
        e_refs = refs[2 * n_p:2 * n_p + n_e]
        o_refs = refs[2 * n_p + n_e:2 * n_p + n_e + n_o]
        acc = refs[-1]
        k = pl.program_id(2)

        @pl.when(k == 0)
        def _():
            acc[...] = jnp.zeros_like(acc)

        s = None
        for ar, br in zip(a_refs, b_refs):
            d = lax.dot_general(ar[...].astype(bf16), br[...].astype(bf16), dims, preferred_element_type=f32)
            s = d if s is None else s + d
        acc[...] += s

        @pl.when(k == nk - 1)
        def _():
            vals = acc[...]
            outs = (vals,) if epilogue is None else epilogue(vals, *[e[...] for e in e_refs])
            for o, val in zip(o_refs, outs):
                o[...] = val.astype(o.dtype)

    res = pl.pallas_call(
        body, name=name, grid=grid,
        in_specs=[a_spec] * n_p + [b_spec] * n_p + list(extra_specs),
        out_specs=list(out_specs), out_shape=list(out_shapes),
        scratch_shapes=[pltpu.VMEM(acc_shape, f32)],
        compiler_params=_params(("parallel", "parallel", "arbitrary")),
    )(*a_list, *b_list, *extra)
    return res


def _sds(shape, dtype):
    return jax.ShapeDtypeStruct(tuple(shape), dtype)


def _ew(name, fn, rows, bcasts, outs, reds=(), tm=None):
    n_r, n_b, n_o, n_d = len(rows), len(bcasts), len(outs), len(reds)
    R = rows[0].shape[-2]
    if tm is None:
        widest = max([r.shape[-1] * (r.shape[0] if r.ndim == 3 else 1) for r in rows] + [c for c, _ in outs])
        tm = _tile(R, max(16, (1 << 19) // widest), 16)

    def body(*refs):
        r = refs[:n_r]
        b = refs[n_r:n_r + n_b]
        o = refs[n_r + n_b:n_r + n_b + n_o]
        d = refs[n_r + n_b + n_o:]
        res = fn(*[x[...] for x in r], *[x[...] for x in b])
        for oo, val in zip(o, res[:n_o]):
            oo[...] = val.astype(oo.dtype)
        if n_d:
            @pl.when(pl.program_id(0) == 0)
            def _():
                for dd in d:
                    dd[...] = jnp.zeros_like(dd)
            for dd, val in zip(d, res[n_o:]):
                dd[...] += val

    in_specs = []
    for x in rows:
        if x.ndim == 2:
            in_specs.append(pl.BlockSpec((tm, x.shape[1]), lambda i: (i, 0)))
        else:
            in_specs.append(pl.BlockSpec((x.shape[0], tm, x.shape[2]), lambda i: (0, i, 0)))
    for x in bcasts:
        in_specs.append(pl.BlockSpec(x.shape, lambda i, nd=x.ndim: (0,) * nd))
    out_shape = [_sds((R, c), dt) for c, dt in outs] + [_sds((1, c), f32) for c in reds]
    out_specs = [pl.BlockSpec((tm, c), lambda i: (i, 0)) for c, _ in outs] + [pl.BlockSpec((1, c), lambda i: (0, 0)) for c in reds]
    return pl.pallas_call(
        body, name=name, grid=(R // tm,), in_specs=in_specs, out_specs=out_specs, out_shape=out_shape,
        compiler_params=_params(("arbitrary",)),
    )(*rows, *bcasts)


def _coords():
    return lax.axis_index("x"), lax.axis_index("y"), lax.axis_index("c")


def all_gather(name, shards):
    n = len(shards)

    def body(*refs):
        x_refs, out_refs = refs[:n], refs[n:2 * n]
        send_sems, recv_sems, local_sems = refs[2 * n:]
        x, y, c = _coords()
        me, sibling = (x, y, c), (x, y, 1 - c)
        chips = [(1 - x, y), (x, 1 - y), (1 - x, 1 - y)]

        def rows(a, px, py, pc):
            return out_refs[a].at[4 * px + 2 * py + pc]

        def copy(a, k, block, to, src=None):
            return pltpu.make_async_remote_copy(
                src_ref=rows(a, *block) if src is None else src, dst_ref=rows(a, *block),
                send_sem=send_sems.at[a, k], recv_sem=recv_sems.at[a, k],
                device_id=to, device_id_type=pl.DeviceIdType.MESH)

        mine = [pltpu.make_async_copy(x_refs[a], rows(a, *me), local_sems.at[a]) for a in range(n)]
        for cp in mine:
            cp.start()
        first = []
        for a in range(n):
            first.append(copy(a, 0, me, sibling, src=x_refs[a]))
            first += [copy(a, 1 + j, me, (*chip, c), src=x_refs[a]) for j, chip in enumerate(chips)]
        for cp in first:
            cp.start()
        passed = []
        for j, chip in enumerate(chips):
            for a in range(n):
                copy(a, 1 + j, (*chip, c), me).wait_recv()
                cp = copy(a, 4 + j, (*chip, c), sibling)
                cp.start()
                passed.append(cp)
        for a in range(n):
            copy(a, 0, sibling, me).wait_recv()
            for j, chip in enumerate(chips):
                copy(a, 4 + j, (*chip, 1 - c), me).wait_recv()
        for cp in first + passed:
            cp.wait_send()
        for cp in mine:
            cp.wait()

    anyspec = pl.BlockSpec(memory_space=pl.ANY)
    return pl.pallas_call(
        body, name=name,
        out_shape=[_sds((NDEV,) + s.shape, s.dtype) for s in shards],
        in_specs=[anyspec] * n, out_specs=[anyspec] * n,
        scratch_shapes=[pltpu.SemaphoreType.DMA((n, 7)), pltpu.SemaphoreType.DMA((n, 7)), pltpu.SemaphoreType.DMA((n,))],
    )(*shards)


def reduce_scatter_recv(name, fulls):
    n = len(fulls)
    _, R, C = fulls[0].shape

    def body(*refs):
        g_refs, recv_ref = refs[:n], refs[n]
        send_sems, recv_sems, local_sems = refs[n + 1:]
        x, y, c = _coords()
        me = 4 * x + 2 * y + c
        local = [pltpu.make_async_copy(g_refs[a].at[me], recv_ref.at[me, a], local_sems.at[a]) for a in range(n)]
        for cp in local:
            cp.start()
        sends = []
        for k in range(1, NDEV):
            px, py, pc = x ^ ((k >> 2) & 1), y ^ ((k >> 1) & 1), c ^ (k & 1)
            p = 4 * px + 2 * py + pc
            for a in range(n):
                cp = pltpu.make_async_remote_copy(
                    src_ref=g_refs[a].at[p], dst_ref=recv_ref.at[me, a],
                    send_sem=send_sems.at[a, k - 1], recv_sem=recv_sems.at[a, k - 1],
                    device_id=(px, py, pc), device_id_type=pl.DeviceIdType.MESH)
                cp.start()
                sends.append((cp, a, k, p))
        for cp, a, k, p in sends:
            pltpu.make_async_remote_copy(
                src_ref=g_refs[a].at[p], dst_ref=recv_ref.at[p, a],
                send_sem=send_sems.at[a, k - 1], recv_sem=recv_sems.at[a, k - 1],
                device_id=(x, y, c), device_id_type=pl.DeviceIdType.MESH).wait_recv()
        for cp, a, k, p in sends:
            cp.wait_send()
        for cp in local:
            cp.wait()

    anyspec = pl.BlockSpec(memory_space=pl.ANY)
    return pl.pallas_call(
        body, name=name,
        out_shape=_sds((NDEV, n, R, C), fulls[0].dtype),
        in_specs=[anyspec] * n, out_specs=anyspec,
        scratch_shapes=[pltpu.SemaphoreType.DMA((n, 7)), pltpu.SemaphoreType.DMA((n, 7)), pltpu.SemaphoreType.DMA((n,))],
    )(*fulls)


def adamw(name, w, m, v, recv):
    C = w.shape[1]

    def fn(w_, m_, v_, r_):
        g = r_[0].astype(f32)
        for j in range(1, NDEV):
            g = g + r_[j].astype(f32)
        m2 = ADAM_B1 * m_ + (1.0 - ADAM_B1) * g
        v2 = ADAM_B2 * v_ + (1.0 - ADAM_B2) * jnp.square(g)
        m_hat = m2 / (1.0 - ADAM_B1 ** ADAM_STEP)
        v_hat = v2 / (1.0 - ADAM_B2 ** ADAM_STEP)
        delta = -ADAM_LR * (m_hat / (jnp.sqrt(v_hat) + ADAM_EPS) + ADAM_WD * w_)
        return g, delta, m2, v2

    return _ew(name, fn, [w, m, v, recv], [], [(C, f32)] * 4)


def rmsnorm_fwd(name, x, g):
    D = x.shape[1]

    def fn(x_, g_):
        r = lax.rsqrt(jnp.mean(x_ * x_, axis=-1, keepdims=True) + EPS)
        return ((x_ * r) * g_,)

    return _ew(name, fn, [x], [g], [(D, bf16)])[0]


def rmsnorm_bwd(name, x, g, dh, dres):
    D = x.shape[1]

    def fn(x_, dh_, dres_, g_):
        r = lax.rsqrt(jnp.mean(x_ * x_, axis=-1, keepdims=True) + EPS)
        xh = x_ * r
        dxh = dh_ * g_
        dx = dres_ + r * (dxh - xh * jnp.mean(dxh * xh, axis=-1, keepdims=True))
        return dx, jnp.sum(dh_ * xh, axis=0, keepdims=True)

    return _ew(name, fn, [x, dh, dres], [g], [(D, f32)], [D])


def _silu_parts(a):
    sig = jax.nn.sigmoid(a)
    return sig, a * sig


def mm_cols(name, h, w4, l, epilogue=None, extra=(), outs=None):
    T, K = h.shape
    Nb = w4.shape[3]
    tm, tk = _tile(T, TM, 16), _tile(K, TK, 128)
    outs = outs or [f32]
    blk = pl.BlockSpec((None, tm, Nb), lambda j, i, k: (j, i, 0))
    return _mm(name, [h], [w4], pl.BlockSpec((tm, tk), lambda j, i, k: (i, k)),
               pl.BlockSpec((None, None, tk, Nb), lambda j, i, k: (j, l, k, 0)), NN, (NDEV, T // tm, K // tk),
               [_sds((NDEV, T, Nb), dt) for dt in outs], [blk] * len(outs), (tm, Nb),
               epilogue, extra, [blk] * len(extra))


def mm_rows(name, p, w4, l, epilogue=None, extra=(), outs=None):
    _, T, Kb = p.shape
    N = w4.shape[3]
    tm, tn = _tile(T, TM, 16), _tile(N, TN, 128)
    outs = outs or [f32]
    blk = pl.BlockSpec((tm, tn), lambda i, n, k: (i, n))
    return _mm(name, [p], [w4], pl.BlockSpec((None, tm, Kb), lambda i, n, k: (k, i, 0)),
               pl.BlockSpec((None, None, Kb, tn), lambda i, n, k: (k, l, 0, n)), NN, (T // tm, N // tn, NDEV),
               [_sds((T, N), dt) for dt in outs], [blk] * len(outs), (tm, tn),
               epilogue, extra, [blk] * len(extra))


def mm_rows_plain(name, a, w4, epilogue=None, extra=(), outs=None):
    T, K = a.shape
    Kb, N = w4.shape[2], w4.shape[3]
    tm, tn = _tile(T, TM, 16), _tile(N, TN, 128)
    outs = outs or [f32]
    blk = pl.BlockSpec((tm, tn), lambda i, n, k: (i, n))
    return _mm(name, [a], [w4], pl.BlockSpec((tm, Kb), lambda i, n, k: (i, k)),
               pl.BlockSpec((None, None, Kb, tn), lambda i, n, k: (k, 0, 0, n)), NN, (T // tm, N // tn, NDEV),
               [_sds((T, N), dt) for dt in outs], [blk] * len(outs), (tm, tn),
               epilogue, extra, [blk] * len(extra))


def mm_t_rows(name, d, w4, l, epilogue=None, extra=(), outs=None):
    T, K = d.shape
    Nb = w4.shape[2]
    tm, tk = _tile(T, TM, 16), _tile(K, TK, 128)
    outs = outs or [f32]
    blk = pl.BlockSpec((None, tm, Nb), lambda j, i, k: (j, i, 0))
    return _mm(name, [d], [w4], pl.BlockSpec((tm, tk), lambda j, i, k: (i, k)),
               pl.BlockSpec((None, None, Nb, tk), lambda j, i, k: (j, l, 0, k)), NT, (NDEV, T // tm, K // tk),
               [_sds((NDEV, T, Nb), dt) for dt in outs], [blk] * len(outs), (tm, Nb),
               epilogue, extra, [blk] * len(extra))


def mm_t_rows_plain(name, d, w4, epilogue=None, extra=(), outs=None):
    T, K = d.shape
    Nb = w4.shape[2]
    tm, tk = _tile(T, TM, 16), _tile(K, TK, 128)
    outs = outs or [f32]
    blk = pl.BlockSpec((tm, Nb), lambda j, i, k: (i, j))
    return _mm(name, [d], [w4], pl.BlockSpec((tm, tk), lambda j, i, k: (i, k)),
               pl.BlockSpec((None, None, Nb, tk), lambda j, i, k: (j, 0, 0, k)), NT, (NDEV, T // tm, K // tk),
               [_sds((T, NDEV * Nb), dt) for dt in outs], [blk] * len(outs), (tm, Nb),
               epilogue, extra, [blk] * len(extra))


def mm_t_cols(name, d_list, w4_list, l, epilogue=None, extra=(), outs=None):
    _, T, Kb = d_list[0].shape
    N = w4_list[0].shape[2]
    tm, tn = _tile(T, TM, 16), _tile(N, TN, 128)
    outs = outs or [f32]
    blk = pl.BlockSpec((tm, tn), lambda i, n, k: (i, n))
    return _mm(name, d_list, w4_list, pl.BlockSpec((None, tm, Kb), lambda i, n, k: (k, i, 0)),
               pl.BlockSpec((None, None, tn, Kb), lambda i, n, k: (k, l, n, 0)), NT, (T // tm, N // tn, NDEV),
               [_sds((T, N), dt) for dt in outs], [blk] * len(outs), (tm, tn),
               epilogue, extra, [blk] * len(extra))


def mm_grad_rows(name, p, d, scale=1.0):
    _, T, Mb = p.shape
    N = d.shape[1]
    tn, tk = _tile(N, TN, 128), _tile(T, TK, 128)
    return _mm(name, [p], [d], pl.BlockSpec((None, tk, Mb), lambda j, n, k: (j, k, 0)),
               pl.BlockSpec((tk, tn), lambda j, n, k: (k, n)), TNd, (NDEV, N // tn, T // tk),
               [_sds((NDEV, Mb, N), bf16)], [pl.BlockSpec((None, Mb, tn), lambda j, n, k: (j, 0, n))], (Mb, tn),
               (lambda acc: (acc * scale,)))[0]


def mm_grad_rows_plain(name, a, d):
    T, M = a.shape
    Mb = M // NDEV
    N = d.shape[1]
    tn, tk = _tile(N, TN, 128), _tile(T, TK, 128)
    return _mm(name, [a], [d], pl.BlockSpec((tk, Mb), lambda j, n, k: (k, j)),
               pl.BlockSpec((tk, tn), lambda j, n, k: (k, n)), TNd, (NDEV, N // tn, T // tk),
               [_sds((NDEV, Mb, N), bf16)], [pl.BlockSpec((None, Mb, tn), lambda j, n, k: (j, 0, n))], (Mb, tn))[0]


def mm_grad_cols(name, h, d):
    T, M = h.shape
    Nb = d.shape[2]
    tm, tk = _tile(M, TM, 128), _tile(T, TK, 128)
    return _mm(name, [h], [d], pl.BlockSpec((tk, tm), lambda j, m, k: (k, m)),
               pl.BlockSpec((None, tk, Nb), lambda j, m, k: (j, k, 0)), TNd, (NDEV, M // tm, T // tk),
               [_sds((NDEV, M, Nb), bf16)], [pl.BlockSpec((None, tm, Nb), lambda j, m, k: (j, m, 0))], (tm, Nb))[0]


def ffn_fwd(tag, x, g, wg4, wu4, wd4, l):
    h = rmsnorm_fwd(tag + "_norm", x, g)
    a = mm_cols(tag + "_gate", h, wg4, l)[0]

    def up_epi(acc, a_):
        _, sl = _silu_parts(a_)
        return acc, sl * acc

    b, p = mm_cols(tag + "_up", h, wu4, l, up_epi, [a], [f32, bf16])
    xo = mm_rows(tag + "_down", p, wd4, l, lambda acc, x_: (x_ + FFN_RES * acc,), [x])[0]
    return xo, (x, h, a, b, p)


def ffn_bwd(tag, dxo, saved, g, wg4, wu4, wd4, l):
    x, h, a, b, p = saved

    def dp_epi(acc, a_, b_):
        dp = FFN_RES * acc
        sig, sl = _silu_parts(a_)
        return dp * b_ * (sig * (1.0 + a_ * (1.0 - sig))), dp * sl

    da, db = mm_t_rows(tag + "_dp", dxo, wd4, l, dp_epi, [a, b], [bf16, bf16])
    g_wd = mm_grad_rows(tag + "_gwd", p, dxo, FFN_RES)
    g_wg = mm_grad_cols(tag + "_gwg", h, da)
    g_wu = mm_grad_cols(tag + "_gwu", h, db)
    dh = mm_t_cols(tag + "_dh", [da, db], [wg4, wu4], l)[0]
    dx, dg = rmsnorm_bwd(tag + "_dnorm", x, g, dh, dxo)
    return dx, dg, g_wg, g_wu, g_wd


def _disc(lr, li, ldt, brt, bit):
    dt = jnp.exp(ldt)
    lr = jnp.minimum(lr, -1e-4)
    mag = jnp.exp(lr * dt)
    ab_re = mag * jnp.cos(li * dt)
    ab_im = mag * jnp.sin(li * dt)
    den = lr * lr + li * li
    n_re = ab_re - 1.0
    f_re = (n_re * lr + ab_im * li) / den
    f_im = (ab_im * lr - n_re * li) / den
    bb_re = f_re[:, None, :] * brt - f_im[:, None, :] * bit
    bb_im = f_re[:, None, :] * bit + f_im[:, None, :] * brt
    return ab_re, ab_im, bb_re, bb_im


def disc_fwd(lr, li, ldt, brt, bit):
    def body(lr_ref, li_ref, ldt_ref, brt_ref, bit_ref, ar_ref, ai_ref, bbr_ref, bbi_ref):
        ar, ai, bbr, bbi = _disc(lr_ref[...], li_ref[...], ldt_ref[...], brt_ref[...], bit_ref[...])
        ar_ref[...] = ar
        ai_ref[...] = ai
        bbr_ref[...] = bbr
        bbi_ref[...] = bbi

    return pl.pallas_call(body, name="s5_disc", out_shape=[_sds(lr.shape, f32), _sds(lr.shape, f32), _sds(brt.shape, f32), _sds(brt.shape, f32)],
                          compiler_params=pltpu.CompilerParams(vmem_limit_bytes=VMEM_LIMIT_V7X))(lr, li, ldt, brt, bit)


def disc_bwd(lr, li, ldt, brt, bit, d_ar, d_ai, d_bbr, d_bbi):
    def body(lr_ref, li_ref, ldt_ref, brt_ref, bit_ref, dar_ref, dai_ref, dbbr_ref, dbbi_ref, o_lr, o_li, o_ldt, o_brt, o_bit):
        _, vjp = jax.vjp(_disc, lr_ref[...], li_ref[...], ldt_ref[...], brt_ref[...], bit_ref[...])
        g = vjp((dar_ref[...], dai_ref[...], dbbr_ref[...], dbbi_ref[...]))
        o_lr[...] = g[0]
        o_li[...] = g[1]
        o_ldt[...] = g[2]
        o_brt[...] = g[3]
        o_bit[...] = g[4]

    return pl.pallas_call(body, name="s5_disc_bwd",
                          out_shape=[_sds(lr.shape, f32), _sds(lr.shape, f32), _sds(ldt.shape, f32), _sds(brt.shape, f32), _sds(brt.shape, f32)],
                          compiler_params=pltpu.CompilerParams(vmem_limit_bytes=VMEM_LIMIT_V7X))(lr, li, ldt, brt, bit, d_ar, d_ai, d_bbr, d_bbi)


def _blockdiag(m):
    G, R, C = m.shape
    eye = jnp.eye(GB, dtype=m.dtype)
    m5 = m.reshape(G // GB, GB, R, 1, C) * eye[None, :, None, :, None]
    return m5.reshape(G // GB, GB * R, GB * C)


def _diag_extract(M, R, C):
    nb = M.shape[0]
    eye = jnp.eye(GB, dtype=M.dtype)
    m5 = M.reshape(nb, GB, R, GB, C) * eye[None, :, None, :, None]
    return m5.sum(axis=3).reshape(nb * GB, R, C)


def _bd_specs(tm, ra, ca):
    return (pl.BlockSpec((tm, ra), lambda b, i, k: (i, b)), pl.BlockSpec((tm, ca), lambda b, i, k: (i, b)))


def mm_blockdiag(name, a_list, bd_list, dims, ca, epilogue=None, extra=(), outs=None):
    T = a_list[0].shape[0]
    nb = bd_list[0].shape[0]
    ra = a_list[0].shape[1] // nb
    tm = _tile(T, TM, 16)
    a_spec, o_spec = _bd_specs(tm, ra, ca)
    r, c = bd_list[0].shape[1:]
    outs = outs or [f32]
    return _mm(name, a_list, bd_list, a_spec, pl.BlockSpec((None, r, c), lambda b, i, k: (b, 0, 0)), dims, (nb, T // tm, 1),
               [_sds((T, nb * ca), dt) for dt in outs], [o_spec] * len(outs), (tm, ca), epilogue, extra, [o_spec] * len(extra))


def mm_blockdiag_grad(name, a, d, nb):
    T = a.shape[0]
    ra, rd = a.shape[1] // nb, d.shape[1] // nb
    tk = _tile(T, TK, 128)
    return _mm(name, [a], [d], pl.BlockSpec((tk, ra), lambda b, n, k: (k, b)), pl.BlockSpec((tk, rd), lambda b, n, k: (k, b)),
               TNd, (nb, 1, T // tk), [_sds((nb, ra, rd), f32)], [pl.BlockSpec((None, ra, rd), lambda b, n, k: (b, 0, 0))], (ra, rd))[0]


def s5_scan_fwd(bu_re, bu_im, ar, ai):
    T, S, N8 = bu_re.shape
    tt = _tile(T, max(8, (1 << 19) // (S * N8)), 8)

    def body(bur_ref, bui_ref, ar_ref, ai_ref, sr_ref, si_ref, st):
        @pl.when(pl.program_id(0) == 0)
        def _():
            st[...] = jnp.zeros_like(st)

        a_r, a_i = ar_ref[...], ai_ref[...]

        def step(t, c):
            sr, si = c
            nr = a_r * sr - a_i * si + bur_ref[t]
            ni = a_r * si + a_i * sr + bui_ref[t]
            sr_ref[t] = nr
            si_ref[t] = ni
            return nr, ni

        sr, si = lax.fori_loop(0, tt, step, (st[0], st[1]))
        st[0] = sr
        st[1] = si

    blk = pl.BlockSpec((tt, S, N8), lambda i: (i, 0, 0))
    par = pl.BlockSpec((S, N8), lambda i: (0, 0))
    return pl.pallas_call(body, name="s5_scan", grid=(T // tt,), in_specs=[blk, blk, par, par], out_specs=[blk, blk],
                          out_shape=[_sds(bu_re.shape, f32)] * 2, scratch_shapes=[pltpu.VMEM((2, S, N8), f32)],
                          compiler_params=_params(("arbitrary",)))(bu_re, bu_im, ar, ai)


def s5_scan_bwd(ds_re, ds_im, s_re, s_im, ar, ai):
    T, S, N8 = ds_re.shape
    tt = _tile(T, max(8, (1 << 18) // (S * N8)), 8)
    nblk = T // tt

    def body(dsr_ref, dsi_ref, sr_ref, si_ref, ar_ref, ai_ref, qr_ref, qi_ref, dar_ref, dai_ref, st):
        @pl.when(pl.program_id(0) == 0)
        def _():
            st[...] = jnp.zeros_like(st)
            dar_ref[...] = jnp.zeros_like(dar_ref)
            dai_ref[...] = jnp.zeros_like(dai_ref)

        a_r, a_i = ar_ref[...], ai_ref[...]

        def step(i, c):
            qr, qi, dar, dai = c
            t = tt - 1 - i
            sr, si = sr_ref[t], si_ref[t]
            dar = dar + qr * sr + qi * si
            dai = dai + qi * sr - qr * si
            nr = dsr_ref[t] + a_r * qr + a_i * qi
            ni = dsi_ref[t] - a_i * qr + a_r * qi
            qr_ref[t] = nr
            qi_ref[t] = ni
            return nr, ni, dar, dai

        qr, qi, dar, dai = lax.fori_loop(0, tt, step, (st[0], st[1], dar_ref[...], dai_ref[...]))
        st[0] = qr
        st[1] = qi
        dar_ref[...] = dar
        dai_ref[...] = dai

    blk = pl.BlockSpec((tt, S, N8), lambda i: (nblk - 1 - i, 0, 0))
    par = pl.BlockSpec((S, N8), lambda i: (0, 0))
    return pl.pallas_call(body, name="s5_scan_bwd", grid=(nblk,), in_specs=[blk, blk, blk, blk, par, par], out_specs=[blk, blk, par, par],
                          out_shape=[_sds(ds_re.shape, f32)] * 2 + [_sds((S, N8), f32)] * 2, scratch_shapes=[pltpu.VMEM((2, S, N8), f32)],
                          compiler_params=_params(("arbitrary",)))(ds_re, ds_im, s_re, s_im, ar, ai)


def _gelu_grad(y):
    c = math.sqrt(2.0 / math.pi)
    th = jnp.tanh(c * (y + 0.044715 * (y * y * y)))
    return 0.5 * (1.0 + th) + 0.5 * y * (1.0 - th * th) * c * (1.0 + 3.0 * 0.044715 * (y * y))


def s5_prepare(lam_re, lam_im, log_dt, b_re, b_im, c_re, c_im):
    G = lam_re.shape[0]
    ldt = log_dt.reshape(G, 1)
    brt, bit = jnp.swapaxes(b_re, 1, 2), jnp.swapaxes(b_im, 1, 2)
    ar, ai, bbr, bbi = disc_fwd(lam_re, lam_im, ldt, brt, bit)
    mats = dict(
        bbd_re=_blockdiag(bbr), bbd_im=_blockdiag(bbi),
        cd_re=_blockdiag(jnp.swapaxes(c_re, 1, 2)), cd_imn=_blockdiag(-jnp.swapaxes(c_im, 1, 2)),
        ar=ar.reshape(8, -1), ai=ai.reshape(8, -1))
    return (lam_re, lam_im, ldt, brt, bit), mats


def s5_fwd(x, gm, w_in4, w_glu4, w_out4, mats, d_skip, b_glu):
    T, D = x.shape
    N = mats["ar"].size
    h = rmsnorm_fwd("s5_norm", x, gm)
    u = mm_rows_plain("s5_in", h, w_in4)[0]
    bu_re = mm_blockdiag("s5_bu_re", [u], [mats["bbd_re"]], NN, GB * S5P)[0]
    bu_im = mm_blockdiag("s5_bu_im", [u], [mats["bbd_im"]], NN, GB * S5P)[0]
    s_re, s_im = s5_scan_fwd(bu_re.reshape(T, 8, N // 8), bu_im.reshape(T, 8, N // 8), mats["ar"], mats["ai"])
    s_re, s_im = s_re.reshape(T, N), s_im.reshape(T, N)

    y2, gl = _s5_y(s_re, s_im, mats, u, d_skip)

    def glu_epi(acc, y2_, b_):
        zg = acc + b_
        return zg, jax.nn.gelu(y2_) * jax.nn.sigmoid(zg)

    Tt, tn = _tile(T, TM, 16), _tile(D, TN, 128)
    zg, o = _mm("s5_glu", [gl], [w_glu4], pl.BlockSpec((Tt, w_glu4.shape[2]), lambda i, n, k: (i, k)),
                pl.BlockSpec((None, None, w_glu4.shape[2], tn), lambda i, n, k: (k, 0, 0, n)), NN, (T // Tt, D // tn, NDEV),
                [_sds((T, D), f32), _sds((T, D), bf16)], [pl.BlockSpec((Tt, tn), lambda i, n, k: (i, n))] * 2, (Tt, tn),
                glu_epi, [y2, b_glu], [pl.BlockSpec((Tt, tn), lambda i, n, k: (i, n)), pl.BlockSpec((1, tn), lambda i, n, k: (0, n))])
    xo = mm_rows_plain("s5_out", o, w_out4, lambda acc, x_: (x_ + acc,), [x])[0]
    return xo, (x, h, u, s_re, s_im, y2, gl, zg, o)


def _s5_y(s_re, s_im, mats, u, d_skip):
    T = s_re.shape[0]
    nb = mats["cd_re"].shape[0]
    tm = _tile(T, TM, 16)
    a_spec, o_spec = _bd_specs(tm, GB * S5P, GB * S5H)
    r, c = mats["cd_re"].shape[1:]

    def y_epi(acc, u_, d_):
        y2 = acc + d_ * u_
        return y2, jax.nn.gelu(y2)

    return _mm("s5_y", [s_re, s_im], [mats["cd_re"], mats["cd_imn"]], a_spec, pl.BlockSpec((None, r, c), lambda b, i, k: (b, 0, 0)), NN,
               (nb, T // tm, 1), [_sds((T, nb * c), f32), _sds((T, nb * c), bf16)], [o_spec] * 2, (tm, c),
               y_epi, [u, d_skip], [o_spec, pl.BlockSpec((1, c), lambda b, i, k: (0, b))])


def s5_bwd(dxo, saved, gm, w_in4, w_glu4, w_out4, mats, d_skip):
    x, h, u, s_re, s_im, y2, gl, zg, o = saved
    T, D = x.shape
    N = mats["ar"].size
    nb = mats["cd_re"].shape[0]

    def do_epi(acc, y2_, zg_):
        sg = jax.nn.sigmoid(zg_)
        return acc * sg, acc * jax.nn.gelu(y2_) * (sg * (1.0 - sg))

    dgl_direct, dzg = mm_t_rows_plain("s5_do", dxo, w_out4, do_epi, [y2, zg], [f32, f32])
    g_wout = mm_grad_rows_plain("s5_gwout", o, dxo)
    g_wglu = mm_grad_rows_plain("s5_gwglu", gl, dzg)
    dy2 = mm_t_rows_plain("s5_dgl", dzg, w_glu4, lambda acc, dd_, y2_: ((acc + dd_) * _gelu_grad(y2_),), [dgl_direct, y2])[0]

    def red_fn(dy2_, u_, dzg_, d_):
        return dy2_ * d_, jnp.sum(dy2_ * u_, axis=0, keepdims=True), jnp.sum(dzg_, axis=0, keepdims=True)

    du_direct, dd, dbglu = _ew("s5_dskip", red_fn, [dy2, u, dzg], [d_skip], [(D, f32)], [D, D])
    ds_re = mm_blockdiag("s5_ds_re", [dy2], [mats["cd_re"]], NT, GB * S5P)[0]
    ds_im = mm_blockdiag("s5_ds_im", [dy2], [mats["cd_imn"]], NT, GB * S5P)[0]
    d_cd_re = mm_blockdiag_grad("s5_gc_re", s_re, dy2, nb)
    d_cd_imn = mm_blockdiag_grad("s5_gc_im", s_im, dy2, nb)
    sh = (T, 8, N // 8)
    q_re, q_im, d_ar, d_ai = s5_scan_bwd(ds_re.reshape(sh), ds_im.reshape(sh), s_re.reshape(sh), s_im.reshape(sh), mats["ar"], mats["ai"])
    q_re, q_im = q_re.reshape(T, N), q_im.reshape(T, N)
    d_bbd_re = mm_blockdiag_grad("s5_gb_re", u, q_re, nb)
    d_bbd_im = mm_blockdiag_grad("s5_gb_im", u, q_im, nb)
    du = mm_blockdiag("s5_du", [q_re, q_im], [mats["bbd_re"], mats["bbd_im"]], NT, GB * S5H, lambda acc, dd_: (acc + dd_,), [du_direct])[0]
    g_win = mm_grad_rows_plain("s5_gwin", h, du)
    dh = mm_t_rows_plain("s5_dh", du, w_in4)[0]
    dx, dgm = rmsnorm_bwd("s5_dnorm", x, gm, dh, dxo)
    small = dict(d_ar=d_ar, d_ai=d_ai, d_bbr=_diag_extract(d_bbd_re, S5H, S5P), d_bbi=_diag_extract(d_bbd_im, S5H, S5P),
                 d_c_re=jnp.swapaxes(_diag_extract(d_cd_re, S5P, S5H), 1, 2), d_c_im=-jnp.swapaxes(_diag_extract(d_cd_imn, S5P, S5H), 1, 2),
                 d_d=dd, d_bglu=dbglu)
    return dx, dgm, g_win, g_wglu, g_wout, small


def qk_prepass(qkv, g_q, g_k, nh):
    _, T, W = qkv.shape
    cpd = W // HEAD
    tm = _tile(T, 1024, 16)

    def body(x_ref, gq_ref, gk_ref, o_ref):
        which = pl.program_id(0) // nh
        x = x_ref[...]
        r = lax.rsqrt(jnp.mean(x * x, axis=-1, keepdims=True) + EPS)
        g = jnp.where(which == 0, gq_ref[...], gk_ref[...])
        o_ref[...] = jnp.where(which == 2, x, (x * r) * g).astype(o_ref.dtype)

    return pl.pallas_call(
        body, name="sb_qknorm", grid=(3 * nh, T // tm),
        in_specs=[pl.BlockSpec((None, tm, HEAD), lambda ch, i: (ch // cpd, i, ch % cpd)),
                  pl.BlockSpec((1, HEAD), lambda ch, i: (0, 0)), pl.BlockSpec((1, HEAD), lambda ch, i: (0, 0))],
        out_specs=pl.BlockSpec((None, tm, HEAD), lambda ch, i: (ch, i, 0)),
        out_shape=_sds((3 * nh, T, HEAD), bf16), compiler_params=_params(("parallel", "parallel")))(qkv, g_q, g_k)


def qk_prepass_bwd(name, qkv, g, dn, off, nh):
    _, T, W = qkv.shape
    cpd = W // HEAD
    tm = _tile(T, 1024, 16)

    def body(x_ref, g_ref, dn_ref, o_ref, dg_ref):
        @pl.when((pl.program_id(0) == 0) & (pl.program_id(1) == 0))
        def _():
            dg_ref[...] = jnp.zeros_like(dg_ref)

        x = x_ref[...]
        dn_ = dn_ref[...]
        r = lax.rsqrt(jnp.mean(x * x, axis=-1, keepdims=True) + EPS)
        xh = x * r
        dxh = dn_ * g_ref[...]
        o_ref[...] = (r * (dxh - xh * jnp.mean(dxh * xh, axis=-1, keepdims=True))).astype(o_ref.dtype)
        dg_ref[...] += jnp.sum(dn_ * xh, axis=0, keepdims=True)

    return pl.pallas_call(
        body, name=name, grid=(nh, T // tm),
        in_specs=[pl.BlockSpec((None, tm, HEAD), lambda hh, i: ((off + hh) // cpd, i, (off + hh) % cpd)),
                  pl.BlockSpec((1, HEAD), lambda hh, i: (0, 0)),
                  pl.BlockSpec((None, tm, HEAD), lambda hh, i: (hh, i, 0))],
        out_specs=[pl.BlockSpec((None, tm, HEAD), lambda hh, i: (hh, i, 0)), pl.BlockSpec((1, HEAD), lambda hh, i: (0, 0))],
        out_shape=[_sds((nh, T, HEAD), bf16), _sds((1, HEAD), f32)], compiler_params=_params(("arbitrary", "arbitrary")))(qkv, g, dn)


def _split_dot(a, tri):
    hi = a.astype(bf16)
    lo = (a - hi.astype(f32)).astype(bf16)
    return lax.dot_general(hi, tri, NN, preferred_element_type=f32) + lax.dot_general(lo, tri, NN, preferred_element_type=f32)


def _sb_logits(q, k, scale):
    z = lax.dot_general(q, k, NT, preferred_element_type=f32) * scale
    sp = jnp.log1p(jnp.exp(-jnp.abs(z)))
    return jnp.minimum(z, 0.0) - sp, -jnp.maximum(z, 0.0) - sp


def attn_fwd(qkvn, nh, tq):
    _, T, _ = qkvn.shape
    nq = T // tq
    scale = 1.0 / math.sqrt(HEAD)

    def body(q_ref, k_ref, v_ref, o_ref):
        qi = pl.program_id(1)
        q = q_ref[...]
        row = lax.broadcasted_iota(jnp.int32, (tq, tq), 0)
        col = lax.broadcasted_iota(jnp.int32, (tq, tq), 1)
        past = col < row
        tri = (row > col).astype(bf16)

        def block(kb, carry, acc, diag):
            ks = pl.ds(pl.multiple_of(kb * tq, tq), tq)
            k, v = k_ref[ks, :], v_ref[ks, :]
            lb, lk = _sb_logits(q, k, scale)
            if diag:
                lk = jnp.where(past, lk, 0.0)
            w = jnp.exp(lb + _split_dot(lk, tri) + carry)
            if diag:
                w = jnp.where(past, w, 0.0)
            acc = acc + lax.dot_general(w.astype(bf16), v, NN, preferred_element_type=f32)
            return carry + jnp.sum(lk, axis=1, keepdims=True), acc

        carry, acc = block(qi, jnp.zeros((tq, 1), f32), jnp.zeros((tq, HEAD), f32), True)
        carry, acc = lax.fori_loop(0, qi, lambda i, c: block(qi - 1 - i, c[0], c[1], False), (carry, acc))
        o_ref[...] = acc

    return pl.pallas_call(
        body, name="sb_attn", grid=(nh, nq),
        in_specs=[pl.BlockSpec((None, tq, HEAD), lambda hh, i: (hh, i, 0)),
                  pl.BlockSpec((None, T, HEAD), lambda hh, i: (nh + hh, 0, 0)),
                  pl.BlockSpec((None, T, HEAD), lambda hh, i: (2 * nh + hh, 0, 0))],
        out_specs=pl.BlockSpec((tq, HEAD), lambda hh, i: (i, hh)),
        out_shape=_sds((T, nh * HEAD), f32), compiler_params=_params(("parallel", "arbitrary")))(qkvn, qkvn, qkvn)


def attn_bwd(qkvn, do, nh, tq):
    _, T, _ = qkvn.shape
    nq = T // tq
    scale = 1.0 / math.sqrt(HEAD)

    def body(q_ref, k_ref, v_ref, do_ref, dq_ref, dk_ref, dv_ref, e_scr, sg_scr):
        qi = pl.program_id(1)

        @pl.when(qi == 0)
        def _():
            dk_ref[...] = jnp.zeros_like(dk_ref)
            dv_ref[...] = jnp.zeros_like(dv_ref)

        q = q_ref[...]
        dob = do_ref[...].astype(bf16)
        row = lax.broadcasted_iota(jnp.int32, (tq, tq), 0)
        col = lax.broadcasted_iota(jnp.int32, (tq, tq), 1)
        past = col < row
        tri_later = (row > col).astype(bf16)
        tri_before = (row < col).astype(bf16)

        def sweep1(kb, carry, diag):
            ks = pl.ds(pl.multiple_of(kb * tq, tq), tq)
            k, v = k_ref[ks, :], v_ref[ks, :]
            lb, lk = _sb_logits(q, k, scale)
            if diag:
                lk = jnp.where(past, lk, 0.0)
            w = jnp.exp(lb + _split_dot(lk, tri_later) + carry)
            if diag:
                w = jnp.where(past, w, 0.0)
            dw = lax.dot_general(dob, v, NT, preferred_element_type=f32)
            e_scr[kb] = dw * w
            sg_scr[kb] = jnp.exp(lb)
            dv_ref[ks, :] += lax.dot_general(w.astype(bf16), dob, TNd, preferred_element_type=f32)
            return carry + jnp.sum(lk, axis=1, keepdims=True)

        carry = sweep1(qi, jnp.zeros((tq, 1), f32), True)
        lax.fori_loop(0, qi, lambda i, c: sweep1(qi - 1 - i, c, False), carry)

        def sweep2(kb, carry, dq, diag):
            ks = pl.ds(pl.multiple_of(kb * tq, tq), tq)
            k = k_ref[ks, :]
            e, sg = e_scr[kb], sg_scr[kb]
            big_e = _split_dot(e, tri_before) + carry
            dz = (e * (1.0 - sg) - big_e * sg) * scale
            if diag:
                dz = jnp.where(past, dz, 0.0)
            dzb = dz.astype(bf16)
            dq = dq + lax.dot_general(dzb, k, NN, preferred_element_type=f32)
            dk_ref[ks, :] += lax.dot_general(dzb, q, TNd, preferred_element_type=f32)
            return carry + jnp.sum(e, axis=1, keepdims=True), dq

        carry, dq = lax.fori_loop(0, qi, lambda i, c: sweep2(i, c[0], c[1], False), (jnp.zeros((tq, 1), f32), jnp.zeros((tq, HEAD), f32)))
        _, dq = sweep2(qi, carry, dq, True)
        dq_ref[...] = dq

    whole = lambda off: pl.BlockSpec((None, T, HEAD), lambda hh, i: (off + hh, 0, 0))
    return pl.pallas_call(
        body, name="sb_attn_bwd", grid=(nh, nq),
        in_specs=[pl.BlockSpec((None, tq, HEAD), lambda hh, i: (hh, i, 0)), whole(nh), whole(2 * nh),
                  pl.BlockSpec((tq, HEAD), lambda hh, i: (i, hh))],
        out_specs=[pl.BlockSpec((None, tq, HEAD), lambda hh, i: (hh, i, 0)), whole(0), whole(0)],
        out_shape=[_sds((nh, T, HEAD), f32)] * 3,
        scratch_shapes=[pltpu.VMEM((nq, tq, tq), f32), pltpu.VMEM((nq, tq, tq), f32)],
        compiler_params=_params(("parallel", "arbitrary")))(qkvn, qkvn, qkvn, do)


def sb_fwd(x, gm, w_qkv4, w_o4, g_q, g_k):
    T, D = x.shape
    nh = D // HEAD
    h = rmsnorm_fwd("sb_norm", x, gm)
    qkv = mm_cols("sb_qkv", h, w_qkv4, 0)[0]
    qkvn = qk_prepass(qkv, g_q, g_k, nh)
    o = attn_fwd(qkvn, nh, _tile(T, 256, 128))
    xo = mm_rows_plain("sb_o", o, w_o4, lambda acc, x_: (x_ + acc,), [x])[0]
    return xo, (x, h, qkv, qkvn, o)


def sb_bwd(dxo, saved, gm, w_qkv4, w_o4, g_q, g_k):
    x, h, qkv, qkvn, o = saved
    T, D = x.shape
    nh = D // HEAD
    nch = 3 * nh
    cpd = w_qkv4.shape[3] // HEAD
    do = mm_t_rows_plain("sb_do", dxo, w_o4)[0]
    g_wo = mm_grad_rows_plain("sb_gwo", o, dxo)
    dqn, dkn, dv = attn_bwd(qkvn, do, nh, _tile(T, 256, 128))
    dq, dgq = qk_prepass_bwd("sb_dq", qkv, g_q, dqn, 0, nh)
    dk, dgk = qk_prepass_bwd("sb_dk", qkv, g_k, dkn, nh, nh)
    dqkv = jnp.concatenate([dq, dk, dv.astype(bf16)], axis=0)
    tm, tn, tk = _tile(T, TM, 16), _tile(D, TN, 128), _tile(T, TK, 128)
    dh = _mm("sb_dh", [dqkv], [w_qkv4], pl.BlockSpec((None, tm, HEAD), lambda i, n, c: (c, i, 0)),
             pl.BlockSpec((None, None, tn, HEAD), lambda i, n, c: (c // cpd, 0, n, c % cpd)), NT, (T // tm, D // tn, nch),
             [_sds((T, D), f32)], [pl.BlockSpec((tm, tn), lambda i, n, c: (i, n))], (tm, tn))[0]
    tmd = _tile(D, TM, 128)
    g_wqkv = _mm("sb_gwqkv", [h], [dqkv], pl.BlockSpec((tk, tmd), lambda c, m, k: (k, m)),
                 pl.BlockSpec((None, tk, HEAD), lambda c, m, k: (c, k, 0)), TNd, (nch, D // tmd, T // tk),
                 [_sds((NDEV, D, cpd * HEAD), bf16)], [pl.BlockSpec((None, tmd, HEAD), lambda c, m, k: (c // cpd, m, c % cpd))], (tmd, HEAD))[0]
    dx, dgm = rmsnorm_bwd("sb_dnorm", x, gm, dh, dxo)
    return dx, dgm, g_wqkv, g_wo, dgq, dgk


def loss_head(y, target):
    D = y.shape[1]

    def fn(y_, t_):
        err = y_ - t_
        return err / D, jnp.sum(err * err, axis=0, keepdims=True)

    dy, sq = _ew("loss_head", fn, [y, target], [], [(D, f32)], [D])
    return 0.5 * (jnp.sum(sq) / D), dy


BIG = ["ffn1_w_gate", "ffn1_w_up", "ffn1_w_down", "s5_w_in", "s5_w_glu", "s5_w_out", "sb_w_qkv", "sb_w_o",
       "ffn2_w_gate", "ffn2_w_up", "ffn2_w_down"]
SMALL = ["norm_ffn1", "norm_mix", "s5_lam_re", "s5_lam_im", "s5_log_dt", "s5_b_re", "s5_b_im", "s5_c_re", "s5_c_im",
         "s5_d", "s5_b_glu", "sb_g_q", "sb_g_k", "norm_ffn2"]
WEIGHTS = ["norm_ffn1", "ffn1_w_gate", "ffn1_w_up", "ffn1_w_down", "norm_mix", "s5_w_in", "s5_lam_re", "s5_lam_im", "s5_log_dt",
           "s5_b_re", "s5_b_im", "s5_c_re", "s5_c_im", "s5_d", "s5_w_glu", "s5_b_glu", "s5_w_out", "sb_w_qkv", "sb_g_q", "sb_g_k",
           "sb_w_o", "norm_ffn2", "ffn2_w_gate", "ffn2_w_up", "ffn2_w_down"]
PACK_LANES = 128
PACK_ROWS = 64


def _pack(arrs):
    flat = jnp.concatenate([a.reshape(-1).astype(f32) for a in arrs])
    pad = (-flat.shape[0]) % (PACK_LANES * PACK_ROWS)
    return jnp.pad(flat, (0, pad)).reshape(-1, PACK_LANES)


def _unpack(buf, like):
    flat = buf.reshape(-1)
    out, off = [], 0
    for a in like:
        out.append(flat[off:off + a.size].reshape(a.shape))
        off += a.size
    return out


def _step(w, m, v, x, target):
    x = x[0]
    target = target[0]
    T, D = x.shape

    sh = {n: w[n].astype(bf16) for n in BIG}
    g1 = all_gather("ag_ffn1", [sh["ffn1_w_gate"], sh["ffn1_w_up"], sh["ffn1_w_down"]])
    gmix = all_gather("ag_mix", [sh["s5_w_in"], sh["s5_w_glu"], sh["s5_w_out"], sh["sb_w_qkv"], sh["sb_w_o"]])
    g2 = all_gather("ag_ffn2", [sh["ffn2_w_gate"], sh["ffn2_w_up"], sh["ffn2_w_down"]])
    W = dict(zip(["ffn1_w_gate", "ffn1_w_up", "ffn1_w_down"], g1))
    W.update(zip(["s5_w_in", "s5_w_glu", "s5_w_out", "sb_w_qkv", "sb_w_o"], gmix))
    W.update(zip(["ffn2_w_gate", "ffn2_w_up", "ffn2_w_down"], g2))

    row = lambda a, l: a[l:l + 1]
    s5_in, mats = s5_prepare(w["s5_lam_re"][0], w["s5_lam_im"][0], w["s5_log_dt"][0], w["s5_b_re"][0], w["s5_b_im"][0],
                             w["s5_c_re"][0], w["s5_c_im"][0])

    x1, sv_f1a = ffn_fwd("l0f1", x, row(w["norm_ffn1"], 0), W["ffn1_w_gate"], W["ffn1_w_up"], W["ffn1_w_down"], 0)
    x2, sv_s5 = s5_fwd(x1, row(w["norm_mix"], 0), W["s5_w_in"], W["s5_w_glu"], W["s5_w_out"], mats, w["s5_d"], w["s5_b_glu"])
    x3, sv_f2a = ffn_fwd("l0f2", x2, row(w["norm_ffn2"], 0), W["ffn2_w_gate"], W["ffn2_w_up"], W["ffn2_w_down"], 0)
    x4, sv_f1b = ffn_fwd("l1f1", x3, row(w["norm_ffn1"], 1), W["ffn1_w_gate"], W["ffn1_w_up"], W["ffn1_w_down"], 1)
    x5, sv_sb = sb_fwd(x4, row(w["norm_mix"], 1), W["sb_w_qkv"], W["sb_w_o"], w["sb_g_q"], w["sb_g_k"])
    x6, sv_f2b = ffn_fwd("l1f2", x5, row(w["norm_ffn2"], 1), W["ffn2_w_gate"], W["ffn2_w_up"], W["ffn2_w_down"], 1)
    loss_local, dy = loss_head(x6, target)
    loss = lax.psum(loss_local, AXES)

    d5, dn_f2b, gg_f2b, gu_f2b, gd_f2b = ffn_bwd("l1f2b", dy, sv_f2b, row(w["norm_ffn2"], 1), W["ffn2_w_gate"], W["ffn2_w_up"], W["ffn2_w_down"], 1)
    d4, dn_sb, g_wqkv, g_wo, dgq, dgk = sb_bwd(d5, sv_sb, row(w["norm_mix"], 1), W["sb_w_qkv"], W["sb_w_o"], w["sb_g_q"], w["sb_g_k"])
    d3, dn_f1b, gg_f1b, gu_f1b, gd_f1b = ffn_bwd("l1f1b", d4, sv_f1b, row(w["norm_ffn1"], 1), W["ffn1_w_gate"], W["ffn1_w_up"], W["ffn1_w_down"], 1)
    d2, dn_f2a, gg_f2a, gu_f2a, gd_f2a = ffn_bwd("l0f2b", d3, sv_f2a, row(w["norm_ffn2"], 0), W["ffn2_w_gate"], W["ffn2_w_up"], W["ffn2_w_down"], 0)
    d1, dn_s5, g_win, g_wglu, g_wout, s5s = s5_bwd(d2, sv_s5, row(w["norm_mix"], 0), W["s5_w_in"], W["s5_w_glu"], W["s5_w_out"], mats, w["s5_d"])
    d0, dn_f1a, gg_f1a, gu_f1a, gd_f1a = ffn_bwd("l0f1b", d1, sv_f1a, row(w["norm_ffn1"], 0), W["ffn1_w_gate"], W["ffn1_w_up"], W["ffn1_w_down"], 0)
    d_lr, d_li, d_ldt, d_brt, d_bit = disc_bwd(*s5_in, s5s["d_ar"].reshape(s5_in[0].shape), s5s["d_ai"].reshape(s5_in[0].shape),
                                               s5s["d_bbr"], s5s["d_bbi"])

    fulls = {
        "ffn1_w_gate": [gg_f1a, gg_f1b], "ffn1_w_up": [gu_f1a, gu_f1b], "ffn1_w_down": [gd_f1a, gd_f1b],
        "ffn2_w_gate": [gg_f2a, gg_f2b], "ffn2_w_up": [gu_f2a, gu_f2b], "ffn2_w_down": [gd_f2a, gd_f2b],
        "s5_w_in": [g_win], "s5_w_glu": [g_wglu], "s5_w_out": [g_wout], "sb_w_qkv": [g_wqkv], "sb_w_o": [g_wo],
    }
    res = {}
    for n in BIG:
        recv = reduce_scatter_recv("rs_" + n, fulls[n])
        C = recv.shape[-1]
        flat = lambda a: a.reshape(-1, C)
        outs = adamw("adamw_" + n, flat(w[n]), flat(m[n]), flat(v[n]), recv.reshape(NDEV, -1, C))
        res[n] = [o.reshape(w[n].shape) for o in outs]

    part = {
        "norm_ffn1": jnp.concatenate([dn_f1a, dn_f1b]), "norm_mix": jnp.concatenate([dn_s5, dn_sb]),
        "norm_ffn2": jnp.concatenate([dn_f2a, dn_f2b]),
        "s5_lam_re": d_lr, "s5_lam_im": d_li, "s5_log_dt": d_ldt, "s5_b_re": jnp.swapaxes(d_brt, 1, 2), "s5_b_im": jnp.swapaxes(d_bit, 1, 2),
        "s5_c_re": s5s["d_c_re"], "s5_c_im": s5s["d_c_im"], "s5_d": s5s["d_d"], "s5_b_glu": s5s["d_bglu"], "sb_g_q": dgq, "sb_g_k": dgk,
    }
    packed = all_gather("ag_small", [_pack([part[n].reshape(w[n].shape) for n in SMALL])])[0]
    outs = adamw("adamw_small", _pack([w[n] for n in SMALL]), _pack([m[n] for n in SMALL]), _pack([v[n] for n in SMALL]), packed)
    un = [_unpack(o, [w[n] for n in SMALL]) for o in outs]
    for i, n in enumerate(SMALL):
        res[n] = [un[k][i] for k in range(4)]

    return (loss, d0[None], *[res[n][0] for n in WEIGHTS], *[res[n][1] for n in WEIGHTS],
            *[res[n][2] for n in WEIGHTS], *[res[n][3] for n in WEIGHTS])


def kernel(x, norm_ffn1, ffn1_w_gate, ffn1_w_up, ffn1_w_down, norm_mix, s5_w_in, s5_lam_re, s5_lam_im, s5_log_dt, s5_b_re, s5_b_im, s5_c_re, s5_c_im, s5_d, s5_w_glu, s5_b_glu, s5_w_out, sb_w_qkv, sb_g_q, sb_g_k, sb_w_o, norm_ffn2, ffn2_w_gate, ffn2_w_up, ffn2_w_down, loss_target, m_norm_ffn1, m_ffn1_w_gate, m_ffn1_w_up, m_ffn1_w_down, m_norm_mix, m_s5_w_in, m_s5_lam_re, m_s5_lam_im, m_s5_log_dt, m_s5_b_re, m_s5_b_im, m_s5_c_re, m_s5_c_im, m_s5_d, m_s5_w_glu, m_s5_b_glu, m_s5_w_out, m_sb_w_qkv, m_sb_g_q, m_sb_g_k, m_sb_w_o, m_norm_ffn2, m_ffn2_w_gate, m_ffn2_w_up, m_ffn2_w_down, v_norm_ffn1, v_ffn1_w_gate, v_ffn1_w_up, v_ffn1_w_down, v_norm_mix, v_s5_w_in, v_s5_lam_re, v_s5_lam_im, v_s5_log_dt, v_s5_b_re, v_s5_b_im, v_s5_c_re, v_s5_c_im, v_s5_d, v_s5_w_glu, v_s5_b_glu, v_s5_w_out, v_sb_w_qkv, v_sb_g_q, v_sb_g_k, v_sb_w_o, v_norm_ffn2, v_ffn2_w_gate, v_ffn2_w_up, v_ffn2_w_down):
    w = dict(norm_ffn1=norm_ffn1, ffn1_w_gate=ffn1_w_gate, ffn1_w_up=ffn1_w_up, ffn1_w_down=ffn1_w_down, norm_mix=norm_mix, s5_w_in=s5_w_in, s5_lam_re=s5_lam_re, s5_lam_im=s5_lam_im, s5_log_dt=s5_log_dt, s5_b_re=s5_b_re, s5_b_im=s5_b_im, s5_c_re=s5_c_re, s5_c_im=s5_c_im, s5_d=s5_d, s5_w_glu=s5_w_glu, s5_b_glu=s5_b_glu, s5_w_out=s5_w_out, sb_w_qkv=sb_w_qkv, sb_g_q=sb_g_q, sb_g_k=sb_g_k, sb_w_o=sb_w_o, norm_ffn2=norm_ffn2, ffn2_w_gate=ffn2_w_gate, ffn2_w_up=ffn2_w_up, ffn2_w_down=ffn2_w_down)
    m = dict(norm_ffn1=m_norm_ffn1, ffn1_w_gate=m_ffn1_w_gate, ffn1_w_up=m_ffn1_w_up, ffn1_w_down=m_ffn1_w_down, norm_mix=m_norm_mix, s5_w_in=m_s5_w_in, s5_lam_re=m_s5_lam_re, s5_lam_im=m_s5_lam_im, s5_log_dt=m_s5_log_dt, s5_b_re=m_s5_b_re, s5_b_im=m_s5_b_im, s5_c_re=m_s5_c_re, s5_c_im=m_s5_c_im, s5_d=m_s5_d, s5_w_glu=m_s5_w_glu, s5_b_glu=m_s5_b_glu, s5_w_out=m_s5_w_out, sb_w_qkv=m_sb_w_qkv, sb_g_q=m_sb_g_q, sb_g_k=m_sb_g_k, sb_w_o=m_sb_w_o, norm_ffn2=m_norm_ffn2, ffn2_w_gate=m_ffn2_w_gate, ffn2_w_up=m_ffn2_w_up, ffn2_w_down=m_ffn2_w_down)
    v = dict(norm_ffn1=v_norm_ffn1, ffn1_w_gate=v_ffn1_w_gate, ffn1_w_up=v_ffn1_w_up, ffn1_w_down=v_ffn1_w_down, norm_mix=v_norm_mix, s5_w_in=v_s5_w_in, s5_lam_re=v_s5_lam_re, s5_lam_im=v_s5_lam_im, s5_log_dt=v_s5_log_dt, s5_b_re=v_s5_b_re, s5_b_im=v_s5_b_im, s5_c_re=v_s5_c_re, s5_c_im=v_s5_c_im, s5_d=v_s5_d, s5_w_glu=v_s5_w_glu, s5_b_glu=v_s5_b_glu, s5_w_out=v_s5_w_out, sb_w_qkv=v_sb_w_qkv, sb_g_q=v_sb_g_q, sb_g_k=v_sb_g_k, sb_w_o=v_sb_w_o, norm_ffn2=v_norm_ffn2, ffn2_w_gate=v_ffn2_w_gate, ffn2_w_up=v_ffn2_w_up, ffn2_w_down=v_ffn2_w_down)
    return _step(w, m, v, x, loss_target)
```

```python
import math

import jax
import jax.numpy as jnp
from jax import lax
from jax.experimental import pallas as pl
from jax.experimental.pallas import tpu as pltpu

f32 = jnp.float32
bf16 = jnp.bfloat16

NDEV = 8
AXES = ("x", "y", "c")
MESH = pl.DeviceIdType.MESH
EPS = 1e-6
HEAD = 128
S5H = 16
S5P = 64
GB = 8
FFN_RES = 0.5
ADAM_LR = 0.001
ADAM_B1 = 0.9
ADAM_B2 = 0.999
ADAM_EPS = 1e-08
ADAM_WD = 0.01
ADAM_STEP = 10
VMEM_LIMIT_V7X = 56 * 2 ** 20

NN = (((1,), (0,)), ((), ()))
NT = (((1,), (1,)), ((), ()))
TNd = (((0,), (0,)), ((), ()))


def _tile(n, target, align):
    if n <= target:
        return n
    t = (target // align) * align
    while t >= align:
        if n % t == 0:
            return t
        t -= align
    return n


def _params(sem):
    return pltpu.CompilerParams(dimension_semantics=sem, vmem_limit_bytes=VMEM_LIMIT_V7X)


def _sds(shape, dtype):
    return jax.ShapeDtypeStruct(tuple(shape), dtype)


def _mm(name, a_list, b_list, a_spec, b_spec, dims, grid, out_shapes, out_specs,
        epilogue=None, extra=(), extra_specs=(), nsub=1, merge_b=False, after=None):
    n_p, n_e, n_o = len(a_list), len(extra), len(out_shapes)
    order = [] if after is None else [after]

    def body(*refs):
        a_refs = refs[:n_p]
        b_refs = refs[n_p:2 * n_p]
        e_refs = refs[2 * n_p:2 * n_p + n_e]
        o_refs = refs[2 * n_p + n_e + len(order):]
        s = None
        for ar, br in zip(a_refs, b_refs):
            if nsub > 1 and not merge_b:
                for j in range(nsub):
                    d = lax.dot_general(ar[j].astype(bf16), br[j].astype(bf16), dims, preferred_element_type=f32)
                    s = d if s is None else s + d
            else:
                b = br[...]
                if merge_b:
                    b = b.reshape(b.shape[0] * b.shape[1], b.shape[2])
                d = lax.dot_general(ar[...].astype(bf16), b.astype(bf16), dims, preferred_element_type=f32)
                s = d if s is None else s + d
        outs = (s,) if epilogue is None else epilogue(s, *[e[...] for e in e_refs])
        for o, val in zip(o_refs, outs):
            o[...] = val.astype(o.dtype)

    return pl.pallas_call(
        body, name=name, grid=grid,
        in_specs=[a_spec] * n_p + [b_spec] * n_p + list(extra_specs) + [pl.BlockSpec(memory_space=pl.ANY)] * len(order),
        out_specs=list(out_specs), out_shape=list(out_shapes),
        compiler_params=_params(("parallel",) * len(grid)),
    )(*a_list, *b_list, *extra, *order)


def _ew(name, fn, rows, bcasts, outs, reds=(), tm=None):
    n_r, n_b, n_o, n_d = len(rows), len(bcasts), len(outs), len(reds)
    R = rows[0].shape[-2]
    if tm is None:
        widest = max([r.shape[-1] * (r.shape[0] if r.ndim == 3 else 1) for r in rows] + [c for c, _ in outs])
        tm = _tile(R, max(16, (1 << 19) // widest), 16)

    def body(*refs):
        r = refs[:n_r]
        b = refs[n_r:n_r + n_b]
        o = refs[n_r + n_b:n_r + n_b + n_o]
        d = refs[n_r + n_b + n_o:]
        res = fn(*[x[...] for x in r], *[x[...] for x in b])
        for oo, val in zip(o, res[:n_o]):
            oo[...] = val.astype(oo.dtype)
        if n_d:
            @pl.when(pl.program_id(0) == 0)
            def _():
                for dd in d:
                    dd[...] = jnp.zeros_like(dd)
            for dd, val in zip(d, res[n_o:]):
                dd[...] += val

    in_specs = []
    for x in rows:
        if x.ndim == 2:
            in_specs.append(pl.BlockSpec((tm, x.shape[1]), lambda i: (i, 0)))
        else:
            in_specs.append(pl.BlockSpec((x.shape[0], tm, x.shape[2]), lambda i: (0, i, 0)))
    for x in bcasts:
        in_specs.append(pl.BlockSpec(x.shape, lambda i, nd=x.ndim: (0,) * nd))
    out_shape = [_sds((R, c), dt) for c, dt in outs] + [_sds((1, c), f32) for c in reds]
    out_specs = [pl.BlockSpec((tm, c), lambda i: (i, 0)) for c, _ in outs] + [pl.BlockSpec((1, c), lambda i: (0, 0)) for c in reds]
    return pl.pallas_call(
        body, name=name, grid=(R // tm,), in_specs=in_specs, out_specs=out_specs, out_shape=out_shape,
        compiler_params=_params(("arbitrary",)),
    )(*rows, *bcasts)


def _coords():
    return lax.axis_index("x"), lax.axis_index("y"), lax.axis_index("c")


def _me():
    x, y, c = _coords()
    return 4 * x + 2 * y + c


def _peer(k):
    x, y, c = _coords()
    return x ^ ((k >> 2) & 1), y ^ ((k >> 1) & 1), c ^ (k & 1)


def all_gather(name, shards):
    n = len(shards)

    def body(*refs):
        x_refs, out_refs, token = refs[:n], refs[n:2 * n], refs[2 * n]
        send_sems, recv_sems, local_sems = refs[2 * n + 1:]
        token[...] = jnp.zeros_like(token)
        x, y, c = _coords()
        me, sibling = (x, y, c), (x, y, 1 - c)
        chips = [(1 - x, y), (x, 1 - y), (1 - x, 1 - y)]

        def rows(a, px, py, pc):
            return out_refs[a].at[4 * px + 2 * py + pc]

        def copy(a, k, block, to, src=None):
            return pltpu.make_async_remote_copy(
                src_ref=rows(a, *block) if src is None else src, dst_ref=rows(a, *block),
                send_sem=send_sems.at[a, k], recv_sem=recv_sems.at[a, k],
                device_id=to, device_id_type=MESH)

        mine = [pltpu.make_async_copy(x_refs[a], rows(a, *me), local_sems.at[a]) for a in range(n)]
        for cp in mine:
            cp.start()
        first = []
        for a in range(n):
            first.append(copy(a, 0, me, sibling, src=x_refs[a]))
            first += [copy(a, 1 + j, me, (*chip, c), src=x_refs[a]) for j, chip in enumerate(chips)]
        for cp in first:
            cp.start()
        passed = []
        for j, chip in enumerate(chips):
            for a in range(n):
                copy(a, 1 + j, (*chip, c), me).wait_recv()
                cp = copy(a, 4 + j, (*chip, c), sibling)
                cp.start()
                passed.append(cp)
        for a in range(n):
            copy(a, 0, sibling, me).wait_recv()
            for j, chip in enumerate(chips):
                copy(a, 4 + j, (*chip, 1 - c), me).wait_recv()
        for cp in first + passed:
            cp.wait_send()
        for cp in mine:
            cp.wait()

    anyspec = pl.BlockSpec(memory_space=pl.ANY)
    outs = pl.pallas_call(
        body, name=name,
        out_shape=[_sds((NDEV,) + s.shape, s.dtype) for s in shards] + [_sds((8, 128), f32)],
        in_specs=[anyspec] * n, out_specs=[anyspec] * n + [pl.BlockSpec(memory_space=pltpu.VMEM)],
        scratch_shapes=[pltpu.SemaphoreType.DMA((n, 7)), pltpu.SemaphoreType.DMA((n, 7)), pltpu.SemaphoreType.DMA((n,))],
    )(*shards)
    return list(outs[:n]), outs[n]


HBM_SPEC = pl.BlockSpec(memory_space=pltpu.HBM)
SEM_SPEC = pl.BlockSpec(memory_space=pltpu.SEMAPHORE)
EFFECT = pltpu.SideEffectType.DATAFLOW_SIDE_EFFECTING


def _own_slot(block):
    land = lax.empty((NDEV,) + block.shape, block.dtype)
    return lax.dynamic_update_slice(land, block[None], (_me(),) + (0,) * block.ndim)


def exchange_start(name, srcs, gather, after=None):
    n = len(srcs)
    order = [] if after is None else [after]
    ns = n * (NDEV - 1)
    me = _me()
    lands = [_own_slot(s if gather else lax.dynamic_index_in_dim(s, me, 0, keepdims=False)) for s in srcs]

    def body(*refs):
        src_refs, land_refs = refs[:n], refs[n:2 * n]
        first = 2 * n + len(order)
        send_sems, recv_sems = refs[first:first + ns], refs[first + ns:first + 2 * ns]
        token = refs[-1]
        me_ = _me()
        for k in range(1, NDEV):
            px, py, pc = _peer(k)
            p = 4 * px + 2 * py + pc
            for a in range(n):
                i = a * (NDEV - 1) + k - 1
                pltpu.make_async_remote_copy(
                    src_ref=src_refs[a] if gather else src_refs[a].at[p], dst_ref=land_refs[a].at[me_],
                    send_sem=send_sems[i], recv_sem=recv_sems[i],
                    device_id=(px, py, pc), device_id_type=MESH).start()
        token[...] = jnp.zeros_like(token)

    outs = pl.pallas_call(
        body, name=name,
        out_shape=(*[pltpu.SemaphoreType.DMA(())] * (2 * ns),
                   *[pltpu.HBM(s.shape, s.dtype) for s in srcs], *[pltpu.HBM(l.shape, l.dtype) for l in lands],
                   _sds((8, 128), f32)),
        in_specs=[HBM_SPEC] * (2 * n) + [pl.BlockSpec(memory_space=pl.ANY)] * len(order),
        out_specs=(*[SEM_SPEC] * (2 * ns), *[HBM_SPEC] * (2 * n), pl.BlockSpec(memory_space=pltpu.VMEM)),
        input_output_aliases={i: 2 * ns + i for i in range(2 * n)},
        compiler_params=pltpu.CompilerParams(has_side_effects=EFFECT),
    )(*[pltpu.with_memory_space_constraint(s, pltpu.HBM) for s in srcs],
      *[pltpu.with_memory_space_constraint(l, pltpu.HBM) for l in lands], *order)
    return dict(n=n, sems=outs[:2 * ns], srcs=outs[2 * ns:2 * ns + n], lands=outs[2 * ns + n:2 * ns + 2 * n], token=outs[-1], gather=gather)


def exchange_wait(name, hd, after):
    n, gather = hd["n"], hd["gather"]
    ns = n * (NDEV - 1)

    def body(*refs):
        src_refs, land_refs = refs[:n], refs[n:2 * n]
        send_sems, recv_sems = refs[2 * n:2 * n + ns], refs[2 * n + ns:2 * n + 2 * ns]
        x, y, c = _coords()
        for k in range(1, NDEV):
            px, py, pc = _peer(k)
            p = 4 * px + 2 * py + pc
            for a in range(n):
                i = a * (NDEV - 1) + k - 1
                cp = pltpu.make_async_remote_copy(
                    src_ref=src_refs[a] if gather else src_refs[a].at[p], dst_ref=land_refs[a].at[p],
                    send_sem=send_sems[i], recv_sem=recv_sems[i],
                    device_id=(x, y, 1 - c), device_id_type=MESH)
                cp.wait_send()
                cp.wait_recv()

    outs = pl.pallas_call(
        body, name=name,
        out_shape=tuple(pltpu.HBM(s.shape, s.dtype) for s in list(hd["srcs"]) + list(hd["lands"])),
        in_specs=[HBM_SPEC] * (2 * n) + [SEM_SPEC] * (2 * ns) + [pl.BlockSpec(memory_space=pl.ANY)],
        out_specs=tuple([HBM_SPEC] * (2 * n)),
        input_output_aliases={i: i for i in range(2 * n)},
        compiler_params=pltpu.CompilerParams(has_side_effects=EFFECT),
    )(*hd["srcs"], *hd["lands"], *hd["sems"], after)
    return list(outs[n:])


def _adam_math(w_, m_, v_, r_):
    g = r_[0].astype(f32)
    for j in range(1, NDEV):
        g = g + r_[j].astype(f32)
    m2 = ADAM_B1 * m_ + (1.0 - ADAM_B1) * g
    v2 = ADAM_B2 * v_ + (1.0 - ADAM_B2) * jnp.square(g)
    m_hat = m2 / (1.0 - ADAM_B1 ** ADAM_STEP)
    v_hat = v2 / (1.0 - ADAM_B2 ** ADAM_STEP)
    delta = -ADAM_LR * (m_hat / (jnp.sqrt(v_hat) + ADAM_EPS) + ADAM_WD * w_)
    return g, delta, m2, v2


def adamw(name, w, m, v, recv):
    return _ew(name, _adam_math, [w, m, v, recv], [], [(w.shape[1], f32)] * 4)


def adamw_layer(name, w3, m3, v3, recv, l, prev):
    L, R, C = w3.shape
    tm = _tile(R, max(16, (1 << 19) // (NDEV * C)), 16)
    n_prev = 0 if prev is None else 4

    def body(*refs):
        w_ref, m_ref, v_ref, r_ref = refs[:4]
        o_refs = refs[4 + n_prev:]
        res = _adam_math(w_ref[...], m_ref[...], v_ref[...], r_ref[...])
        for o, val in zip(o_refs, res):
            o[...] = val

    lay = pl.BlockSpec((None, tm, C), lambda i: (l, i, 0))
    return pl.pallas_call(
        body, name=name, grid=(R // tm,),
        in_specs=[lay, lay, lay, pl.BlockSpec((NDEV, tm, C), lambda i: (0, i, 0))] + [pl.BlockSpec(memory_space=pl.ANY)] * n_prev,
        out_specs=[lay] * 4, out_shape=[_sds((L, R, C), f32)] * 4,
        input_output_aliases={4 + i: i for i in range(n_prev)},
        compiler_params=_params(("parallel",)),
    )(w3, m3, v3, recv, *(prev or []))


def rmsnorm_fwd(name, x, g):
    D = x.shape[1]

    def fn(x_, g_):
        r = lax.rsqrt(jnp.mean(x_ * x_, axis=-1, keepdims=True) + EPS)
        return ((x_ * r) * g_,)

    return _ew(name, fn, [x], [g], [(D, bf16)])[0]


def rmsnorm_bwd(name, x, g, dh, dres):
    D = x.shape[1]

    def fn(x_, dh_, dres_, g_):
        r = lax.rsqrt(jnp.mean(x_ * x_, axis=-1, keepdims=True) + EPS)
        xh = x_ * r
        dxh = dh_ * g_
        dx = dres_ + r * (dxh - xh * jnp.mean(dxh * xh, axis=-1, keepdims=True))
        return dx, jnp.sum(dh_ * xh, axis=0, keepdims=True)

    return _ew(name, fn, [x, dh, dres], [g], [(D, f32)], [D])


def _silu_parts(a):
    sig = jax.nn.sigmoid(a)
    return sig, a * sig


def mm_cols(name, h, w3, epilogue=None, extra=(), outs=None):
    T, K = h.shape
    Nb = w3.shape[2]
    tm = _tile(T, 1024, 16)
    outs = outs or [f32]
    blk = pl.BlockSpec((None, tm, Nb), lambda j, i: (j, i, 0))
    return _mm(name, [h], [w3], pl.BlockSpec((tm, K), lambda j, i: (i, 0)), pl.BlockSpec((None, K, Nb), lambda j, i: (j, 0, 0)),
               NN, (NDEV, T // tm), [_sds((NDEV, T, Nb), dt) for dt in outs], [blk] * len(outs), epilogue, extra, [blk] * len(extra))


def mm_rows(name, p, w3, epilogue=None, extra=(), outs=None):
    _, T, Kb = p.shape
    N = w3.shape[2]
    tm, tn = _tile(T, 512, 16), _tile(N, 512, 128)
    outs = outs or [f32]
    blk = pl.BlockSpec((tm, tn), lambda i, n: (i, n))
    return _mm(name, [p], [w3], pl.BlockSpec((NDEV, tm, Kb), lambda i, n: (0, i, 0)), pl.BlockSpec((NDEV, Kb, tn), lambda i, n: (0, 0, n)),
               NN, (T // tm, N // tn), [_sds((T, N), dt) for dt in outs], [blk] * len(outs), epilogue, extra, [blk] * len(extra), nsub=NDEV)


def mm_rows_plain(name, a, w3, epilogue=None, extra=(), extra_specs=None, outs=None):
    T, K = a.shape
    Kb, N = w3.shape[1], w3.shape[2]
    tm, tn = _tile(T, 512, 16), _tile(N, 1024, 128)
    outs = outs or [f32]
    blk = pl.BlockSpec((tm, tn), lambda i, n: (i, n))
    return _mm(name, [a], [w3], pl.BlockSpec((tm, K), lambda i, n: (i, 0)), pl.BlockSpec((NDEV, Kb, tn), lambda i, n: (0, 0, n)),
               NN, (T // tm, N // tn), [_sds((T, N), dt) for dt in outs], [blk] * len(outs), epilogue, extra,
               extra_specs or [blk] * len(extra), nsub=NDEV, merge_b=True)


def mm_t_rows(name, d, w3, epilogue=None, extra=(), outs=None, after=None):
    T, K = d.shape
    Nb = w3.shape[1]
    tm = _tile(T, 512, 16)
    outs = outs or [f32]
    blk = pl.BlockSpec((None, tm, Nb), lambda j, i: (j, i, 0))
    return _mm(name, [d], [w3], pl.BlockSpec((tm, K), lambda j, i: (i, 0)), pl.BlockSpec((None, Nb, K), lambda j, i: (j, 0, 0)),
               NT, (NDEV, T // tm), [_sds((NDEV, T, Nb), dt) for dt in outs], [blk] * len(outs), epilogue, extra, [blk] * len(extra),
               after=after)


def mm_t_rows_plain(name, d, w3, epilogue=None, extra=(), outs=None, after=None):
    T, K = d.shape
    Nb = w3.shape[1]
    tm = _tile(T, 256, 16)
    outs = outs or [f32]
    blk = pl.BlockSpec((tm, NDEV * Nb), lambda i: (i, 0))
    return _mm(name, [d], [w3], pl.BlockSpec((tm, K), lambda i: (i, 0)), pl.BlockSpec((NDEV, Nb, K), lambda i: (0, 0, 0)),
               NT, (T // tm,), [_sds((T, NDEV * Nb), dt) for dt in outs], [blk] * len(outs), epilogue, extra, [blk] * len(extra),
               nsub=NDEV, merge_b=True, after=after)


def mm_t_cols(name, d_list, w3_list, epilogue=None, extra=(), outs=None):
    _, T, Kb = d_list[0].shape
    N = w3_list[0].shape[1]
    tm, tn = _tile(T, 512, 16), _tile(N, 256, 128)
    outs = outs or [f32]
    blk = pl.BlockSpec((tm, tn), lambda i, n: (i, n))
    return _mm(name, d_list, w3_list, pl.BlockSpec((NDEV, tm, Kb), lambda i, n: (0, i, 0)), pl.BlockSpec((NDEV, tn, Kb), lambda i, n: (0, n, 0)),
               NT, (T // tm, N // tn), [_sds((T, N), dt) for dt in outs], [blk] * len(outs), epilogue, extra, [blk] * len(extra), nsub=NDEV)


def mm_grad_rows(name, p, d, scale=1.0):
    _, T, Mb = p.shape
    N = d.shape[1]
    tn = _tile(N, 512, 128)
    return _mm(name, [p], [d], pl.BlockSpec((None, T, Mb), lambda j, n: (j, 0, 0)), pl.BlockSpec((T, tn), lambda j, n: (0, n)),
               TNd, (NDEV, N // tn), [_sds((NDEV, Mb, N), bf16)], [pl.BlockSpec((None, Mb, tn), lambda j, n: (j, 0, n))],
               (lambda acc: (acc * scale,)))[0]


def mm_grad_rows_plain(name, a, d):
    T, M = a.shape
    N = d.shape[1]
    tm, tn = _tile(M, 512, 128), _tile(N, 512, 128)
    g = _mm(name, [a], [d], pl.BlockSpec((T, tm), lambda m, n: (0, m)), pl.BlockSpec((T, tn), lambda m, n: (0, n)),
            TNd, (M // tm, N // tn), [_sds((M, N), bf16)], [pl.BlockSpec((tm, tn), lambda m, n: (m, n))])[0]
    return g.reshape(NDEV, M // NDEV, N)


def mm_grad_cols(name, h, d):
    T, M = h.shape
    Nb = d.shape[2]
    tm = _tile(M, 512, 128)
    return _mm(name, [h], [d], pl.BlockSpec((T, tm), lambda j, m: (0, m)), pl.BlockSpec((None, T, Nb), lambda j, m: (j, 0, 0)),
               TNd, (NDEV, M // tm), [_sds((NDEV, M, Nb), bf16)], [pl.BlockSpec((None, tm, Nb), lambda j, m: (j, m, 0))])[0]


def ffn_fwd(tag, x, g, wg3, wu3, wd3):
    h = rmsnorm_fwd(tag + "_norm", x, g)
    a = mm_cols(tag + "_gate", h, wg3)[0]

    def up_epi(acc, a_):
        _, sl = _silu_parts(a_)
        return acc, sl * acc

    b, p = mm_cols(tag + "_up", h, wu3, up_epi, [a], [f32, bf16])
    xo = mm_rows(tag + "_down", p, wd3, lambda acc, x_: (x_ + FFN_RES * acc,), [x])[0]
    return xo, (x, h, a, b, p)


def ffn_bwd(tag, dxo, saved, g, wg3, wu3, wd3, after=None):
    x, h, a, b, p = saved

    def dp_epi(acc, a_, b_):
        dp = FFN_RES * acc
        sig, sl = _silu_parts(a_)
        return dp * b_ * (sig * (1.0 + a_ * (1.0 - sig))), dp * sl

    da, db = mm_t_rows(tag + "_dp", dxo, wd3, dp_epi, [a, b], [bf16, bf16], after=after)
    g_wd = mm_grad_rows(tag + "_gwd", p, dxo, FFN_RES)
    g_wg = mm_grad_cols(tag + "_gwg", h, da)
    g_wu = mm_grad_cols(tag + "_gwu", h, db)
    dh = mm_t_cols(tag + "_dh", [da, db], [wg3, wu3])[0]
    dx, dg = rmsnorm_bwd(tag + "_dnorm", x, g, dh, dxo)
    return dx, dg, g_wg, g_wu, g_wd


def _disc(lr, li, ldt, brt, bit):
    dt = jnp.exp(ldt)
    lr = jnp.minimum(lr, -1e-4)
    mag = jnp.exp(lr * dt)
    ab_re = mag * jnp.cos(li * dt)
    ab_im = mag * jnp.sin(li * dt)
    den = lr * lr + li * li
    n_re = ab_re - 1.0
    f_re = (n_re * lr + ab_im * li) / den
    f_im = (ab_im * lr - n_re * li) / den
    bb_re = f_re[:, None, :] * brt - f_im[:, None, :] * bit
    bb_im = f_re[:, None, :] * bit + f_im[:, None, :] * brt
    return ab_re, ab_im, bb_re, bb_im


def disc_fwd(lr, li, ldt, brt, bit):
    def body(lr_ref, li_ref, ldt_ref, brt_ref, bit_ref, ar_ref, ai_ref, bbr_ref, bbi_ref):
        ar, ai, bbr, bbi = _disc(lr_ref[...], li_ref[...], ldt_ref[...], brt_ref[...], bit_ref[...])
        ar_ref[...] = ar
        ai_ref[...] = ai
        bbr_ref[...] = bbr
        bbi_ref[...] = bbi

    return pl.pallas_call(body, name="s5_disc", out_shape=[_sds(lr.shape, f32), _sds(lr.shape, f32), _sds(brt.shape, f32), _sds(brt.shape, f32)],
                          compiler_params=pltpu.CompilerParams(vmem_limit_bytes=VMEM_LIMIT_V7X))(lr, li, ldt, brt, bit)


def disc_bwd(lr, li, ldt, brt, bit, d_ar, d_ai, d_bbr, d_bbi):
    def body(lr_ref, li_ref, ldt_ref, brt_ref, bit_ref, dar_ref, dai_ref, dbbr_ref, dbbi_ref, o_lr, o_li, o_ldt, o_brt, o_bit):
        _, vjp = jax.vjp(_disc, lr_ref[...], li_ref[...], ldt_ref[...], brt_ref[...], bit_ref[...])
        g = vjp((dar_ref[...], dai_ref[...], dbbr_ref[...], dbbi_ref[...]))
        o_lr[...] = g[0]
        o_li[...] = g[1]
        o_ldt[...] = g[2]
        o_brt[...] = g[3]
        o_bit[...] = g[4]

    return pl.pallas_call(body, name="s5_disc_bwd",
                          out_shape=[_sds(lr.shape, f32), _sds(lr.shape, f32), _sds(ldt.shape, f32), _sds(brt.shape, f32), _sds(brt.shape, f32)],
                          compiler_params=pltpu.CompilerParams(vmem_limit_bytes=VMEM_LIMIT_V7X))(lr, li, ldt, brt, bit, d_ar, d_ai, d_bbr, d_bbi)


def _blockdiag(m):
    G, R, C = m.shape
    eye = jnp.eye(GB, dtype=m.dtype)
    m5 = m.reshape(G // GB, GB, R, 1, C) * eye[None, :, None, :, None]
    return m5.reshape(G // GB, GB * R, GB * C)


def _diag_extract(M, R, C):
    nb = M.shape[0]
    eye = jnp.eye(GB, dtype=M.dtype)
    m5 = M.reshape(nb, GB, R, GB, C) * eye[None, :, None, :, None]
    return m5.sum(axis=3).reshape(nb * GB, R, C)


def mm_blockdiag(name, a_list, bd_list, dims, epilogue=None, extra=(), extra_bcast=(), outs=None):
    T = a_list[0].shape[0]
    nb, r, c = bd_list[0].shape
    ra = a_list[0].shape[1] // nb
    ca = c if dims == NN else r
    n_p, n_e, n_b = len(a_list), len(extra), len(extra_bcast)
    outs = outs or [f32]
    tm = _tile(T, 128, 16)

    def body(*refs):
        a_refs, b_refs = refs[:n_p], refs[n_p:2 * n_p]
        e_refs = refs[2 * n_p:2 * n_p + n_e + n_b]
        o_refs = refs[2 * n_p + n_e + n_b:]
        for b in range(nb):
            s = None
            for ar, br in zip(a_refs, b_refs):
                d = lax.dot_general(ar[:, b * ra:(b + 1) * ra].astype(bf16), br[b], dims, preferred_element_type=f32)
                s = d if s is None else s + d
            cols = slice(b * ca, (b + 1) * ca)
            vals = (s,) if epilogue is None else epilogue(s, *[e[:, cols] for e in e_refs])
            for o, val in zip(o_refs, vals):
                o[:, cols] = val.astype(o.dtype)

    row = lambda w: pl.BlockSpec((tm, w), lambda i: (i, 0))
    return pl.pallas_call(
        body, name=name, grid=(T // tm,),
        in_specs=[row(nb * ra)] * n_p + [pl.BlockSpec((nb, r, c), lambda i: (0, 0, 0))] * n_p + [row(nb * ca)] * n_e
        + [pl.BlockSpec((1, nb * ca), lambda i: (0, 0))] * n_b,
        out_specs=[row(nb * ca)] * len(outs), out_shape=[_sds((T, nb * ca), dt) for dt in outs],
        compiler_params=_params(("parallel",)),
    )(*a_list, *bd_list, *extra, *extra_bcast)


def mm_blockdiag_grad(name, a, d, nb):
    T = a.shape[0]
    ra, rd = a.shape[1] // nb, d.shape[1] // nb
    tk = _tile(T, 256, 16)

    def body(a_ref, d_ref, o_ref):
        @pl.when(pl.program_id(0) == 0)
        def _():
            o_ref[...] = jnp.zeros_like(o_ref)

        for b in range(nb):
            o_ref[b] += lax.dot_general(a_ref[:, b * ra:(b + 1) * ra].astype(bf16), d_ref[:, b * rd:(b + 1) * rd].astype(bf16),
                                        TNd, preferred_element_type=f32)

    return pl.pallas_call(
        body, name=name, grid=(T // tk,),
        in_specs=[pl.BlockSpec((tk, nb * ra), lambda k: (k, 0)), pl.BlockSpec((tk, nb * rd), lambda k: (k, 0))],
        out_specs=pl.BlockSpec((nb, ra, rd), lambda k: (0, 0, 0)), out_shape=_sds((nb, ra, rd), f32),
        compiler_params=_params(("arbitrary",)),
    )(a, d)


def s5_scan_fwd(bu_re, bu_im, ar, ai):
    T, S, N8 = bu_re.shape
    tt = _tile(T, max(8, (1 << 19) // (S * N8)), 8)

    def body(bur_ref, bui_ref, ar_ref, ai_ref, sr_ref, si_ref, st):
        @pl.when(pl.program_id(0) == 0)
        def _():
            st[...] = jnp.zeros_like(st)

        a_r, a_i = ar_ref[...], ai_ref[...]

        def step(t, c):
            sr, si = c
            nr = a_r * sr - a_i * si + bur_ref[t]
            ni = a_r * si + a_i * sr + bui_ref[t]
            sr_ref[t] = nr
            si_ref[t] = ni
            return nr, ni

        sr, si = lax.fori_loop(0, tt, step, (st[0], st[1]))
        st[0] = sr
        st[1] = si

    blk = pl.BlockSpec((tt, S, N8), lambda i: (i, 0, 0))
    par = pl.BlockSpec((S, N8), lambda i: (0, 0))
    return pl.pallas_call(body, name="s5_scan", grid=(T // tt,), in_specs=[blk, blk, par, par], out_specs=[blk, blk],
                          out_shape=[_sds(bu_re.shape, f32)] * 2, scratch_shapes=[pltpu.VMEM((2, S, N8), f32)],
                          compiler_params=_params(("arbitrary",)))(bu_re, bu_im, ar, ai)


def s5_scan_bwd(ds_re, ds_im, s_re, s_im, ar, ai):
    T, S, N8 = ds_re.shape
    tt = _tile(T, max(8, (1 << 18) // (S * N8)), 8)
    nblk = T // tt

    def body(dsr_ref, dsi_ref, sr_ref, si_ref, ar_ref, ai_ref, qr_ref, qi_ref, dar_ref, dai_ref, st):
        @pl.when(pl.program_id(0) == 0)
        def _():
            st[...] = jnp.zeros_like(st)
            dar_ref[...] = jnp.zeros_like(dar_ref)
            dai_ref[...] = jnp.zeros_like(dai_ref)

        a_r, a_i = ar_ref[...], ai_ref[...]

        def step(i, c):
            qr, qi, dar, dai = c
            t = tt - 1 - i
            sr, si = sr_ref[t], si_ref[t]
            dar = dar + qr * sr + qi * si
            dai = dai + qi * sr - qr * si
            nr = dsr_ref[t] + a_r * qr + a_i * qi
            ni = dsi_ref[t] - a_i * qr + a_r * qi
            qr_ref[t] = nr
            qi_ref[t] = ni
            return nr, ni, dar, dai

        qr, qi, dar, dai = lax.fori_loop(0, tt, step, (st[0], st[1], dar_ref[...], dai_ref[...]))
        st[0] = qr
        st[1] = qi
        dar_ref[...] = dar
        dai_ref[...] = dai

    blk = pl.BlockSpec((tt, S, N8), lambda i: (nblk - 1 - i, 0, 0))
    par = pl.BlockSpec((S, N8), lambda i: (0, 0))
    return pl.pallas_call(body, name="s5_scan_bwd", grid=(nblk,), in_specs=[blk, blk, blk, blk, par, par], out_specs=[blk, blk, par, par],
                          out_shape=[_sds(ds_re.shape, f32)] * 2 + [_sds((S, N8), f32)] * 2, scratch_shapes=[pltpu.VMEM((2, S, N8), f32)],
                          compiler_params=_params(("arbitrary",)))(ds_re, ds_im, s_re, s_im, ar, ai)


def _gelu_grad(y):
    c = math.sqrt(2.0 / math.pi)
    th = jnp.tanh(c * (y + 0.044715 * (y * y * y)))
    return 0.5 * (1.0 + th) + 0.5 * y * (1.0 - th * th) * c * (1.0 + 3.0 * 0.044715 * (y * y))


def s5_prepare(lam_re, lam_im, log_dt, b_re, b_im, c_re, c_im):
    G = lam_re.shape[0]
    ldt = log_dt.reshape(G, 1)
    brt, bit = jnp.swapaxes(b_re, 1, 2), jnp.swapaxes(b_im, 1, 2)
    ar, ai, bbr, bbi = disc_fwd(lam_re, lam_im, ldt, brt, bit)
    mats = dict(
        bbd_re=_blockdiag(bbr).astype(bf16), bbd_im=_blockdiag(bbi).astype(bf16),
        cd_re=_blockdiag(jnp.swapaxes(c_re, 1, 2)).astype(bf16),
        cd_imn=_blockdiag(-jnp.swapaxes(c_im, 1, 2)).astype(bf16),
        ar=ar.reshape(8, -1), ai=ai.reshape(8, -1))
    return (lam_re, lam_im, ldt, brt, bit), mats


def s5_fwd(x, gm, w_in3, w_glu3, w_out3, mats, d_skip, b_glu):
    T, D = x.shape
    N = mats["ar"].size
    h = rmsnorm_fwd("s5_norm", x, gm)
    u = mm_rows_plain("s5_in", h, w_in3)[0]
    bu_re = mm_blockdiag("s5_bu_re", [u], [mats["bbd_re"]], NN)[0]
    bu_im = mm_blockdiag("s5_bu_im", [u], [mats["bbd_im"]], NN)[0]
    s_re, s_im = s5_scan_fwd(bu_re.reshape(T, 8, N // 8), bu_im.reshape(T, 8, N // 8), mats["ar"], mats["ai"])
    s_re, s_im = s_re.reshape(T, N), s_im.reshape(T, N)

    def y_epi(acc, u_, d_):
        y2 = acc + d_ * u_
        return y2, jax.nn.gelu(y2)

    y2, gl = mm_blockdiag("s5_y", [s_re, s_im], [mats["cd_re"], mats["cd_imn"]], NN, y_epi, [u], [d_skip], [f32, bf16])

    def glu_epi(acc, y2_, b_):
        zg = acc + b_
        return zg, jax.nn.gelu(y2_) * jax.nn.sigmoid(zg)

    tm, tn = _tile(T, 512, 16), _tile(D, 1024, 128)
    zg, o = mm_rows_plain("s5_glu", gl, w_glu3, glu_epi, [y2, b_glu],
                          [pl.BlockSpec((tm, tn), lambda i, n: (i, n)), pl.BlockSpec((1, tn), lambda i, n: (0, n))], [f32, bf16])
    xo = mm_rows_plain("s5_out", o, w_out3, lambda acc, x_: (x_ + acc,), [x])[0]
    return xo, (x, h, u, s_re, s_im, y2, gl, zg, o)


def s5_bwd(dxo, saved, gm, w_in3, w_glu3, w_out3, mats, d_skip, after=None):
    x, h, u, s_re, s_im, y2, gl, zg, o = saved
    T, D = x.shape
    N = mats["ar"].size
    nb = mats["cd_re"].shape[0]

    def do_epi(acc, y2_, zg_):
        sg = jax.nn.sigmoid(zg_)
        return acc * sg, acc * jax.nn.gelu(y2_) * (sg * (1.0 - sg))

    dgl_direct, dzg = mm_t_rows_plain("s5_do", dxo, w_out3, do_epi, [y2, zg], [f32, f32], after=after)
    g_wout = mm_grad_rows_plain("s5_gwout", o, dxo)
    g_wglu = mm_grad_rows_plain("s5_gwglu", gl, dzg)
    dy2 = mm_t_rows_plain("s5_dgl", dzg, w_glu3, lambda acc, dd_, y2_: ((acc + dd_) * _gelu_grad(y2_),), [dgl_direct, y2])[0]

    def red_fn(dy2_, u_, dzg_, d_):
        return dy2_ * d_, jnp.sum(dy2_ * u_, axis=0, keepdims=True), jnp.sum(dzg_, axis=0, keepdims=True)

    du_direct, dd, dbglu = _ew("s5_dskip", red_fn, [dy2, u, dzg], [d_skip], [(D, f32)], [D, D])
    ds_re = mm_blockdiag("s5_ds_re", [dy2], [mats["cd_re"]], NT)[0]
    ds_im = mm_blockdiag("s5_ds_im", [dy2], [mats["cd_imn"]], NT)[0]
    d_cd_re = mm_blockdiag_grad("s5_gc_re", s_re, dy2, nb)
    d_cd_imn = mm_blockdiag_grad("s5_gc_im", s_im, dy2, nb)
    sh = (T, 8, N // 8)
    q_re, q_im, d_ar, d_ai = s5_scan_bwd(ds_re.reshape(sh), ds_im.reshape(sh), s_re.reshape(sh), s_im.reshape(sh), mats["ar"], mats["ai"])
    q_re, q_im = q_re.reshape(T, N), q_im.reshape(T, N)
    d_bbd_re = mm_blockdiag_grad("s5_gb_re", u, q_re, nb)
    d_bbd_im = mm_blockdiag_grad("s5_gb_im", u, q_im, nb)
    du = mm_blockdiag("s5_du", [q_re, q_im], [mats["bbd_re"], mats["bbd_im"]], NT, lambda acc, dd_: (acc + dd_,), [du_direct])[0]
    g_win = mm_grad_rows_plain("s5_gwin", h, du)
    dh = mm_t_rows_plain("s5_dh", du, w_in3)[0]
    dx, dgm = rmsnorm_bwd("s5_dnorm", x, gm, dh, dxo)
    small = dict(d_ar=d_ar, d_ai=d_ai, d_bbr=_diag_extract(d_bbd_re, S5H, S5P), d_bbi=_diag_extract(d_bbd_im, S5H, S5P),
                 d_c_re=jnp.swapaxes(_diag_extract(d_cd_re, S5P, S5H), 1, 2), d_c_im=-jnp.swapaxes(_diag_extract(d_cd_imn, S5P, S5H), 1, 2),
                 d_d=dd, d_bglu=dbglu)
    return dx, dgm, g_win, g_wglu, g_wout, small


def qk_prepass(qkv, g_q, g_k, nh):
    _, T, W = qkv.shape
    cpd = W // HEAD
    tm = _tile(T, 1024, 16)

    def body(x_ref, gq_ref, gk_ref, o_ref):
        which = pl.program_id(0) // nh
        x = x_ref[...]
        r = lax.rsqrt(jnp.mean(x * x, axis=-1, keepdims=True) + EPS)
        g = jnp.where(which == 0, gq_ref[...], gk_ref[...])
        o_ref[...] = jnp.where(which == 2, x, (x * r) * g).astype(o_ref.dtype)

    return pl.pallas_call(
        body, name="sb_qknorm", grid=(3 * nh, T // tm),
        in_specs=[pl.BlockSpec((None, tm, HEAD), lambda ch, i: (ch // cpd, i, ch % cpd)),
                  pl.BlockSpec((1, HEAD), lambda ch, i: (0, 0)), pl.BlockSpec((1, HEAD), lambda ch, i: (0, 0))],
        out_specs=pl.BlockSpec((None, tm, HEAD), lambda ch, i: (ch, i, 0)),
        out_shape=_sds((3 * nh, T, HEAD), bf16), compiler_params=_params(("parallel", "parallel")))(qkv, g_q, g_k)


def qk_prepass_bwd(qkv, g_q, g_k, dqn, dkn, dv, nh):
    _, T, W = qkv.shape
    cpd = W // HEAD
    tm = _tile(T, 1024, 16)

    def body(x_ref, gq_ref, gk_ref, dq_ref, dk_ref, dv_ref, o_ref, dgq_ref, dgk_ref):
        ch = pl.program_id(1)
        which = ch // nh

        @pl.when((pl.program_id(0) == 0) & (ch == 0))
        def _():
            dgq_ref[...] = jnp.zeros_like(dgq_ref)
            dgk_ref[...] = jnp.zeros_like(dgk_ref)

        x = x_ref[...]
        dn = jnp.where(which == 0, dq_ref[...], dk_ref[...])
        g = jnp.where(which == 0, gq_ref[...], gk_ref[...])
        r = lax.rsqrt(jnp.mean(x * x, axis=-1, keepdims=True) + EPS)
        xh = x * r
        dxh = dn * g
        dx = r * (dxh - xh * jnp.mean(dxh * xh, axis=-1, keepdims=True))
        o_ref[...] = jnp.where(which == 2, dv_ref[...], dx).astype(o_ref.dtype)
        dg = jnp.sum(dn * xh, axis=0, keepdims=True)
        dgq_ref[...] += jnp.where(which == 0, dg, 0.0)
        dgk_ref[...] += jnp.where(which == 1, dg, 0.0)

    chunk = pl.BlockSpec((None, tm, HEAD), lambda i, ch: (ch // cpd, i, ch % cpd))
    head = lambda off: pl.BlockSpec((None, tm, HEAD), lambda i, ch: (jnp.clip(ch - off, 0, nh - 1), i, 0))
    vec = pl.BlockSpec((1, HEAD), lambda i, ch: (0, 0))
    return pl.pallas_call(
        body, name="sb_qknorm_bwd", grid=(T // tm, 3 * nh),
        in_specs=[chunk, vec, vec, head(0), head(nh), head(2 * nh)], out_specs=[chunk, vec, vec],
        out_shape=[_sds(qkv.shape, bf16), _sds((1, HEAD), f32), _sds((1, HEAD), f32)],
        compiler_params=_params(("arbitrary", "arbitrary")))(qkv, g_q, g_k, dqn, dkn, dv)


def _split_dot(a, tri):
    hi = a.astype(bf16)
    lo = (a - hi.astype(f32)).astype(bf16)
    return lax.dot_general(hi, tri, NN, preferred_element_type=f32) + lax.dot_general(lo, tri, NN, preferred_element_type=f32)


def _sb_logits(q, k, scale):
    z = lax.dot_general(q, k, NT, preferred_element_type=f32) * scale
    sp = jnp.log1p(jnp.exp(-jnp.abs(z)))
    return jnp.minimum(z, 0.0) - sp, -jnp.maximum(z, 0.0) - sp


def attn_fwd(qkvn, nh, tq):
    _, T, _ = qkvn.shape
    nq = T // tq
    scale = 1.0 / math.sqrt(HEAD)

    def body(q_ref, k_ref, v_ref, o_ref):
        qi = pl.program_id(1)
        q = q_ref[...]
        row = lax.broadcasted_iota(jnp.int32, (tq, tq), 0)
        col = lax.broadcasted_iota(jnp.int32, (tq, tq), 1)
        past = col < row
        tri = (row > col).astype(bf16)

        def block(kb, carry, acc, diag):
            ks = pl.ds(pl.multiple_of(kb * tq, tq), tq)
            k, v = k_ref[ks, :], v_ref[ks, :]
            lb, lk = _sb_logits(q, k, scale)
            if diag:
                lk = jnp.where(past, lk, 0.0)
            w = jnp.exp(lb + _split_dot(lk, tri) + carry)
            if diag:
                w = jnp.where(past, w, 0.0)
            acc = acc + lax.dot_general(w.astype(bf16), v, NN, preferred_element_type=f32)
            return carry + jnp.sum(lk, axis=1, keepdims=True), acc

        carry, acc = block(qi, jnp.zeros((tq, 1), f32), jnp.zeros((tq, HEAD), f32), True)
        carry, acc = lax.fori_loop(0, qi, lambda i, c: block(qi - 1 - i, c[0], c[1], False), (carry, acc))
        o_ref[...] = acc

    return pl.pallas_call(
        body, name="sb_attn", grid=(nh, nq),
        in_specs=[pl.BlockSpec((None, tq, HEAD), lambda hh, i: (hh, i, 0)),
                  pl.BlockSpec((None, T, HEAD), lambda hh, i: (nh + hh, 0, 0)),
                  pl.BlockSpec((None, T, HEAD), lambda hh, i: (2 * nh + hh, 0, 0))],
        out_specs=pl.BlockSpec((tq, HEAD), lambda hh, i: (i, hh)),
        out_shape=_sds((T, nh * HEAD), f32), compiler_params=_params(("parallel", "arbitrary")))(qkvn, qkvn, qkvn)


def attn_bwd(qkvn, do, nh, tq):
    _, T, _ = qkvn.shape
    nq = T // tq
    scale = 1.0 / math.sqrt(HEAD)

    def body(q_ref, k_ref, v_ref, do_ref, dq_ref, dk_ref, dv_ref, e_scr, sg_scr):
        qi = pl.program_id(1)

        @pl.when(qi == 0)
        def _():
            dk_ref[...] = jnp.zeros_like(dk_ref)
            dv_ref[...] = jnp.zeros_like(dv_ref)

        q = q_ref[...]
        dob = do_ref[...].astype(bf16)
        row = lax.broadcasted_iota(jnp.int32, (tq, tq), 0)
        col = lax.broadcasted_iota(jnp.int32, (tq, tq), 1)
        past = col < row
        tri_later = (row > col).astype(bf16)
        tri_before = (row < col).astype(bf16)

        def sweep1(kb, carry, diag):
            ks = pl.ds(pl.multiple_of(kb * tq, tq), tq)
            k, v = k_ref[ks, :], v_ref[ks, :]
            lb, lk = _sb_logits(q, k, scale)
            if diag:
                lk = jnp.where(past, lk, 0.0)
            w = jnp.exp(lb + _split_dot(lk, tri_later) + carry)
            if diag:
                w = jnp.where(past, w, 0.0)
            dw = lax.dot_general(dob, v, NT, preferred_element_type=f32)
            e_scr[kb] = dw * w
            sg_scr[kb] = jnp.exp(lb)
            dv_ref[ks, :] += lax.dot_general(w.astype(bf16), dob, TNd, preferred_element_type=f32)
            return carry + jnp.sum(lk, axis=1, keepdims=True)

        carry = sweep1(qi, jnp.zeros((tq, 1), f32), True)
        lax.fori_loop(0, qi, lambda i, c: sweep1(qi - 1 - i, c, False), carry)

        def sweep2(kb, carry, dq, diag):
            ks = pl.ds(pl.multiple_of(kb * tq, tq), tq)
            k = k_ref[ks, :]
            e, sg = e_scr[kb], sg_scr[kb]
            big_e = _split_dot(e, tri_before) + carry
            dz = (e * (1.0 - sg) - big_e * sg) * scale
            if diag:
                dz = jnp.where(past, dz, 0.0)
            dzb = dz.astype(bf16)
            dq = dq + lax.dot_general(dzb, k, NN, preferred_element_type=f32)
            dk_ref[ks, :] += lax.dot_general(dzb, q, TNd, preferred_element_type=f32)
            return carry + jnp.sum(e, axis=1, keepdims=True), dq

        carry, dq = lax.fori_loop(0, qi, lambda i, c: sweep2(i, c[0], c[1], False), (jnp.zeros((tq, 1), f32), jnp.zeros((tq, HEAD), f32)))
        _, dq = sweep2(qi, carry, dq, True)
        dq_ref[...] = dq

    whole = lambda off: pl.BlockSpec((None, T, HEAD), lambda hh, i: (off + hh, 0, 0))
    return pl.pallas_call(
        body, name="sb_attn_bwd", grid=(nh, nq),
        in_specs=[pl.BlockSpec((None, tq, HEAD), lambda hh, i: (hh, i, 0)), whole(nh), whole(2 * nh),
                  pl.BlockSpec((tq, HEAD), lambda hh, i: (i, hh))],
        out_specs=[pl.BlockSpec((None, tq, HEAD), lambda hh, i: (hh, i, 0)), whole(0), whole(0)],
        out_shape=[_sds((nh, T, HEAD), f32)] * 3,
        scratch_shapes=[pltpu.VMEM((nq, tq, tq), f32), pltpu.VMEM((nq, tq, tq), f32)],
        compiler_params=_params(("parallel", "arbitrary")))(qkvn, qkvn, qkvn, do)


def sb_fwd(x, gm, w_qkv3, w_o3, g_q, g_k):
    T, D = x.shape
    nh = D // HEAD
    h = rmsnorm_fwd("sb_norm", x, gm)
    qkv = mm_cols("sb_qkv", h, w_qkv3)[0]
    qkvn = qk_prepass(qkv, g_q, g_k, nh)
    o = attn_fwd(qkvn, nh, _tile(T, 256, 128))
    xo = mm_rows_plain("sb_o", o, w_o3, lambda acc, x_: (x_ + acc,), [x])[0]
    return xo, (x, h, qkv, qkvn, o)


def sb_bwd(dxo, saved, gm, w_qkv3, w_o3, g_q, g_k, after=None):
    x, h, qkv, qkvn, o = saved
    T, D = x.shape
    nh = D // HEAD
    do = mm_t_rows_plain("sb_do", dxo, w_o3, after=after)[0]
    g_wo = mm_grad_rows_plain("sb_gwo", o, dxo)
    dqn, dkn, dv = attn_bwd(qkvn, do, nh, _tile(T, 256, 128))
    dqkv, dgq, dgk = qk_prepass_bwd(qkv, g_q, g_k, dqn, dkn, dv, nh)
    dh = mm_t_cols("sb_dh", [dqkv], [w_qkv3])[0]
    g_wqkv = mm_grad_cols("sb_gwqkv", h, dqkv)
    dx, dgm = rmsnorm_bwd("sb_dnorm", x, gm, dh, dxo)
    return dx, dgm, g_wqkv, g_wo, dgq, dgk


def loss_head(y, target):
    D = y.shape[1]

    def fn(y_, t_):
        err = y_ - t_
        return err / D, jnp.sum(err * err, axis=0, keepdims=True)

    dy, sq = _ew("loss_head", fn, [y, target], [], [(D, f32)], [D])
    return 0.5 * (jnp.sum(sq) / D), dy


FFN = ["w_gate", "w_up", "w_down"]
SMALL = ["norm_ffn1", "norm_mix", "s5_lam_re", "s5_lam_im", "s5_log_dt", "s5_b_re", "s5_b_im", "s5_c_re", "s5_c_im",
         "s5_d", "s5_b_glu", "sb_g_q", "sb_g_k", "norm_ffn2"]
WEIGHTS = ["norm_ffn1", "ffn1_w_gate", "ffn1_w_up", "ffn1_w_down", "norm_mix", "s5_w_in", "s5_lam_re", "s5_lam_im", "s5_log_dt",
           "s5_b_re", "s5_b_im", "s5_c_re", "s5_c_im", "s5_d", "s5_w_glu", "s5_b_glu", "s5_w_out", "sb_w_qkv", "sb_g_q", "sb_g_k",
           "sb_w_o", "norm_ffn2", "ffn2_w_gate", "ffn2_w_up", "ffn2_w_down"]
PACK_LANES = 128
PACK_ROWS = 64


def _pack(arrs):
    flat = jnp.concatenate([a.reshape(-1).astype(f32) for a in arrs])
    pad = (-flat.shape[0]) % (PACK_LANES * PACK_ROWS)
    return jnp.pad(flat, (0, pad)).reshape(-1, PACK_LANES)


def _unpack(buf, like):
    flat = buf.reshape(-1)
    out, off = [], 0
    for a in like:
        out.append(flat[off:off + a.size].reshape(a.shape))
        off += a.size
    return out


def _after(token, g):
    return g + token[0, 0]


def _step(w, m, v, x, target):
    x = x[0]
    target = target[0]

    def shard(n, l):
        return w[n][l].astype(bf16)

    def gain(n, l):
        return w[n][l:l + 1]

    wf1a, tok = all_gather("ag_l0f1", [shard("ffn1_" + s, 0) for s in FFN])
    h_s5 = exchange_start("ag_s5", [shard("s5_w_in", 0), shard("s5_w_glu", 0), shard("s5_w_out", 0)], True, tok)
    h_f2a = exchange_start("ag_l0f2", [shard("ffn2_" + s, 0) for s in FFN], True, h_s5["token"])
    h_f1b = exchange_start("ag_l1f1", [shard("ffn1_" + s, 1) for s in FFN], True, h_f2a["token"])
    h_sb = exchange_start("ag_sb", [shard("sb_w_qkv", 0), shard("sb_w_o", 0)], True, h_f1b["token"])
    h_f2b = exchange_start("ag_l1f2", [shard("ffn2_" + s, 1) for s in FFN], True, h_sb["token"])
    tok = h_f2b["token"]

    s5_in, mats = s5_prepare(w["s5_lam_re"][0], w["s5_lam_im"][0], w["s5_log_dt"][0], w["s5_b_re"][0], w["s5_b_im"][0],
                             w["s5_c_re"][0], w["s5_c_im"][0])

    x1, sv_f1a = ffn_fwd("l0f1", x, _after(tok, gain("norm_ffn1", 0)), *wf1a)
    ws5 = exchange_wait("agw_s5", h_s5, x1)
    x2, sv_s5 = s5_fwd(x1, gain("norm_mix", 0), *ws5, mats, w["s5_d"], w["s5_b_glu"])
    wf2a = exchange_wait("agw_l0f2", h_f2a, x2)
    x3, sv_f2a = ffn_fwd("l0f2", x2, gain("norm_ffn2", 0), *wf2a)
    wf1b = exchange_wait("agw_l1f1", h_f1b, x3)
    x4, sv_f1b = ffn_fwd("l1f1", x3, gain("norm_ffn1", 1), *wf1b)
    wsb = exchange_wait("agw_sb", h_sb, x4)
    x5, sv_sb = sb_fwd(x4, gain("norm_mix", 1), *wsb, w["sb_g_q"], w["sb_g_k"])
    wf2b = exchange_wait("agw_l1f2", h_f2b, x5)
    x6, sv_f2b = ffn_fwd("l1f2", x5, gain("norm_ffn2", 1), *wf2b)
    loss_local, dy = loss_head(x6, target)
    loss = lax.psum(loss_local, AXES)

    d5, dn_f2b, *gr = ffn_bwd("l1f2b", dy, sv_f2b, gain("norm_ffn2", 1), *wf2b)
    r_f2b = exchange_start("rs_l1f2", gr, False)
    d4, dn_sb, g_wqkv, g_wo, dgq, dgk = sb_bwd(d5, sv_sb, gain("norm_mix", 1), *wsb, w["sb_g_q"], w["sb_g_k"], r_f2b["token"])
    r_sb = exchange_start("rs_sb", [g_wqkv, g_wo], False)
    d3, dn_f1b, *gr = ffn_bwd("l1f1b", d4, sv_f1b, gain("norm_ffn1", 1), *wf1b, r_sb["token"])
    r_f1b = exchange_start("rs_l1f1", gr, False)
    d2, dn_f2a, *gr = ffn_bwd("l0f2b", d3, sv_f2a, gain("norm_ffn2", 0), *wf2a, r_f1b["token"])
    r_f2a = exchange_start("rs_l0f2", gr, False)
    d1, dn_s5, g_win, g_wglu, g_wout, s5s = s5_bwd(d2, sv_s5, gain("norm_mix", 0), *ws5, mats, w["s5_d"], r_f2a["token"])
    r_s5 = exchange_start("rs_s5", [g_win, g_wglu, g_wout], False)
    d0, dn_f1a, *gr = ffn_bwd("l0f1b", d1, sv_f1a, gain("norm_ffn1", 0), *wf1a, r_s5["token"])
    r_f1a = exchange_start("rs_l0f1", gr, False)
    d_lr, d_li, d_ldt, d_brt, d_bit = disc_bwd(*s5_in, s5s["d_ar"].reshape(s5_in[0].shape), s5s["d_ai"].reshape(s5_in[0].shape),
                                               s5s["d_bbr"], _after(r_f1a["token"], s5s["d_bbi"]))

    res = {}
    rv_f2b = exchange_wait("rsw_l1f2", r_f2b, d0)
    rv_sb = exchange_wait("rsw_sb", r_sb, rv_f2b[0])
    rv_f1b = exchange_wait("rsw_l1f1", r_f1b, rv_sb[0])
    rv_f2a = exchange_wait("rsw_l0f2", r_f2a, rv_f1b[0])
    rv_s5 = exchange_wait("rsw_s5", r_s5, rv_f2a[0])
    rv_f1a = exchange_wait("rsw_l0f1", r_f1a, rv_s5[0])
    for fam, lay0, lay1 in (("ffn1_", rv_f1a, rv_f1b), ("ffn2_", rv_f2a, rv_f2b)):
        for i, s in enumerate(FFN):
            n = fam + s
            first = adamw_layer("adamw0_" + n, w[n], m[n], v[n], lay0[i], 0, None)
            res[n] = adamw_layer("adamw1_" + n, w[n], m[n], v[n], lay1[i], 1, first)
    for n, recv in (("s5_w_in", rv_s5[0]), ("s5_w_glu", rv_s5[1]), ("s5_w_out", rv_s5[2]), ("sb_w_qkv", rv_sb[0]), ("sb_w_o", rv_sb[1])):
        res[n] = [o[None] for o in adamw("adamw_" + n, w[n][0], m[n][0], v[n][0], recv)]

    part = {
        "norm_ffn1": jnp.concatenate([dn_f1a, dn_f1b]), "norm_mix": jnp.concatenate([dn_s5, dn_sb]),
        "norm_ffn2": jnp.concatenate([dn_f2a, dn_f2b]),
        "s5_lam_re": d_lr, "s5_lam_im": d_li, "s5_log_dt": d_ldt, "s5_b_re": jnp.swapaxes(d_brt, 1, 2), "s5_b_im": jnp.swapaxes(d_bit, 1, 2),
        "s5_c_re": s5s["d_c_re"], "s5_c_im": s5s["d_c_im"], "s5_d": s5s["d_d"], "s5_b_glu": s5s["d_bglu"], "sb_g_q": dgq, "sb_g_k": dgk,
    }
    packed = all_gather("ag_small", [_pack([part[n].reshape(w[n].shape) for n in SMALL])])[0][0]
    outs = adamw("adamw_small", _pack([w[n] for n in SMALL]), _pack([m[n] for n in SMALL]), _pack([v[n] for n in SMALL]), packed)
    un = [_unpack(o, [w[n] for n in SMALL]) for o in outs]
    for i, n in enumerate(SMALL):
        res[n] = [un[k][i] for k in range(4)]

    return (loss, d0[None], *[res[n][0] for n in WEIGHTS], *[res[n][1] for n in WEIGHTS],
            *[res[n][2] for n in WEIGHTS], *[res[n][3] for n in WEIGHTS])


def kernel(x, norm_ffn1, ffn1_w_gate, ffn1_w_up, ffn1_w_down, norm_mix, s5_w_in, s5_lam_re, s5_lam_im, s5_log_dt, s5_b_re, s5_b_im, s5_c_re, s5_c_im, s5_d, s5_w_glu, s5_b_glu, s5_w_out, sb_w_qkv, sb_g_q, sb_g_k, sb_w_o, norm_ffn2, ffn2_w_gate, ffn2_w_up, ffn2_w_down, loss_target, m_norm_ffn1, m_ffn1_w_gate, m_ffn1_w_up, m_ffn1_w_down, m_norm_mix, m_s5_w_in, m_s5_lam_re, m_s5_lam_im, m_s5_log_dt, m_s5_b_re, m_s5_b_im, m_s5_c_re, m_s5_c_im, m_s5_d, m_s5_w_glu, m_s5_b_glu, m_s5_w_out, m_sb_w_qkv, m_sb_g_q, m_sb_g_k, m_sb_w_o, m_norm_ffn2, m_ffn2_w_gate, m_ffn2_w_up, m_ffn2_w_down, v_norm_ffn1, v_ffn1_w_gate, v_ffn1_w_up, v_ffn1_w_down, v_norm_mix, v_s5_w_in, v_s5_lam_re, v_s5_lam_im, v_s5_log_dt, v_s5_b_re, v_s5_b_im, v_s5_c_re, v_s5_c_im, v_s5_d, v_s5_w_glu, v_s5_b_glu, v_s5_w_out, v_sb_w_qkv, v_sb_g_q, v_sb_g_k, v_sb_w_o, v_norm_ffn2, v_ffn2_w_gate, v_ffn2_w_up, v_ffn2_w_down):
    w = dict(norm_ffn1=norm_ffn1, ffn1_w_gate=ffn1_w_gate, ffn1_w_up=ffn1_w_up, ffn1_w_down=ffn1_w_down, norm_mix=norm_mix, s5_w_in=s5_w_in, s5_lam_re=s5_lam_re, s5_lam_im=s5_lam_im, s5_log_dt=s5_log_dt, s5_b_re=s5_b_re, s5_b_im=s5_b_im, s5_c_re=s5_c_re, s5_c_im=s5_c_im, s5_d=s5_d, s5_w_glu=s5_w_glu, s5_b_glu=s5_b_glu, s5_w_out=s5_w_out, sb_w_qkv=sb_w_qkv, sb_g_q=sb_g_q, sb_g_k=sb_g_k, sb_w_o=sb_w_o, norm_ffn2=norm_ffn2, ffn2_w_gate=ffn2_w_gate, ffn2_w_up=ffn2_w_up, ffn2_w_down=ffn2_w_down)
    m = dict(norm_ffn1=m_norm_ffn1, ffn1_w_gate=m_ffn1_w_gate, ffn1_w_up=m_ffn1_w_up, ffn1_w_down=m_ffn1_w_down, norm_mix=m_norm_mix, s5_w_in=m_s5_w_in, s5_lam_re=m_s5_lam_re, s5_lam_im=m_s5_lam_im, s5_log_dt=m_s5_log_dt, s5_b_re=m_s5_b_re, s5_b_im=m_s5_b_im, s5_c_re=m_s5_c_re, s5_c_im=m_s5_c_im, s5_d=m_s5_d, s5_w_glu=m_s5_w_glu, s5_b_glu=m_s5_b_glu, s5_w_out=m_s5_w_out, sb_w_qkv=m_sb_w_qkv, sb_g_q=m_sb_g_q, sb_g_k=m_sb_g_k, sb_w_o=m_sb_w_o, norm_ffn2=m_norm_ffn2, ffn2_w_gate=m_ffn2_w_gate, ffn2_w_up=m_ffn2_w_up, ffn2_w_down=m_ffn2_w_down)
    v = dict(norm_ffn1=v_norm_ffn1, ffn1_w_gate=v_ffn1_w_gate, ffn1_w_up=v_ffn1_w_up, ffn1_w_down=v_ffn1_w_down, norm_mix=v_norm_mix, s5_w_in=v_s5_w_in, s5_lam_re=v_s5_lam_re, s5_lam_im=v_s5_lam_im, s5_log_dt=v_s5_log_dt, s5_b_re=v_s5_b_re, s5_b_im=v_s5_b_im, s5_c_re=v_s5_c_re, s5_c_im=v_s5_c_im, s5_d=v_s5_d, s5_w_glu=v_s5_w_glu, s5_b_glu=v_s5_b_glu, s5_w_out=v_s5_w_out, sb_w_qkv=v_sb_w_qkv, sb_g_q=v_sb_g_q, sb_g_k=v_sb_g_k, sb_w_o=v_sb_w_o, norm_ffn2=v_norm_ffn2, ffn2_w_gate=v_ffn2_w_gate, ffn2_w_up=v_ffn2_w_up, ffn2_w_down=v_ffn2_w_down)
    return _step(w, m, v, x, loss_target)
```

```python
import math

import jax
import jax.numpy as jnp
from jax import lax
from jax.experimental import pallas as pl
from jax.experimental.pallas import tpu as pltpu

f32 = jnp.float32
bf16 = jnp.bfloat16

NDEV = 8
AXES = ("x", "y", "c")
MESH = pl.DeviceIdType.MESH
EPS = 1e-6
HEAD = 128
S5H = 16
S5P = 64
GB = 8
FFN_RES = 0.5
ADAM_LR = 0.001
ADAM_B1 = 0.9
ADAM_B2 = 0.999
ADAM_EPS = 1e-08
ADAM_WD = 0.01
ADAM_STEP = 10
VMEM_LIMIT_V7X = 56 * 2 ** 20

NN = (((1,), (0,)), ((), ()))
NT = (((1,), (1,)), ((), ()))
TNd = (((0,), (0,)), ((), ()))


def _tile(n, target, align):
    if n <= target:
        return n
    t = (target // align) * align
    while t >= align:
        if n % t == 0:
            return t
        t -= align
    return n


def _params(sem):
    return pltpu.CompilerParams(dimension_semantics=sem, vmem_limit_bytes=VMEM_LIMIT_V7X)


def _sds(shape, dtype):
    return jax.ShapeDtypeStruct(tuple(shape), dtype)


def _mm(name, a_list, b_list, a_spec, b_spec, dims, grid, out_shapes, out_specs,
        epilogue=None, extra=(), extra_specs=(), nsub=1, merge_b=False, after=None, chunks=None):
    n_p, n_e, n_o = len(a_list), len(extra), len(out_shapes)
    order = [] if after is None else [after]

    def body(*refs):
        a_refs = refs[:n_p]
        b_refs = refs[n_p:2 * n_p]
        e_refs = refs[2 * n_p:2 * n_p + n_e]
        o_refs = refs[2 * n_p + n_e + len(order):]
        if chunks:
            a_vals = [ar[...].astype(bf16) for ar in a_refs]
            for c0, c1 in chunks:
                s = None
                for av, br in zip(a_vals, b_refs):
                    b = br[:, c0:c1] if dims == NN else br[c0:c1, :]
                    d = lax.dot_general(av, b.astype(bf16), dims, preferred_element_type=f32)
                    s = d if s is None else s + d
                outs = (s,) if epilogue is None else epilogue(s, *[e[:, c0:c1] for e in e_refs])
                for o, val in zip(o_refs, outs):
                    o[:, c0:c1] = val.astype(o.dtype)
            return
        s = None
        for ar, br in zip(a_refs, b_refs):
            if nsub > 1 and not merge_b:
                for j in range(nsub):
                    d = lax.dot_general(ar[j].astype(bf16), br[j].astype(bf16), dims, preferred_element_type=f32)
                    s = d if s is None else s + d
            else:
                b = br[...]
                if merge_b:
                    b = b.reshape(b.shape[0] * b.shape[1], b.shape[2])
                d = lax.dot_general(ar[...].astype(bf16), b.astype(bf16), dims, preferred_element_type=f32)
                s = d if s is None else s + d
        outs = (s,) if epilogue is None else epilogue(s, *[e[...] for e in e_refs])
        for o, val in zip(o_refs, outs):
            o[...] = val.astype(o.dtype)

    return pl.pallas_call(
        body, name=name, grid=grid,
        in_specs=[a_spec] * n_p + [b_spec] * n_p + list(extra_specs) + [pl.BlockSpec(memory_space=pl.ANY)] * len(order),
        out_specs=list(out_specs), out_shape=list(out_shapes),
        compiler_params=_params(("parallel",) * len(grid)),
    )(*a_list, *b_list, *extra, *order)


def _ew(name, fn, rows, bcasts, outs, reds=(), tm=None):
    n_r, n_b, n_o, n_d = len(rows), len(bcasts), len(outs), len(reds)
    R = rows[0].shape[-2]
    if tm is None:
        widest = max([r.shape[-1] * (r.shape[0] if r.ndim == 3 else 1) for r in rows] + [c for c, _ in outs])
        tm = _tile(R, max(16, (1 << 19) // widest), 16)

    def body(*refs):
        r = refs[:n_r]
        b = refs[n_r:n_r + n_b]
        o = refs[n_r + n_b:n_r + n_b + n_o]
        d = refs[n_r + n_b + n_o:]
        res = fn(*[x[...] for x in r], *[x[...] for x in b])
        for oo, val in zip(o, res[:n_o]):
            oo[...] = val.astype(oo.dtype)
        if n_d:
            @pl.when(pl.program_id(0) == 0)
            def _():
                for dd in d:
                    dd[...] = jnp.zeros_like(dd)
            for dd, val in zip(d, res[n_o:]):
                dd[...] += val

    in_specs = []
    for x in rows:
        if x.ndim == 2:
            in_specs.append(pl.BlockSpec((tm, x.shape[1]), lambda i: (i, 0)))
        else:
            in_specs.append(pl.BlockSpec((x.shape[0], tm, x.shape[2]), lambda i: (0, i, 0)))
    for x in bcasts:
        in_specs.append(pl.BlockSpec(x.shape, lambda i, nd=x.ndim: (0,) * nd))
    out_shape = [_sds((R, c), dt) for c, dt in outs] + [_sds((1, c), f32) for c in reds]
    out_specs = [pl.BlockSpec((tm, c), lambda i: (i, 0)) for c, _ in outs] + [pl.BlockSpec((1, c), lambda i: (0, 0)) for c in reds]
    return pl.pallas_call(
        body, name=name, grid=(R // tm,), in_specs=in_specs, out_specs=out_specs, out_shape=out_shape,
        compiler_params=_params(("arbitrary",)),
    )(*rows, *bcasts)


def _coords():
    return lax.axis_index("x"), lax.axis_index("y"), lax.axis_index("c")


def _me():
    x, y, c = _coords()
    return 4 * x + 2 * y + c


def _peer(k):
    x, y, c = _coords()
    return x ^ ((k >> 2) & 1), y ^ ((k >> 1) & 1), c ^ (k & 1)


def all_gather(name, shards):
    n = len(shards)

    def body(*refs):
        x_refs, out_refs, token = refs[:n], refs[n:2 * n], refs[2 * n]
        send_sems, recv_sems, local_sems = refs[2 * n + 1:]
        token[...] = jnp.zeros_like(token)
        x, y, c = _coords()
        me, sibling = (x, y, c), (x, y, 1 - c)
        chips = [(1 - x, y), (x, 1 - y), (1 - x, 1 - y)]

        def rows(a, px, py, pc):
            return out_refs[a].at[4 * px + 2 * py + pc]

        def copy(a, k, block, to, src=None):
            return pltpu.make_async_remote_copy(
                src_ref=rows(a, *block) if src is None else src, dst_ref=rows(a, *block),
                send_sem=send_sems.at[a, k], recv_sem=recv_sems.at[a, k],
                device_id=to, device_id_type=MESH)

        mine = [pltpu.make_async_copy(x_refs[a], rows(a, *me), local_sems.at[a]) for a in range(n)]
        for cp in mine:
            cp.start()
        first = []
        for a in range(n):
            first.append(copy(a, 0, me, sibling, src=x_refs[a]))
            first += [copy(a, 1 + j, me, (*chip, c), src=x_refs[a]) for j, chip in enumerate(chips)]
        for cp in first:
            cp.start()
        passed = []
        for j, chip in enumerate(chips):
            for a in range(n):
                copy(a, 1 + j, (*chip, c), me).wait_recv()
                cp = copy(a, 4 + j, (*chip, c), sibling)
                cp.start()
                passed.append(cp)
        for a in range(n):
            copy(a, 0, sibling, me).wait_recv()
            for j, chip in enumerate(chips):
                copy(a, 4 + j, (*chip, 1 - c), me).wait_recv()
        for cp in first + passed:
            cp.wait_send()
        for cp in mine:
            cp.wait()

    anyspec = pl.BlockSpec(memory_space=pl.ANY)
    outs = pl.pallas_call(
        body, name=name,
        out_shape=[_sds((NDEV,) + s.shape, s.dtype) for s in shards] + [_sds((8, 128), f32)],
        in_specs=[anyspec] * n, out_specs=[anyspec] * n + [pl.BlockSpec(memory_space=pltpu.VMEM)],
        scratch_shapes=[pltpu.SemaphoreType.DMA((n, 7)), pltpu.SemaphoreType.DMA((n, 7)), pltpu.SemaphoreType.DMA((n,))],
    )(*shards)
    return list(outs[:n]), outs[n]


HBM_SPEC = pl.BlockSpec(memory_space=pltpu.HBM)
SEM_SPEC = pl.BlockSpec(memory_space=pltpu.SEMAPHORE)
EFFECT = pltpu.SideEffectType.DATAFLOW_SIDE_EFFECTING


def _own_slot(block):
    land = lax.empty((NDEV,) + block.shape, block.dtype)
    return lax.dynamic_update_slice(land, block[None], (_me(),) + (0,) * block.ndim)


def exchange_start(name, srcs, gather, after=None):
    n = len(srcs)
    order = [] if after is None else [after]
    ns = n * (NDEV - 1)
    me = _me()
    lands = [_own_slot(s if gather else lax.dynamic_index_in_dim(s, me, 0, keepdims=False)) for s in srcs]

    def body(*refs):
        src_refs, land_refs = refs[:n], refs[n:2 * n]
        first = 2 * n + len(order)
        send_sems, recv_sems = refs[first:first + ns], refs[first + ns:first + 2 * ns]
        token = refs[-1]
        me_ = _me()
        for k in range(1, NDEV):
            px, py, pc = _peer(k)
            p = 4 * px + 2 * py + pc
            for a in range(n):
                i = a * (NDEV - 1) + k - 1
                pltpu.make_async_remote_copy(
                    src_ref=src_refs[a] if gather else src_refs[a].at[p], dst_ref=land_refs[a].at[me_],
                    send_sem=send_sems[i], recv_sem=recv_sems[i],
                    device_id=(px, py, pc), device_id_type=MESH).start()
        token[...] = jnp.zeros_like(token)

    outs = pl.pallas_call(
        body, name=name,
        out_shape=(*[pltpu.SemaphoreType.DMA(())] * (2 * ns),
                   *[pltpu.HBM(s.shape, s.dtype) for s in srcs], *[pltpu.HBM(l.shape, l.dtype) for l in lands],
                   _sds((8, 128), f32)),
        in_specs=[HBM_SPEC] * (2 * n) + [pl.BlockSpec(memory_space=pl.ANY)] * len(order),
        out_specs=(*[SEM_SPEC] * (2 * ns), *[HBM_SPEC] * (2 * n), pl.BlockSpec(memory_space=pltpu.VMEM)),
        input_output_aliases={i: 2 * ns + i for i in range(2 * n)},
        compiler_params=pltpu.CompilerParams(has_side_effects=EFFECT),
    )(*[pltpu.with_memory_space_constraint(s, pltpu.HBM) for s in srcs],
      *[pltpu.with_memory_space_constraint(l, pltpu.HBM) for l in lands], *order)
    return dict(n=n, sems=outs[:2 * ns], srcs=outs[2 * ns:2 * ns + n], lands=outs[2 * ns + n:2 * ns + 2 * n], token=outs[-1], gather=gather)


def exchange_wait(name, hd, after):
    n, gather = hd["n"], hd["gather"]
    ns = n * (NDEV - 1)

    def body(*refs):
        src_refs, land_refs = refs[:n], refs[n:2 * n]
        send_sems, recv_sems = refs[2 * n:2 * n + ns], refs[2 * n + ns:2 * n + 2 * ns]
        x, y, c = _coords()
        for k in range(1, NDEV):
            px, py, pc = _peer(k)
            p = 4 * px + 2 * py + pc
            for a in range(n):
                i = a * (NDEV - 1) + k - 1
                cp = pltpu.make_async_remote_copy(
                    src_ref=src_refs[a] if gather else src_refs[a].at[p], dst_ref=land_refs[a].at[p],
                    send_sem=send_sems[i], recv_sem=recv_sems[i],
                    device_id=(x, y, 1 - c), device_id_type=MESH)
                cp.wait_send()
                cp.wait_recv()

    outs = pl.pallas_call(
        body, name=name,
        out_shape=tuple(pltpu.HBM(s.shape, s.dtype) for s in list(hd["srcs"]) + list(hd["lands"])),
        in_specs=[HBM_SPEC] * (2 * n) + [SEM_SPEC] * (2 * ns) + [pl.BlockSpec(memory_space=pl.ANY)],
        out_specs=tuple([HBM_SPEC] * (2 * n)),
        input_output_aliases={i: i for i in range(2 * n)},
        compiler_params=pltpu.CompilerParams(has_side_effects=EFFECT),
    )(*hd["srcs"], *hd["lands"], *hd["sems"], after)
    return list(outs[n:])


def _adam_math(w_, m_, v_, r_):
    g = r_[0].astype(f32)
    for j in range(1, NDEV):
        g = g + r_[j].astype(f32)
    m2 = ADAM_B1 * m_ + (1.0 - ADAM_B1) * g
    v2 = ADAM_B2 * v_ + (1.0 - ADAM_B2) * jnp.square(g)
    m_hat = m2 / (1.0 - ADAM_B1 ** ADAM_STEP)
    v_hat = v2 / (1.0 - ADAM_B2 ** ADAM_STEP)
    delta = -ADAM_LR * (m_hat / (jnp.sqrt(v_hat) + ADAM_EPS) + ADAM_WD * w_)
    return g, delta, m2, v2


def adamw(name, w, m, v, recv):
    return _ew(name, _adam_math, [w, m, v, recv], [], [(w.shape[1], f32)] * 4)


def adamw_layer(name, w3, m3, v3, recv, l, prev):
    L, R, C = w3.shape
    tm = _tile(R, max(16, (1 << 19) // (NDEV * C)), 16)
    n_prev = 0 if prev is None else 4

    def body(*refs):
        w_ref, m_ref, v_ref, r_ref = refs[:4]
        o_refs = refs[4 + n_prev:]
        res = _adam_math(w_ref[...], m_ref[...], v_ref[...], r_ref[...])
        for o, val in zip(o_refs, res):
            o[...] = val

    lay = pl.BlockSpec((None, tm, C), lambda i: (l, i, 0))
    return pl.pallas_call(
        body, name=name, grid=(R // tm,),
        in_specs=[lay, lay, lay, pl.BlockSpec((NDEV, tm, C), lambda i: (0, i, 0))] + [pl.BlockSpec(memory_space=pl.ANY)] * n_prev,
        out_specs=[lay] * 4, out_shape=[_sds((L, R, C), f32)] * 4,
        input_output_aliases={4 + i: i for i in range(n_prev)},
        compiler_params=_params(("parallel",)),
    )(w3, m3, v3, recv, *(prev or []))


def rmsnorm_fwd(name, x, g):
    D = x.shape[1]

    def fn(x_, g_):
        r = lax.rsqrt(jnp.mean(x_ * x_, axis=-1, keepdims=True) + EPS)
        return ((x_ * r) * g_,)

    return _ew(name, fn, [x], [g], [(D, bf16)])[0]


def rmsnorm_bwd(name, x, g, dh, dres):
    D = x.shape[1]

    def fn(x_, dh_, dres_, g_):
        r = lax.rsqrt(jnp.mean(x_ * x_, axis=-1, keepdims=True) + EPS)
        xh = x_ * r
        dxh = dh_ * g_
        dx = dres_ + r * (dxh - xh * jnp.mean(dxh * xh, axis=-1, keepdims=True))
        return dx, jnp.sum(dh_ * xh, axis=0, keepdims=True)

    return _ew(name, fn, [x, dh, dres], [g], [(D, f32)], [D])


def _silu_parts(a):
    sig = jax.nn.sigmoid(a)
    return sig, a * sig


MXU_COLS_V7X = 256


def _chunks(n):
    if n <= MXU_COLS_V7X:
        return None
    return [(c, min(c + MXU_COLS_V7X, n)) for c in range(0, n, MXU_COLS_V7X)]


def mm_cols(name, h, w3, epilogue=None, extra=(), outs=None):
    T, K = h.shape
    Nb = w3.shape[2]
    tm = _tile(T, 1024, 16)
    outs = outs or [f32]
    blk = pl.BlockSpec((None, tm, Nb), lambda j, i: (j, i, 0))
    return _mm(name, [h], [w3], pl.BlockSpec((tm, K), lambda j, i: (i, 0)), pl.BlockSpec((None, K, Nb), lambda j, i: (j, 0, 0)),
               NN, (NDEV, T // tm), [_sds((NDEV, T, Nb), dt) for dt in outs], [blk] * len(outs), epilogue, extra, [blk] * len(extra),
               chunks=_chunks(Nb) if epilogue is not None else None)


def mm_rows(name, p, w3, epilogue=None, extra=(), outs=None):
    _, T, Kb = p.shape
    N = w3.shape[2]
    tm, tn = _tile(T, 512, 16), _tile(N, 512, 128)
    outs = outs or [f32]
    blk = pl.BlockSpec((tm, tn), lambda i, n: (i, n))
    return _mm(name, [p], [w3], pl.BlockSpec((NDEV, tm, Kb), lambda i, n: (0, i, 0)), pl.BlockSpec((NDEV, Kb, tn), lambda i, n: (0, 0, n)),
               NN, (T // tm, N // tn), [_sds((T, N), dt) for dt in outs], [blk] * len(outs), epilogue, extra, [blk] * len(extra), nsub=NDEV)


def mm_rows_plain(name, a, w3, epilogue=None, extra=(), extra_specs=None, outs=None):
    T, K = a.shape
    Kb, N = w3.shape[1], w3.shape[2]
    tm, tn = _tile(T, 512, 16), _tile(N, 1024, 128)
    outs = outs or [f32]
    blk = pl.BlockSpec((tm, tn), lambda i, n: (i, n))
    return _mm(name, [a], [w3], pl.BlockSpec((tm, K), lambda i, n: (i, 0)), pl.BlockSpec((NDEV, Kb, tn), lambda i, n: (0, 0, n)),
               NN, (T // tm, N // tn), [_sds((T, N), dt) for dt in outs], [blk] * len(outs), epilogue, extra,
               extra_specs or [blk] * len(extra), nsub=NDEV, merge_b=True)


def mm_t_rows(name, d, w3, epilogue=None, extra=(), outs=None, after=None):
    T, K = d.shape
    Nb = w3.shape[1]
    tm = _tile(T, 512, 16)
    outs = outs or [f32]
    blk = pl.BlockSpec((None, tm, Nb), lambda j, i: (j, i, 0))
    return _mm(name, [d], [w3], pl.BlockSpec((tm, K), lambda j, i: (i, 0)), pl.BlockSpec((None, Nb, K), lambda j, i: (j, 0, 0)),
               NT, (NDEV, T // tm), [_sds((NDEV, T, Nb), dt) for dt in outs], [blk] * len(outs), epilogue, extra, [blk] * len(extra),
               after=after, chunks=_chunks(Nb) if epilogue is not None else None)


def mm_t_rows_plain(name, d, w3, epilogue=None, extra=(), outs=None, after=None):
    T, K = d.shape
    Nb = w3.shape[1]
    tm = _tile(T, 256, 16)
    outs = outs or [f32]
    blk = pl.BlockSpec((tm, NDEV * Nb), lambda i: (i, 0))
    return _mm(name, [d], [w3], pl.BlockSpec((tm, K), lambda i: (i, 0)), pl.BlockSpec((NDEV, Nb, K), lambda i: (0, 0, 0)),
               NT, (T // tm,), [_sds((T, NDEV * Nb), dt) for dt in outs], [blk] * len(outs), epilogue, extra, [blk] * len(extra),
               nsub=NDEV, merge_b=True, after=after)


def mm_t_cols(name, d_list, w3_list, epilogue=None, extra=(), outs=None, after=None):
    _, T, Kb = d_list[0].shape
    N = w3_list[0].shape[1]
    tm, tn = _tile(T, 512, 16), _tile(N, 256, 128)
    outs = outs or [f32]
    blk = pl.BlockSpec((tm, tn), lambda i, n: (i, n))
    return _mm(name, d_list, w3_list, pl.BlockSpec((NDEV, tm, Kb), lambda i, n: (0, i, 0)), pl.BlockSpec((NDEV, tn, Kb), lambda i, n: (0, n, 0)),
               NT, (T // tm, N // tn), [_sds((T, N), dt) for dt in outs], [blk] * len(outs), epilogue, extra, [blk] * len(extra), nsub=NDEV,
               after=after)


def mm_grad_rows(name, p, d, scale=1.0):
    _, T, Mb = p.shape
    N = d.shape[1]
    tn = _tile(N, 512, 128)
    return _mm(name, [p], [d], pl.BlockSpec((None, T, Mb), lambda j, n: (j, 0, 0)), pl.BlockSpec((T, tn), lambda j, n: (0, n)),
               TNd, (NDEV, N // tn), [_sds((NDEV, Mb, N), bf16)], [pl.BlockSpec((None, Mb, tn), lambda j, n: (j, 0, n))],
               (lambda acc: (acc * scale,)))[0]


def mm_grad_rows_plain(name, a, d):
    T, M = a.shape
    N = d.shape[1]
    tm, tn = _tile(M, 512, 128), _tile(N, 512, 128)
    g = _mm(name, [a], [d], pl.BlockSpec((T, tm), lambda m, n: (0, m)), pl.BlockSpec((T, tn), lambda m, n: (0, n)),
            TNd, (M // tm, N // tn), [_sds((M, N), bf16)], [pl.BlockSpec((tm, tn), lambda m, n: (m, n))])[0]
    return g.reshape(NDEV, M // NDEV, N)


def mm_grad_cols(name, h, d):
    T, M = h.shape
    Nb = d.shape[2]
    tm = _tile(M, 512, 128)
    return _mm(name, [h], [d], pl.BlockSpec((T, tm), lambda j, m: (0, m)), pl.BlockSpec((None, T, Nb), lambda j, m: (j, 0, 0)),
               TNd, (NDEV, M // tm), [_sds((NDEV, M, Nb), bf16)], [pl.BlockSpec((None, tm, Nb), lambda j, m: (j, m, 0))])[0]


def ffn_fwd(tag, x, g, wg3, wu3, wd3):
    h = rmsnorm_fwd(tag + "_norm", x, g)
    a = mm_cols(tag + "_gate", h, wg3)[0]

    def up_epi(acc, a_):
        _, sl = _silu_parts(a_)
        return acc, sl * acc

    b, p = mm_cols(tag + "_up", h, wu3, up_epi, [a], [f32, bf16])
    xo = mm_rows(tag + "_down", p, wd3, lambda acc, x_: (x_ + FFN_RES * acc,), [x])[0]
    return xo, (x, h, a, b, p)


def ffn_bwd(tag, dxo, saved, g, wg3, wu3, wd3, send):
    x, h, a, b, p = saved

    def dp_epi(acc, a_, b_):
        dp = FFN_RES * acc
        sig, sl = _silu_parts(a_)
        return dp * b_ * (sig * (1.0 + a_ * (1.0 - sig))), dp * sl

    da, db = mm_t_rows(tag + "_dp", dxo, wd3, dp_epi, [a, b], [bf16, bf16])
    g_wd = mm_grad_rows(tag + "_gwd", p, dxo, FFN_RES)
    g_wg = mm_grad_cols(tag + "_gwg", h, da)
    g_wu = mm_grad_cols(tag + "_gwu", h, db)
    tok = send([g_wg, g_wu, g_wd])
    dh = mm_t_cols(tag + "_dh", [da, db], [wg3, wu3], after=tok)[0]
    dx, dg = rmsnorm_bwd(tag + "_dnorm", x, g, dh, dxo)
    return dx, dg


def _disc(lr, li, ldt, brt, bit):
    dt = jnp.exp(ldt)
    lr = jnp.minimum(lr, -1e-4)
    mag = jnp.exp(lr * dt)
    ab_re = mag * jnp.cos(li * dt)
    ab_im = mag * jnp.sin(li * dt)
    den = lr * lr + li * li
    n_re = ab_re - 1.0
    f_re = (n_re * lr + ab_im * li) / den
    f_im = (ab_im * lr - n_re * li) / den
    bb_re = f_re[:, None, :] * brt - f_im[:, None, :] * bit
    bb_im = f_re[:, None, :] * bit + f_im[:, None, :] * brt
    return ab_re, ab_im, bb_re, bb_im


def disc_fwd(lr, li, ldt, brt, bit):
    def body(lr_ref, li_ref, ldt_ref, brt_ref, bit_ref, ar_ref, ai_ref, bbr_ref, bbi_ref):
        ar, ai, bbr, bbi = _disc(lr_ref[...], li_ref[...], ldt_ref[...], brt_ref[...], bit_ref[...])
        ar_ref[...] = ar
        ai_ref[...] = ai
        bbr_ref[...] = bbr
        bbi_ref[...] = bbi

    return pl.pallas_call(body, name="s5_disc", out_shape=[_sds(lr.shape, f32), _sds(lr.shape, f32), _sds(brt.shape, f32), _sds(brt.shape, f32)],
                          compiler_params=pltpu.CompilerParams(vmem_limit_bytes=VMEM_LIMIT_V7X))(lr, li, ldt, brt, bit)


def disc_bwd(lr, li, ldt, brt, bit, d_ar, d_ai, d_bbr, d_bbi):
    def body(lr_ref, li_ref, ldt_ref, brt_ref, bit_ref, dar_ref, dai_ref, dbbr_ref, dbbi_ref, o_lr, o_li, o_ldt, o_brt, o_bit):
        _, vjp = jax.vjp(_disc, lr_ref[...], li_ref[...], ldt_ref[...], brt_ref[...], bit_ref[...])
        g = vjp((dar_ref[...], dai_ref[...], dbbr_ref[...], dbbi_ref[...]))
        o_lr[...] = g[0]
        o_li[...] = g[1]
        o_ldt[...] = g[2]
        o_brt[...] = g[3]
        o_bit[...] = g[4]

    return pl.pallas_call(body, name="s5_disc_bwd",
                          out_shape=[_sds(lr.shape, f32), _sds(lr.shape, f32), _sds(ldt.shape, f32), _sds(brt.shape, f32), _sds(brt.shape, f32)],
                          compiler_params=pltpu.CompilerParams(vmem_limit_bytes=VMEM_LIMIT_V7X))(lr, li, ldt, brt, bit, d_ar, d_ai, d_bbr, d_bbi)


def _blockdiag(m):
    G, R, C = m.shape
    eye = jnp.eye(GB, dtype=m.dtype)
    m5 = m.reshape(G // GB, GB, R, 1, C) * eye[None, :, None, :, None]
    return m5.reshape(G // GB, GB * R, GB * C)


def _diag_extract(M, R, C):
    nb = M.shape[0]
    eye = jnp.eye(GB, dtype=M.dtype)
    m5 = M.reshape(nb, GB, R, GB, C) * eye[None, :, None, :, None]
    return m5.sum(axis=3).reshape(nb * GB, R, C)


def mm_blockdiag(name, a_list, bd_list, dims, epilogue=None, extra=(), extra_bcast=(), outs=None):
    T = a_list[0].shape[0]
    nb, r, c = bd_list[0].shape
    ra = a_list[0].shape[1] // nb
    ca = c if dims == NN else r
    n_p, n_e, n_b = len(a_list), len(extra), len(extra_bcast)
    outs = outs or [f32]
    tm = _tile(T, 128, 16)

    def body(*refs):
        a_refs, b_refs = refs[:n_p], refs[n_p:2 * n_p]
        e_refs = refs[2 * n_p:2 * n_p + n_e + n_b]
        o_refs = refs[2 * n_p + n_e + n_b:]
        for b in range(nb):
            s = None
            for ar, br in zip(a_refs, b_refs):
                d = lax.dot_general(ar[:, b * ra:(b + 1) * ra].astype(bf16), br[b], dims, preferred_element_type=f32)
                s = d if s is None else s + d
            cols = slice(b * ca, (b + 1) * ca)
            vals = (s,) if epilogue is None else epilogue(s, *[e[:, cols] for e in e_refs])
            for o, val in zip(o_refs, vals):
                o[:, cols] = val.astype(o.dtype)

    row = lambda w: pl.BlockSpec((tm, w), lambda i: (i, 0))
    return pl.pallas_call(
        body, name=name, grid=(T // tm,),
        in_specs=[row(nb * ra)] * n_p + [pl.BlockSpec((nb, r, c), lambda i: (0, 0, 0))] * n_p + [row(nb * ca)] * n_e
        + [pl.BlockSpec((1, nb * ca), lambda i: (0, 0))] * n_b,
        out_specs=[row(nb * ca)] * len(outs), out_shape=[_sds((T, nb * ca), dt) for dt in outs],
        compiler_params=_params(("parallel",)),
    )(*a_list, *bd_list, *extra, *extra_bcast)


def mm_blockdiag_grad(name, a, d, nb):
    T = a.shape[0]
    ra, rd = a.shape[1] // nb, d.shape[1] // nb
    tk = _tile(T, 256, 16)

    def body(a_ref, d_ref, o_ref):
        @pl.when(pl.program_id(0) == 0)
        def _():
            o_ref[...] = jnp.zeros_like(o_ref)

        for b in range(nb):
            o_ref[b] += lax.dot_general(a_ref[:, b * ra:(b + 1) * ra].astype(bf16), d_ref[:, b * rd:(b + 1) * rd].astype(bf16),
                                        TNd, preferred_element_type=f32)

    return pl.pallas_call(
        body, name=name, grid=(T // tk,),
        in_specs=[pl.BlockSpec((tk, nb * ra), lambda k: (k, 0)), pl.BlockSpec((tk, nb * rd), lambda k: (k, 0))],
        out_specs=pl.BlockSpec((nb, ra, rd), lambda k: (0, 0, 0)), out_shape=_sds((nb, ra, rd), f32),
        compiler_params=_params(("arbitrary",)),
    )(a, d)


def s5_scan_fwd(bu_re, bu_im, ar, ai):
    T, S, N8 = bu_re.shape
    tt = _tile(T, max(8, (1 << 19) // (S * N8)), 8)

    def body(bur_ref, bui_ref, ar_ref, ai_ref, sr_ref, si_ref, st):
        @pl.when(pl.program_id(0) == 0)
        def _():
            st[...] = jnp.zeros_like(st)

        a_r, a_i = ar_ref[...], ai_ref[...]

        def step(t, c):
            sr, si = c
            nr = a_r * sr - a_i * si + bur_ref[t]
            ni = a_r * si + a_i * sr + bui_ref[t]
            sr_ref[t] = nr
            si_ref[t] = ni
            return nr, ni

        sr, si = lax.fori_loop(0, tt, step, (st[0], st[1]))
        st[0] = sr
        st[1] = si

    blk = pl.BlockSpec((tt, S, N8), lambda i: (i, 0, 0))
    par = pl.BlockSpec((S, N8), lambda i: (0, 0))
    return pl.pallas_call(body, name="s5_scan", grid=(T // tt,), in_specs=[blk, blk, par, par], out_specs=[blk, blk],
                          out_shape=[_sds(bu_re.shape, f32)] * 2, scratch_shapes=[pltpu.VMEM((2, S, N8), f32)],
                          compiler_params=_params(("arbitrary",)))(bu_re, bu_im, ar, ai)


def s5_scan_bwd(ds_re, ds_im, s_re, s_im, ar, ai):
    T, S, N8 = ds_re.shape
    tt = _tile(T, max(8, (1 << 18) // (S * N8)), 8)
    nblk = T // tt

    def body(dsr_ref, dsi_ref, sr_ref, si_ref, ar_ref, ai_ref, qr_ref, qi_ref, dar_ref, dai_ref, st):
        @pl.when(pl.program_id(0) == 0)
        def _():
            st[...] = jnp.zeros_like(st)
            dar_ref[...] = jnp.zeros_like(dar_ref)
            dai_ref[...] = jnp.zeros_like(dai_ref)

        a_r, a_i = ar_ref[...], ai_ref[...]

        def step(i, c):
            qr, qi, dar, dai = c
            t = tt - 1 - i
            sr, si = sr_ref[t], si_ref[t]
            dar = dar + qr * sr + qi * si
            dai = dai + qi * sr - qr * si
            nr = dsr_ref[t] + a_r * qr + a_i * qi
            ni = dsi_ref[t] - a_i * qr + a_r * qi
            qr_ref[t] = nr
            qi_ref[t] = ni
            return nr, ni, dar, dai

        qr, qi, dar, dai = lax.fori_loop(0, tt, step, (st[0], st[1], dar_ref[...], dai_ref[...]))
        st[0] = qr
        st[1] = qi
        dar_ref[...] = dar
        dai_ref[...] = dai

    blk = pl.BlockSpec((tt, S, N8), lambda i: (nblk - 1 - i, 0, 0))
    par = pl.BlockSpec((S, N8), lambda i: (0, 0))
    return pl.pallas_call(body, name="s5_scan_bwd", grid=(nblk,), in_specs=[blk, blk, blk, blk, par, par], out_specs=[blk, blk, par, par],
                          out_shape=[_sds(ds_re.shape, f32)] * 2 + [_sds((S, N8), f32)] * 2, scratch_shapes=[pltpu.VMEM((2, S, N8), f32)],
                          compiler_params=_params(("arbitrary",)))(ds_re, ds_im, s_re, s_im, ar, ai)


def _gelu_grad(y):
    c = math.sqrt(2.0 / math.pi)
    th = jnp.tanh(c * (y + 0.044715 * (y * y * y)))
    return 0.5 * (1.0 + th) + 0.5 * y * (1.0 - th * th) * c * (1.0 + 3.0 * 0.044715 * (y * y))


def s5_prepare(lam_re, lam_im, log_dt, b_re, b_im, c_re, c_im):
    G = lam_re.shape[0]
    ldt = log_dt.reshape(G, 1)
    brt, bit = jnp.swapaxes(b_re, 1, 2), jnp.swapaxes(b_im, 1, 2)
    ar, ai, bbr, bbi = disc_fwd(lam_re, lam_im, ldt, brt, bit)
    mats = dict(
        bbd_re=_blockdiag(bbr).astype(bf16), bbd_im=_blockdiag(bbi).astype(bf16),
        cd_re=_blockdiag(jnp.swapaxes(c_re, 1, 2)).astype(bf16),
        cd_imn=_blockdiag(-jnp.swapaxes(c_im, 1, 2)).astype(bf16),
        ar=ar.reshape(8, -1), ai=ai.reshape(8, -1))
    return (lam_re, lam_im, ldt, brt, bit), mats


def s5_fwd(x, gm, w_in3, w_glu3, w_out3, mats, d_skip, b_glu):
    T, D = x.shape
    N = mats["ar"].size
    h = rmsnorm_fwd("s5_norm", x, gm)
    u = mm_rows_plain("s5_in", h, w_in3)[0]
    bu_re = mm_blockdiag("s5_bu_re", [u], [mats["bbd_re"]], NN)[0]
    bu_im = mm_blockdiag("s5_bu_im", [u], [mats["bbd_im"]], NN)[0]
    s_re, s_im = s5_scan_fwd(bu_re.reshape(T, 8, N // 8), bu_im.reshape(T, 8, N // 8), mats["ar"], mats["ai"])
    s_re, s_im = s_re.reshape(T, N), s_im.reshape(T, N)

    def y_epi(acc, u_, d_):
        y2 = acc + d_ * u_
        return y2, jax.nn.gelu(y2)

    y2, gl = mm_blockdiag("s5_y", [s_re, s_im], [mats["cd_re"], mats["cd_imn"]], NN, y_epi, [u], [d_skip], [f32, bf16])

    def glu_epi(acc, y2_, b_):
        zg = acc + b_
        return zg, jax.nn.gelu(y2_) * jax.nn.sigmoid(zg)

    tm, tn = _tile(T, 512, 16), _tile(D, 1024, 128)
    zg, o = mm_rows_plain("s5_glu", gl, w_glu3, glu_epi, [y2, b_glu],
                          [pl.BlockSpec((tm, tn), lambda i, n: (i, n)), pl.BlockSpec((1, tn), lambda i, n: (0, n))], [f32, bf16])
    xo = mm_rows_plain("s5_out", o, w_out3, lambda acc, x_: (x_ + acc,), [x])[0]
    return xo, (x, h, u, s_re, s_im, y2, gl, zg, o)


def s5_bwd(dxo, saved, gm, w_in3, w_glu3, w_out3, mats, d_skip, send):
    x, h, u, s_re, s_im, y2, gl, zg, o = saved
    T, D = x.shape
    N = mats["ar"].size
    nb = mats["cd_re"].shape[0]

    def do_epi(acc, y2_, zg_):
        sg = jax.nn.sigmoid(zg_)
        return acc * sg, acc * jax.nn.gelu(y2_) * (sg * (1.0 - sg))

    dgl_direct, dzg = mm_t_rows_plain("s5_do", dxo, w_out3, do_epi, [y2, zg], [f32, f32])
    g_wout = mm_grad_rows_plain("s5_gwout", o, dxo)
    g_wglu = mm_grad_rows_plain("s5_gwglu", gl, dzg)
    dy2 = mm_t_rows_plain("s5_dgl", dzg, w_glu3, lambda acc, dd_, y2_: ((acc + dd_) * _gelu_grad(y2_),), [dgl_direct, y2])[0]

    def red_fn(dy2_, u_, dzg_, d_):
        return dy2_ * d_, jnp.sum(dy2_ * u_, axis=0, keepdims=True), jnp.sum(dzg_, axis=0, keepdims=True)

    du_direct, dd, dbglu = _ew("s5_dskip", red_fn, [dy2, u, dzg], [d_skip], [(D, f32)], [D, D])
    ds_re = mm_blockdiag("s5_ds_re", [dy2], [mats["cd_re"]], NT)[0]
    ds_im = mm_blockdiag("s5_ds_im", [dy2], [mats["cd_imn"]], NT)[0]
    d_cd_re = mm_blockdiag_grad("s5_gc_re", s_re, dy2, nb)
    d_cd_imn = mm_blockdiag_grad("s5_gc_im", s_im, dy2, nb)
    sh = (T, 8, N // 8)
    q_re, q_im, d_ar, d_ai = s5_scan_bwd(ds_re.reshape(sh), ds_im.reshape(sh), s_re.reshape(sh), s_im.reshape(sh), mats["ar"], mats["ai"])
    q_re, q_im = q_re.reshape(T, N), q_im.reshape(T, N)
    d_bbd_re = mm_blockdiag_grad("s5_gb_re", u, q_re, nb)
    d_bbd_im = mm_blockdiag_grad("s5_gb_im", u, q_im, nb)
    du = mm_blockdiag("s5_du", [q_re, q_im], [mats["bbd_re"], mats["bbd_im"]], NT, lambda acc, dd_: (acc + dd_,), [du_direct])[0]
    g_win = mm_grad_rows_plain("s5_gwin", h, du)
    tok = send([g_win, g_wglu, g_wout])
    dh = mm_t_rows_plain("s5_dh", du, w_in3, after=tok)[0]
    dx, dgm = rmsnorm_bwd("s5_dnorm", x, gm, dh, dxo)
    small = dict(d_ar=d_ar, d_ai=d_ai, d_bbr=_diag_extract(d_bbd_re, S5H, S5P), d_bbi=_diag_extract(d_bbd_im, S5H, S5P),
                 d_c_re=jnp.swapaxes(_diag_extract(d_cd_re, S5P, S5H), 1, 2), d_c_im=-jnp.swapaxes(_diag_extract(d_cd_imn, S5P, S5H), 1, 2),
                 d_d=dd, d_bglu=dbglu)
    return dx, dgm, small


def qk_prepass(qkv, g_q, g_k, nh):
    _, T, W = qkv.shape
    cpd = W // HEAD
    tm = _tile(T, 1024, 16)

    def body(x_ref, gq_ref, gk_ref, o_ref):
        which = pl.program_id(0) // nh
        x = x_ref[...]
        r = lax.rsqrt(jnp.mean(x * x, axis=-1, keepdims=True) + EPS)
        g = jnp.where(which == 0, gq_ref[...], gk_ref[...])
        o_ref[...] = jnp.where(which == 2, x, (x * r) * g).astype(o_ref.dtype)

    return pl.pallas_call(
        body, name="sb_qknorm", grid=(3 * nh, T // tm),
        in_specs=[pl.BlockSpec((None, tm, HEAD), lambda ch, i: (ch // cpd, i, ch % cpd)),
                  pl.BlockSpec((1, HEAD), lambda ch, i: (0, 0)), pl.BlockSpec((1, HEAD), lambda ch, i: (0, 0))],
        out_specs=pl.BlockSpec((None, tm, HEAD), lambda ch, i: (ch, i, 0)),
        out_shape=_sds((3 * nh, T, HEAD), bf16), compiler_params=_params(("parallel", "parallel")))(qkv, g_q, g_k)


def qk_prepass_bwd(qkv, g_q, g_k, dqn, dkn, dv, nh):
    _, T, W = qkv.shape
    cpd = W // HEAD
    tm = _tile(T, 1024, 16)

    def body(x_ref, gq_ref, gk_ref, dq_ref, dk_ref, dv_ref, o_ref, dgq_ref, dgk_ref):
        ch = pl.program_id(1)
        which = ch // nh

        @pl.when((pl.program_id(0) == 0) & (ch == 0))
        def _():
            dgq_ref[...] = jnp.zeros_like(dgq_ref)
            dgk_ref[...] = jnp.zeros_like(dgk_ref)

        x = x_ref[...]
        dn = jnp.where(which == 0, dq_ref[...], dk_ref[...])
        g = jnp.where(which == 0, gq_ref[...], gk_ref[...])
        r = lax.rsqrt(jnp.mean(x * x, axis=-1, keepdims=True) + EPS)
        xh = x * r
        dxh = dn * g
        dx = r * (dxh - xh * jnp.mean(dxh * xh, axis=-1, keepdims=True))
        o_ref[...] = jnp.where(which == 2, dv_ref[...], dx).astype(o_ref.dtype)
        dg = jnp.sum(dn * xh, axis=0, keepdims=True)
        dgq_ref[...] += jnp.where(which == 0, dg, 0.0)
        dgk_ref[...] += jnp.where(which == 1, dg, 0.0)

    chunk = pl.BlockSpec((None, tm, HEAD), lambda i, ch: (ch // cpd, i, ch % cpd))
    head = lambda off: pl.BlockSpec((None, tm, HEAD), lambda i, ch: (jnp.clip(ch - off, 0, nh - 1), i, 0))
    vec = pl.BlockSpec((1, HEAD), lambda i, ch: (0, 0))
    return pl.pallas_call(
        body, name="sb_qknorm_bwd", grid=(T // tm, 3 * nh),
        in_specs=[chunk, vec, vec, head(0), head(nh), head(2 * nh)], out_specs=[chunk, vec, vec],
        out_shape=[_sds(qkv.shape, bf16), _sds((1, HEAD), f32), _sds((1, HEAD), f32)],
        compiler_params=_params(("arbitrary", "arbitrary")))(qkv, g_q, g_k, dqn, dkn, dv)


def _split_dot(a, tri):
    hi = a.astype(bf16)
    lo = (a - hi.astype(f32)).astype(bf16)
    return lax.dot_general(hi, tri, NN, preferred_element_type=f32) + lax.dot_general(lo, tri, NN, preferred_element_type=f32)


LOG2E = 1.4426950408889634


def _sb_logits(q, k, scale):
    t = lax.dot_general(q, k, NT, preferred_element_type=f32) * (scale * LOG2E)
    sp = jnp.log2(1.0 + jnp.exp2(-jnp.abs(t)))
    return jnp.minimum(t, 0.0) - sp, -(jnp.maximum(t, 0.0) + sp)


HP = 2


def attn_fwd(qkvn, nh, tq):
    _, T, _ = qkvn.shape
    nq = T // tq
    scale = 1.0 / math.sqrt(HEAD)

    def body(q_ref, k_ref, v_ref, o_ref):
        qi = pl.program_id(1)
        qs = [q_ref[h] for h in range(HP)]
        row = lax.broadcasted_iota(jnp.int32, (tq, tq), 0)
        col = lax.broadcasted_iota(jnp.int32, (tq, tq), 1)
        past = col < row
        tri = (row > col).astype(bf16)

        def block(kb, carry, acc, diag):
            ks = pl.ds(pl.multiple_of(kb * tq, tq), tq)
            new_c, new_a = [], []
            for h in range(HP):
                k, v = k_ref[h, ks, :], v_ref[h, ks, :]
                lb, lk = _sb_logits(qs[h], k, scale)
                if diag:
                    lk = jnp.where(past, lk, 0.0)
                w = jnp.exp2(lb + _split_dot(lk, tri) + carry[h])
                if diag:
                    w = jnp.where(past, w, 0.0)
                new_a.append(acc[h] + lax.dot_general(w.astype(bf16), v, NN, preferred_element_type=f32))
                new_c.append(carry[h] + jnp.sum(lk, axis=1, keepdims=True))
            return tuple(new_c), tuple(new_a)

        zero_c = tuple(jnp.zeros((tq, 1), f32) for _ in range(HP))
        zero_a = tuple(jnp.zeros((tq, HEAD), f32) for _ in range(HP))
        carry, acc = block(qi, zero_c, zero_a, True)
        carry, acc = lax.fori_loop(0, qi, lambda i, c: block(qi - 1 - i, c[0], c[1], False), (carry, acc))
        for h in range(HP):
            o_ref[:, h * HEAD:(h + 1) * HEAD] = acc[h]

    heads = lambda off: pl.BlockSpec((HP, T, HEAD), lambda hh, i: (off // HP + hh, 0, 0))
    return pl.pallas_call(
        body, name="sb_attn", grid=(nh // HP, nq),
        in_specs=[pl.BlockSpec((HP, tq, HEAD), lambda hh, i: (hh, i, 0)), heads(nh), heads(2 * nh)],
        out_specs=pl.BlockSpec((tq, HP * HEAD), lambda hh, i: (i, hh)),
        out_shape=_sds((T, nh * HEAD), f32), compiler_params=_params(("parallel", "arbitrary")))(qkvn, qkvn, qkvn)


def attn_bwd(qkvn, do, nh, tq):
    _, T, _ = qkvn.shape
    nq = T // tq
    scale = 1.0 / math.sqrt(HEAD)

    def body(q_ref, k_ref, v_ref, do_ref, dq_ref, dk_ref, dv_ref, e_scr, sg_scr):
        qi = pl.program_id(1)

        @pl.when(qi == 0)
        def _():
            dk_ref[...] = jnp.zeros_like(dk_ref)
            dv_ref[...] = jnp.zeros_like(dv_ref)

        qs = [q_ref[h] for h in range(HP)]
        dobs = [do_ref[:, h * HEAD:(h + 1) * HEAD].astype(bf16) for h in range(HP)]
        row = lax.broadcasted_iota(jnp.int32, (tq, tq), 0)
        col = lax.broadcasted_iota(jnp.int32, (tq, tq), 1)
        past = col < row
        tri_later = (row > col).astype(bf16)
        tri_before = (row < col).astype(bf16)

        def sweep1(kb, carry, diag):
            ks = pl.ds(pl.multiple_of(kb * tq, tq), tq)
            new_c = []
            for h in range(HP):
                k, v = k_ref[h, ks, :], v_ref[h, ks, :]
                lb, lk = _sb_logits(qs[h], k, scale)
                if diag:
                    lk = jnp.where(past, lk, 0.0)
                w = jnp.exp2(lb + _split_dot(lk, tri_later) + carry[h])
                if diag:
                    w = jnp.where(past, w, 0.0)
                dw = lax.dot_general(dobs[h], v, NT, preferred_element_type=f32)
                e_scr[h, kb] = dw * w
                sg_scr[h, kb] = jnp.exp2(lb)
                dv_ref[h, ks, :] += lax.dot_general(w.astype(bf16), dobs[h], TNd, preferred_element_type=f32)
                new_c.append(carry[h] + jnp.sum(lk, axis=1, keepdims=True))
            return tuple(new_c)

        zero_c = tuple(jnp.zeros((tq, 1), f32) for _ in range(HP))
        carry = sweep1(qi, zero_c, True)
        lax.fori_loop(0, qi, lambda i, c: sweep1(qi - 1 - i, c, False), carry)

        def sweep2(kb, carry, dq, diag):
            ks = pl.ds(pl.multiple_of(kb * tq, tq), tq)
            new_c, new_q = [], []
            for h in range(HP):
                k = k_ref[h, ks, :]
                e, sg = e_scr[h, kb], sg_scr[h, kb]
                big_e = _split_dot(e, tri_before) + carry[h]
                dz = (e * (1.0 - sg) - big_e * sg) * scale
                if diag:
                    dz = jnp.where(past, dz, 0.0)
                dzb = dz.astype(bf16)
                new_q.append(dq[h] + lax.dot_general(dzb, k, NN, preferred_element_type=f32))
                dk_ref[h, ks, :] += lax.dot_general(dzb, qs[h], TNd, preferred_element_type=f32)
                new_c.append(carry[h] + jnp.sum(e, axis=1, keepdims=True))
            return tuple(new_c), tuple(new_q)

        zero_q = tuple(jnp.zeros((tq, HEAD), f32) for _ in range(HP))
        carry, dq = lax.fori_loop(0, qi, lambda i, c: sweep2(i, c[0], c[1], False), (zero_c, zero_q))
        _, dq = sweep2(qi, carry, dq, True)
        for h in range(HP):
            dq_ref[h] = dq[h]

    heads = lambda off: pl.BlockSpec((HP, T, HEAD), lambda hh, i: (off // HP + hh, 0, 0))
    return pl.pallas_call(
        body, name="sb_attn_bwd", grid=(nh // HP, nq),
        in_specs=[pl.BlockSpec((HP, tq, HEAD), lambda hh, i: (hh, i, 0)), heads(nh), heads(2 * nh),
                  pl.BlockSpec((tq, HP * HEAD), lambda hh, i: (i, hh))],
        out_specs=[pl.BlockSpec((HP, tq, HEAD), lambda hh, i: (hh, i, 0)), heads(0), heads(0)],
        out_shape=[_sds((nh, T, HEAD), f32)] * 3,
        scratch_shapes=[pltpu.VMEM((HP, nq, tq, tq), f32), pltpu.VMEM((HP, nq, tq, tq), f32)],
        compiler_params=_params(("parallel", "arbitrary")))(qkvn, qkvn, qkvn, do)


def sb_fwd(x, gm, w_qkv3, w_o3, g_q, g_k):
    T, D = x.shape
    nh = D // HEAD
    h = rmsnorm_fwd("sb_norm", x, gm)
    qkv = mm_cols("sb_qkv", h, w_qkv3)[0]
    qkvn = qk_prepass(qkv, g_q, g_k, nh)
    o = attn_fwd(qkvn, nh, _tile(T, 256, 128))
    xo = mm_rows_plain("sb_o", o, w_o3, lambda acc, x_: (x_ + acc,), [x])[0]
    return xo, (x, h, qkv, qkvn, o)


def sb_bwd(dxo, saved, gm, w_qkv3, w_o3, g_q, g_k, send):
    x, h, qkv, qkvn, o = saved
    T, D = x.shape
    nh = D // HEAD
    do = mm_t_rows_plain("sb_do", dxo, w_o3)[0]
    g_wo = mm_grad_rows_plain("sb_gwo", o, dxo)
    dqn, dkn, dv = attn_bwd(qkvn, do, nh, _tile(T, 256, 128))
    dqkv, dgq, dgk = qk_prepass_bwd(qkv, g_q, g_k, dqn, dkn, dv, nh)
    g_wqkv = mm_grad_cols("sb_gwqkv", h, dqkv)
    tok = send([g_wqkv, g_wo])
    dh = mm_t_cols("sb_dh", [dqkv], [w_qkv3], after=tok)[0]
    dx, dgm = rmsnorm_bwd("sb_dnorm", x, gm, dh, dxo)
    return dx, dgm, dgq, dgk


def loss_head(y, target):
    D = y.shape[1]

    def fn(y_, t_):
        err = y_ - t_
        return err / D, jnp.sum(err * err, axis=0, keepdims=True)

    dy, sq = _ew("loss_head", fn, [y, target], [], [(D, f32)], [D])
    return 0.5 * (jnp.sum(sq) / D), dy


FFN = ["w_gate", "w_up", "w_down"]
SMALL = ["norm_ffn1", "norm_mix", "s5_lam_re", "s5_lam_im", "s5_log_dt", "s5_b_re", "s5_b_im", "s5_c_re", "s5_c_im",
         "s5_d", "s5_b_glu", "sb_g_q", "sb_g_k", "norm_ffn2"]
WEIGHTS = ["norm_ffn1", "ffn1_w_gate", "ffn1_w_up", "ffn1_w_down", "norm_mix", "s5_w_in", "s5_lam_re", "s5_lam_im", "s5_log_dt",
           "s5_b_re", "s5_b_im", "s5_c_re", "s5_c_im", "s5_d", "s5_w_glu", "s5_b_glu", "s5_w_out", "sb_w_qkv", "sb_g_q", "sb_g_k",
           "sb_w_o", "norm_ffn2", "ffn2_w_gate", "ffn2_w_up", "ffn2_w_down"]
PACK_LANES = 128
PACK_ROWS = 64


def _pack(arrs):
    flat = jnp.concatenate([a.reshape(-1).astype(f32) for a in arrs])
    pad = (-flat.shape[0]) % (PACK_LANES * PACK_ROWS)
    return jnp.pad(flat, (0, pad)).reshape(-1, PACK_LANES)


def _unpack(buf, like):
    flat = buf.reshape(-1)
    out, off = [], 0
    for a in like:
        out.append(flat[off:off + a.size].reshape(a.shape))
        off += a.size
    return out


def _after(token, g):
    return g + token[0, 0]


def _step(w, m, v, x, target):
    x = x[0]
    target = target[0]

    def shard(n, l):
        return w[n][l].astype(bf16)

    def gain(n, l):
        return w[n][l:l + 1]

    wf1a, tok = all_gather("ag_l0f1", [shard("ffn1_" + s, 0) for s in FFN])
    h_s5 = exchange_start("ag_s5", [shard("s5_w_in", 0), shard("s5_w_glu", 0), shard("s5_w_out", 0)], True, tok)
    h_f2a = exchange_start("ag_l0f2", [shard("ffn2_" + s, 0) for s in FFN], True, h_s5["token"])
    h_f1b = exchange_start("ag_l1f1", [shard("ffn1_" + s, 1) for s in FFN], True, h_f2a["token"])
    h_sb = exchange_start("ag_sb", [shard("sb_w_qkv", 0), shard("sb_w_o", 0)], True, h_f1b["token"])
    h_f2b = exchange_start("ag_l1f2", [shard("ffn2_" + s, 1) for s in FFN], True, h_sb["token"])
    tok = h_f2b["token"]

    s5_in, mats = s5_prepare(w["s5_lam_re"][0], w["s5_lam_im"][0], w["s5_log_dt"][0], w["s5_b_re"][0], w["s5_b_im"][0],
                             w["s5_c_re"][0], w["s5_c_im"][0])

    x1, sv_f1a = ffn_fwd("l0f1", x, _after(tok, gain("norm_ffn1", 0)), *wf1a)
    ws5 = exchange_wait("agw_s5", h_s5, x1)
    x2, sv_s5 = s5_fwd(x1, gain("norm_mix", 0), *ws5, mats, w["s5_d"], w["s5_b_glu"])
    wf2a = exchange_wait("agw_l0f2", h_f2a, x2)
    x3, sv_f2a = ffn_fwd("l0f2", x2, gain("norm_ffn2", 0), *wf2a)
    wf1b = exchange_wait("agw_l1f1", h_f1b, x3)
    x4, sv_f1b = ffn_fwd("l1f1", x3, gain("norm_ffn1", 1), *wf1b)
    wsb = exchange_wait("agw_sb", h_sb, x4)
    x5, sv_sb = sb_fwd(x4, gain("norm_mix", 1), *wsb, w["sb_g_q"], w["sb_g_k"])
    wf2b = exchange_wait("agw_l1f2", h_f2b, x5)
    x6, sv_f2b = ffn_fwd("l1f2", x5, gain("norm_ffn2", 1), *wf2b)
    loss_local, dy = loss_head(x6, target)
    loss = lax.psum(loss_local, AXES)

    rs = {}

    def sender(key):
        def send(grads):
            rs[key] = exchange_start("rs_" + key, grads, False)
            return rs[key]["token"]
        return send

    d5, dn_f2b = ffn_bwd("l1f2b", dy, sv_f2b, gain("norm_ffn2", 1), *wf2b, sender("l1f2"))
    d4, dn_sb, dgq, dgk = sb_bwd(d5, sv_sb, gain("norm_mix", 1), *wsb, w["sb_g_q"], w["sb_g_k"], sender("sb"))
    d3, dn_f1b = ffn_bwd("l1f1b", d4, sv_f1b, gain("norm_ffn1", 1), *wf1b, sender("l1f1"))
    d2, dn_f2a = ffn_bwd("l0f2b", d3, sv_f2a, gain("norm_ffn2", 0), *wf2a, sender("l0f2"))
    d1, dn_s5, s5s = s5_bwd(d2, sv_s5, gain("norm_mix", 0), *ws5, mats, w["s5_d"], sender("s5"))
    d0, dn_f1a = ffn_bwd("l0f1b", d1, sv_f1a, gain("norm_ffn1", 0), *wf1a, sender("l0f1"))
    d_lr, d_li, d_ldt, d_brt, d_bit = disc_bwd(*s5_in, s5s["d_ar"].reshape(s5_in[0].shape), s5s["d_ai"].reshape(s5_in[0].shape),
                                               s5s["d_bbr"], s5s["d_bbi"])

    part = {
        "norm_ffn1": jnp.concatenate([dn_f1a, dn_f1b]), "norm_mix": jnp.concatenate([dn_s5, dn_sb]),
        "norm_ffn2": jnp.concatenate([dn_f2a, dn_f2b]),
        "s5_lam_re": d_lr, "s5_lam_im": d_li, "s5_log_dt": d_ldt, "s5_b_re": jnp.swapaxes(d_brt, 1, 2), "s5_b_im": jnp.swapaxes(d_bit, 1, 2),
        "s5_c_re": s5s["d_c_re"], "s5_c_im": s5s["d_c_im"], "s5_d": s5s["d_d"], "s5_b_glu": s5s["d_bglu"], "sb_g_q": dgq, "sb_g_k": dgk,
    }
    h_small = exchange_start("ag_small", [_pack([part[n].reshape(w[n].shape) for n in SMALL])], True)

    res = {}
    late = h_small["token"]
    for key, names in (("l1f2", [("ffn2_" + s, 1) for s in FFN]), ("sb", [("sb_w_qkv", None), ("sb_w_o", None)]),
                       ("l1f1", [("ffn1_" + s, 1) for s in FFN]), ("l0f2", [("ffn2_" + s, 0) for s in FFN]),
                       ("s5", [("s5_w_in", None), ("s5_w_glu", None), ("s5_w_out", None)]), ("small", None),
                       ("l0f1", [("ffn1_" + s, 0) for s in FFN])):
        if key == "small":
            packed = exchange_wait("agw_small", h_small, late)[0]
            outs = adamw("adamw_small", _pack([w[n] for n in SMALL]), _pack([m[n] for n in SMALL]), _pack([v[n] for n in SMALL]), packed)
            un = [_unpack(o, [w[n] for n in SMALL]) for o in outs]
            for i, n in enumerate(SMALL):
                res[n] = [un[k][i] for k in range(4)]
            late = outs[0]
            continue
        recv = exchange_wait("rsw_" + key, rs[key], late)
        for (n, l), r in zip(names, recv):
            if l is None:
                res[n] = [o[None] for o in adamw("adamw_" + n, w[n][0], m[n][0], v[n][0], r)]
            else:
                res[n] = adamw_layer("adamw%d_%s" % (l, n), w[n], m[n], v[n], r, l, res.get(n))
            late = res[n][0]

    return (loss, d0[None], *[res[n][0] for n in WEIGHTS], *[res[n][1] for n in WEIGHTS],
            *[res[n][2] for n in WEIGHTS], *[res[n][3] for n in WEIGHTS])


def kernel(x, norm_ffn1, ffn1_w_gate, ffn1_w_up, ffn1_w_down, norm_mix, s5_w_in, s5_lam_re, s5_lam_im, s5_log_dt, s5_b_re, s5_b_im, s5_c_re, s5_c_im, s5_d, s5_w_glu, s5_b_glu, s5_w_out, sb_w_qkv, sb_g_q, sb_g_k, sb_w_o, norm_ffn2, ffn2_w_gate, ffn2_w_up, ffn2_w_down, loss_target, m_norm_ffn1, m_ffn1_w_gate, m_ffn1_w_up, m_ffn1_w_down, m_norm_mix, m_s5_w_in, m_s5_lam_re, m_s5_lam_im, m_s5_log_dt, m_s5_b_re, m_s5_b_im, m_s5_c_re, m_s5_c_im, m_s5_d, m_s5_w_glu, m_s5_b_glu, m_s5_w_out, m_sb_w_qkv, m_sb_g_q, m_sb_g_k, m_sb_w_o, m_norm_ffn2, m_ffn2_w_gate, m_ffn2_w_up, m_ffn2_w_down, v_norm_ffn1, v_ffn1_w_gate, v_ffn1_w_up, v_ffn1_w_down, v_norm_mix, v_s5_w_in, v_s5_lam_re, v_s5_lam_im, v_s5_log_dt, v_s5_b_re, v_s5_b_im, v_s5_c_re, v_s5_c_im, v_s5_d, v_s5_w_glu, v_s5_b_glu, v_s5_w_out, v_sb_w_qkv, v_sb_g_q, v_sb_g_k, v_sb_w_o, v_norm_ffn2, v_ffn2_w_gate, v_ffn2_w_up, v_ffn2_w_down):
    w = dict(norm_ffn1=norm_ffn1, ffn1_w_gate=ffn1_w_gate, ffn1_w_up=ffn1_w_up, ffn1_w_down=ffn1_w_down, norm_mix=norm_mix, s5_w_in=s5_w_in, s5_lam_re=s5_lam_re, s5_lam_im=s5_lam_im, s5_log_dt=s5_log_dt, s5_b_re=s5_b_re, s5_b_im=s5_b_im, s5_c_re=s5_c_re, s5_c_im=s5_c_im, s5_d=s5_d, s5_w_glu=s5_w_glu, s5_b_glu=s5_b_glu, s5_w_out=s5_w_out, sb_w_qkv=sb_w_qkv, sb_g_q=sb_g_q, sb_g_k=sb_g_k, sb_w_o=sb_w_o, norm_ffn2=norm_ffn2, ffn2_w_gate=ffn2_w_gate, ffn2_w_up=ffn2_w_up, ffn2_w_down=ffn2_w_down)
    m = dict(norm_ffn1=m_norm_ffn1, ffn1_w_gate=m_ffn1_w_gate, ffn1_w_up=m_ffn1_w_up, ffn1_w_down=m_ffn1_w_down, norm_mix=m_norm_mix, s5_w_in=m_s5_w_in, s5_lam_re=m_s5_lam_re, s5_lam_im=m_s5_lam_im, s5_log_dt=m_s5_log_dt, s5_b_re=m_s5_b_re, s5_b_im=m_s5_b_im, s5_c_re=m_s5_c_re, s5_c_im=m_s5_c_im, s5_d=m_s5_d, s5_w_glu=m_s5_w_glu, s5_b_glu=m_s5_b_glu, s5_w_out=m_s5_w_out, sb_w_qkv=m_sb_w_qkv, sb_g_q=m_sb_g_q, sb_g_k=m_sb_g_k, sb_w_o=m_sb_w_o, norm_ffn2=m_norm_ffn2, ffn2_w_gate=m_ffn2_w_gate, ffn2_w_up=m_ffn2_w_up, ffn2_w_down=m_ffn2_w_down)
    v = dict(norm_ffn1=v_norm_ffn1, ffn1_w_gate=v_ffn1_w_gate, ffn1_w_up=v_ffn1_w_up, ffn1_w_down=v_ffn1_w_down, norm_mix=v_norm_mix, s5_w_in=v_s5_w_in, s5_lam_re=v_s5_lam_re, s5_lam_im=v_s5_lam_im, s5_log_dt=v_s5_log_dt, s5_b_re=v_s5_b_re, s5_b_im=v_s5_b_im, s5_c_re=v_s5_c_re, s5_c_im=v_s5_c_im, s5_d=v_s5_d, s5_w_glu=v_s5_w_glu, s5_b_glu=v_s5_b_glu, s5_w_out=v_s5_w_out, sb_w_qkv=v_sb_w_qkv, sb_g_q=v_sb_g_q, sb_g_k=v_sb_g_k, sb_w_o=v_sb_w_o, norm_ffn2=v_norm_ffn2, ffn2_w_gate=v_ffn2_w_gate, ffn2_w_up=v_ffn2_w_up, ffn2_w_down=v_ffn2_w_down)
    return _step(w, m, v, x, loss_target)
```

```python
import math

import jax
import jax.numpy as jnp
from jax import lax
from jax.experimental import pallas as pl
from jax.experimental.pallas import tpu as pltpu

f32 = jnp.float32
bf16 = jnp.bfloat16

NDEV = 8
AXES = ("x", "y", "c")
MESH = pl.DeviceIdType.MESH
EPS = 1e-6
HEAD = 128
S5H = 16
S5P = 64
GB = 8
FFN_RES = 0.5
ADAM_LR = 0.001
ADAM_B1 = 0.9
ADAM_B2 = 0.999
ADAM_EPS = 1e-08
ADAM_WD = 0.01
ADAM_STEP = 10
VMEM_LIMIT_V7X = 56 * 2 ** 20

NN = (((1,), (0,)), ((), ()))
NT = (((1,), (1,)), ((), ()))
TNd = (((0,), (0,)), ((), ()))


def _tile(n, target, align):
    if n <= target:
        return n
    t = (target // align) * align
    while t >= align:
        if n % t == 0:
            return t
        t -= align
    return n


def _params(sem):
    return pltpu.CompilerParams(dimension_semantics=sem, vmem_limit_bytes=VMEM_LIMIT_V7X)


def _sds(shape, dtype):
    return jax.ShapeDtypeStruct(tuple(shape), dtype)


def _mm(name, a_list, b_list, a_spec, b_spec, dims, grid, out_shapes, out_specs,
        epilogue=None, extra=(), extra_specs=(), nsub=1, merge_b=False, after=None, separate=False):
    n_a, n_p, n_e = len(a_list), len(b_list), len(extra)
    order = [] if after is None else [after]

    def body(*refs):
        a_refs = refs[:n_a]
        b_refs = refs[n_a:n_a + n_p]
        e_refs = refs[n_a + n_p:n_a + n_p + n_e]
        o_refs = refs[n_a + n_p + n_e + len(order):]
        prods = []
        for p, br in enumerate(b_refs):
            ar = a_refs[p if n_a > 1 else 0]
            if nsub > 1 and not merge_b:
                for j in range(nsub):
                    prods.append(lax.dot_general(ar[j].astype(bf16), br[j].astype(bf16), dims, preferred_element_type=f32))
            else:
                b = br[...]
                if merge_b:
                    b = b.reshape(b.shape[0] * b.shape[1], b.shape[2])
                prods.append(lax.dot_general(ar[...].astype(bf16), b.astype(bf16), dims, preferred_element_type=f32))
        if not separate:
            s = prods[0]
            for d in prods[1:]:
                s = s + d
            prods = [s]
        outs = tuple(prods) if epilogue is None else epilogue(*prods, *[e[...] for e in e_refs])
        for o, val in zip(o_refs, outs):
            o[...] = val.astype(o.dtype)

    return pl.pallas_call(
        body, name=name, grid=grid,
        in_specs=[a_spec] * n_a + [b_spec] * n_p + list(extra_specs) + [pl.BlockSpec(memory_space=pl.ANY)] * len(order),
        out_specs=list(out_specs), out_shape=list(out_shapes),
        compiler_params=_params(("parallel",) * len(grid)),
    )(*a_list, *b_list, *extra, *order)


EW_TILE_ELEMS = 1 << 19
ADAM_TILE_ELEMS = 1 << 21


def _ew(name, fn, rows, bcasts, outs, reds=(), tm=None, budget=EW_TILE_ELEMS):
    n_r, n_b, n_o, n_d = len(rows), len(bcasts), len(outs), len(reds)
    R = rows[0].shape[-2]
    if tm is None:
        widest = max([r.shape[-1] * (r.shape[0] if r.ndim == 3 else 1) for r in rows] + [c for c, _ in outs])
        tm = _tile(R, max(16, budget // widest), 16)

    def body(*refs):
        r = refs[:n_r]
        b = refs[n_r:n_r + n_b]
        o = refs[n_r + n_b:n_r + n_b + n_o]
        d = refs[n_r + n_b + n_o:]
        res = fn(*[x[...] for x in r], *[x[...] for x in b])
        for oo, val in zip(o, res[:n_o]):
            oo[...] = val.astype(oo.dtype)
        if n_d:
            @pl.when(pl.program_id(0) == 0)
            def _():
                for dd in d:
                    dd[...] = jnp.zeros_like(dd)
            for dd, val in zip(d, res[n_o:]):
                dd[...] += val

    in_specs = []
    for x in rows:
        if x.ndim == 2:
            in_specs.append(pl.BlockSpec((tm, x.shape[1]), lambda i: (i, 0)))
        else:
            in_specs.append(pl.BlockSpec((x.shape[0], tm, x.shape[2]), lambda i: (0, i, 0)))
    for x in bcasts:
        in_specs.append(pl.BlockSpec(x.shape, lambda i, nd=x.ndim: (0,) * nd))
    out_shape = [_sds((R, c), dt) for c, dt in outs] + [_sds((1, c), f32) for c in reds]
    out_specs = [pl.BlockSpec((tm, c), lambda i: (i, 0)) for c, _ in outs] + [pl.BlockSpec((1, c), lambda i: (0, 0)) for c in reds]
    return pl.pallas_call(
        body, name=name, grid=(R // tm,), in_specs=in_specs, out_specs=out_specs, out_shape=out_shape,
        compiler_params=_params(("arbitrary",)),
    )(*rows, *bcasts)


def _coords():
    return lax.axis_index("x"), lax.axis_index("y"), lax.axis_index("c")


def _me():
    x, y, c = _coords()
    return 4 * x + 2 * y + c


def _peer(k):
    x, y, c = _coords()
    return x ^ ((k >> 2) & 1), y ^ ((k >> 1) & 1), c ^ (k & 1)


def all_gather(name, shards):
    n = len(shards)

    def body(*refs):
        x_refs, out_refs, token = refs[:n], refs[n:2 * n], refs[2 * n]
        send_sems, recv_sems, local_sems = refs[2 * n + 1:]
        token[...] = jnp.zeros_like(token)
        x, y, c = _coords()
        me, sibling = (x, y, c), (x, y, 1 - c)
        chips = [(1 - x, y), (x, 1 - y), (1 - x, 1 - y)]

        def rows(a, px, py, pc):
            return out_refs[a].at[4 * px + 2 * py + pc]

        def copy(a, k, block, to, src=None):
            return pltpu.make_async_remote_copy(
                src_ref=rows(a, *block) if src is None else src, dst_ref=rows(a, *block),
                send_sem=send_sems.at[a, k], recv_sem=recv_sems.at[a, k],
                device_id=to, device_id_type=MESH)

        mine = [pltpu.make_async_copy(x_refs[a], rows(a, *me), local_sems.at[a]) for a in range(n)]
        for cp in mine:
            cp.start()
        first = []
        for a in range(n):
            first.append(copy(a, 0, me, sibling, src=x_refs[a]))
            first += [copy(a, 1 + j, me, (*chip, c), src=x_refs[a]) for j, chip in enumerate(chips)]
        for cp in first:
            cp.start()
        passed = []
        for j, chip in enumerate(chips):
            for a in range(n):
                copy(a, 1 + j, (*chip, c), me).wait_recv()
                cp = copy(a, 4 + j, (*chip, c), sibling)
                cp.start()
                passed.append(cp)
        for a in range(n):
            copy(a, 0, sibling, me).wait_recv()
            for j, chip in enumerate(chips):
                copy(a, 4 + j, (*chip, 1 - c), me).wait_recv()
        for cp in first + passed:
            cp.wait_send()
        for cp in mine:
            cp.wait()

    anyspec = pl.BlockSpec(memory_space=pl.ANY)
    outs = pl.pallas_call(
        body, name=name,
        out_shape=[_sds((NDEV,) + s.shape, s.dtype) for s in shards] + [_sds((8, 128), f32)],
        in_specs=[anyspec] * n, out_specs=[anyspec] * n + [pl.BlockSpec(memory_space=pltpu.VMEM)],
        scratch_shapes=[pltpu.SemaphoreType.DMA((n, 7)), pltpu.SemaphoreType.DMA((n, 7)), pltpu.SemaphoreType.DMA((n,))],
    )(*shards)
    return list(outs[:n]), outs[n]


HBM_SPEC = pl.BlockSpec(memory_space=pltpu.HBM)
SEM_SPEC = pl.BlockSpec(memory_space=pltpu.SEMAPHORE)
EFFECT = pltpu.SideEffectType.DATAFLOW_SIDE_EFFECTING


def _own_slot(block):
    land = lax.empty((NDEV,) + block.shape, block.dtype)
    return lax.dynamic_update_slice(land, block[None], (_me(),) + (0,) * block.ndim)


def exchange_start(name, srcs, gather, after=None):
    n = len(srcs)
    order = [] if after is None else [after]
    ns = n * (NDEV - 1)
    me = _me()
    lands = [_own_slot(s if gather else lax.dynamic_index_in_dim(s, me, 0, keepdims=False)) for s in srcs]

    def body(*refs):
        src_refs, land_refs = refs[:n], refs[n:2 * n]
        first = 2 * n + len(order)
        send_sems, recv_sems = refs[first:first + ns], refs[first + ns:first + 2 * ns]
        token = refs[-1]
        me_ = _me()
        for k in range(1, NDEV):
            px, py, pc = _peer(k)
            p = 4 * px + 2 * py + pc
            for a in range(n):
                i = a * (NDEV - 1) + k - 1
                pltpu.make_async_remote_copy(
                    src_ref=src_refs[a] if gather else src_refs[a].at[p], dst_ref=land_refs[a].at[me_],
                    send_sem=send_sems[i], recv_sem=recv_sems[i],
                    device_id=(px, py, pc), device_id_type=MESH).start()
        token[...] = jnp.zeros_like(token)

    outs = pl.pallas_call(
        body, name=name,
        out_shape=(*[pltpu.SemaphoreType.DMA(())] * (2 * ns),
                   *[pltpu.HBM(s.shape, s.dtype) for s in srcs], *[pltpu.HBM(l.shape, l.dtype) for l in lands],
                   _sds((8, 128), f32)),
        in_specs=[HBM_SPEC] * (2 * n) + [pl.BlockSpec(memory_space=pl.ANY)] * len(order),
        out_specs=(*[SEM_SPEC] * (2 * ns), *[HBM_SPEC] * (2 * n), pl.BlockSpec(memory_space=pltpu.VMEM)),
        input_output_aliases={i: 2 * ns + i for i in range(2 * n)},
        compiler_params=pltpu.CompilerParams(has_side_effects=EFFECT),
    )(*[pltpu.with_memory_space_constraint(s, pltpu.HBM) for s in srcs],
      *[pltpu.with_memory_space_constraint(l, pltpu.HBM) for l in lands], *order)
    return dict(n=n, sems=outs[:2 * ns], srcs=outs[2 * ns:2 * ns + n], lands=outs[2 * ns + n:2 * ns + 2 * n], token=outs[-1], gather=gather)


def exchange_wait(name, hd, after):
    n, gather = hd["n"], hd["gather"]
    ns = n * (NDEV - 1)

    def body(*refs):
        src_refs, land_refs = refs[:n], refs[n:2 * n]
        send_sems, recv_sems = refs[2 * n:2 * n + ns], refs[2 * n + ns:2 * n + 2 * ns]
        x, y, c = _coords()
        for k in range(1, NDEV):
            px, py, pc = _peer(k)
            p = 4 * px + 2 * py + pc
            for a in range(n):
                i = a * (NDEV - 1) + k - 1
                cp = pltpu.make_async_remote_copy(
                    src_ref=src_refs[a] if gather else src_refs[a].at[p], dst_ref=land_refs[a].at[p],
                    send_sem=send_sems[i], recv_sem=recv_sems[i],
                    device_id=(x, y, 1 - c), device_id_type=MESH)
                cp.wait_send()
                cp.wait_recv()

    outs = pl.pallas_call(
        body, name=name,
        out_shape=tuple(pltpu.HBM(s.shape, s.dtype) for s in list(hd["srcs"]) + list(hd["lands"])),
        in_specs=[HBM_SPEC] * (2 * n) + [SEM_SPEC] * (2 * ns) + [pl.BlockSpec(memory_space=pl.ANY)],
        out_specs=tuple([HBM_SPEC] * (2 * n)),
        input_output_aliases={i: i for i in range(2 * n)},
        compiler_params=pltpu.CompilerParams(has_side_effects=EFFECT),
    )(*hd["srcs"], *hd["lands"], *hd["sems"], after)
    return list(outs[n:])


def _adam_math(w_, m_, v_, r_):
    g = r_[0].astype(f32)
    for j in range(1, NDEV):
        g = g + r_[j].astype(f32)
    m2 = ADAM_B1 * m_ + (1.0 - ADAM_B1) * g
    v2 = ADAM_B2 * v_ + (1.0 - ADAM_B2) * jnp.square(g)
    m_hat = m2 / (1.0 - ADAM_B1 ** ADAM_STEP)
    v_hat = v2 / (1.0 - ADAM_B2 ** ADAM_STEP)
    delta = -ADAM_LR * (m_hat / (jnp.sqrt(v_hat) + ADAM_EPS) + ADAM_WD * w_)
    return g, delta, m2, v2


def adamw(name, w, m, v, recv):
    return _ew(name, _adam_math, [w, m, v, recv], [], [(w.shape[1], f32)] * 4, budget=ADAM_TILE_ELEMS)


def adamw_layer(name, w3, m3, v3, recv, l, prev):
    L, R, C = w3.shape
    tm = _tile(R, max(16, ADAM_TILE_ELEMS // (NDEV * C)), 16)
    n_prev = 0 if prev is None else 4

    def body(*refs):
        w_ref, m_ref, v_ref, r_ref = refs[:4]
        o_refs = refs[4 + n_prev:]
        res = _adam_math(w_ref[...], m_ref[...], v_ref[...], r_ref[...])
        for o, val in zip(o_refs, res):
            o[...] = val

    lay = pl.BlockSpec((None, tm, C), lambda i: (l, i, 0))
    return pl.pallas_call(
        body, name=name, grid=(R // tm,),
        in_specs=[lay, lay, lay, pl.BlockSpec((NDEV, tm, C), lambda i: (0, i, 0))] + [pl.BlockSpec(memory_space=pl.ANY)] * n_prev,
        out_specs=[lay] * 4, out_shape=[_sds((L, R, C), f32)] * 4,
        input_output_aliases={4 + i: i for i in range(n_prev)},
        compiler_params=_params(("parallel",)),
    )(w3, m3, v3, recv, *(prev or []))


def rmsnorm_fwd(name, x, g):
    D = x.shape[1]

    def fn(x_, g_):
        r = lax.rsqrt(jnp.mean(x_ * x_, axis=-1, keepdims=True) + EPS)
        return ((x_ * r) * g_,)

    return _ew(name, fn, [x], [g], [(D, bf16)])[0]


def rmsnorm_bwd(name, x, g, dh, dres):
    D = x.shape[1]

    def fn(x_, dh_, dres_, g_):
        r = lax.rsqrt(jnp.mean(x_ * x_, axis=-1, keepdims=True) + EPS)
        xh = x_ * r
        dxh = dh_ * g_
        dx = dres_ + r * (dxh - xh * jnp.mean(dxh * xh, axis=-1, keepdims=True))
        return dx, dx, jnp.sum(dh_ * xh, axis=0, keepdims=True)

    return _ew(name, fn, [x, dh, dres], [g], [(D, f32), (D, bf16)], [D])


def _silu_parts(a):
    sig = jax.nn.sigmoid(a)
    return sig, a * sig


def mm_cols(name, h, w3_list, epilogue=None, extra=(), outs=None):
    T, K = h.shape
    Nb = w3_list[0].shape[2]
    tm = _tile(T, 1024, 16)
    outs = outs or [f32]
    blk = pl.BlockSpec((None, tm, Nb), lambda j, i: (j, i, 0))
    return _mm(name, [h], w3_list, pl.BlockSpec((tm, K), lambda j, i: (i, 0)), pl.BlockSpec((None, K, Nb), lambda j, i: (j, 0, 0)),
               NN, (NDEV, T // tm), [_sds((NDEV, T, Nb), dt) for dt in outs], [blk] * len(outs), epilogue, extra, [blk] * len(extra),
               separate=True)


def mm_rows(name, p, w3, epilogue=None, extra=(), outs=None):
    _, T, Kb = p.shape
    N = w3.shape[2]
    tm, tn = _tile(T, 512, 16), _tile(N, 512, 128)
    outs = outs or [f32]
    blk = pl.BlockSpec((tm, tn), lambda i, n: (i, n))
    return _mm(name, [p], [w3], pl.BlockSpec((NDEV, tm, Kb), lambda i, n: (0, i, 0)), pl.BlockSpec((NDEV, Kb, tn), lambda i, n: (0, 0, n)),
               NN, (T // tm, N // tn), [_sds((T, N), dt) for dt in outs], [blk] * len(outs), epilogue, extra, [blk] * len(extra), nsub=NDEV)


def mm_rows_plain(name, a, w3, epilogue=None, extra=(), extra_specs=None, outs=None):
    T, K = a.shape
    Kb, N = w3.shape[1], w3.shape[2]
    tm, tn = _tile(T, 512, 16), _tile(N, 1024, 128)
    outs = outs or [f32]
    blk = pl.BlockSpec((tm, tn), lambda i, n: (i, n))
    return _mm(name, [a], [w3], pl.BlockSpec((tm, K), lambda i, n: (i, 0)), pl.BlockSpec((NDEV, Kb, tn), lambda i, n: (0, 0, n)),
               NN, (T // tm, N // tn), [_sds((T, N), dt) for dt in outs], [blk] * len(outs), epilogue, extra,
               extra_specs or [blk] * len(extra), nsub=NDEV, merge_b=True)


def mm_t_rows(name, d, w3, epilogue=None, extra=(), outs=None, after=None):
    T, K = d.shape
    Nb = w3.shape[1]
    tm = _tile(T, 512, 16)
    outs = outs or [f32]
    blk = pl.BlockSpec((None, tm, Nb), lambda j, i: (j, i, 0))
    return _mm(name, [d], [w3], pl.BlockSpec((tm, K), lambda j, i: (i, 0)), pl.BlockSpec((None, Nb, K), lambda j, i: (j, 0, 0)),
               NT, (NDEV, T // tm), [_sds((NDEV, T, Nb), dt) for dt in outs], [blk] * len(outs), epilogue, extra, [blk] * len(extra),
               after=after)


def mm_t_rows_plain(name, d, w3, epilogue=None, extra=(), outs=None, after=None):
    T, K = d.shape
    Nb = w3.shape[1]
    tm = _tile(T, 256, 16)
    outs = outs or [f32]
    blk = pl.BlockSpec((tm, NDEV * Nb), lambda i: (i, 0))
    return _mm(name, [d], [w3], pl.BlockSpec((tm, K), lambda i: (i, 0)), pl.BlockSpec((NDEV, Nb, K), lambda i: (0, 0, 0)),
               NT, (T // tm,), [_sds((T, NDEV * Nb), dt) for dt in outs], [blk] * len(outs), epilogue, extra, [blk] * len(extra),
               nsub=NDEV, merge_b=True, after=after)


def mm_t_cols(name, d_list, w3_list, epilogue=None, extra=(), outs=None, after=None):
    _, T, Kb = d_list[0].shape
    N = w3_list[0].shape[1]
    tm, tn = _tile(T, 512, 16), _tile(N, 256, 128)
    outs = outs or [f32]
    blk = pl.BlockSpec((tm, tn), lambda i, n: (i, n))
    return _mm(name, d_list, w3_list, pl.BlockSpec((NDEV, tm, Kb), lambda i, n: (0, i, 0)), pl.BlockSpec((NDEV, tn, Kb), lambda i, n: (0, n, 0)),
               NT, (T // tm, N // tn), [_sds((T, N), dt) for dt in outs], [blk] * len(outs), epilogue, extra, [blk] * len(extra), nsub=NDEV,
               after=after)


def mm_grad_rows(name, p, d, scale=1.0):
    _, T, Mb = p.shape
    N = d.shape[1]
    tn = _tile(N, 512, 128)
    return _mm(name, [p], [d], pl.BlockSpec((None, T, Mb), lambda j, n: (j, 0, 0)), pl.BlockSpec((T, tn), lambda j, n: (0, n)),
               TNd, (NDEV, N // tn), [_sds((NDEV, Mb, N), bf16)], [pl.BlockSpec((None, Mb, tn), lambda j, n: (j, 0, n))],
               (lambda acc: (acc * scale,)))[0]


def mm_grad_rows_plain(name, a, d):
    T, M = a.shape
    N = d.shape[1]
    tm, tn = _tile(M, 512, 128), _tile(N, 512, 128)
    g = _mm(name, [a], [d], pl.BlockSpec((T, tm), lambda m, n: (0, m)), pl.BlockSpec((T, tn), lambda m, n: (0, n)),
            TNd, (M // tm, N // tn), [_sds((M, N), bf16)], [pl.BlockSpec((tm, tn), lambda m, n: (m, n))])[0]
    return g.reshape(NDEV, M // NDEV, N)


def mm_grad_cols(name, h, d):
    T, M = h.shape
    Nb = d.shape[2]
    tm = _tile(M, 512, 128)
    return _mm(name, [h], [d], pl.BlockSpec((T, tm), lambda j, m: (0, m)), pl.BlockSpec((None, T, Nb), lambda j, m: (j, 0, 0)),
               TNd, (NDEV, M // tm), [_sds((NDEV, M, Nb), bf16)], [pl.BlockSpec((None, tm, Nb), lambda j, m: (j, m, 0))])[0]


def ffn_fwd(tag, x, g, wg3, wu3, wd3):
    h = rmsnorm_fwd(tag + "_norm", x, g)

    def gate_up_epi(a_, b_):
        _, sl = _silu_parts(a_)
        return a_, b_, sl * b_

    a, b, p = mm_cols(tag + "_gateup", h, [wg3, wu3], gate_up_epi, outs=[bf16, bf16, bf16])
    xo = mm_rows(tag + "_down", p, wd3, lambda acc, x_: (x_ + FFN_RES * acc,), [x])[0]
    return xo, (x, h, a, b, p)


def ffn_bwd(tag, dxo, dxo16, saved, g, wg3, wu3, wd3, send, each=False):
    x, h, a, b, p = saved

    def dp_epi(acc, a_, b_):
        dp = FFN_RES * acc
        a_, b_ = a_.astype(f32), b_.astype(f32)
        sig, sl = _silu_parts(a_)
        return dp * b_ * (sig * (1.0 + a_ * (1.0 - sig))), dp * sl

    da, db = mm_t_rows(tag + "_dp", dxo16, wd3, dp_epi, [a, b], [bf16, bf16])
    g_wd = mm_grad_rows(tag + "_gwd", p, dxo16, FFN_RES)
    if each:
        send([g_wd], [2])
    g_wg = mm_grad_cols(tag + "_gwg", h, da)
    if each:
        send([g_wg], [0])
    g_wu = mm_grad_cols(tag + "_gwu", h, db)
    tok = send([g_wu], [1]) if each else send([g_wg, g_wu, g_wd], [0, 1, 2])
    dh = mm_t_cols(tag + "_dh", [da, db], [wg3, wu3], after=tok)[0]
    return rmsnorm_bwd(tag + "_dnorm", x, g, dh, dxo)


def _disc(lr, li, ldt, brt, bit):
    dt = jnp.exp(ldt)
    lr = jnp.minimum(lr, -1e-4)
    mag = jnp.exp(lr * dt)
    ab_re = mag * jnp.cos(li * dt)
    ab_im = mag * jnp.sin(li * dt)
    den = lr * lr + li * li
    n_re = ab_re - 1.0
    f_re = (n_re * lr + ab_im * li) / den
    f_im = (ab_im * lr - n_re * li) / den
    bb_re = f_re[:, None, :] * brt - f_im[:, None, :] * bit
    bb_im = f_re[:, None, :] * bit + f_im[:, None, :] * brt
    return ab_re, ab_im, bb_re, bb_im


def disc_fwd(lr, li, ldt, brt, bit):
    def body(lr_ref, li_ref, ldt_ref, brt_ref, bit_ref, ar_ref, ai_ref, bbr_ref, bbi_ref):
        ar, ai, bbr, bbi = _disc(lr_ref[...], li_ref[...], ldt_ref[...], brt_ref[...], bit_ref[...])
        ar_ref[...] = ar
        ai_ref[...] = ai
        bbr_ref[...] = bbr
        bbi_ref[...] = bbi

    return pl.pallas_call(body, name="s5_disc", out_shape=[_sds(lr.shape, f32), _sds(lr.shape, f32), _sds(brt.shape, f32), _sds(brt.shape, f32)],
                          compiler_params=pltpu.CompilerParams(vmem_limit_bytes=VMEM_LIMIT_V7X))(lr, li, ldt, brt, bit)


def disc_bwd(lr, li, ldt, brt, bit, d_ar, d_ai, d_bbr, d_bbi):
    def body(lr_ref, li_ref, ldt_ref, brt_ref, bit_ref, dar_ref, dai_ref, dbbr_ref, dbbi_ref, o_lr, o_li, o_ldt, o_brt, o_bit):
        _, vjp = jax.vjp(_disc, lr_ref[...], li_ref[...], ldt_ref[...], brt_ref[...], bit_ref[...])
        g = vjp((dar_ref[...], dai_ref[...], dbbr_ref[...], dbbi_ref[...]))
        o_lr[...] = g[0]
        o_li[...] = g[1]
        o_ldt[...] = g[2]
        o_brt[...] = g[3]
        o_bit[...] = g[4]

    return pl.pallas_call(body, name="s5_disc_bwd",
                          out_shape=[_sds(lr.shape, f32), _sds(lr.shape, f32), _sds(ldt.shape, f32), _sds(brt.shape, f32), _sds(brt.shape, f32)],
                          compiler_params=pltpu.CompilerParams(vmem_limit_bytes=VMEM_LIMIT_V7X))(lr, li, ldt, brt, bit, d_ar, d_ai, d_bbr, d_bbi)


def _blockdiag(m):
    G, R, C = m.shape
    eye = jnp.eye(GB, dtype=m.dtype)
    m5 = m.reshape(G // GB, GB, R, 1, C) * eye[None, :, None, :, None]
    return m5.reshape(G // GB, GB * R, GB * C)


def _diag_extract(M, R, C):
    nb = M.shape[0]
    eye = jnp.eye(GB, dtype=M.dtype)
    m5 = M.reshape(nb, GB, R, GB, C) * eye[None, :, None, :, None]
    return m5.sum(axis=3).reshape(nb * GB, R, C)


def mm_blockdiag(name, a_list, bd_list, dims, epilogue=None, extra=(), extra_bcast=(), outs=None):
    T = a_list[0].shape[0]
    nb, r, c = bd_list[0].shape
    ra = a_list[0].shape[1] // nb
    ca = c if dims == NN else r
    n_p, n_e, n_b = len(a_list), len(extra), len(extra_bcast)
    outs = outs or [f32]
    tm = _tile(T, 128, 16)

    def body(*refs):
        a_refs, b_refs = refs[:n_p], refs[n_p:2 * n_p]
        e_refs = refs[2 * n_p:2 * n_p + n_e + n_b]
        o_refs = refs[2 * n_p + n_e + n_b:]
        for b in range(nb):
            s = None
            for ar, br in zip(a_refs, b_refs):
                d = lax.dot_general(ar[:, b * ra:(b + 1) * ra].astype(bf16), br[b], dims, preferred_element_type=f32)
                s = d if s is None else s + d
            cols = slice(b * ca, (b + 1) * ca)
            vals = (s,) if epilogue is None else epilogue(s, *[e[:, cols] for e in e_refs])
            for o, val in zip(o_refs, vals):
                o[:, cols] = val.astype(o.dtype)

    row = lambda w: pl.BlockSpec((tm, w), lambda i: (i, 0))
    return pl.pallas_call(
        body, name=name, grid=(T // tm,),
        in_specs=[row(nb * ra)] * n_p + [pl.BlockSpec((nb, r, c), lambda i: (0, 0, 0))] * n_p + [row(nb * ca)] * n_e
        + [pl.BlockSpec((1, nb * ca), lambda i: (0, 0))] * n_b,
        out_specs=[row(nb * ca)] * len(outs), out_shape=[_sds((T, nb * ca), dt) for dt in outs],
        compiler_params=_params(("parallel",)),
    )(*a_list, *bd_list, *extra, *extra_bcast)


def mm_blockdiag_grad(name, a, d, nb):
    T = a.shape[0]
    ra, rd = a.shape[1] // nb, d.shape[1] // nb
    tk = _tile(T, 256, 16)

    def body(a_ref, d_ref, o_ref):
        @pl.when(pl.program_id(0) == 0)
        def _():
            o_ref[...] = jnp.zeros_like(o_ref)

        for b in range(nb):
            o_ref[b] += lax.dot_general(a_ref[:, b * ra:(b + 1) * ra].astype(bf16), d_ref[:, b * rd:(b + 1) * rd].astype(bf16),
                                        TNd, preferred_element_type=f32)

    return pl.pallas_call(
        body, name=name, grid=(T // tk,),
        in_specs=[pl.BlockSpec((tk, nb * ra), lambda k: (k, 0)), pl.BlockSpec((tk, nb * rd), lambda k: (k, 0))],
        out_specs=pl.BlockSpec((nb, ra, rd), lambda k: (0, 0, 0)), out_shape=_sds((nb, ra, rd), f32),
        compiler_params=_params(("arbitrary",)),
    )(a, d)


def s5_scan_fwd(bu_re, bu_im, ar, ai):
    T, S, N8 = bu_re.shape
    tt = _tile(T, max(8, (1 << 19) // (S * N8)), 8)

    def body(bur_ref, bui_ref, ar_ref, ai_ref, sr_ref, si_ref, st):
        @pl.when(pl.program_id(0) == 0)
        def _():
            st[...] = jnp.zeros_like(st)

        a_r, a_i = ar_ref[...], ai_ref[...]

        def step(t, c):
            sr, si = c
            nr = a_r * sr - a_i * si + bur_ref[t]
            ni = a_r * si + a_i * sr + bui_ref[t]
            sr_ref[t] = nr
            si_ref[t] = ni
            return nr, ni

        sr, si = lax.fori_loop(0, tt, step, (st[0], st[1]))
        st[0] = sr
        st[1] = si

    blk = pl.BlockSpec((tt, S, N8), lambda i: (i, 0, 0))
    par = pl.BlockSpec((S, N8), lambda i: (0, 0))
    return pl.pallas_call(body, name="s5_scan", grid=(T // tt,), in_specs=[blk, blk, par, par], out_specs=[blk, blk],
                          out_shape=[_sds(bu_re.shape, f32)] * 2, scratch_shapes=[pltpu.VMEM((2, S, N8), f32)],
                          compiler_params=_params(("arbitrary",)))(bu_re, bu_im, ar, ai)


def s5_scan_bwd(ds_re, ds_im, s_re, s_im, ar, ai):
    T, S, N8 = ds_re.shape
    tt = _tile(T, max(8, (1 << 18) // (S * N8)), 8)
    nblk = T // tt

    def body(dsr_ref, dsi_ref, sr_ref, si_ref, ar_ref, ai_ref, qr_ref, qi_ref, dar_ref, dai_ref, st):
        @pl.when(pl.program_id(0) == 0)
        def _():
            st[...] = jnp.zeros_like(st)
            dar_ref[...] = jnp.zeros_like(dar_ref)
            dai_ref[...] = jnp.zeros_like(dai_ref)

        a_r, a_i = ar_ref[...], ai_ref[...]

        def step(i, c):
            qr, qi, dar, dai = c
            t = tt - 1 - i
            sr, si = sr_ref[t], si_ref[t]
            dar = dar + qr * sr + qi * si
            dai = dai + qi * sr - qr * si
            nr = dsr_ref[t] + a_r * qr + a_i * qi
            ni = dsi_ref[t] - a_i * qr + a_r * qi
            qr_ref[t] = nr
            qi_ref[t] = ni
            return nr, ni, dar, dai

        qr, qi, dar, dai = lax.fori_loop(0, tt, step, (st[0], st[1], dar_ref[...], dai_ref[...]))
        st[0] = qr
        st[1] = qi
        dar_ref[...] = dar
        dai_ref[...] = dai

    blk = pl.BlockSpec((tt, S, N8), lambda i: (nblk - 1 - i, 0, 0))
    par = pl.BlockSpec((S, N8), lambda i: (0, 0))
    return pl.pallas_call(body, name="s5_scan_bwd", grid=(nblk,), in_specs=[blk, blk, blk, blk, par, par], out_specs=[blk, blk, par, par],
                          out_shape=[_sds(ds_re.shape, f32)] * 2 + [_sds((S, N8), f32)] * 2, scratch_shapes=[pltpu.VMEM((2, S, N8), f32)],
                          compiler_params=_params(("arbitrary",)))(ds_re, ds_im, s_re, s_im, ar, ai)


def _gelu_grad(y):
    c = math.sqrt(2.0 / math.pi)
    th = jnp.tanh(c * (y + 0.044715 * (y * y * y)))
    return 0.5 * (1.0 + th) + 0.5 * y * (1.0 - th * th) * c * (1.0 + 3.0 * 0.044715 * (y * y))


def s5_prepare(lam_re, lam_im, log_dt, b_re, b_im, c_re, c_im):
    G = lam_re.shape[0]
    ldt = log_dt.reshape(G, 1)
    brt, bit = jnp.swapaxes(b_re, 1, 2), jnp.swapaxes(b_im, 1, 2)
    ar, ai, bbr, bbi = disc_fwd(lam_re, lam_im, ldt, brt, bit)
    mats = dict(
        bbd_re=_blockdiag(bbr).astype(bf16), bbd_im=_blockdiag(bbi).astype(bf16),
        cd_re=_blockdiag(jnp.swapaxes(c_re, 1, 2)).astype(bf16),
        cd_imn=_blockdiag(-jnp.swapaxes(c_im, 1, 2)).astype(bf16),
        ar=ar.reshape(8, -1), ai=ai.reshape(8, -1))
    return (lam_re, lam_im, ldt, brt, bit), mats


def s5_fwd(x, gm, w_in3, w_glu3, w_out3, mats, d_skip, b_glu):
    T, D = x.shape
    N = mats["ar"].size
    h = rmsnorm_fwd("s5_norm", x, gm)
    u = mm_rows_plain("s5_in", h, w_in3)[0]
    bu_re = mm_blockdiag("s5_bu_re", [u], [mats["bbd_re"]], NN)[0]
    bu_im = mm_blockdiag("s5_bu_im", [u], [mats["bbd_im"]], NN)[0]
    s_re, s_im = s5_scan_fwd(bu_re.reshape(T, 8, N // 8), bu_im.reshape(T, 8, N // 8), mats["ar"], mats["ai"])
    s_re, s_im = s_re.reshape(T, N), s_im.reshape(T, N)

    def y_epi(acc, u_, d_):
        y2 = acc + d_ * u_
        return y2, jax.nn.gelu(y2)

    y2, gl = mm_blockdiag("s5_y", [s_re, s_im], [mats["cd_re"], mats["cd_imn"]], NN, y_epi, [u], [d_skip], [f32, bf16])

    def glu_epi(acc, y2_, b_):
        zg = acc + b_
        return zg, jax.nn.gelu(y2_) * jax.nn.sigmoid(zg)

    tm, tn = _tile(T, 512, 16), _tile(D, 1024, 128)
    zg, o = mm_rows_plain("s5_glu", gl, w_glu3, glu_epi, [y2, b_glu],
                          [pl.BlockSpec((tm, tn), lambda i, n: (i, n)), pl.BlockSpec((1, tn), lambda i, n: (0, n))], [f32, bf16])
    xo = mm_rows_plain("s5_out", o, w_out3, lambda acc, x_: (x_ + acc,), [x])[0]
    return xo, (x, h, u, s_re, s_im, y2, gl, zg, o)


def s5_bwd(dxo, dxo16, saved, gm, w_in3, w_glu3, w_out3, mats, d_skip, send):
    x, h, u, s_re, s_im, y2, gl, zg, o = saved
    T, D = x.shape
    N = mats["ar"].size
    nb = mats["cd_re"].shape[0]

    def do_epi(acc, y2_, zg_):
        sg = jax.nn.sigmoid(zg_)
        return acc * sg, acc * jax.nn.gelu(y2_) * (sg * (1.0 - sg))

    dgl_direct, dzg = mm_t_rows_plain("s5_do", dxo16, w_out3, do_epi, [y2, zg], [f32, f32])
    g_wout = mm_grad_rows_plain("s5_gwout", o, dxo16)
    g_wglu = mm_grad_rows_plain("s5_gwglu", gl, dzg)
    dy2 = mm_t_rows_plain("s5_dgl", dzg, w_glu3, lambda acc, dd_, y2_: ((acc + dd_) * _gelu_grad(y2_),), [dgl_direct, y2])[0]

    def red_fn(dy2_, u_, dzg_, d_):
        return dy2_ * d_, jnp.sum(dy2_ * u_, axis=0, keepdims=True), jnp.sum(dzg_, axis=0, keepdims=True)

    du_direct, dd, dbglu = _ew("s5_dskip", red_fn, [dy2, u, dzg], [d_skip], [(D, f32)], [D, D])
    ds_re = mm_blockdiag("s5_ds_re", [dy2], [mats["cd_re"]], NT)[0]
    ds_im = mm_blockdiag("s5_ds_im", [dy2], [mats["cd_imn"]], NT)[0]
    d_cd_re = mm_blockdiag_grad("s5_gc_re", s_re, dy2, nb)
    d_cd_imn = mm_blockdiag_grad("s5_gc_im", s_im, dy2, nb)
    sh = (T, 8, N // 8)
    q_re, q_im, d_ar, d_ai = s5_scan_bwd(ds_re.reshape(sh), ds_im.reshape(sh), s_re.reshape(sh), s_im.reshape(sh), mats["ar"], mats["ai"])
    q_re, q_im = q_re.reshape(T, N), q_im.reshape(T, N)
    d_bbd_re = mm_blockdiag_grad("s5_gb_re", u, q_re, nb)
    d_bbd_im = mm_blockdiag_grad("s5_gb_im", u, q_im, nb)
    du = mm_blockdiag("s5_du", [q_re, q_im], [mats["bbd_re"], mats["bbd_im"]], NT, lambda acc, dd_: (acc + dd_,), [du_direct])[0]
    g_win = mm_grad_rows_plain("s5_gwin", h, du)
    tok = send([g_win, g_wglu, g_wout], [0, 1, 2])
    dh = mm_t_rows_plain("s5_dh", du, w_in3, after=tok)[0]
    dx, dx16, dgm = rmsnorm_bwd("s5_dnorm", x, gm, dh, dxo)
    small = dict(d_ar=d_ar, d_ai=d_ai, d_bbr=_diag_extract(d_bbd_re, S5H, S5P), d_bbi=_diag_extract(d_bbd_im, S5H, S5P),
                 d_c_re=jnp.swapaxes(_diag_extract(d_cd_re, S5P, S5H), 1, 2), d_c_im=-jnp.swapaxes(_diag_extract(d_cd_imn, S5P, S5H), 1, 2),
                 d_d=dd, d_bglu=dbglu)
    return dx, dx16, dgm, small


def qk_prepass(qkv, g_q, g_k, nh):
    _, T, W = qkv.shape
    cpd = W // HEAD
    tm = _tile(T, 1024, 16)

    def body(x_ref, gq_ref, gk_ref, o_ref):
        which = pl.program_id(0) // nh
        x = x_ref[...]
        r = lax.rsqrt(jnp.mean(x * x, axis=-1, keepdims=True) + EPS)
        g = jnp.where(which == 0, gq_ref[...], gk_ref[...])
        o_ref[...] = jnp.where(which == 2, x, (x * r) * g).astype(o_ref.dtype)

    return pl.pallas_call(
        body, name="sb_qknorm", grid=(3 * nh, T // tm),
        in_specs=[pl.BlockSpec((None, tm, HEAD), lambda ch, i: (ch // cpd, i, ch % cpd)),
                  pl.BlockSpec((1, HEAD), lambda ch, i: (0, 0)), pl.BlockSpec((1, HEAD), lambda ch, i: (0, 0))],
        out_specs=pl.BlockSpec((None, tm, HEAD), lambda ch, i: (ch, i, 0)),
        out_shape=_sds((3 * nh, T, HEAD), bf16), compiler_params=_params(("parallel", "parallel")))(qkv, g_q, g_k)


def qk_prepass_bwd(qkv, g_q, g_k, dqn, dkn, dv, nh):
    _, T, W = qkv.shape
    cpd = W // HEAD
    tm = _tile(T, 1024, 16)

    def body(x_ref, gq_ref, gk_ref, dq_ref, dk_ref, dv_ref, o_ref, dgq_ref, dgk_ref):
        ch = pl.program_id(1)
        which = ch // nh

        @pl.when((pl.program_id(0) == 0) & (ch == 0))
        def _():
            dgq_ref[...] = jnp.zeros_like(dgq_ref)
            dgk_ref[...] = jnp.zeros_like(dgk_ref)

        x = x_ref[...]
        dn = jnp.where(which == 0, dq_ref[...], dk_ref[...])
        g = jnp.where(which == 0, gq_ref[...], gk_ref[...])
        r = lax.rsqrt(jnp.mean(x * x, axis=-1, keepdims=True) + EPS)
        xh = x * r
        dxh = dn * g
        dx = r * (dxh - xh * jnp.mean(dxh * xh, axis=-1, keepdims=True))
        o_ref[...] = jnp.where(which == 2, dv_ref[...], dx).astype(o_ref.dtype)
        dg = jnp.sum(dn * xh, axis=0, keepdims=True)
        dgq_ref[...] += jnp.where(which == 0, dg, 0.0)
        dgk_ref[...] += jnp.where(which == 1, dg, 0.0)

    chunk = pl.BlockSpec((None, tm, HEAD), lambda i, ch: (ch // cpd, i, ch % cpd))
    head = lambda off: pl.BlockSpec((None, tm, HEAD), lambda i, ch: (jnp.clip(ch - off, 0, nh - 1), i, 0))
    vec = pl.BlockSpec((1, HEAD), lambda i, ch: (0, 0))
    return pl.pallas_call(
        body, name="sb_qknorm_bwd", grid=(T // tm, 3 * nh),
        in_specs=[chunk, vec, vec, head(0), head(nh), head(2 * nh)], out_specs=[chunk, vec, vec],
        out_shape=[_sds(qkv.shape, bf16), _sds((1, HEAD), f32), _sds((1, HEAD), f32)],
        compiler_params=_params(("arbitrary", "arbitrary")))(qkv, g_q, g_k, dqn, dkn, dv)


def _split_dot(a, tri):
    hi = a.astype(bf16)
    lo = (a - hi.astype(f32)).astype(bf16)
    return lax.dot_general(hi, tri, NN, preferred_element_type=f32) + lax.dot_general(lo, tri, NN, preferred_element_type=f32)


LOG2E = 1.4426950408889634


def _sb_logits(q, k, scale):
    t = lax.dot_general(q, k, NT, preferred_element_type=f32) * (scale * LOG2E)
    sp = jnp.log2(1.0 + jnp.exp2(-jnp.abs(t)))
    return jnp.minimum(t, 0.0) - sp, -(jnp.maximum(t, 0.0) + sp)


HP = 2


def attn_fwd(qkvn, nh, tq):
    _, T, _ = qkvn.shape
    nq = T // tq
    scale = 1.0 / math.sqrt(HEAD)

    def body(q_ref, k_ref, v_ref, o_ref):
        qi = pl.program_id(1)
        qs = [q_ref[h] for h in range(HP)]
        row = lax.broadcasted_iota(jnp.int32, (tq, tq), 0)
        col = lax.broadcasted_iota(jnp.int32, (tq, tq), 1)
        past = col < row
        tri = (row > col).astype(bf16)

        def block(kb, carry, acc, diag):
            ks = pl.ds(pl.multiple_of(kb * tq, tq), tq)
            new_c, new_a = [], []
            for h in range(HP):
                k, v = k_ref[h, ks, :], v_ref[h, ks, :]
                lb, lk = _sb_logits(qs[h], k, scale)
                if diag:
                    lk = jnp.where(past, lk, 0.0)
                w = jnp.exp2(lb + _split_dot(lk, tri) + carry[h])
                if diag:
                    w = jnp.where(past, w, 0.0)
                new_a.append(acc[h] + lax.dot_general(w.astype(bf16), v, NN, preferred_element_type=f32))
                new_c.append(carry[h] + jnp.sum(lk, axis=1, keepdims=True))
            return tuple(new_c), tuple(new_a)

        zero_c = tuple(jnp.zeros((tq, 1), f32) for _ in range(HP))
        zero_a = tuple(jnp.zeros((tq, HEAD), f32) for _ in range(HP))
        carry, acc = block(qi, zero_c, zero_a, True)
        carry, acc = lax.fori_loop(0, qi, lambda i, c: block(qi - 1 - i, c[0], c[1], False), (carry, acc))
        for h in range(HP):
            o_ref[:, h * HEAD:(h + 1) * HEAD] = acc[h]

    heads = lambda off: pl.BlockSpec((HP, T, HEAD), lambda hh, i: (off // HP + hh, 0, 0))
    return pl.pallas_call(
        body, name="sb_attn", grid=(nh // HP, nq),
        in_specs=[pl.BlockSpec((HP, tq, HEAD), lambda hh, i: (hh, i, 0)), heads(nh), heads(2 * nh)],
        out_specs=pl.BlockSpec((tq, HP * HEAD), lambda hh, i: (i, hh)),
        out_shape=_sds((T, nh * HEAD), f32), compiler_params=_params(("parallel", "arbitrary")))(qkvn, qkvn, qkvn)


def attn_bwd(qkvn, do, nh, tq):
    _, T, _ = qkvn.shape
    nq = T // tq
    scale = 1.0 / math.sqrt(HEAD)

    def body(q_ref, k_ref, v_ref, do_ref, dq_ref, dk_ref, dv_ref, e_scr, sg_scr):
        qi = pl.program_id(1)

        @pl.when(qi == 0)
        def _():
            dk_ref[...] = jnp.zeros_like(dk_ref)
            dv_ref[...] = jnp.zeros_like(dv_ref)

        qs = [q_ref[h] for h in range(HP)]
        dobs = [do_ref[:, h * HEAD:(h + 1) * HEAD].astype(bf16) for h in range(HP)]
        row = lax.broadcasted_iota(jnp.int32, (tq, tq), 0)
        col = lax.broadcasted_iota(jnp.int32, (tq, tq), 1)
        past = col < row
        tri_later = (row > col).astype(bf16)
        tri_before = (row < col).astype(bf16)

        def sweep1(kb, carry, diag):
            ks = pl.ds(pl.multiple_of(kb * tq, tq), tq)
            new_c = []
            for h in range(HP):
                k, v = k_ref[h, ks, :], v_ref[h, ks, :]
                lb, lk = _sb_logits(qs[h], k, scale)
                if diag:
                    lk = jnp.where(past, lk, 0.0)
                w = jnp.exp2(lb + _split_dot(lk, tri_later) + carry[h])
                if diag:
                    w = jnp.where(past, w, 0.0)
                dw = lax.dot_general(dobs[h], v, NT, preferred_element_type=f32)
                e_scr[h, kb] = dw * w
                sg_scr[h, kb] = jnp.exp2(lb)
                dv_ref[h, ks, :] += lax.dot_general(w.astype(bf16), dobs[h], TNd, preferred_element_type=f32)
                new_c.append(carry[h] + jnp.sum(lk, axis=1, keepdims=True))
            return tuple(new_c)

        zero_c = tuple(jnp.zeros((tq, 1), f32) for _ in range(HP))
        carry = sweep1(qi, zero_c, True)
        lax.fori_loop(0, qi, lambda i, c: sweep1(qi - 1 - i, c, False), carry)

        def sweep2(kb, carry, dq, diag):
            ks = pl.ds(pl.multiple_of(kb * tq, tq), tq)
            new_c, new_q = [], []
            for h in range(HP):
                k = k_ref[h, ks, :]
                e, sg = e_scr[h, kb], sg_scr[h, kb]
                big_e = _split_dot(e, tri_before) + carry[h]
                dz = (e * (1.0 - sg) - big_e * sg) * scale
                if diag:
                    dz = jnp.where(past, dz, 0.0)
                dzb = dz.astype(bf16)
                new_q.append(dq[h] + lax.dot_general(dzb, k, NN, preferred_element_type=f32))
                dk_ref[h, ks, :] += lax.dot_general(dzb, qs[h], TNd, preferred_element_type=f32)
                new_c.append(carry[h] + jnp.sum(e, axis=1, keepdims=True))
            return tuple(new_c), tuple(new_q)

        zero_q = tuple(jnp.zeros((tq, HEAD), f32) for _ in range(HP))
        carry, dq = lax.fori_loop(0, qi, lambda i, c: sweep2(i, c[0], c[1], False), (zero_c, zero_q))
        _, dq = sweep2(qi, carry, dq, True)
        for h in range(HP):
            dq_ref[h] = dq[h]

    heads = lambda off: pl.BlockSpec((HP, T, HEAD), lambda hh, i: (off // HP + hh, 0, 0))
    return pl.pallas_call(
        body, name="sb_attn_bwd", grid=(nh // HP, nq),
        in_specs=[pl.BlockSpec((HP, tq, HEAD), lambda hh, i: (hh, i, 0)), heads(nh), heads(2 * nh),
                  pl.BlockSpec((tq, HP * HEAD), lambda hh, i: (i, hh))],
        out_specs=[pl.BlockSpec((HP, tq, HEAD), lambda hh, i: (hh, i, 0)), heads(0), heads(0)],
        out_shape=[_sds((nh, T, HEAD), f32)] * 3,
        scratch_shapes=[pltpu.VMEM((HP, nq, tq, tq), f32), pltpu.VMEM((HP, nq, tq, tq), f32)],
        compiler_params=_params(("parallel", "arbitrary")))(qkvn, qkvn, qkvn, do)


def sb_fwd(x, gm, w_qkv3, w_o3, g_q, g_k):
    T, D = x.shape
    nh = D // HEAD
    h = rmsnorm_fwd("sb_norm", x, gm)
    qkv = mm_cols("sb_qkv", h, [w_qkv3])[0]
    qkvn = qk_prepass(qkv, g_q, g_k, nh)
    o = attn_fwd(qkvn, nh, _tile(T, 256, 128))
    xo = mm_rows_plain("sb_o", o, w_o3, lambda acc, x_: (x_ + acc,), [x])[0]
    return xo, (x, h, qkv, qkvn, o)


def sb_bwd(dxo, dxo16, saved, gm, w_qkv3, w_o3, g_q, g_k, send):
    x, h, qkv, qkvn, o = saved
    T, D = x.shape
    nh = D // HEAD
    do = mm_t_rows_plain("sb_do", dxo16, w_o3)[0]
    g_wo = mm_grad_rows_plain("sb_gwo", o, dxo16)
    dqn, dkn, dv = attn_bwd(qkvn, do, nh, _tile(T, 256, 128))
    dqkv, dgq, dgk = qk_prepass_bwd(qkv, g_q, g_k, dqn, dkn, dv, nh)
    g_wqkv = mm_grad_cols("sb_gwqkv", h, dqkv)
    tok = send([g_wqkv, g_wo], [0, 1])
    dh = mm_t_cols("sb_dh", [dqkv], [w_qkv3], after=tok)[0]
    dx, dx16, dgm = rmsnorm_bwd("sb_dnorm", x, gm, dh, dxo)
    return dx, dx16, dgm, dgq, dgk


def loss_head(y, target):
    D = y.shape[1]

    def fn(y_, t_):
        err = y_ - t_
        dy = err / D
        return dy, dy, jnp.sum(err * err, axis=0, keepdims=True)

    dy, dy16, sq = _ew("loss_head", fn, [y, target], [], [(D, f32), (D, bf16)], [D])
    return 0.5 * (jnp.sum(sq) / D), dy, dy16


FFN = ["w_gate", "w_up", "w_down"]
SMALL = ["norm_ffn1", "norm_mix", "s5_lam_re", "s5_lam_im", "s5_log_dt", "s5_b_re", "s5_b_im", "s5_c_re", "s5_c_im",
         "s5_d", "s5_b_glu", "sb_g_q", "sb_g_k", "norm_ffn2"]
WEIGHTS = ["norm_ffn1", "ffn1_w_gate", "ffn1_w_up", "ffn1_w_down", "norm_mix", "s5_w_in", "s5_lam_re", "s5_lam_im", "s5_log_dt",
           "s5_b_re", "s5_b_im", "s5_c_re", "s5_c_im", "s5_d", "s5_w_glu", "s5_b_glu", "s5_w_out", "sb_w_qkv", "sb_g_q", "sb_g_k",
           "sb_w_o", "norm_ffn2", "ffn2_w_gate", "ffn2_w_up", "ffn2_w_down"]
PACK_LANES = 128
PACK_ROWS = 64


def _pack(arrs):
    flat = jnp.concatenate([a.reshape(-1).astype(f32) for a in arrs])
    pad = (-flat.shape[0]) % (PACK_LANES * PACK_ROWS)
    return jnp.pad(flat, (0, pad)).reshape(-1, PACK_LANES)


def _unpack(buf, like):
    flat = buf.reshape(-1)
    out, off = [], 0
    for a in like:
        out.append(flat[off:off + a.size].reshape(a.shape))
        off += a.size
    return out


def _after(token, g):
    return g + token[0, 0]


def _step(w, m, v, x, target):
    x = x[0]
    target = target[0]

    def shard(n, l):
        return w[n][l].astype(bf16)

    def gain(n, l):
        return w[n][l:l + 1]

    wf1a, tok = all_gather("ag_l0f1", [shard("ffn1_" + s, 0) for s in FFN])
    h_s5 = exchange_start("ag_s5", [shard("s5_w_in", 0), shard("s5_w_glu", 0), shard("s5_w_out", 0)], True, tok)
    h_f2a = exchange_start("ag_l0f2", [shard("ffn2_" + s, 0) for s in FFN], True, h_s5["token"])
    h_f1b = exchange_start("ag_l1f1", [shard("ffn1_" + s, 1) for s in FFN], True, h_f2a["token"])
    h_sb = exchange_start("ag_sb", [shard("sb_w_qkv", 0), shard("sb_w_o", 0)], True, h_f1b["token"])
    h_f2b = exchange_start("ag_l1f2", [shard("ffn2_" + s, 1) for s in FFN], True, h_sb["token"])
    tok = h_f2b["token"]

    s5_in, mats = s5_prepare(w["s5_lam_re"][0], w["s5_lam_im"][0], w["s5_log_dt"][0], w["s5_b_re"][0], w["s5_b_im"][0],
                             w["s5_c_re"][0], w["s5_c_im"][0])

    x1, sv_f1a = ffn_fwd("l0f1", x, _after(tok, gain("norm_ffn1", 0)), *wf1a)
    ws5 = exchange_wait("agw_s5", h_s5, x1)
    x2, sv_s5 = s5_fwd(x1, gain("norm_mix", 0), *ws5, mats, w["s5_d"], w["s5_b_glu"])
    wf2a = exchange_wait("agw_l0f2", h_f2a, x2)
    x3, sv_f2a = ffn_fwd("l0f2", x2, gain("norm_ffn2", 0), *wf2a)
    wf1b = exchange_wait("agw_l1f1", h_f1b, x3)
    x4, sv_f1b = ffn_fwd("l1f1", x3, gain("norm_ffn1", 1), *wf1b)
    wsb = exchange_wait("agw_sb", h_sb, x4)
    x5, sv_sb = sb_fwd(x4, gain("norm_mix", 1), *wsb, w["sb_g_q"], w["sb_g_k"])
    wf2b = exchange_wait("agw_l1f2", h_f2b, x5)
    x6, sv_f2b = ffn_fwd("l1f2", x5, gain("norm_ffn2", 1), *wf2b)
    loss_local, dy, dy16 = loss_head(x6, target)
    loss = lax.psum(loss_local, AXES)

    rs = {}

    def sender(key):
        def send(grads, places):
            sent = rs.setdefault(key, [])
            sent.append((exchange_start("rs_%s_%d" % (key, len(sent)), grads, False), places))
            return sent[-1][0]["token"]
        return send

    d5, d5h, dn_f2b = ffn_bwd("l1f2b", dy, dy16, sv_f2b, gain("norm_ffn2", 1), *wf2b, sender("l1f2"))
    d4, d4h, dn_sb, dgq, dgk = sb_bwd(d5, d5h, sv_sb, gain("norm_mix", 1), *wsb, w["sb_g_q"], w["sb_g_k"], sender("sb"))
    d3, d3h, dn_f1b = ffn_bwd("l1f1b", d4, d4h, sv_f1b, gain("norm_ffn1", 1), *wf1b, sender("l1f1"))
    d2, d2h, dn_f2a = ffn_bwd("l0f2b", d3, d3h, sv_f2a, gain("norm_ffn2", 0), *wf2a, sender("l0f2"))
    d1, d1h, dn_s5, s5s = s5_bwd(d2, d2h, sv_s5, gain("norm_mix", 0), *ws5, mats, w["s5_d"], sender("s5"))
    d0, _, dn_f1a = ffn_bwd("l0f1b", d1, d1h, sv_f1a, gain("norm_ffn1", 0), *wf1a, sender("l0f1"), each=True)
    d_lr, d_li, d_ldt, d_brt, d_bit = disc_bwd(*s5_in, s5s["d_ar"].reshape(s5_in[0].shape), s5s["d_ai"].reshape(s5_in[0].shape),
                                               s5s["d_bbr"], s5s["d_bbi"])

    part = {
        "norm_ffn1": jnp.concatenate([dn_f1a, dn_f1b]), "norm_mix": jnp.concatenate([dn_s5, dn_sb]),
        "norm_ffn2": jnp.concatenate([dn_f2a, dn_f2b]),
        "s5_lam_re": d_lr, "s5_lam_im": d_li, "s5_log_dt": d_ldt, "s5_b_re": jnp.swapaxes(d_brt, 1, 2), "s5_b_im": jnp.swapaxes(d_bit, 1, 2),
        "s5_c_re": s5s["d_c_re"], "s5_c_im": s5s["d_c_im"], "s5_d": s5s["d_d"], "s5_b_glu": s5s["d_bglu"], "sb_g_q": dgq, "sb_g_k": dgk,
    }
    h_small = exchange_start("ag_small", [_pack([part[n].reshape(w[n].shape) for n in SMALL])], True)

    res = {}
    late = h_small["token"]
    for key, names in (("l1f2", [("ffn2_" + s, 1) for s in FFN]), ("sb", [("sb_w_qkv", None), ("sb_w_o", None)]),
                       ("l1f1", [("ffn1_" + s, 1) for s in FFN]), ("l0f2", [("ffn2_" + s, 0) for s in FFN]),
                       ("s5", [("s5_w_in", None), ("s5_w_glu", None), ("s5_w_out", None)]), ("small", None),
                       ("l0f1", [("ffn1_" + s, 0) for s in FFN])):
        if key == "small":
            packed = exchange_wait("agw_small", h_small, late)[0]
            outs = adamw("adamw_small", _pack([w[n] for n in SMALL]), _pack([m[n] for n in SMALL]), _pack([v[n] for n in SMALL]), packed)
            un = [_unpack(o, [w[n] for n in SMALL]) for o in outs]
            for i, n in enumerate(SMALL):
                res[n] = [un[k][i] for k in range(4)]
            late = outs[0]
            continue
        for k, (handle, places) in enumerate(rs[key]):
            recv = exchange_wait("rsw_%s_%d" % (key, k), handle, late)
            for place, r in zip(places, recv):
                n, l = names[place]
                if l is None:
                    res[n] = [o[None] for o in adamw("adamw_" + n, w[n][0], m[n][0], v[n][0], r)]
                else:
                    res[n] = adamw_layer("adamw%d_%s" % (l, n), w[n], m[n], v[n], r, l, res.get(n))
                late = res[n][0]

    return (loss, d0[None], *[res[n][0] for n in WEIGHTS], *[res[n][1] for n in WEIGHTS],
            *[res[n][2] for n in WEIGHTS], *[res[n][3] for n in WEIGHTS])


def kernel(x, norm_ffn1, ffn1_w_gate, ffn1_w_up, ffn1_w_down, norm_mix, s5_w_in, s5_lam_re, s5_lam_im, s5_log_dt, s5_b_re, s5_b_im, s5_c_re, s5_c_im, s5_d, s5_w_glu, s5_b_glu, s5_w_out, sb_w_qkv, sb_g_q, sb_g_k, sb_w_o, norm_ffn2, ffn2_w_gate, ffn2_w_up, ffn2_w_down, loss_target, m_norm_ffn1, m_ffn1_w_gate, m_ffn1_w_up, m_ffn1_w_down, m_norm_mix, m_s5_w_in, m_s5_lam_re, m_s5_lam_im, m_s5_log_dt, m_s5_b_re, m_s5_b_im, m_s5_c_re, m_s5_c_im, m_s5_d, m_s5_w_glu, m_s5_b_glu, m_s5_w_out, m_sb_w_qkv, m_sb_g_q, m_sb_g_k, m_sb_w_o, m_norm_ffn2, m_ffn2_w_gate, m_ffn2_w_up, m_ffn2_w_down, v_norm_ffn1, v_ffn1_w_gate, v_ffn1_w_up, v_ffn1_w_down, v_norm_mix, v_s5_w_in, v_s5_lam_re, v_s5_lam_im, v_s5_log_dt, v_s5_b_re, v_s5_b_im, v_s5_c_re, v_s5_c_im, v_s5_d, v_s5_w_glu, v_s5_b_glu, v_s5_w_out, v_sb_w_qkv, v_sb_g_q, v_sb_g_k, v_sb_w_o, v_norm_ffn2, v_ffn2_w_gate, v_ffn2_w_up, v_ffn2_w_down):
    w = dict(norm_ffn1=norm_ffn1, ffn1_w_gate=ffn1_w_gate, ffn1_w_up=ffn1_w_up, ffn1_w_down=ffn1_w_down, norm_mix=norm_mix, s5_w_in=s5_w_in, s5_lam_re=s5_lam_re, s5_lam_im=s5_lam_im, s5_log_dt=s5_log_dt, s5_b_re=s5_b_re, s5_b_im=s5_b_im, s5_c_re=s5_c_re, s5_c_im=s5_c_im, s5_d=s5_d, s5_w_glu=s5_w_glu, s5_b_glu=s5_b_glu, s5_w_out=s5_w_out, sb_w_qkv=sb_w_qkv, sb_g_q=sb_g_q, sb_g_k=sb_g_k, sb_w_o=sb_w_o, norm_ffn2=norm_ffn2, ffn2_w_gate=ffn2_w_gate, ffn2_w_up=ffn2_w_up, ffn2_w_down=ffn2_w_down)
    m = dict(norm_ffn1=m_norm_ffn1, ffn1_w_gate=m_ffn1_w_gate, ffn1_w_up=m_ffn1_w_up, ffn1_w_down=m_ffn1_w_down, norm_mix=m_norm_mix, s5_w_in=m_s5_w_in, s5_lam_re=m_s5_lam_re, s5_lam_im=m_s5_lam_im, s5_log_dt=m_s5_log_dt, s5_b_re=m_s5_b_re, s5_b_im=m_s5_b_im, s5_c_re=m_s5_c_re, s5_c_im=m_s5_c_im, s5_d=m_s5_d, s5_w_glu=m_s5_w_glu, s5_b_glu=m_s5_b_glu, s5_w_out=m_s5_w_out, sb_w_qkv=m_sb_w_qkv, sb_g_q=m_sb_g_q, sb_g_k=m_sb_g_k, sb_w_o=m_sb_w_o, norm_ffn2=m_norm_ffn2, ffn2_w_gate=m_ffn2_w_gate, ffn2_w_up=m_ffn2_w_up, ffn2_w_down=m_ffn2_w_down)
    v = dict(norm_ffn1=v_norm_ffn1, ffn1_w_gate=v_ffn1_w_gate, ffn1_w_up=v_ffn1_w_up, ffn1_w_down=v_ffn1_w_down, norm_mix=v_norm_mix, s5_w_in=v_s5_w_in, s5_lam_re=v_s5_lam_re, s5_lam_im=v_s5_lam_im, s5_log_dt=v_s5_log_dt, s5_b_re=v_s5_b_re, s5_b_im=v_s5_b_im, s5_c_re=v_s5_c_re, s5_c_im=v_s5_c_im, s5_d=v_s5_d, s5_w_glu=v_s5_w_glu, s5_b_glu=v_s5_b_glu, s5_w_out=v_s5_w_out, sb_w_qkv=v_sb_w_qkv, sb_g_q=v_sb_g_q, sb_g_k=v_sb_g_k, sb_w_o=v_sb_w_o, norm_ffn2=v_norm_ffn2, ffn2_w_gate=v_ffn2_w_gate, ffn2_w_up=v_ffn2_w_up, ffn2_w_down=v_ffn2_w_down)
    return _step(w, m, v, x, loss_target)
```

```python
import math

import jax
import jax.numpy as jnp
from jax import lax
from jax.experimental import pallas as pl
from jax.experimental.pallas import tpu as pltpu

f32 = jnp.float32
bf16 = jnp.bfloat16

NDEV = 8
AXES = ("x", "y", "c")
MESH = pl.DeviceIdType.MESH
EPS = 1e-6
HEAD = 128
S5H = 16
S5P = 64
GB = 8
FFN_RES = 0.5
ADAM_LR = 0.001
ADAM_B1 = 0.9
ADAM_B2 = 0.999
ADAM_EPS = 1e-08
ADAM_WD = 0.01
ADAM_STEP = 10
VMEM_LIMIT_V7X = 56 * 2 ** 20

NN = (((1,), (0,)), ((), ()))
NT = (((1,), (1,)), ((), ()))
TNd = (((0,), (0,)), ((), ()))


def _tile(n, target, align):
    if n <= target:
        return n
    t = (target // align) * align
    while t >= align:
        if n % t == 0:
            return t
        t -= align
    return n


def _params(sem):
    return pltpu.CompilerParams(dimension_semantics=sem, vmem_limit_bytes=VMEM_LIMIT_V7X)


def _sds(shape, dtype):
    return jax.ShapeDtypeStruct(tuple(shape), dtype)


def _mm(name, a_list, b_list, a_spec, b_spec, dims, grid, out_shapes, out_specs,
        epilogue=None, extra=(), extra_specs=(), nsub=1, merge_b=False, after=None, separate=False):
    n_a, n_p, n_e = len(a_list), len(b_list), len(extra)
    order = [] if after is None else [after]

    def body(*refs):
        a_refs = refs[:n_a]
        b_refs = refs[n_a:n_a + n_p]
        e_refs = refs[n_a + n_p:n_a + n_p + n_e]
        o_refs = refs[n_a + n_p + n_e + len(order):]
        prods = []
        for p, br in enumerate(b_refs):
            ar = a_refs[p if n_a > 1 else 0]
            if nsub > 1 and not merge_b:
                for j in range(nsub):
                    prods.append(lax.dot_general(ar[j].astype(bf16), br[j].astype(bf16), dims, preferred_element_type=f32))
            else:
                b = br[...]
                if merge_b:
                    b = b.reshape(b.shape[0] * b.shape[1], b.shape[2])
                prods.append(lax.dot_general(ar[...].astype(bf16), b.astype(bf16), dims, preferred_element_type=f32))
        if not separate:
            s = prods[0]
            for d in prods[1:]:
                s = s + d
            prods = [s]
        outs = tuple(prods) if epilogue is None else epilogue(*prods, *[e[...] for e in e_refs])
        for o, val in zip(o_refs, outs):
            o[...] = val.astype(o.dtype)

    return pl.pallas_call(
        body, name=name, grid=grid,
        in_specs=[a_spec] * n_a + [b_spec] * n_p + list(extra_specs) + [pl.BlockSpec(memory_space=pl.ANY)] * len(order),
        out_specs=list(out_specs), out_shape=list(out_shapes),
        compiler_params=_params(("parallel",) * len(grid)),
    )(*a_list, *b_list, *extra, *order)


EW_TILE_ELEMS = 1 << 19
ADAM_TILE_ELEMS = 1 << 21


def _ew(name, fn, rows, bcasts, outs, reds=(), tm=None, budget=EW_TILE_ELEMS):
    n_r, n_b, n_o, n_d = len(rows), len(bcasts), len(outs), len(reds)
    R = rows[0].shape[-2]
    if tm is None:
        widest = max([r.shape[-1] * (r.shape[0] if r.ndim == 3 else 1) for r in rows] + [c for c, _ in outs])
        tm = _tile(R, max(16, budget // widest), 16)

    def body(*refs):
        r = refs[:n_r]
        b = refs[n_r:n_r + n_b]
        o = refs[n_r + n_b:n_r + n_b + n_o]
        d = refs[n_r + n_b + n_o:]
        res = fn(*[x[...] for x in r], *[x[...] for x in b])
        for oo, val in zip(o, res[:n_o]):
            oo[...] = val.astype(oo.dtype)
        if n_d:
            @pl.when(pl.program_id(0) == 0)
            def _():
                for dd in d:
                    dd[...] = jnp.zeros_like(dd)
            for dd, val in zip(d, res[n_o:]):
                dd[...] += val

    in_specs = []
    for x in rows:
        if x.ndim == 2:
            in_specs.append(pl.BlockSpec((tm, x.shape[1]), lambda i: (i, 0)))
        else:
            in_specs.append(pl.BlockSpec((x.shape[0], tm, x.shape[2]), lambda i: (0, i, 0)))
    for x in bcasts:
        in_specs.append(pl.BlockSpec(x.shape, lambda i, nd=x.ndim: (0,) * nd))
    out_shape = [_sds((R, c), dt) for c, dt in outs] + [_sds((1, c), f32) for c in reds]
    out_specs = [pl.BlockSpec((tm, c), lambda i: (i, 0)) for c, _ in outs] + [pl.BlockSpec((1, c), lambda i: (0, 0)) for c in reds]
    return pl.pallas_call(
        body, name=name, grid=(R // tm,), in_specs=in_specs, out_specs=out_specs, out_shape=out_shape,
        compiler_params=_params(("arbitrary",)),
    )(*rows, *bcasts)


def _coords():
    return lax.axis_index("x"), lax.axis_index("y"), lax.axis_index("c")


def _me():
    x, y, c = _coords()
    return 4 * x + 2 * y + c


def _peer(k):
    x, y, c = _coords()
    return x ^ ((k >> 2) & 1), y ^ ((k >> 1) & 1), c ^ (k & 1)


def all_gather(name, shards):
    n = len(shards)

    def body(*refs):
        x_refs, out_refs, token = refs[:n], refs[n:2 * n], refs[2 * n]
        send_sems, recv_sems, local_sems = refs[2 * n + 1:]
        token[...] = jnp.zeros_like(token)
        x, y, c = _coords()
        me, sibling = (x, y, c), (x, y, 1 - c)
        chips = [(1 - x, y), (x, 1 - y), (1 - x, 1 - y)]

        def rows(a, px, py, pc):
            return out_refs[a].at[4 * px + 2 * py + pc]

        def copy(a, k, block, to, src=None):
            return pltpu.make_async_remote_copy(
                src_ref=rows(a, *block) if src is None else src, dst_ref=rows(a, *block),
                send_sem=send_sems.at[a, k], recv_sem=recv_sems.at[a, k],
                device_id=to, device_id_type=MESH)

        mine = [pltpu.make_async_copy(x_refs[a], rows(a, *me), local_sems.at[a]) for a in range(n)]
        for cp in mine:
            cp.start()
        first = []
        for a in range(n):
            first.append(copy(a, 0, me, sibling, src=x_refs[a]))
            first += [copy(a, 1 + j, me, (*chip, c), src=x_refs[a]) for j, chip in enumerate(chips)]
        for cp in first:
            cp.start()
        passed = []
        for j, chip in enumerate(chips):
            for a in range(n):
                copy(a, 1 + j, (*chip, c), me).wait_recv()
                cp = copy(a, 4 + j, (*chip, c), sibling)
                cp.start()
                passed.append(cp)
        for a in range(n):
            copy(a, 0, sibling, me).wait_recv()
            for j, chip in enumerate(chips):
                copy(a, 4 + j, (*chip, 1 - c), me).wait_recv()
        for cp in first + passed:
            cp.wait_send()
        for cp in mine:
            cp.wait()

    anyspec = pl.BlockSpec(memory_space=pl.ANY)
    outs = pl.pallas_call(
        body, name=name,
        out_shape=[_sds((NDEV,) + s.shape, s.dtype) for s in shards] + [_sds((8, 128), f32)],
        in_specs=[anyspec] * n, out_specs=[anyspec] * n + [pl.BlockSpec(memory_space=pltpu.VMEM)],
        scratch_shapes=[pltpu.SemaphoreType.DMA((n, 7)), pltpu.SemaphoreType.DMA((n, 7)), pltpu.SemaphoreType.DMA((n,))],
    )(*shards)
    return list(outs[:n]), outs[n]


def to_bf16(name, w3, l):
    _, R, C = w3.shape
    tm = _tile(R, max(16, 2 * EW_TILE_ELEMS // C), 16)

    def body(x_ref, o_ref):
        o_ref[...] = x_ref[...].astype(bf16)

    return pl.pallas_call(
        body, name=name, grid=(R // tm,), in_specs=[pl.BlockSpec((None, tm, C), lambda i: (l, i, 0))],
        out_specs=pl.BlockSpec((tm, C), lambda i: (i, 0)), out_shape=_sds((R, C), bf16),
        compiler_params=_params(("parallel",)))(w3)


HBM_SPEC = pl.BlockSpec(memory_space=pltpu.HBM)
SEM_SPEC = pl.BlockSpec(memory_space=pltpu.SEMAPHORE)
EFFECT = pltpu.SideEffectType.DATAFLOW_SIDE_EFFECTING


def exchange_start(name, srcs, gather, after=None):
    n = len(srcs)
    order = [] if after is None else [after]
    ns = n * (NDEV - 1)
    n_sem = 2 * ns + n
    lands = [lax.empty(((NDEV,) + s.shape) if gather else s.shape, s.dtype) for s in srcs]

    def body(*refs):
        src_refs, land_refs = refs[:n], refs[n:2 * n]
        first = 2 * n + len(order)
        send_sems, recv_sems = refs[first:first + ns], refs[first + ns:first + 2 * ns]
        local_sems = refs[first + 2 * ns:first + n_sem]
        token = refs[-1]
        me_ = _me()
        for k in range(1, NDEV):
            px, py, pc = _peer(k)
            p = 4 * px + 2 * py + pc
            for a in range(n):
                i = a * (NDEV - 1) + k - 1
                pltpu.make_async_remote_copy(
                    src_ref=src_refs[a] if gather else src_refs[a].at[p], dst_ref=land_refs[a].at[me_],
                    send_sem=send_sems[i], recv_sem=recv_sems[i],
                    device_id=(px, py, pc), device_id_type=MESH).start()
        for a in range(n):
            pltpu.make_async_copy(src_refs[a] if gather else src_refs[a].at[me_], land_refs[a].at[me_], local_sems[a]).start()
        token[...] = jnp.zeros_like(token)

    outs = pl.pallas_call(
        body, name=name,
        out_shape=(*[pltpu.SemaphoreType.DMA(())] * n_sem,
                   *[pltpu.HBM(s.shape, s.dtype) for s in srcs], *[pltpu.HBM(l.shape, l.dtype) for l in lands],
                   _sds((8, 128), f32)),
        in_specs=[HBM_SPEC] * (2 * n) + [pl.BlockSpec(memory_space=pl.ANY)] * len(order),
        out_specs=(*[SEM_SPEC] * n_sem, *[HBM_SPEC] * (2 * n), pl.BlockSpec(memory_space=pltpu.VMEM)),
        input_output_aliases={i: n_sem + i for i in range(2 * n)},
        compiler_params=pltpu.CompilerParams(has_side_effects=EFFECT),
    )(*[pltpu.with_memory_space_constraint(s, pltpu.HBM) for s in srcs],
      *[pltpu.with_memory_space_constraint(l, pltpu.HBM) for l in lands], *order)
    return dict(n=n, sems=outs[:n_sem], srcs=outs[n_sem:n_sem + n], lands=outs[n_sem + n:n_sem + 2 * n], token=outs[-1], gather=gather)


def exchange_wait(name, hd, after):
    n, gather = hd["n"], hd["gather"]
    ns = n * (NDEV - 1)

    def body(*refs):
        src_refs, land_refs = refs[:n], refs[n:2 * n]
        send_sems, recv_sems = refs[2 * n:2 * n + ns], refs[2 * n + ns:2 * n + 2 * ns]
        local_sems = refs[2 * n + 2 * ns:2 * n + 2 * ns + n]
        x, y, c = _coords()
        me_ = _me()
        for a in range(n):
            pltpu.make_async_copy(src_refs[a] if gather else src_refs[a].at[me_], land_refs[a].at[me_], local_sems[a]).wait()
        for k in range(1, NDEV):
            px, py, pc = _peer(k)
            p = 4 * px + 2 * py + pc
            for a in range(n):
                i = a * (NDEV - 1) + k - 1
                cp = pltpu.make_async_remote_copy(
                    src_ref=src_refs[a] if gather else src_refs[a].at[p], dst_ref=land_refs[a].at[p],
                    send_sem=send_sems[i], recv_sem=recv_sems[i],
                    device_id=(x, y, 1 - c), device_id_type=MESH)
                cp.wait_send()
                cp.wait_recv()

    outs = pl.pallas_call(
        body, name=name,
        out_shape=tuple(pltpu.HBM(s.shape, s.dtype) for s in list(hd["srcs"]) + list(hd["lands"])),
        in_specs=[HBM_SPEC] * (2 * n) + [SEM_SPEC] * (2 * ns + n) + [pl.BlockSpec(memory_space=pl.ANY)],
        out_specs=tuple([HBM_SPEC] * (2 * n)),
        input_output_aliases={i: i for i in range(2 * n)},
        compiler_params=pltpu.CompilerParams(has_side_effects=EFFECT),
    )(*hd["srcs"], *hd["lands"], *hd["sems"], after)
    return list(outs[n:])


def _adam_math(w_, m_, v_, r_):
    g = r_[0].astype(f32)
    for j in range(1, NDEV):
        g = g + r_[j].astype(f32)
    m2 = ADAM_B1 * m_ + (1.0 - ADAM_B1) * g
    v2 = ADAM_B2 * v_ + (1.0 - ADAM_B2) * jnp.square(g)
    m_hat = m2 / (1.0 - ADAM_B1 ** ADAM_STEP)
    v_hat = v2 / (1.0 - ADAM_B2 ** ADAM_STEP)
    delta = -ADAM_LR * (m_hat / (jnp.sqrt(v_hat) + ADAM_EPS) + ADAM_WD * w_)
    return g, delta, m2, v2


def adamw(name, w, m, v, recv):
    return _ew(name, _adam_math, [w, m, v, recv], [], [(w.shape[1], f32)] * 4, budget=ADAM_TILE_ELEMS)


def adamw_layer(name, w3, m3, v3, recv, l, prev):
    L, R, C = w3.shape
    tm = _tile(R, max(16, ADAM_TILE_ELEMS // (NDEV * C)), 16)
    n_prev = 0 if prev is None else 4

    def body(*refs):
        w_ref, m_ref, v_ref, r_ref = refs[:4]
        o_refs = refs[4 + n_prev:]
        res = _adam_math(w_ref[...], m_ref[...], v_ref[...], r_ref[...])
        for o, val in zip(o_refs, res):
            o[...] = val

    lay = pl.BlockSpec((None, tm, C), lambda i: (l, i, 0))
    return pl.pallas_call(
        body, name=name, grid=(R // tm,),
        in_specs=[lay, lay, lay, pl.BlockSpec((NDEV, tm, C), lambda i: (0, i, 0))] + [pl.BlockSpec(memory_space=pl.ANY)] * n_prev,
        out_specs=[lay] * 4, out_shape=[_sds((L, R, C), f32)] * 4,
        input_output_aliases={4 + i: i for i in range(n_prev)},
        compiler_params=_params(("parallel",)),
    )(w3, m3, v3, recv, *(prev or []))


def rmsnorm_fwd(name, x, g):
    D = x.shape[1]

    def fn(x_, g_):
        r = lax.rsqrt(jnp.mean(x_ * x_, axis=-1, keepdims=True) + EPS)
        return ((x_ * r) * g_,)

    return _ew(name, fn, [x], [g], [(D, bf16)])[0]


def rmsnorm_bwd(name, x, g, dh, dres):
    D = x.shape[1]

    def fn(x_, dh_, dres_, g_):
        r = lax.rsqrt(jnp.mean(x_ * x_, axis=-1, keepdims=True) + EPS)
        xh = x_ * r
        dxh = dh_ * g_
        dx = dres_ + r * (dxh - xh * jnp.mean(dxh * xh, axis=-1, keepdims=True))
        return dx, dx, jnp.sum(dh_ * xh, axis=0, keepdims=True)

    return _ew(name, fn, [x, dh, dres], [g], [(D, f32), (D, bf16)], [D])


def _silu_parts(a):
    sig = jax.nn.sigmoid(a)
    return sig, a * sig


def mm_cols(name, h, w3_list, epilogue=None, extra=(), outs=None):
    T, K = h.shape
    Nb = w3_list[0].shape[2]
    tm = _tile(T, 1024, 16)
    outs = outs or [f32]
    blk = pl.BlockSpec((None, tm, Nb), lambda j, i: (j, i, 0))
    return _mm(name, [h], w3_list, pl.BlockSpec((tm, K), lambda j, i: (i, 0)), pl.BlockSpec((None, K, Nb), lambda j, i: (j, 0, 0)),
               NN, (NDEV, T // tm), [_sds((NDEV, T, Nb), dt) for dt in outs], [blk] * len(outs), epilogue, extra, [blk] * len(extra),
               separate=True)


def mm_rows(name, p, w3, epilogue=None, extra=(), outs=None):
    _, T, Kb = p.shape
    N = w3.shape[2]
    tm, tn = _tile(T, 512, 16), _tile(N, 512, 128)
    outs = outs or [f32]
    blk = pl.BlockSpec((tm, tn), lambda i, n: (i, n))
    return _mm(name, [p], [w3], pl.BlockSpec((NDEV, tm, Kb), lambda i, n: (0, i, 0)), pl.BlockSpec((NDEV, Kb, tn), lambda i, n: (0, 0, n)),
               NN, (T // tm, N // tn), [_sds((T, N), dt) for dt in outs], [blk] * len(outs), epilogue, extra, [blk] * len(extra), nsub=NDEV)


def mm_rows_plain(name, a, w3, epilogue=None, extra=(), extra_specs=None, outs=None):
    T, K = a.shape
    Kb, N = w3.shape[1], w3.shape[2]
    tm, tn = _tile(T, 512, 16), _tile(N, 1024, 128)
    outs = outs or [f32]
    blk = pl.BlockSpec((tm, tn), lambda i, n: (i, n))
    return _mm(name, [a], [w3], pl.BlockSpec((tm, K), lambda i, n: (i, 0)), pl.BlockSpec((NDEV, Kb, tn), lambda i, n: (0, 0, n)),
               NN, (T // tm, N // tn), [_sds((T, N), dt) for dt in outs], [blk] * len(outs), epilogue, extra,
               extra_specs or [blk] * len(extra), nsub=NDEV, merge_b=True)


def mm_t_rows(name, d, w3, epilogue=None, extra=(), outs=None, after=None):
    T, K = d.shape
    Nb = w3.shape[1]
    tm = _tile(T, 512, 16)
    outs = outs or [f32]
    blk = pl.BlockSpec((None, tm, Nb), lambda j, i: (j, i, 0))
    return _mm(name, [d], [w3], pl.BlockSpec((tm, K), lambda j, i: (i, 0)), pl.BlockSpec((None, Nb, K), lambda j, i: (j, 0, 0)),
               NT, (NDEV, T // tm), [_sds((NDEV, T, Nb), dt) for dt in outs], [blk] * len(outs), epilogue, extra, [blk] * len(extra),
               after=after)


def mm_t_rows_plain(name, d, w3, epilogue=None, extra=(), outs=None, after=None):
    T, K = d.shape
    Nb = w3.shape[1]
    tm = _tile(T, 256, 16)
    outs = outs or [f32]
    blk = pl.BlockSpec((tm, NDEV * Nb), lambda i: (i, 0))
    return _mm(name, [d], [w3], pl.BlockSpec((tm, K), lambda i: (i, 0)), pl.BlockSpec((NDEV, Nb, K), lambda i: (0, 0, 0)),
               NT, (T // tm,), [_sds((T, NDEV * Nb), dt) for dt in outs], [blk] * len(outs), epilogue, extra, [blk] * len(extra),
               nsub=NDEV, merge_b=True, after=after)


def mm_t_cols(name, d_list, w3_list, epilogue=None, extra=(), outs=None, after=None):
    _, T, Kb = d_list[0].shape
    N = w3_list[0].shape[1]
    tm, tn = _tile(T, 512, 16), _tile(N, 256, 128)
    outs = outs or [f32]
    blk = pl.BlockSpec((tm, tn), lambda i, n: (i, n))
    return _mm(name, d_list, w3_list, pl.BlockSpec((NDEV, tm, Kb), lambda i, n: (0, i, 0)), pl.BlockSpec((NDEV, tn, Kb), lambda i, n: (0, n, 0)),
               NT, (T // tm, N // tn), [_sds((T, N), dt) for dt in outs], [blk] * len(outs), epilogue, extra, [blk] * len(extra), nsub=NDEV,
               after=after)


def mm_grad_rows(name, p, d, scale=1.0):
    _, T, Mb = p.shape
    N = d.shape[1]
    tn = _tile(N, 512, 128)
    return _mm(name, [p], [d], pl.BlockSpec((None, T, Mb), lambda j, n: (j, 0, 0)), pl.BlockSpec((T, tn), lambda j, n: (0, n)),
               TNd, (NDEV, N // tn), [_sds((NDEV, Mb, N), bf16)], [pl.BlockSpec((None, Mb, tn), lambda j, n: (j, 0, n))],
               (lambda acc: (acc * scale,)))[0]


def mm_grad_rows_plain(name, a, d):
    T, M = a.shape
    N = d.shape[1]
    tm, tn = _tile(M, 512, 128), _tile(N, 512, 128)
    g = _mm(name, [a], [d], pl.BlockSpec((T, tm), lambda m, n: (0, m)), pl.BlockSpec((T, tn), lambda m, n: (0, n)),
            TNd, (M // tm, N // tn), [_sds((M, N), bf16)], [pl.BlockSpec((tm, tn), lambda m, n: (m, n))])[0]
    return g.reshape(NDEV, M // NDEV, N)


def mm_grad_cols(name, h, d):
    T, M = h.shape
    Nb = d.shape[2]
    tm = _tile(M, 512, 128)
    return _mm(name, [h], [d], pl.BlockSpec((T, tm), lambda j, m: (0, m)), pl.BlockSpec((None, T, Nb), lambda j, m: (j, 0, 0)),
               TNd, (NDEV, M // tm), [_sds((NDEV, M, Nb), bf16)], [pl.BlockSpec((None, tm, Nb), lambda j, m: (j, m, 0))])[0]


def ffn_fwd(tag, x, g, wg3, wu3, wd3):
    h = rmsnorm_fwd(tag + "_norm", x, g)

    def gate_up_epi(a_, b_):
        _, sl = _silu_parts(a_)
        return a_, b_, sl * b_

    a, b, p = mm_cols(tag + "_gateup", h, [wg3, wu3], gate_up_epi, outs=[bf16, bf16, bf16])
    xo = mm_rows(tag + "_down", p, wd3, lambda acc, x_: (x_ + FFN_RES * acc,), [x])[0]
    return xo, (x, h, a, b, p)


def ffn_bwd(tag, dxo, dxo16, saved, g, wg3, wu3, wd3, send, each=False):
    x, h, a, b, p = saved

    def dp_epi(acc, a_, b_):
        dp = FFN_RES * acc
        a_, b_ = a_.astype(f32), b_.astype(f32)
        sig, sl = _silu_parts(a_)
        return dp * b_ * (sig * (1.0 + a_ * (1.0 - sig))), dp * sl

    da, db = mm_t_rows(tag + "_dp", dxo16, wd3, dp_epi, [a, b], [bf16, bf16])
    g_wd = mm_grad_rows(tag + "_gwd", p, dxo16, FFN_RES)
    if each:
        send([g_wd], [2])
    g_wg = mm_grad_cols(tag + "_gwg", h, da)
    if each:
        send([g_wg], [0])
    g_wu = mm_grad_cols(tag + "_gwu", h, db)
    tok = send([g_wu], [1]) if each else send([g_wg, g_wu, g_wd], [0, 1, 2])
    dh = mm_t_cols(tag + "_dh", [da, db], [wg3, wu3], after=tok)[0]
    return rmsnorm_bwd(tag + "_dnorm", x, g, dh, dxo)


def _disc(lr, li, ldt, brt, bit):
    dt = jnp.exp(ldt)
    lr = jnp.minimum(lr, -1e-4)
    mag = jnp.exp(lr * dt)
    ab_re = mag * jnp.cos(li * dt)
    ab_im = mag * jnp.sin(li * dt)
    den = lr * lr + li * li
    n_re = ab_re - 1.0
    f_re = (n_re * lr + ab_im * li) / den
    f_im = (ab_im * lr - n_re * li) / den
    bb_re = f_re[:, None, :] * brt - f_im[:, None, :] * bit
    bb_im = f_re[:, None, :] * bit + f_im[:, None, :] * brt
    return ab_re, ab_im, bb_re, bb_im


def disc_fwd(lr, li, ldt, brt, bit):
    def body(lr_ref, li_ref, ldt_ref, brt_ref, bit_ref, ar_ref, ai_ref, bbr_ref, bbi_ref):
        ar, ai, bbr, bbi = _disc(lr_ref[...], li_ref[...], ldt_ref[...], brt_ref[...], bit_ref[...])
        ar_ref[...] = ar
        ai_ref[...] = ai
        bbr_ref[...] = bbr
        bbi_ref[...] = bbi

    return pl.pallas_call(body, name="s5_disc", out_shape=[_sds(lr.shape, f32), _sds(lr.shape, f32), _sds(brt.shape, f32), _sds(brt.shape, f32)],
                          compiler_params=pltpu.CompilerParams(vmem_limit_bytes=VMEM_LIMIT_V7X))(lr, li, ldt, brt, bit)


def disc_bwd(lr, li, ldt, brt, bit, d_ar, d_ai, d_bbr, d_bbi):
    def body(lr_ref, li_ref, ldt_ref, brt_ref, bit_ref, dar_ref, dai_ref, dbbr_ref, dbbi_ref, o_lr, o_li, o_ldt, o_brt, o_bit):
        _, vjp = jax.vjp(_disc, lr_ref[...], li_ref[...], ldt_ref[...], brt_ref[...], bit_ref[...])
        g = vjp((dar_ref[...], dai_ref[...], dbbr_ref[...], dbbi_ref[...]))
        o_lr[...] = g[0]
        o_li[...] = g[1]
        o_ldt[...] = g[2]
        o_brt[...] = g[3]
        o_bit[...] = g[4]

    return pl.pallas_call(body, name="s5_disc_bwd",
                          out_shape=[_sds(lr.shape, f32), _sds(lr.shape, f32), _sds(ldt.shape, f32), _sds(brt.shape, f32), _sds(brt.shape, f32)],
                          compiler_params=pltpu.CompilerParams(vmem_limit_bytes=VMEM_LIMIT_V7X))(lr, li, ldt, brt, bit, d_ar, d_ai, d_bbr, d_bbi)


def _blockdiag(m):
    G, R, C = m.shape
    eye = jnp.eye(GB, dtype=m.dtype)
    m5 = m.reshape(G // GB, GB, R, 1, C) * eye[None, :, None, :, None]
    return m5.reshape(G // GB, GB * R, GB * C)


def _diag_extract(M, R, C):
    nb = M.shape[0]
    eye = jnp.eye(GB, dtype=M.dtype)
    m5 = M.reshape(nb, GB, R, GB, C) * eye[None, :, None, :, None]
    return m5.sum(axis=3).reshape(nb * GB, R, C)


def mm_blockdiag(name, a_list, bd_list, dims, epilogue=None, extra=(), extra_bcast=(), outs=None):
    T = a_list[0].shape[0]
    nb, r, c = bd_list[0].shape
    ra = a_list[0].shape[1] // nb
    ca = c if dims == NN else r
    n_p, n_e, n_b = len(a_list), len(extra), len(extra_bcast)
    outs = outs or [f32]
    tm = _tile(T, 128, 16)

    def body(*refs):
        a_refs, b_refs = refs[:n_p], refs[n_p:2 * n_p]
        e_refs = refs[2 * n_p:2 * n_p + n_e + n_b]
        o_refs = refs[2 * n_p + n_e + n_b:]
        for b in range(nb):
            s = None
            for ar, br in zip(a_refs, b_refs):
                d = lax.dot_general(ar[:, b * ra:(b + 1) * ra].astype(bf16), br[b], dims, preferred_element_type=f32)
                s = d if s is None else s + d
            cols = slice(b * ca, (b + 1) * ca)
            vals = (s,) if epilogue is None else epilogue(s, *[e[:, cols] for e in e_refs])
            for o, val in zip(o_refs, vals):
                o[:, cols] = val.astype(o.dtype)

    row = lambda w: pl.BlockSpec((tm, w), lambda i: (i, 0))
    return pl.pallas_call(
        body, name=name, grid=(T // tm,),
        in_specs=[row(nb * ra)] * n_p + [pl.BlockSpec((nb, r, c), lambda i: (0, 0, 0))] * n_p + [row(nb * ca)] * n_e
        + [pl.BlockSpec((1, nb * ca), lambda i: (0, 0))] * n_b,
        out_specs=[row(nb * ca)] * len(outs), out_shape=[_sds((T, nb * ca), dt) for dt in outs],
        compiler_params=_params(("parallel",)),
    )(*a_list, *bd_list, *extra, *extra_bcast)


def mm_blockdiag_grad(name, a, d, nb):
    T = a.shape[0]
    ra, rd = a.shape[1] // nb, d.shape[1] // nb
    tk = _tile(T, 256, 16)

    def body(a_ref, d_ref, o_ref):
        @pl.when(pl.program_id(0) == 0)
        def _():
            o_ref[...] = jnp.zeros_like(o_ref)

        for b in range(nb):
            o_ref[b] += lax.dot_general(a_ref[:, b * ra:(b + 1) * ra].astype(bf16), d_ref[:, b * rd:(b + 1) * rd].astype(bf16),
                                        TNd, preferred_element_type=f32)

    return pl.pallas_call(
        body, name=name, grid=(T // tk,),
        in_specs=[pl.BlockSpec((tk, nb * ra), lambda k: (k, 0)), pl.BlockSpec((tk, nb * rd), lambda k: (k, 0))],
        out_specs=pl.BlockSpec((nb, ra, rd), lambda k: (0, 0, 0)), out_shape=_sds((nb, ra, rd), f32),
        compiler_params=_params(("arbitrary",)),
    )(a, d)


SUB = 8
SCAN_LANES = 1024
SCAN_GROUPS = 64


def s5_tables(ar, ai):
    N = ar.shape[1]

    def body(ar_ref, ai_ref, f_ref, b_ref):
        r, i = ar_ref[...], ai_ref[...]
        pw = [(r, i)]
        for _ in range(SUB - 1):
            pr, pi = pw[-1]
            pw.append((pr * r - pi * i, pr * i + pi * r))
        row = lax.broadcasted_iota(jnp.int32, (SUB, N), 0)
        for t, k in enumerate((1, 2, 4)):
            pr, pi = pw[k - 1]
            f_ref[2 * t] = jnp.where(row >= k, pr, 0.0)
            f_ref[2 * t + 1] = jnp.where(row >= k, pi, 0.0)
            b_ref[2 * t] = jnp.where(row <= SUB - 1 - k, pr, 0.0)
            b_ref[2 * t + 1] = jnp.where(row <= SUB - 1 - k, -pi, 0.0)
        fr = fi = br = bi = jnp.zeros((SUB, N), f32)
        for j in range(SUB):
            fr, fi = jnp.where(row == j, pw[j][0], fr), jnp.where(row == j, pw[j][1], fi)
            br, bi = jnp.where(row == j, pw[SUB - 1 - j][0], br), jnp.where(row == j, -pw[SUB - 1 - j][1], bi)
        f_ref[6], f_ref[7] = fr, fi
        b_ref[6], b_ref[7] = br, bi

    return pl.pallas_call(body, name="s5_tables", out_shape=[_sds((8, SUB, N), f32)] * 2,
                          compiler_params=pltpu.CompilerParams(vmem_limit_bytes=VMEM_LIMIT_V7X))(ar, ai)


def _group_scan(xr, xi, m, cr, ci, up):
    for t, k in enumerate((1, 2, 4)):
        shift = SUB - k if up else k
        pr, pi = pltpu.roll(xr, shift, 0), pltpu.roll(xi, shift, 0)
        xr, xi = xr + m[2 * t] * pr - m[2 * t + 1] * pi, xi + m[2 * t] * pi + m[2 * t + 1] * pr
    return xr + m[6] * cr - m[7] * ci, xi + m[6] * ci + m[7] * cr


def s5_scan_fwd(bu_re, bu_im, tab):
    T8, _, N = bu_re.shape
    W, tb = _tile(N, SCAN_LANES, 128), _tile(T8, SCAN_GROUPS, 1)

    def body(bur_ref, bui_ref, tab_ref, sr_ref, si_ref, st):
        @pl.when(pl.program_id(1) == 0)
        def _():
            st[...] = jnp.zeros_like(st)

        def step(g, c):
            m = [tab_ref[t] for t in range(8)]
            xr, xi = _group_scan(bur_ref[g], bui_ref[g], m, c[0], c[1], False)
            sr_ref[g] = xr
            si_ref[g] = xi
            return jnp.broadcast_to(xr[SUB - 1:SUB], (SUB, W)), jnp.broadcast_to(xi[SUB - 1:SUB], (SUB, W))

        cr, ci = lax.fori_loop(0, tb, step, (st[0], st[1]))
        st[0] = cr
        st[1] = ci

    blk = pl.BlockSpec((tb, SUB, W), lambda j, i: (i, 0, j))
    tabs = pl.BlockSpec((8, SUB, W), lambda j, i: (0, 0, j))
    return pl.pallas_call(body, name="s5_scan", grid=(N // W, T8 // tb), in_specs=[blk, blk, tabs], out_specs=[blk, blk],
                          out_shape=[_sds(bu_re.shape, f32)] * 2, scratch_shapes=[pltpu.VMEM((2, SUB, W), f32)],
                          compiler_params=_params(("parallel", "arbitrary")))(bu_re, bu_im, tab)


def s5_scan_bwd(ds_re, ds_im, s_re, s_im, tab):
    T8, _, N = ds_re.shape
    W, tb = _tile(N, SCAN_LANES, 128), _tile(T8, SCAN_GROUPS // 2, 1)
    nblk = T8 // tb

    def body(dsr_ref, dsi_ref, sr_ref, si_ref, tab_ref, qr_ref, qi_ref, dar_ref, dai_ref, st):
        @pl.when(pl.program_id(1) == 0)
        def _():
            st[...] = jnp.zeros_like(st)
            dar_ref[...] = jnp.zeros_like(dar_ref)
            dai_ref[...] = jnp.zeros_like(dai_ref)

        last_row = lax.broadcasted_iota(jnp.int32, (SUB, W), 0) == SUB - 1

        def step(i, c):
            cr, ci, dar, dai = c
            g = tb - 1 - i
            m = [tab_ref[t] for t in range(8)]
            xr, xi = _group_scan(dsr_ref[g], dsi_ref[g], m, cr, ci, True)
            qr_ref[g] = xr
            qi_ref[g] = xi
            nr = jnp.where(last_row, cr, pltpu.roll(xr, SUB - 1, 0))
            ni = jnp.where(last_row, ci, pltpu.roll(xi, SUB - 1, 0))
            sr, si = sr_ref[g], si_ref[g]
            dar = dar + nr * sr + ni * si
            dai = dai + ni * sr - nr * si
            return jnp.broadcast_to(xr[0:1], (SUB, W)), jnp.broadcast_to(xi[0:1], (SUB, W)), dar, dai

        cr, ci, dar, dai = lax.fori_loop(0, tb, step, (st[0], st[1], dar_ref[...], dai_ref[...]))
        st[0] = cr
        st[1] = ci
        dar_ref[...] = dar
        dai_ref[...] = dai

        @pl.when(pl.program_id(1) == nblk - 1)
        def _():
            dar_ref[...] = jnp.broadcast_to(jnp.sum(dar, axis=0, keepdims=True), (SUB, W))
            dai_ref[...] = jnp.broadcast_to(jnp.sum(dai, axis=0, keepdims=True), (SUB, W))

    blk = pl.BlockSpec((tb, SUB, W), lambda j, i: (nblk - 1 - i, 0, j))
    tabs = pl.BlockSpec((8, SUB, W), lambda j, i: (0, 0, j))
    acc = pl.BlockSpec((SUB, W), lambda j, i: (0, j))
    return pl.pallas_call(body, name="s5_scan_bwd", grid=(N // W, nblk), in_specs=[blk, blk, blk, blk, tabs], out_specs=[blk, blk, acc, acc],
                          out_shape=[_sds(ds_re.shape, f32)] * 2 + [_sds((SUB, N), f32)] * 2, scratch_shapes=[pltpu.VMEM((2, SUB, W), f32)],
                          compiler_params=_params(("parallel", "arbitrary")))(ds_re, ds_im, s_re, s_im, tab)


def _gelu_grad(y):
    c = math.sqrt(2.0 / math.pi)
    th = jnp.tanh(c * (y + 0.044715 * (y * y * y)))
    return 0.5 * (1.0 + th) + 0.5 * y * (1.0 - th * th) * c * (1.0 + 3.0 * 0.044715 * (y * y))


def s5_prepare(lam_re, lam_im, log_dt, b_re, b_im, c_re, c_im):
    G = lam_re.shape[0]
    ldt = log_dt.reshape(G, 1)
    brt, bit = jnp.swapaxes(b_re, 1, 2), jnp.swapaxes(b_im, 1, 2)
    ar, ai, bbr, bbi = disc_fwd(lam_re, lam_im, ldt, brt, bit)
    tab_f, tab_b = s5_tables(ar.reshape(1, -1), ai.reshape(1, -1))
    mats = dict(
        bbd_re=_blockdiag(bbr).astype(bf16), bbd_im=_blockdiag(bbi).astype(bf16),
        cd_re=_blockdiag(jnp.swapaxes(c_re, 1, 2)).astype(bf16),
        cd_imn=_blockdiag(-jnp.swapaxes(c_im, 1, 2)).astype(bf16),
        tab_f=tab_f, tab_b=tab_b)
    return (lam_re, lam_im, ldt, brt, bit), mats


def s5_fwd(x, gm, w_in3, w_glu3, w_out3, mats, d_skip, b_glu):
    T, D = x.shape
    N = mats["tab_f"].shape[2]
    h = rmsnorm_fwd("s5_norm", x, gm)
    u = mm_rows_plain("s5_in", h, w_in3)[0]
    bu_re = mm_blockdiag("s5_bu_re", [u], [mats["bbd_re"]], NN)[0]
    bu_im = mm_blockdiag("s5_bu_im", [u], [mats["bbd_im"]], NN)[0]
    grouped = (T // SUB, SUB, N)
    s_re, s_im = s5_scan_fwd(bu_re.reshape(grouped), bu_im.reshape(grouped), mats["tab_f"])
    s_re, s_im = s_re.reshape(T, N), s_im.reshape(T, N)

    def y_epi(acc, u_, d_):
        y2 = acc + d_ * u_
        return y2, jax.nn.gelu(y2)

    y2, gl = mm_blockdiag("s5_y", [s_re, s_im], [mats["cd_re"], mats["cd_imn"]], NN, y_epi, [u], [d_skip], [f32, bf16])

    def glu_epi(acc, y2_, b_):
        zg = acc + b_
        return zg, jax.nn.gelu(y2_) * jax.nn.sigmoid(zg)

    tm, tn = _tile(T, 512, 16), _tile(D, 1024, 128)
    zg, o = mm_rows_plain("s5_glu", gl, w_glu3, glu_epi, [y2, b_glu],
                          [pl.BlockSpec((tm, tn), lambda i, n: (i, n)), pl.BlockSpec((1, tn), lambda i, n: (0, n))], [f32, bf16])
    xo = mm_rows_plain("s5_out", o, w_out3, lambda acc, x_: (x_ + acc,), [x])[0]
    return xo, (x, h, u, s_re, s_im, y2, gl, zg, o)


def s5_bwd(dxo, dxo16, saved, gm, w_in3, w_glu3, w_out3, mats, d_skip, send):
    x, h, u, s_re, s_im, y2, gl, zg, o = saved
    T, D = x.shape
    N = mats["tab_b"].shape[2]
    nb = mats["cd_re"].shape[0]

    def do_epi(acc, y2_, zg_):
        sg = jax.nn.sigmoid(zg_)
        return acc * sg, acc * jax.nn.gelu(y2_) * (sg * (1.0 - sg))

    dgl_direct, dzg = mm_t_rows_plain("s5_do", dxo16, w_out3, do_epi, [y2, zg], [f32, f32])
    g_wout = mm_grad_rows_plain("s5_gwout", o, dxo16)
    g_wglu = mm_grad_rows_plain("s5_gwglu", gl, dzg)
    dy2 = mm_t_rows_plain("s5_dgl", dzg, w_glu3, lambda acc, dd_, y2_: ((acc + dd_) * _gelu_grad(y2_),), [dgl_direct, y2])[0]

    def red_fn(dy2_, u_, dzg_, d_):
        return dy2_ * d_, jnp.sum(dy2_ * u_, axis=0, keepdims=True), jnp.sum(dzg_, axis=0, keepdims=True)

    du_direct, dd, dbglu = _ew("s5_dskip", red_fn, [dy2, u, dzg], [d_skip], [(D, f32)], [D, D])
    ds_re = mm_blockdiag("s5_ds_re", [dy2], [mats["cd_re"]], NT)[0]
    ds_im = mm_blockdiag("s5_ds_im", [dy2], [mats["cd_imn"]], NT)[0]
    d_cd_re = mm_blockdiag_grad("s5_gc_re", s_re, dy2, nb)
    d_cd_imn = mm_blockdiag_grad("s5_gc_im", s_im, dy2, nb)
    sh = (T // SUB, SUB, N)
    q_re, q_im, d_ar, d_ai = s5_scan_bwd(ds_re.reshape(sh), ds_im.reshape(sh), s_re.reshape(sh), s_im.reshape(sh), mats["tab_b"])
    q_re, q_im, d_ar, d_ai = q_re.reshape(T, N), q_im.reshape(T, N), d_ar[0], d_ai[0]
    d_bbd_re = mm_blockdiag_grad("s5_gb_re", u, q_re, nb)
    d_bbd_im = mm_blockdiag_grad("s5_gb_im", u, q_im, nb)
    du = mm_blockdiag("s5_du", [q_re, q_im], [mats["bbd_re"], mats["bbd_im"]], NT, lambda acc, dd_: (acc + dd_,), [du_direct])[0]
    g_win = mm_grad_rows_plain("s5_gwin", h, du)
    tok = send([g_win, g_wglu, g_wout], [0, 1, 2])
    dh = mm_t_rows_plain("s5_dh", du, w_in3, after=tok)[0]
    dx, dx16, dgm = rmsnorm_bwd("s5_dnorm", x, gm, dh, dxo)
    small = dict(d_ar=d_ar, d_ai=d_ai, d_bbr=_diag_extract(d_bbd_re, S5H, S5P), d_bbi=_diag_extract(d_bbd_im, S5H, S5P),
                 d_c_re=jnp.swapaxes(_diag_extract(d_cd_re, S5P, S5H), 1, 2), d_c_im=-jnp.swapaxes(_diag_extract(d_cd_imn, S5P, S5H), 1, 2),
                 d_d=dd, d_bglu=dbglu)
    return dx, dx16, dgm, small


def qk_prepass(qkv, g_q, g_k, nh):
    _, T, W = qkv.shape
    cpd = W // HEAD
    tm = _tile(T, 1024, 16)

    def body(x_ref, gq_ref, gk_ref, o_ref):
        which = pl.program_id(0) // nh
        x = x_ref[...]
        r = lax.rsqrt(jnp.mean(x * x, axis=-1, keepdims=True) + EPS)
        g = jnp.where(which == 0, gq_ref[...], gk_ref[...])
        o_ref[...] = jnp.where(which == 2, x, (x * r) * g).astype(o_ref.dtype)

    return pl.pallas_call(
        body, name="sb_qknorm", grid=(3 * nh, T // tm),
        in_specs=[pl.BlockSpec((None, tm, HEAD), lambda ch, i: (ch // cpd, i, ch % cpd)),
                  pl.BlockSpec((1, HEAD), lambda ch, i: (0, 0)), pl.BlockSpec((1, HEAD), lambda ch, i: (0, 0))],
        out_specs=pl.BlockSpec((None, tm, HEAD), lambda ch, i: (ch, i, 0)),
        out_shape=_sds((3 * nh, T, HEAD), bf16), compiler_params=_params(("parallel", "parallel")))(qkv, g_q, g_k)


def qk_prepass_bwd(qkv, g_q, g_k, dqn, dkn, dv, nh):
    _, T, W = qkv.shape
    cpd = W // HEAD
    tm = _tile(T, 1024, 16)

    def body(x_ref, gq_ref, gk_ref, dq_ref, dk_ref, dv_ref, o_ref, dgq_ref, dgk_ref):
        ch = pl.program_id(1)
        which = ch // nh

        @pl.when((pl.program_id(0) == 0) & (ch == 0))
        def _():
            dgq_ref[...] = jnp.zeros_like(dgq_ref)
            dgk_ref[...] = jnp.zeros_like(dgk_ref)

        x = x_ref[...]
        dn = jnp.where(which == 0, dq_ref[...], dk_ref[...])
        g = jnp.where(which == 0, gq_ref[...], gk_ref[...])
        r = lax.rsqrt(jnp.mean(x * x, axis=-1, keepdims=True) + EPS)
        xh = x * r
        dxh = dn * g
        dx = r * (dxh - xh * jnp.mean(dxh * xh, axis=-1, keepdims=True))
        o_ref[...] = jnp.where(which == 2, dv_ref[...], dx).astype(o_ref.dtype)
        dg = jnp.sum(dn * xh, axis=0, keepdims=True)
        dgq_ref[...] += jnp.where(which == 0, dg, 0.0)
        dgk_ref[...] += jnp.where(which == 1, dg, 0.0)

    chunk = pl.BlockSpec((None, tm, HEAD), lambda i, ch: (ch // cpd, i, ch % cpd))
    head = lambda off: pl.BlockSpec((None, tm, HEAD), lambda i, ch: (jnp.clip(ch - off, 0, nh - 1), i, 0))
    vec = pl.BlockSpec((1, HEAD), lambda i, ch: (0, 0))
    return pl.pallas_call(
        body, name="sb_qknorm_bwd", grid=(T // tm, 3 * nh),
        in_specs=[chunk, vec, vec, head(0), head(nh), head(2 * nh)], out_specs=[chunk, vec, vec],
        out_shape=[_sds(qkv.shape, bf16), _sds((1, HEAD), f32), _sds((1, HEAD), f32)],
        compiler_params=_params(("arbitrary", "arbitrary")))(qkv, g_q, g_k, dqn, dkn, dv)


def _split_dot(a, tri):
    hi = a.astype(bf16)
    lo = (a - hi.astype(f32)).astype(bf16)
    return lax.dot_general(hi, tri, NN, preferred_element_type=f32) + lax.dot_general(lo, tri, NN, preferred_element_type=f32)


LOG2E = 1.4426950408889634


def _sb_logits(q, k, scale):
    t = lax.dot_general(q, k, NT, preferred_element_type=f32) * (scale * LOG2E)
    sp = jnp.log2(1.0 + jnp.exp2(-jnp.abs(t)))
    return jnp.minimum(t, 0.0) - sp, -(jnp.maximum(t, 0.0) + sp)


HP = 2


def attn_fwd(qkvn, nh, tq):
    _, T, _ = qkvn.shape
    nq = T // tq
    scale = 1.0 / math.sqrt(HEAD)

    def body(q_ref, k_ref, v_ref, o_ref):
        qi = pl.program_id(1)
        qs = [q_ref[h] for h in range(HP)]
        row = lax.broadcasted_iota(jnp.int32, (tq, tq), 0)
        col = lax.broadcasted_iota(jnp.int32, (tq, tq), 1)
        past = col < row
        tri = (row > col).astype(bf16)

        def block(kb, carry, acc, diag):
            ks = pl.ds(pl.multiple_of(kb * tq, tq), tq)
            new_c, new_a = [], []
            for h in range(HP):
                k, v = k_ref[h, ks, :], v_ref[h, ks, :]
                lb, lk = _sb_logits(qs[h], k, scale)
                if diag:
                    lk = jnp.where(past, lk, 0.0)
                w = jnp.exp2(lb + _split_dot(lk, tri) + carry[h])
                if diag:
                    w = jnp.where(past, w, 0.0)
                new_a.append(acc[h] + lax.dot_general(w.astype(bf16), v, NN, preferred_element_type=f32))
                new_c.append(carry[h] + jnp.sum(lk, axis=1, keepdims=True))
            return tuple(new_c), tuple(new_a)

        zero_c = tuple(jnp.zeros((tq, 1), f32) for _ in range(HP))
        zero_a = tuple(jnp.zeros((tq, HEAD), f32) for _ in range(HP))
        carry, acc = block(qi, zero_c, zero_a, True)
        carry, acc = lax.fori_loop(0, qi, lambda i, c: block(qi - 1 - i, c[0], c[1], False), (carry, acc))
        for h in range(HP):
            o_ref[:, h * HEAD:(h + 1) * HEAD] = acc[h]

    heads = lambda off: pl.BlockSpec((HP, T, HEAD), lambda hh, i: (off // HP + hh, 0, 0))
    return pl.pallas_call(
        body, name="sb_attn", grid=(nh // HP, nq),
        in_specs=[pl.BlockSpec((HP, tq, HEAD), lambda hh, i: (hh, i, 0)), heads(nh), heads(2 * nh)],
        out_specs=pl.BlockSpec((tq, HP * HEAD), lambda hh, i: (i, hh)),
        out_shape=_sds((T, nh * HEAD), f32), compiler_params=_params(("parallel", "arbitrary")))(qkvn, qkvn, qkvn)


def attn_bwd(qkvn, do, nh, tq):
    _, T, _ = qkvn.shape
    nq = T // tq
    scale = 1.0 / math.sqrt(HEAD)

    def body(q_ref, k_ref, v_ref, do_ref, dq_ref, dk_ref, dv_ref, e_scr, sg_scr):
        qi = pl.program_id(1)

        @pl.when(qi == 0)
        def _():
            dk_ref[...] = jnp.zeros_like(dk_ref)
            dv_ref[...] = jnp.zeros_like(dv_ref)

        qs = [q_ref[h] for h in range(HP)]
        dobs = [do_ref[:, h * HEAD:(h + 1) * HEAD].astype(bf16) for h in range(HP)]
        row = lax.broadcasted_iota(jnp.int32, (tq, tq), 0)
        col = lax.broadcasted_iota(jnp.int32, (tq, tq), 1)
        past = col < row
        tri_later = (row > col).astype(bf16)
        tri_before = (row < col).astype(bf16)

        def sweep1(kb, carry, diag):
            ks = pl.ds(pl.multiple_of(kb * tq, tq), tq)
            new_c = []
            for h in range(HP):
                k, v = k_ref[h, ks, :], v_ref[h, ks, :]
                lb, lk = _sb_logits(qs[h], k, scale)
                if diag:
                    lk = jnp.where(past, lk, 0.0)
                w = jnp.exp2(lb + _split_dot(lk, tri_later) + carry[h])
                if diag:
                    w = jnp.where(past, w, 0.0)
                dw = lax.dot_general(dobs[h], v, NT, preferred_element_type=f32)
                e_scr[h, kb] = dw * w
                sg_scr[h, kb] = jnp.exp2(lb)
                dv_ref[h, ks, :] += lax.dot_general(w.astype(bf16), dobs[h], TNd, preferred_element_type=f32)
                new_c.append(carry[h] + jnp.sum(lk, axis=1, keepdims=True))
            return tuple(new_c)

        zero_c = tuple(jnp.zeros((tq, 1), f32) for _ in range(HP))
        carry = sweep1(qi, zero_c, True)
        lax.fori_loop(0, qi, lambda i, c: sweep1(qi - 1 - i, c, False), carry)

        def sweep2(kb, carry, dq, diag):
            ks = pl.ds(pl.multiple_of(kb * tq, tq), tq)
            new_c, new_q = [], []
            for h in range(HP):
                k = k_ref[h, ks, :]
                e, sg = e_scr[h, kb], sg_scr[h, kb]
                big_e = _split_dot(e, tri_before) + carry[h]
                dz = (e * (1.0 - sg) - big_e * sg) * scale
                if diag:
                    dz = jnp.where(past, dz, 0.0)
                dzb = dz.astype(bf16)
                new_q.append(dq[h] + lax.dot_general(dzb, k, NN, preferred_element_type=f32))
                dk_ref[h, ks, :] += lax.dot_general(dzb, qs[h], TNd, preferred_element_type=f32)
                new_c.append(carry[h] + jnp.sum(e, axis=1, keepdims=True))
            return tuple(new_c), tuple(new_q)

        zero_q = tuple(jnp.zeros((tq, HEAD), f32) for _ in range(HP))
        carry, dq = lax.fori_loop(0, qi, lambda i, c: sweep2(i, c[0], c[1], False), (zero_c, zero_q))
        _, dq = sweep2(qi, carry, dq, True)
        for h in range(HP):
            dq_ref[h] = dq[h]

    heads = lambda off: pl.BlockSpec((HP, T, HEAD), lambda hh, i: (off // HP + hh, 0, 0))
    return pl.pallas_call(
        body, name="sb_attn_bwd", grid=(nh // HP, nq),
        in_specs=[pl.BlockSpec((HP, tq, HEAD), lambda hh, i: (hh, i, 0)), heads(nh), heads(2 * nh),
                  pl.BlockSpec((tq, HP * HEAD), lambda hh, i: (i, hh))],
        out_specs=[pl.BlockSpec((HP, tq, HEAD), lambda hh, i: (hh, i, 0)), heads(0), heads(0)],
        out_shape=[_sds((nh, T, HEAD), f32)] * 3,
        scratch_shapes=[pltpu.VMEM((HP, nq, tq, tq), f32), pltpu.VMEM((HP, nq, tq, tq), f32)],
        compiler_params=_params(("parallel", "arbitrary")))(qkvn, qkvn, qkvn, do)


def sb_fwd(x, gm, w_qkv3, w_o3, g_q, g_k):
    T, D = x.shape
    nh = D // HEAD
    h = rmsnorm_fwd("sb_norm", x, gm)
    qkv = mm_cols("sb_qkv", h, [w_qkv3])[0]
    qkvn = qk_prepass(qkv, g_q, g_k, nh)
    o = attn_fwd(qkvn, nh, _tile(T, 256, 128))
    xo = mm_rows_plain("sb_o", o, w_o3, lambda acc, x_: (x_ + acc,), [x])[0]
    return xo, (x, h, qkv, qkvn, o)


def sb_bwd(dxo, dxo16, saved, gm, w_qkv3, w_o3, g_q, g_k, send):
    x, h, qkv, qkvn, o = saved
    T, D = x.shape
    nh = D // HEAD
    do = mm_t_rows_plain("sb_do", dxo16, w_o3)[0]
    g_wo = mm_grad_rows_plain("sb_gwo", o, dxo16)
    dqn, dkn, dv = attn_bwd(qkvn, do, nh, _tile(T, 256, 128))
    dqkv, dgq, dgk = qk_prepass_bwd(qkv, g_q, g_k, dqn, dkn, dv, nh)
    g_wqkv = mm_grad_cols("sb_gwqkv", h, dqkv)
    tok = send([g_wqkv, g_wo], [0, 1])
    dh = mm_t_cols("sb_dh", [dqkv], [w_qkv3], after=tok)[0]
    dx, dx16, dgm = rmsnorm_bwd("sb_dnorm", x, gm, dh, dxo)
    return dx, dx16, dgm, dgq, dgk


def loss_head(y, target):
    D = y.shape[1]

    def fn(y_, t_):
        err = y_ - t_
        dy = err / D
        return dy, dy, jnp.sum(err * err, axis=0, keepdims=True)

    dy, dy16, sq = _ew("loss_head", fn, [y, target], [], [(D, f32), (D, bf16)], [D])
    return 0.5 * (jnp.sum(sq) / D), dy, dy16


FFN = ["w_gate", "w_up", "w_down"]
SMALL = ["norm_ffn1", "norm_mix", "s5_lam_re", "s5_lam_im", "s5_log_dt", "s5_b_re", "s5_b_im", "s5_c_re", "s5_c_im",
         "s5_d", "s5_b_glu", "sb_g_q", "sb_g_k", "norm_ffn2"]
WEIGHTS = ["norm_ffn1", "ffn1_w_gate", "ffn1_w_up", "ffn1_w_down", "norm_mix", "s5_w_in", "s5_lam_re", "s5_lam_im", "s5_log_dt",
           "s5_b_re", "s5_b_im", "s5_c_re", "s5_c_im", "s5_d", "s5_w_glu", "s5_b_glu", "s5_w_out", "sb_w_qkv", "sb_g_q", "sb_g_k",
           "sb_w_o", "norm_ffn2", "ffn2_w_gate", "ffn2_w_up", "ffn2_w_down"]
PACK_LANES = 128
PACK_ROWS = 64


def _pack(arrs):
    flat = jnp.concatenate([a.reshape(-1).astype(f32) for a in arrs])
    pad = (-flat.shape[0]) % (PACK_LANES * PACK_ROWS)
    return jnp.pad(flat, (0, pad)).reshape(-1, PACK_LANES)


def _unpack(buf, like):
    flat = buf.reshape(-1)
    out, off = [], 0
    for a in like:
        out.append(flat[off:off + a.size].reshape(a.shape))
        off += a.size
    return out


def _after(token, g):
    return g + token[0, 0]


def _step(w, m, v, x, target):
    x = x[0]
    target = target[0]

    def shard(n, l):
        return to_bf16("cast%d_%s" % (l, n), w[n], l)

    def gain(n, l):
        return w[n][l:l + 1]

    wf1a, tok = all_gather("ag_l0f1", [shard("ffn1_" + s, 0) for s in FFN])
    h_s5 = exchange_start("ag_s5", [shard("s5_w_in", 0), shard("s5_w_glu", 0), shard("s5_w_out", 0)], True, tok)
    h_f2a = exchange_start("ag_l0f2", [shard("ffn2_" + s, 0) for s in FFN], True, h_s5["token"])
    h_f1b = exchange_start("ag_l1f1", [shard("ffn1_" + s, 1) for s in FFN], True, h_f2a["token"])
    h_sb = exchange_start("ag_sb", [shard("sb_w_qkv", 0), shard("sb_w_o", 0)], True, h_f1b["token"])
    h_f2b = exchange_start("ag_l1f2", [shard("ffn2_" + s, 1) for s in FFN], True, h_sb["token"])
    tok = h_f2b["token"]

    s5_in, mats = s5_prepare(w["s5_lam_re"][0], w["s5_lam_im"][0], w["s5_log_dt"][0], w["s5_b_re"][0], w["s5_b_im"][0],
                             w["s5_c_re"][0], w["s5_c_im"][0])

    x1, sv_f1a = ffn_fwd("l0f1", x, _after(tok, gain("norm_ffn1", 0)), *wf1a)
    ws5 = exchange_wait("agw_s5", h_s5, x1)
    x2, sv_s5 = s5_fwd(x1, gain("norm_mix", 0), *ws5, mats, w["s5_d"], w["s5_b_glu"])
    wf2a = exchange_wait("agw_l0f2", h_f2a, x2)
    x3, sv_f2a = ffn_fwd("l0f2", x2, gain("norm_ffn2", 0), *wf2a)
    wf1b = exchange_wait("agw_l1f1", h_f1b, x3)
    x4, sv_f1b = ffn_fwd("l1f1", x3, gain("norm_ffn1", 1), *wf1b)
    wsb = exchange_wait("agw_sb", h_sb, x4)
    x5, sv_sb = sb_fwd(x4, gain("norm_mix", 1), *wsb, w["sb_g_q"], w["sb_g_k"])
    wf2b = exchange_wait("agw_l1f2", h_f2b, x5)
    x6, sv_f2b = ffn_fwd("l1f2", x5, gain("norm_ffn2", 1), *wf2b)
    loss_local, dy, dy16 = loss_head(x6, target)
    loss = lax.psum(loss_local, AXES)

    rs = {}

    def sender(key):
        def send(grads, places):
            sent = rs.setdefault(key, [])
            sent.append((exchange_start("rs_%s_%d" % (key, len(sent)), grads, False), places))
            return sent[-1][0]["token"]
        return send

    d5, d5h, dn_f2b = ffn_bwd("l1f2b", dy, dy16, sv_f2b, gain("norm_ffn2", 1), *wf2b, sender("l1f2"))
    d4, d4h, dn_sb, dgq, dgk = sb_bwd(d5, d5h, sv_sb, gain("norm_mix", 1), *wsb, w["sb_g_q"], w["sb_g_k"], sender("sb"))
    d3, d3h, dn_f1b = ffn_bwd("l1f1b", d4, d4h, sv_f1b, gain("norm_ffn1", 1), *wf1b, sender("l1f1"))
    d2, d2h, dn_f2a = ffn_bwd("l0f2b", d3, d3h, sv_f2a, gain("norm_ffn2", 0), *wf2a, sender("l0f2"))
    d1, d1h, dn_s5, s5s = s5_bwd(d2, d2h, sv_s5, gain("norm_mix", 0), *ws5, mats, w["s5_d"], sender("s5"))
    d0, _, dn_f1a = ffn_bwd("l0f1b", d1, d1h, sv_f1a, gain("norm_ffn1", 0), *wf1a, sender("l0f1"), each=True)
    d_lr, d_li, d_ldt, d_brt, d_bit = disc_bwd(*s5_in, s5s["d_ar"].reshape(s5_in[0].shape), s5s["d_ai"].reshape(s5_in[0].shape),
                                               s5s["d_bbr"], s5s["d_bbi"])

    part = {
        "norm_ffn1": jnp.concatenate([dn_f1a, dn_f1b]), "norm_mix": jnp.concatenate([dn_s5, dn_sb]),
        "norm_ffn2": jnp.concatenate([dn_f2a, dn_f2b]),
        "s5_lam_re": d_lr, "s5_lam_im": d_li, "s5_log_dt": d_ldt, "s5_b_re": jnp.swapaxes(d_brt, 1, 2), "s5_b_im": jnp.swapaxes(d_bit, 1, 2),
        "s5_c_re": s5s["d_c_re"], "s5_c_im": s5s["d_c_im"], "s5_d": s5s["d_d"], "s5_b_glu": s5s["d_bglu"], "sb_g_q": dgq, "sb_g_k": dgk,
    }
    h_small = exchange_start("ag_small", [_pack([part[n].reshape(w[n].shape) for n in SMALL])], True)

    res = {}
    late = h_small["token"]
    for key, names in (("l1f2", [("ffn2_" + s, 1) for s in FFN]), ("sb", [("sb_w_qkv", None), ("sb_w_o", None)]),
                       ("l1f1", [("ffn1_" + s, 1) for s in FFN]), ("l0f2", [("ffn2_" + s, 0) for s in FFN]),
                       ("s5", [("s5_w_in", None), ("s5_w_glu", None), ("s5_w_out", None)]), ("small", None),
                       ("l0f1", [("ffn1_" + s, 0) for s in FFN])):
        if key == "small":
            packed = exchange_wait("agw_small", h_small, late)[0]
            outs = adamw("adamw_small", _pack([w[n] for n in SMALL]), _pack([m[n] for n in SMALL]), _pack([v[n] for n in SMALL]), packed)
            un = [_unpack(o, [w[n] for n in SMALL]) for o in outs]
            for i, n in enumerate(SMALL):
                res[n] = [un[k][i] for k in range(4)]
            late = outs[0]
            continue
        for k, (handle, places) in enumerate(rs[key]):
            recv = exchange_wait("rsw_%s_%d" % (key, k), handle, late)
            for place, r in zip(places, recv):
                n, l = names[place]
                if l is None:
                    res[n] = [o[None] for o in adamw("adamw_" + n, w[n][0], m[n][0], v[n][0], r)]
                else:
                    res[n] = adamw_layer("adamw%d_%s" % (l, n), w[n], m[n], v[n], r, l, res.get(n))
                late = res[n][0]

    return (loss, d0[None], *[res[n][0] for n in WEIGHTS], *[res[n][1] for n in WEIGHTS],
            *[res[n][2] for n in WEIGHTS], *[res[n][3] for n in WEIGHTS])


def kernel(x, norm_ffn1, ffn1_w_gate, ffn1_w_up, ffn1_w_down, norm_mix, s5_w_in, s5_lam_re, s5_lam_im, s5_log_dt, s5_b_re, s5_b_im, s5_c_re, s5_c_im, s5_d, s5_w_glu, s5_b_glu, s5_w_out, sb_w_qkv, sb_g_q, sb_g_k, sb_w_o, norm_ffn2, ffn2_w_gate, ffn2_w_up, ffn2_w_down, loss_target, m_norm_ffn1, m_ffn1_w_gate, m_ffn1_w_up, m_ffn1_w_down, m_norm_mix, m_s5_w_in, m_s5_lam_re, m_s5_lam_im, m_s5_log_dt, m_s5_b_re, m_s5_b_im, m_s5_c_re, m_s5_c_im, m_s5_d, m_s5_w_glu, m_s5_b_glu, m_s5_w_out, m_sb_w_qkv, m_sb_g_q, m_sb_g_k, m_sb_w_o, m_norm_ffn2, m_ffn2_w_gate, m_ffn2_w_up, m_ffn2_w_down, v_norm_ffn1, v_ffn1_w_gate, v_ffn1_w_up, v_ffn1_w_down, v_norm_mix, v_s5_w_in, v_s5_lam_re, v_s5_lam_im, v_s5_log_dt, v_s5_b_re, v_s5_b_im, v_s5_c_re, v_s5_c_im, v_s5_d, v_s5_w_glu, v_s5_b_glu, v_s5_w_out, v_sb_w_qkv, v_sb_g_q, v_sb_g_k, v_sb_w_o, v_norm_ffn2, v_ffn2_w_gate, v_ffn2_w_up, v_ffn2_w_down):
    w = dict(norm_ffn1=norm_ffn1, ffn1_w_gate=ffn1_w_gate, ffn1_w_up=ffn1_w_up, ffn1_w_down=ffn1_w_down, norm_mix=norm_mix, s5_w_in=s5_w_in, s5_lam_re=s5_lam_re, s5_lam_im=s5_lam_im, s5_log_dt=s5_log_dt, s5_b_re=s5_b_re, s5_b_im=s5_b_im, s5_c_re=s5_c_re, s5_c_im=s5_c_im, s5_d=s5_d, s5_w_glu=s5_w_glu, s5_b_glu=s5_b_glu, s5_w_out=s5_w_out, sb_w_qkv=sb_w_qkv, sb_g_q=sb_g_q, sb_g_k=sb_g_k, sb_w_o=sb_w_o, norm_ffn2=norm_ffn2, ffn2_w_gate=ffn2_w_gate, ffn2_w_up=ffn2_w_up, ffn2_w_down=ffn2_w_down)
    m = dict(norm_ffn1=m_norm_ffn1, ffn1_w_gate=m_ffn1_w_gate, ffn1_w_up=m_ffn1_w_up, ffn1_w_down=m_ffn1_w_down, norm_mix=m_norm_mix, s5_w_in=m_s5_w_in, s5_lam_re=m_s5_lam_re, s5_lam_im=m_s5_lam_im, s5_log_dt=m_s5_log_dt, s5_b_re=m_s5_b_re, s5_b_im=m_s5_b_im, s5_c_re=m_s5_c_re, s5_c_im=m_s5_c_im, s5_d=m_s5_d, s5_w_glu=m_s5_w_glu, s5_b_glu=m_s5_b_glu, s5_w_out=m_s5_w_out, sb_w_qkv=m_sb_w_qkv, sb_g_q=m_sb_g_q, sb_g_k=m_sb_g_k, sb_w_o=m_sb_w_o, norm_ffn2=m_norm_ffn2, ffn2_w_gate=m_ffn2_w_gate, ffn2_w_up=m_ffn2_w_up, ffn2_w_down=m_ffn2_w_down)
    v = dict(norm_ffn1=v_norm_ffn1, ffn1_w_gate=v_ffn1_w_gate, ffn1_w_up=v_ffn1_w_up, ffn1_w_down=v_ffn1_w_down, norm_mix=v_norm_mix, s5_w_in=v_s5_w_in, s5_lam_re=v_s5_lam_re, s5_lam_im=v_s5_lam_im, s5_log_dt=v_s5_log_dt, s5_b_re=v_s5_b_re, s5_b_im=v_s5_b_im, s5_c_re=v_s5_c_re, s5_c_im=v_s5_c_im, s5_d=v_s5_d, s5_w_glu=v_s5_w_glu, s5_b_glu=v_s5_b_glu, s5_w_out=v_s5_w_out, sb_w_qkv=v_sb_w_qkv, sb_g_q=v_sb_g_q, sb_g_k=v_sb_g_k, sb_w_o=v_sb_w_o, norm_ffn2=v_norm_ffn2, ffn2_w_gate=v_ffn2_w_gate, ffn2_w_up=v_ffn2_w_up, ffn2_w_down=v_ffn2_w_down)
    return _step(w, m, v, x, loss_target)
```

```python
import math

import jax
import jax.numpy as jnp
from jax import lax
from jax.experimental import pallas as pl
from jax.experimental.pallas import tpu as pltpu

f32 = jnp.float32
bf16 = jnp.bfloat16

NDEV = 8
AXES = ("x", "y", "c")
MESH = pl.DeviceIdType.MESH
EPS = 1e-6
HEAD = 128
S5H = 16
S5P = 64
GB = 8
FFN_RES = 0.5
ADAM_LR = 0.001
ADAM_B1 = 0.9
ADAM_B2 = 0.999
ADAM_EPS = 1e-08
ADAM_WD = 0.01
ADAM_STEP = 10
VMEM_LIMIT_V7X = 56 * 2 ** 20

NN = (((1,), (0,)), ((), ()))
NT = (((1,), (1,)), ((), ()))
TNd = (((0,), (0,)), ((), ()))


def _tile(n, target, align):
    if n <= target:
        return n
    t = (target // align) * align
    while t >= align:
        if n % t == 0:
            return t
        t -= align
    return n


def _params(sem):
    return pltpu.CompilerParams(dimension_semantics=sem, vmem_limit_bytes=VMEM_LIMIT_V7X)


def _sds(shape, dtype):
    return jax.ShapeDtypeStruct(tuple(shape), dtype)


def _mm(name, a_list, b_list, a_spec, b_spec, dims, grid, out_shapes, out_specs,
        epilogue=None, extra=(), extra_specs=(), nsub=1, merge_b=False, after=None, separate=False):
    n_a, n_p, n_e = len(a_list), len(b_list), len(extra)
    order = [] if after is None else [after]

    def body(*refs):
        a_refs = refs[:n_a]
        b_refs = refs[n_a:n_a + n_p]
        e_refs = refs[n_a + n_p:n_a + n_p + n_e]
        o_refs = refs[n_a + n_p + n_e + len(order):]
        prods = []
        for p, br in enumerate(b_refs):
            ar = a_refs[p if n_a > 1 else 0]
            if nsub > 1 and not merge_b:
                for j in range(nsub):
                    prods.append(lax.dot_general(ar[j].astype(bf16), br[j].astype(bf16), dims, preferred_element_type=f32))
            else:
                b = br[...]
                if merge_b:
                    b = b.reshape(b.shape[0] * b.shape[1], b.shape[2])
                prods.append(lax.dot_general(ar[...].astype(bf16), b.astype(bf16), dims, preferred_element_type=f32))
        if not separate:
            s = prods[0]
            for d in prods[1:]:
                s = s + d
            prods = [s]
        outs = tuple(prods) if epilogue is None else epilogue(*prods, *[e[...] for e in e_refs])
        for o, val in zip(o_refs, outs):
            o[...] = val.astype(o.dtype)

    return pl.pallas_call(
        body, name=name, grid=grid,
        in_specs=[a_spec] * n_a + [b_spec] * n_p + list(extra_specs) + [pl.BlockSpec(memory_space=pl.ANY)] * len(order),
        out_specs=list(out_specs), out_shape=list(out_shapes),
        compiler_params=_params(("parallel",) * len(grid)),
    )(*a_list, *b_list, *extra, *order)


EW_TILE_ELEMS = 1 << 19
ADAM_TILE_ELEMS = 1 << 21


def _ew(name, fn, rows, bcasts, outs, reds=(), tm=None, budget=EW_TILE_ELEMS):
    n_r, n_b, n_o, n_d = len(rows), len(bcasts), len(outs), len(reds)
    R = rows[0].shape[-2]
    if tm is None:
        widest = max([r.shape[-1] * (r.shape[0] if r.ndim == 3 else 1) for r in rows] + [c for c, _ in outs])
        tm = _tile(R, max(16, budget // widest), 16)

    def body(*refs):
        r = refs[:n_r]
        b = refs[n_r:n_r + n_b]
        o = refs[n_r + n_b:n_r + n_b + n_o]
        d = refs[n_r + n_b + n_o:]
        res = fn(*[x[...] for x in r], *[x[...] for x in b])
        for oo, val in zip(o, res[:n_o]):
            oo[...] = val.astype(oo.dtype)
        if n_d:
            @pl.when(pl.program_id(0) == 0)
            def _():
                for dd in d:
                    dd[...] = jnp.zeros_like(dd)
            for dd, val in zip(d, res[n_o:]):
                dd[...] += val

    in_specs = []
    for x in rows:
        if x.ndim == 2:
            in_specs.append(pl.BlockSpec((tm, x.shape[1]), lambda i: (i, 0)))
        else:
            in_specs.append(pl.BlockSpec((x.shape[0], tm, x.shape[2]), lambda i: (0, i, 0)))
    for x in bcasts:
        in_specs.append(pl.BlockSpec(x.shape, lambda i, nd=x.ndim: (0,) * nd))
    out_shape = [_sds((R, c), dt) for c, dt in outs] + [_sds((1, c), f32) for c in reds]
    out_specs = [pl.BlockSpec((tm, c), lambda i: (i, 0)) for c, _ in outs] + [pl.BlockSpec((1, c), lambda i: (0, 0)) for c in reds]
    return pl.pallas_call(
        body, name=name, grid=(R // tm,), in_specs=in_specs, out_specs=out_specs, out_shape=out_shape,
        compiler_params=_params(("arbitrary",)),
    )(*rows, *bcasts)


def _coords():
    return lax.axis_index("x"), lax.axis_index("y"), lax.axis_index("c")


def _me():
    x, y, c = _coords()
    return 4 * x + 2 * y + c


def _peer(k):
    x, y, c = _coords()
    return x ^ ((k >> 2) & 1), y ^ ((k >> 1) & 1), c ^ (k & 1)


def all_gather(name, shards):
    n = len(shards)

    def body(*refs):
        x_refs, out_refs, token = refs[:n], refs[n:2 * n], refs[2 * n]
        send_sems, recv_sems, local_sems = refs[2 * n + 1:]
        token[...] = jnp.zeros_like(token)
        x, y, c = _coords()
        me, sibling = (x, y, c), (x, y, 1 - c)
        chips = [(1 - x, y), (x, 1 - y), (1 - x, 1 - y)]

        def rows(a, px, py, pc):
            return out_refs[a].at[4 * px + 2 * py + pc]

        def copy(a, k, block, to, src=None):
            return pltpu.make_async_remote_copy(
                src_ref=rows(a, *block) if src is None else src, dst_ref=rows(a, *block),
                send_sem=send_sems.at[a, k], recv_sem=recv_sems.at[a, k],
                device_id=to, device_id_type=MESH)

        mine = [pltpu.make_async_copy(x_refs[a], rows(a, *me), local_sems.at[a]) for a in range(n)]
        for cp in mine:
            cp.start()
        first = []
        for a in range(n):
            first.append(copy(a, 0, me, sibling, src=x_refs[a]))
            first += [copy(a, 1 + j, me, (*chip, c), src=x_refs[a]) for j, chip in enumerate(chips)]
        for cp in first:
            cp.start()
        passed = []
        for j, chip in enumerate(chips):
            for a in range(n):
                copy(a, 1 + j, (*chip, c), me).wait_recv()
                cp = copy(a, 4 + j, (*chip, c), sibling)
                cp.start()
                passed.append(cp)
        for a in range(n):
            copy(a, 0, sibling, me).wait_recv()
            for j, chip in enumerate(chips):
                copy(a, 4 + j, (*chip, 1 - c), me).wait_recv()
        for cp in first + passed:
            cp.wait_send()
        for cp in mine:
            cp.wait()

    anyspec = pl.BlockSpec(memory_space=pl.ANY)
    outs = pl.pallas_call(
        body, name=name,
        out_shape=[_sds((NDEV,) + s.shape, s.dtype) for s in shards] + [_sds((8, 128), f32)],
        in_specs=[anyspec] * n, out_specs=[anyspec] * n + [pl.BlockSpec(memory_space=pltpu.VMEM)],
        scratch_shapes=[pltpu.SemaphoreType.DMA((n, 7)), pltpu.SemaphoreType.DMA((n, 7)), pltpu.SemaphoreType.DMA((n,))],
    )(*shards)
    return list(outs[:n]), outs[n]


def to_bf16(name, w3, l):
    _, R, C = w3.shape
    tm = _tile(R, max(16, 2 * EW_TILE_ELEMS // C), 16)

    def body(x_ref, o_ref):
        o_ref[...] = x_ref[...].astype(bf16)

    return pl.pallas_call(
        body, name=name, grid=(R // tm,), in_specs=[pl.BlockSpec((None, tm, C), lambda i: (l, i, 0))],
        out_specs=pl.BlockSpec((tm, C), lambda i: (i, 0)), out_shape=_sds((R, C), bf16),
        compiler_params=_params(("parallel",)))(w3)


HBM_SPEC = pl.BlockSpec(memory_space=pltpu.HBM)
SEM_SPEC = pl.BlockSpec(memory_space=pltpu.SEMAPHORE)
EFFECT = pltpu.SideEffectType.DATAFLOW_SIDE_EFFECTING


def exchange_start(name, srcs, gather, after=None):
    n = len(srcs)
    order = [] if after is None else [after]
    ns = n * (NDEV - 1)
    n_sem = 2 * ns + n
    lands = [lax.empty(((NDEV,) + s.shape) if gather else s.shape, s.dtype) for s in srcs]

    def body(*refs):
        src_refs, land_refs = refs[:n], refs[n:2 * n]
        first = 2 * n + len(order)
        send_sems, recv_sems = refs[first:first + ns], refs[first + ns:first + 2 * ns]
        local_sems = refs[first + 2 * ns:first + n_sem]
        token = refs[-1]
        me_ = _me()
        for k in range(1, NDEV):
            px, py, pc = _peer(k)
            p = 4 * px + 2 * py + pc
            for a in range(n):
                i = a * (NDEV - 1) + k - 1
                pltpu.make_async_remote_copy(
                    src_ref=src_refs[a] if gather else src_refs[a].at[p], dst_ref=land_refs[a].at[me_],
                    send_sem=send_sems[i], recv_sem=recv_sems[i],
                    device_id=(px, py, pc), device_id_type=MESH).start()
        for a in range(n):
            pltpu.make_async_copy(src_refs[a] if gather else src_refs[a].at[me_], land_refs[a].at[me_], local_sems[a]).start()
        token[...] = jnp.zeros_like(token)

    outs = pl.pallas_call(
        body, name=name,
        out_shape=(*[pltpu.SemaphoreType.DMA(())] * n_sem,
                   *[pltpu.HBM(s.shape, s.dtype) for s in srcs], *[pltpu.HBM(l.shape, l.dtype) for l in lands],
                   _sds((8, 128), f32)),
        in_specs=[HBM_SPEC] * (2 * n) + [pl.BlockSpec(memory_space=pl.ANY)] * len(order),
        out_specs=(*[SEM_SPEC] * n_sem, *[HBM_SPEC] * (2 * n), pl.BlockSpec(memory_space=pltpu.VMEM)),
        input_output_aliases={i: n_sem + i for i in range(2 * n)},
        compiler_params=pltpu.CompilerParams(has_side_effects=EFFECT),
    )(*[pltpu.with_memory_space_constraint(s, pltpu.HBM) for s in srcs],
      *[pltpu.with_memory_space_constraint(l, pltpu.HBM) for l in lands], *order)
    return dict(n=n, sems=outs[:n_sem], srcs=outs[n_sem:n_sem + n], lands=outs[n_sem + n:n_sem + 2 * n], token=outs[-1], gather=gather)


def exchange_wait(name, hd, after):
    n, gather = hd["n"], hd["gather"]
    ns = n * (NDEV - 1)

    def body(*refs):
        src_refs, land_refs = refs[:n], refs[n:2 * n]
        send_sems, recv_sems = refs[2 * n:2 * n + ns], refs[2 * n + ns:2 * n + 2 * ns]
        local_sems = refs[2 * n + 2 * ns:2 * n + 2 * ns + n]
        x, y, c = _coords()
        me_ = _me()
        for a in range(n):
            pltpu.make_async_copy(src_refs[a] if gather else src_refs[a].at[me_], land_refs[a].at[me_], local_sems[a]).wait()
        for k in range(1, NDEV):
            px, py, pc = _peer(k)
            p = 4 * px + 2 * py + pc
            for a in range(n):
                i = a * (NDEV - 1) + k - 1
                cp = pltpu.make_async_remote_copy(
                    src_ref=src_refs[a] if gather else src_refs[a].at[p], dst_ref=land_refs[a].at[p],
                    send_sem=send_sems[i], recv_sem=recv_sems[i],
                    device_id=(x, y, 1 - c), device_id_type=MESH)
                cp.wait_send()
                cp.wait_recv()

    outs = pl.pallas_call(
        body, name=name,
        out_shape=tuple(pltpu.HBM(s.shape, s.dtype) for s in list(hd["srcs"]) + list(hd["lands"])),
        in_specs=[HBM_SPEC] * (2 * n) + [SEM_SPEC] * (2 * ns + n) + [pl.BlockSpec(memory_space=pl.ANY)],
        out_specs=tuple([HBM_SPEC] * (2 * n)),
        input_output_aliases={i: i for i in range(2 * n)},
        compiler_params=pltpu.CompilerParams(has_side_effects=EFFECT),
    )(*hd["srcs"], *hd["lands"], *hd["sems"], after)
    return list(outs[n:])


def _adam_math(w_, m_, v_, r_):
    g = r_[0].astype(f32)
    for j in range(1, NDEV):
        g = g + r_[j].astype(f32)
    m2 = ADAM_B1 * m_ + (1.0 - ADAM_B1) * g
    v2 = ADAM_B2 * v_ + (1.0 - ADAM_B2) * jnp.square(g)
    m_hat = m2 / (1.0 - ADAM_B1 ** ADAM_STEP)
    v_hat = v2 / (1.0 - ADAM_B2 ** ADAM_STEP)
    delta = -ADAM_LR * (m_hat / (jnp.sqrt(v_hat) + ADAM_EPS) + ADAM_WD * w_)
    return g, delta, m2, v2


def adamw(name, w, m, v, recv):
    return _ew(name, _adam_math, [w, m, v, recv], [], [(w.shape[1], f32)] * 4, budget=ADAM_TILE_ELEMS)


def adamw_layer(name, w3, m3, v3, recv, l, prev):
    L, R, C = w3.shape
    tm = _tile(R, max(16, ADAM_TILE_ELEMS // (NDEV * C)), 16)
    n_prev = 0 if prev is None else 4

    def body(*refs):
        w_ref, m_ref, v_ref, r_ref = refs[:4]
        o_refs = refs[4 + n_prev:]
        res = _adam_math(w_ref[...], m_ref[...], v_ref[...], r_ref[...])
        for o, val in zip(o_refs, res):
            o[...] = val

    lay = pl.BlockSpec((None, tm, C), lambda i: (l, i, 0))
    return pl.pallas_call(
        body, name=name, grid=(R // tm,),
        in_specs=[lay, lay, lay, pl.BlockSpec((NDEV, tm, C), lambda i: (0, i, 0))] + [pl.BlockSpec(memory_space=pl.ANY)] * n_prev,
        out_specs=[lay] * 4, out_shape=[_sds((L, R, C), f32)] * 4,
        input_output_aliases={4 + i: i for i in range(n_prev)},
        compiler_params=_params(("parallel",)),
    )(w3, m3, v3, recv, *(prev or []))


def rmsnorm_fwd(name, x, g):
    D = x.shape[1]

    def fn(x_, g_):
        r = lax.rsqrt(jnp.mean(x_ * x_, axis=-1, keepdims=True) + EPS)
        return ((x_ * r) * g_,)

    return _ew(name, fn, [x], [g], [(D, bf16)])[0]


def rmsnorm_bwd(name, x, g, dh, dres):
    D = x.shape[1]

    def fn(x_, dh_, dres_, g_):
        r = lax.rsqrt(jnp.mean(x_ * x_, axis=-1, keepdims=True) + EPS)
        xh = x_ * r
        dxh = dh_ * g_
        dx = dres_ + r * (dxh - xh * jnp.mean(dxh * xh, axis=-1, keepdims=True))
        return dx, dx, jnp.sum(dh_ * xh, axis=0, keepdims=True)

    return _ew(name, fn, [x, dh, dres], [g], [(D, f32), (D, bf16)], [D])


def _silu_parts(a):
    sig = jax.nn.sigmoid(a)
    return sig, a * sig


def mm_cols(name, h, w3_list, epilogue=None, extra=(), outs=None):
    T, K = h.shape
    Nb = w3_list[0].shape[2]
    tm = _tile(T, 1024, 16)
    outs = outs or [f32]
    blk = pl.BlockSpec((None, tm, Nb), lambda j, i: (j, i, 0))
    return _mm(name, [h], w3_list, pl.BlockSpec((tm, K), lambda j, i: (i, 0)), pl.BlockSpec((None, K, Nb), lambda j, i: (j, 0, 0)),
               NN, (NDEV, T // tm), [_sds((NDEV, T, Nb), dt) for dt in outs], [blk] * len(outs), epilogue, extra, [blk] * len(extra),
               separate=True)


def mm_rows(name, p_list, w3_list, epilogue=None, extra=(), outs=None, after=None):
    _, T, Kb = p_list[0].shape
    N = w3_list[0].shape[2]
    tm, tn = _tile(T, 512, 16), _tile(N, 512 // len(p_list), 128)
    outs = outs or [f32]
    blk = pl.BlockSpec((tm, tn), lambda i, n: (i, n))
    return _mm(name, p_list, w3_list, pl.BlockSpec((NDEV, tm, Kb), lambda i, n: (0, i, 0)), pl.BlockSpec((NDEV, Kb, tn), lambda i, n: (0, 0, n)),
               NN, (T // tm, N // tn), [_sds((T, N), dt) for dt in outs], [blk] * len(outs), epilogue, extra, [blk] * len(extra), nsub=NDEV,
               after=after)


def mm_rows_plain(name, a, w3, epilogue=None, extra=(), extra_specs=None, outs=None):
    T, K = a.shape
    Kb, N = w3.shape[1], w3.shape[2]
    tm, tn = _tile(T, 512, 16), _tile(N, 1024, 128)
    outs = outs or [f32]
    blk = pl.BlockSpec((tm, tn), lambda i, n: (i, n))
    return _mm(name, [a], [w3], pl.BlockSpec((tm, K), lambda i, n: (i, 0)), pl.BlockSpec((NDEV, Kb, tn), lambda i, n: (0, 0, n)),
               NN, (T // tm, N // tn), [_sds((T, N), dt) for dt in outs], [blk] * len(outs), epilogue, extra,
               extra_specs or [blk] * len(extra), nsub=NDEV, merge_b=True)


def mm_t_rows(name, d, w3_list, epilogue=None, extra=(), outs=None, rows=512):
    T, K = d.shape
    Nb = w3_list[0].shape[1]
    tm = _tile(T, rows, 16)
    outs = outs or [f32]
    blk = pl.BlockSpec((None, tm, Nb), lambda j, i: (j, i, 0))
    return _mm(name, [d], w3_list, pl.BlockSpec((tm, K), lambda j, i: (i, 0)), pl.BlockSpec((None, Nb, K), lambda j, i: (j, 0, 0)),
               NT, (NDEV, T // tm), [_sds((NDEV, T, Nb), dt) for dt in outs], [blk] * len(outs), epilogue, extra, [blk] * len(extra),
               separate=True)


def mm_t_rows_plain(name, d, w3, epilogue=None, extra=(), outs=None, after=None):
    T, K = d.shape
    Nb = w3.shape[1]
    tm = _tile(T, 256, 16)
    outs = outs or [f32]
    blk = pl.BlockSpec((tm, NDEV * Nb), lambda i: (i, 0))
    return _mm(name, [d], [w3], pl.BlockSpec((tm, K), lambda i: (i, 0)), pl.BlockSpec((NDEV, Nb, K), lambda i: (0, 0, 0)),
               NT, (T // tm,), [_sds((T, NDEV * Nb), dt) for dt in outs], [blk] * len(outs), epilogue, extra, [blk] * len(extra),
               nsub=NDEV, merge_b=True, after=after)


def mm_t_cols(name, d_list, w3_list, epilogue=None, extra=(), outs=None, after=None):
    _, T, Kb = d_list[0].shape
    N = w3_list[0].shape[1]
    tm, tn = _tile(T, 512, 16), _tile(N, 256, 128)
    outs = outs or [f32]
    blk = pl.BlockSpec((tm, tn), lambda i, n: (i, n))
    return _mm(name, d_list, w3_list, pl.BlockSpec((NDEV, tm, Kb), lambda i, n: (0, i, 0)), pl.BlockSpec((NDEV, tn, Kb), lambda i, n: (0, n, 0)),
               NT, (T // tm, N // tn), [_sds((T, N), dt) for dt in outs], [blk] * len(outs), epilogue, extra, [blk] * len(extra), nsub=NDEV,
               after=after)


def mm_grad_rows(name, p, d, scale=1.0):
    _, T, Mb = p.shape
    N = d.shape[1]
    tn = _tile(N, 512, 128)
    return _mm(name, [p], [d], pl.BlockSpec((None, T, Mb), lambda j, n: (j, 0, 0)), pl.BlockSpec((T, tn), lambda j, n: (0, n)),
               TNd, (NDEV, N // tn), [_sds((NDEV, Mb, N), bf16)], [pl.BlockSpec((None, Mb, tn), lambda j, n: (j, 0, n))],
               (lambda acc: (acc * scale,)))[0]


def mm_grad_rows_plain(name, a, d):
    T, M = a.shape
    N = d.shape[1]
    tm, tn = _tile(M, 512, 128), _tile(N, 512, 128)
    g = _mm(name, [a], [d], pl.BlockSpec((T, tm), lambda m, n: (0, m)), pl.BlockSpec((T, tn), lambda m, n: (0, n)),
            TNd, (M // tm, N // tn), [_sds((M, N), bf16)], [pl.BlockSpec((tm, tn), lambda m, n: (m, n))])[0]
    return g.reshape(NDEV, M // NDEV, N)


def mm_grad_cols(name, h, d):
    T, M = h.shape
    Nb = d.shape[2]
    tm = _tile(M, 512, 128)
    return _mm(name, [h], [d], pl.BlockSpec((T, tm), lambda j, m: (0, m)), pl.BlockSpec((None, T, Nb), lambda j, m: (j, 0, 0)),
               TNd, (NDEV, M // tm), [_sds((NDEV, M, Nb), bf16)], [pl.BlockSpec((None, tm, Nb), lambda j, m: (j, m, 0))])[0]


def ffn_fwd(tag, x, g, wg3, wu3, wd3):
    h = rmsnorm_fwd(tag + "_norm", x, g)

    def gate_up_epi(a_, b_):
        _, sl = _silu_parts(a_)
        return a_, b_, sl * b_

    a, b, p = mm_t_rows(tag + "_gateup", h, [wg3, wu3], gate_up_epi, outs=[bf16, bf16, bf16], rows=1024)
    xo = mm_rows(tag + "_down", [p], [wd3], lambda acc, x_: (x_ + FFN_RES * acc,), [x])[0]
    return xo, (x, h, a, b, p)


def ffn_bwd(tag, dxo, dxo16, saved, g, wg3, wu3, wd3, send, each=False):
    x, h, a, b, p = saved

    def dp_epi(acc, a_, b_):
        dp = FFN_RES * acc
        a_, b_ = a_.astype(f32), b_.astype(f32)
        sig, sl = _silu_parts(a_)
        return dp * b_ * (sig * (1.0 + a_ * (1.0 - sig))), dp * sl

    da, db = mm_t_rows(tag + "_dp", dxo16, [wd3], dp_epi, [a, b], [bf16, bf16])
    g_wd = mm_grad_rows(tag + "_gwd", p, dxo16, FFN_RES)
    if each:
        send([g_wd], [2])
    g_wg = mm_grad_rows(tag + "_gwg", da, h)
    if each:
        send([g_wg], [0])
    g_wu = mm_grad_rows(tag + "_gwu", db, h)
    tok = send([g_wu], [1]) if each else send([g_wg, g_wu, g_wd], [0, 1, 2])
    dh = mm_rows(tag + "_dh", [da, db], [wg3, wu3], after=tok)[0]
    return rmsnorm_bwd(tag + "_dnorm", x, g, dh, dxo)


def _disc(lr, li, ldt, brt, bit):
    dt = jnp.exp(ldt)
    lr = jnp.minimum(lr, -1e-4)
    mag = jnp.exp(lr * dt)
    ab_re = mag * jnp.cos(li * dt)
    ab_im = mag * jnp.sin(li * dt)
    den = lr * lr + li * li
    n_re = ab_re - 1.0
    f_re = (n_re * lr + ab_im * li) / den
    f_im = (ab_im * lr - n_re * li) / den
    bb_re = f_re[:, None, :] * brt - f_im[:, None, :] * bit
    bb_im = f_re[:, None, :] * bit + f_im[:, None, :] * brt
    return ab_re, ab_im, bb_re, bb_im


def disc_fwd(lr, li, ldt, brt, bit):
    def body(lr_ref, li_ref, ldt_ref, brt_ref, bit_ref, ar_ref, ai_ref, bbr_ref, bbi_ref):
        ar, ai, bbr, bbi = _disc(lr_ref[...], li_ref[...], ldt_ref[...], brt_ref[...], bit_ref[...])
        ar_ref[...] = ar
        ai_ref[...] = ai
        bbr_ref[...] = bbr
        bbi_ref[...] = bbi

    return pl.pallas_call(body, name="s5_disc", out_shape=[_sds(lr.shape, f32), _sds(lr.shape, f32), _sds(brt.shape, f32), _sds(brt.shape, f32)],
                          compiler_params=pltpu.CompilerParams(vmem_limit_bytes=VMEM_LIMIT_V7X))(lr, li, ldt, brt, bit)


def disc_bwd(lr, li, ldt, brt, bit, d_ar, d_ai, d_bbr, d_bbi):
    def body(lr_ref, li_ref, ldt_ref, brt_ref, bit_ref, dar_ref, dai_ref, dbbr_ref, dbbi_ref, o_lr, o_li, o_ldt, o_brt, o_bit):
        _, vjp = jax.vjp(_disc, lr_ref[...], li_ref[...], ldt_ref[...], brt_ref[...], bit_ref[...])
        g = vjp((dar_ref[...], dai_ref[...], dbbr_ref[...], dbbi_ref[...]))
        o_lr[...] = g[0]
        o_li[...] = g[1]
        o_ldt[...] = g[2]
        o_brt[...] = g[3]
        o_bit[...] = g[4]

    return pl.pallas_call(body, name="s5_disc_bwd",
                          out_shape=[_sds(lr.shape, f32), _sds(lr.shape, f32), _sds(ldt.shape, f32), _sds(brt.shape, f32), _sds(brt.shape, f32)],
                          compiler_params=pltpu.CompilerParams(vmem_limit_bytes=VMEM_LIMIT_V7X))(lr, li, ldt, brt, bit, d_ar, d_ai, d_bbr, d_bbi)


def _blockdiag(m):
    G, R, C = m.shape
    eye = jnp.eye(GB, dtype=m.dtype)
    m5 = m.reshape(G // GB, GB, R, 1, C) * eye[None, :, None, :, None]
    return m5.reshape(G // GB, GB * R, GB * C)


def _diag_extract(M, R, C):
    nb = M.shape[0]
    eye = jnp.eye(GB, dtype=M.dtype)
    m5 = M.reshape(nb, GB, R, GB, C) * eye[None, :, None, :, None]
    return m5.sum(axis=3).reshape(nb * GB, R, C)


def mm_blockdiag(name, a_list, bd_list, dims, epilogue=None, extra=(), extra_bcast=(), outs=None):
    T = a_list[0].shape[0]
    nb, r, c = bd_list[0].shape
    ra = a_list[0].shape[1] // nb
    ca = c if dims == NN else r
    n_p, n_e, n_b = len(a_list), len(extra), len(extra_bcast)
    outs = outs or [f32]
    tm = _tile(T, 128, 16)

    def body(*refs):
        a_refs, b_refs = refs[:n_p], refs[n_p:2 * n_p]
        e_refs = refs[2 * n_p:2 * n_p + n_e + n_b]
        o_refs = refs[2 * n_p + n_e + n_b:]
        for b in range(nb):
            s = None
            for ar, br in zip(a_refs, b_refs):
                d = lax.dot_general(ar[:, b * ra:(b + 1) * ra].astype(bf16), br[b], dims, preferred_element_type=f32)
                s = d if s is None else s + d
            cols = slice(b * ca, (b + 1) * ca)
            vals = (s,) if epilogue is None else epilogue(s, *[e[:, cols] for e in e_refs])
            for o, val in zip(o_refs, vals):
                o[:, cols] = val.astype(o.dtype)

    row = lambda w: pl.BlockSpec((tm, w), lambda i: (i, 0))
    return pl.pallas_call(
        body, name=name, grid=(T // tm,),
        in_specs=[row(nb * ra)] * n_p + [pl.BlockSpec((nb, r, c), lambda i: (0, 0, 0))] * n_p + [row(nb * ca)] * n_e
        + [pl.BlockSpec((1, nb * ca), lambda i: (0, 0))] * n_b,
        out_specs=[row(nb * ca)] * len(outs), out_shape=[_sds((T, nb * ca), dt) for dt in outs],
        compiler_params=_params(("parallel",)),
    )(*a_list, *bd_list, *extra, *extra_bcast)


def mm_blockdiag_grad(name, a, d, nb):
    T = a.shape[0]
    ra, rd = a.shape[1] // nb, d.shape[1] // nb
    tk = _tile(T, 256, 16)

    def body(a_ref, d_ref, o_ref):
        @pl.when(pl.program_id(0) == 0)
        def _():
            o_ref[...] = jnp.zeros_like(o_ref)

        for b in range(nb):
            o_ref[b] += lax.dot_general(a_ref[:, b * ra:(b + 1) * ra].astype(bf16), d_ref[:, b * rd:(b + 1) * rd].astype(bf16),
                                        TNd, preferred_element_type=f32)

    return pl.pallas_call(
        body, name=name, grid=(T // tk,),
        in_specs=[pl.BlockSpec((tk, nb * ra), lambda k: (k, 0)), pl.BlockSpec((tk, nb * rd), lambda k: (k, 0))],
        out_specs=pl.BlockSpec((nb, ra, rd), lambda k: (0, 0, 0)), out_shape=_sds((nb, ra, rd), f32),
        compiler_params=_params(("arbitrary",)),
    )(a, d)


SUB = 8
SCAN_LANES = 1024
SCAN_GROUPS = 64


def s5_tables(ar, ai):
    N = ar.shape[1]

    def body(ar_ref, ai_ref, f_ref, b_ref):
        r, i = ar_ref[...], ai_ref[...]
        pw = [(r, i)]
        for _ in range(SUB - 1):
            pr, pi = pw[-1]
            pw.append((pr * r - pi * i, pr * i + pi * r))
        row = lax.broadcasted_iota(jnp.int32, (SUB, N), 0)
        for t, k in enumerate((1, 2, 4)):
            pr, pi = pw[k - 1]
            f_ref[2 * t] = jnp.where(row >= k, pr, 0.0)
            f_ref[2 * t + 1] = jnp.where(row >= k, pi, 0.0)
            b_ref[2 * t] = jnp.where(row <= SUB - 1 - k, pr, 0.0)
            b_ref[2 * t + 1] = jnp.where(row <= SUB - 1 - k, -pi, 0.0)
        fr = fi = br = bi = jnp.zeros((SUB, N), f32)
        for j in range(SUB):
            fr, fi = jnp.where(row == j, pw[j][0], fr), jnp.where(row == j, pw[j][1], fi)
            br, bi = jnp.where(row == j, pw[SUB - 1 - j][0], br), jnp.where(row == j, -pw[SUB - 1 - j][1], bi)
        f_ref[6], f_ref[7] = fr, fi
        b_ref[6], b_ref[7] = br, bi

    return pl.pallas_call(body, name="s5_tables", out_shape=[_sds((8, SUB, N), f32)] * 2,
                          compiler_params=pltpu.CompilerParams(vmem_limit_bytes=VMEM_LIMIT_V7X))(ar, ai)


def _group_scan(xr, xi, m, cr, ci, up):
    for t, k in enumerate((1, 2, 4)):
        shift = SUB - k if up else k
        pr, pi = pltpu.roll(xr, shift, 0), pltpu.roll(xi, shift, 0)
        xr, xi = xr + m[2 * t] * pr - m[2 * t + 1] * pi, xi + m[2 * t] * pi + m[2 * t + 1] * pr
    return xr + m[6] * cr - m[7] * ci, xi + m[6] * ci + m[7] * cr


def s5_scan_fwd(bu_re, bu_im, tab):
    T8, _, N = bu_re.shape
    W, tb = _tile(N, SCAN_LANES, 128), _tile(T8, SCAN_GROUPS, 1)

    def body(bur_ref, bui_ref, tab_ref, sr_ref, si_ref, st):
        @pl.when(pl.program_id(1) == 0)
        def _():
            st[...] = jnp.zeros_like(st)

        def step(g, c):
            m = [tab_ref[t] for t in range(8)]
            xr, xi = _group_scan(bur_ref[g], bui_ref[g], m, c[0], c[1], False)
            sr_ref[g] = xr
            si_ref[g] = xi
            return jnp.broadcast_to(xr[SUB - 1:SUB], (SUB, W)), jnp.broadcast_to(xi[SUB - 1:SUB], (SUB, W))

        cr, ci = lax.fori_loop(0, tb, step, (st[0], st[1]))
        st[0] = cr
        st[1] = ci

    blk = pl.BlockSpec((tb, SUB, W), lambda j, i: (i, 0, j))
    tabs = pl.BlockSpec((8, SUB, W), lambda j, i: (0, 0, j))
    return pl.pallas_call(body, name="s5_scan", grid=(N // W, T8 // tb), in_specs=[blk, blk, tabs], out_specs=[blk, blk],
                          out_shape=[_sds(bu_re.shape, f32)] * 2, scratch_shapes=[pltpu.VMEM((2, SUB, W), f32)],
                          compiler_params=_params(("parallel", "arbitrary")))(bu_re, bu_im, tab)


def s5_scan_bwd(ds_re, ds_im, s_re, s_im, tab):
    T8, _, N = ds_re.shape
    W, tb = _tile(N, SCAN_LANES, 128), _tile(T8, SCAN_GROUPS // 2, 1)
    nblk = T8 // tb

    def body(dsr_ref, dsi_ref, sr_ref, si_ref, tab_ref, qr_ref, qi_ref, dar_ref, dai_ref, st):
        @pl.when(pl.program_id(1) == 0)
        def _():
            st[...] = jnp.zeros_like(st)
            dar_ref[...] = jnp.zeros_like(dar_ref)
            dai_ref[...] = jnp.zeros_like(dai_ref)

        last_row = lax.broadcasted_iota(jnp.int32, (SUB, W), 0) == SUB - 1

        def step(i, c):
            cr, ci, dar, dai = c
            g = tb - 1 - i
            m = [tab_ref[t] for t in range(8)]
            xr, xi = _group_scan(dsr_ref[g], dsi_ref[g], m, cr, ci, True)
            qr_ref[g] = xr
            qi_ref[g] = xi
            nr = jnp.where(last_row, cr, pltpu.roll(xr, SUB - 1, 0))
            ni = jnp.where(last_row, ci, pltpu.roll(xi, SUB - 1, 0))
            sr, si = sr_ref[g], si_ref[g]
            dar = dar + nr * sr + ni * si
            dai = dai + ni * sr - nr * si
            return jnp.broadcast_to(xr[0:1], (SUB, W)), jnp.broadcast_to(xi[0:1], (SUB, W)), dar, dai

        cr, ci, dar, dai = lax.fori_loop(0, tb, step, (st[0], st[1], dar_ref[...], dai_ref[...]))
        st[0] = cr
        st[1] = ci
        dar_ref[...] = dar
        dai_ref[...] = dai

        @pl.when(pl.program_id(1) == nblk - 1)
        def _():
            dar_ref[...] = jnp.broadcast_to(jnp.sum(dar, axis=0, keepdims=True), (SUB, W))
            dai_ref[...] = jnp.broadcast_to(jnp.sum(dai, axis=0, keepdims=True), (SUB, W))

    blk = pl.BlockSpec((tb, SUB, W), lambda j, i: (nblk - 1 - i, 0, j))
    tabs = pl.BlockSpec((8, SUB, W), lambda j, i: (0, 0, j))
    acc = pl.BlockSpec((SUB, W), lambda j, i: (0, j))
    return pl.pallas_call(body, name="s5_scan_bwd", grid=(N // W, nblk), in_specs=[blk, blk, blk, blk, tabs], out_specs=[blk, blk, acc, acc],
                          out_shape=[_sds(ds_re.shape, f32)] * 2 + [_sds((SUB, N), f32)] * 2, scratch_shapes=[pltpu.VMEM((2, SUB, W), f32)],
                          compiler_params=_params(("parallel", "arbitrary")))(ds_re, ds_im, s_re, s_im, tab)


def _gelu_grad(y):
    c = math.sqrt(2.0 / math.pi)
    th = jnp.tanh(c * (y + 0.044715 * (y * y * y)))
    return 0.5 * (1.0 + th) + 0.5 * y * (1.0 - th * th) * c * (1.0 + 3.0 * 0.044715 * (y * y))


def s5_prepare(lam_re, lam_im, log_dt, b_re, b_im, c_re, c_im):
    G = lam_re.shape[0]
    ldt = log_dt.reshape(G, 1)
    brt, bit = jnp.swapaxes(b_re, 1, 2), jnp.swapaxes(b_im, 1, 2)
    ar, ai, bbr, bbi = disc_fwd(lam_re, lam_im, ldt, brt, bit)
    tab_f, tab_b = s5_tables(ar.reshape(1, -1), ai.reshape(1, -1))
    mats = dict(
        bbd_re=_blockdiag(bbr).astype(bf16), bbd_im=_blockdiag(bbi).astype(bf16),
        cd_re=_blockdiag(jnp.swapaxes(c_re, 1, 2)).astype(bf16),
        cd_imn=_blockdiag(-jnp.swapaxes(c_im, 1, 2)).astype(bf16),
        tab_f=tab_f, tab_b=tab_b)
    return (lam_re, lam_im, ldt, brt, bit), mats


def s5_fwd(x, gm, w_in3, w_glu3, w_out3, mats, d_skip, b_glu):
    T, D = x.shape
    N = mats["tab_f"].shape[2]
    h = rmsnorm_fwd("s5_norm", x, gm)
    u = mm_rows_plain("s5_in", h, w_in3)[0]
    bu_re = mm_blockdiag("s5_bu_re", [u], [mats["bbd_re"]], NN)[0]
    bu_im = mm_blockdiag("s5_bu_im", [u], [mats["bbd_im"]], NN)[0]
    grouped = (T // SUB, SUB, N)
    s_re, s_im = s5_scan_fwd(bu_re.reshape(grouped), bu_im.reshape(grouped), mats["tab_f"])
    s_re, s_im = s_re.reshape(T, N), s_im.reshape(T, N)

    def y_epi(acc, u_, d_):
        y2 = acc + d_ * u_
        return y2, jax.nn.gelu(y2)

    y2, gl = mm_blockdiag("s5_y", [s_re, s_im], [mats["cd_re"], mats["cd_imn"]], NN, y_epi, [u], [d_skip], [f32, bf16])

    def glu_epi(acc, y2_, b_):
        zg = acc + b_
        return zg, jax.nn.gelu(y2_) * jax.nn.sigmoid(zg)

    tm, tn = _tile(T, 512, 16), _tile(D, 1024, 128)
    zg, o = mm_rows_plain("s5_glu", gl, w_glu3, glu_epi, [y2, b_glu],
                          [pl.BlockSpec((tm, tn), lambda i, n: (i, n)), pl.BlockSpec((1, tn), lambda i, n: (0, n))], [f32, bf16])
    xo = mm_rows_plain("s5_out", o, w_out3, lambda acc, x_: (x_ + acc,), [x])[0]
    return xo, (x, h, u, s_re, s_im, y2, gl, zg, o)


def s5_bwd(dxo, dxo16, saved, gm, w_in3, w_glu3, w_out3, mats, d_skip, send):
    x, h, u, s_re, s_im, y2, gl, zg, o = saved
    T, D = x.shape
    N = mats["tab_b"].shape[2]
    nb = mats["cd_re"].shape[0]

    def do_epi(acc, y2_, zg_):
        sg = jax.nn.sigmoid(zg_)
        return acc * sg, acc * jax.nn.gelu(y2_) * (sg * (1.0 - sg))

    dgl_direct, dzg = mm_t_rows_plain("s5_do", dxo16, w_out3, do_epi, [y2, zg], [f32, f32])
    g_wout = mm_grad_rows_plain("s5_gwout", o, dxo16)
    g_wglu = mm_grad_rows_plain("s5_gwglu", gl, dzg)
    dy2 = mm_t_rows_plain("s5_dgl", dzg, w_glu3, lambda acc, dd_, y2_: ((acc + dd_) * _gelu_grad(y2_),), [dgl_direct, y2])[0]

    def red_fn(dy2_, u_, dzg_, d_):
        return dy2_ * d_, jnp.sum(dy2_ * u_, axis=0, keepdims=True), jnp.sum(dzg_, axis=0, keepdims=True)

    du_direct, dd, dbglu = _ew("s5_dskip", red_fn, [dy2, u, dzg], [d_skip], [(D, f32)], [D, D])
    ds_re = mm_blockdiag("s5_ds_re", [dy2], [mats["cd_re"]], NT)[0]
    ds_im = mm_blockdiag("s5_ds_im", [dy2], [mats["cd_imn"]], NT)[0]
    d_cd_re = mm_blockdiag_grad("s5_gc_re", s_re, dy2, nb)
    d_cd_imn = mm_blockdiag_grad("s5_gc_im", s_im, dy2, nb)
    sh = (T // SUB, SUB, N)
    q_re, q_im, d_ar, d_ai = s5_scan_bwd(ds_re.reshape(sh), ds_im.reshape(sh), s_re.reshape(sh), s_im.reshape(sh), mats["tab_b"])
    q_re, q_im, d_ar, d_ai = q_re.reshape(T, N), q_im.reshape(T, N), d_ar[0], d_ai[0]
    d_bbd_re = mm_blockdiag_grad("s5_gb_re", u, q_re, nb)
    d_bbd_im = mm_blockdiag_grad("s5_gb_im", u, q_im, nb)
    du = mm_blockdiag("s5_du", [q_re, q_im], [mats["bbd_re"], mats["bbd_im"]], NT, lambda acc, dd_: (acc + dd_,), [du_direct])[0]
    g_win = mm_grad_rows_plain("s5_gwin", h, du)
    tok = send([g_win, g_wglu, g_wout], [0, 1, 2])
    dh = mm_t_rows_plain("s5_dh", du, w_in3, after=tok)[0]
    dx, dx16, dgm = rmsnorm_bwd("s5_dnorm", x, gm, dh, dxo)
    small = dict(d_ar=d_ar, d_ai=d_ai, d_bbr=_diag_extract(d_bbd_re, S5H, S5P), d_bbi=_diag_extract(d_bbd_im, S5H, S5P),
                 d_c_re=jnp.swapaxes(_diag_extract(d_cd_re, S5P, S5H), 1, 2), d_c_im=-jnp.swapaxes(_diag_extract(d_cd_imn, S5P, S5H), 1, 2),
                 d_d=dd, d_bglu=dbglu)
    return dx, dx16, dgm, small


def qk_prepass(qkv, g_q, g_k, nh):
    _, T, W = qkv.shape
    cpd = W // HEAD
    tm = _tile(T, 1024, 16)

    def body(x_ref, gq_ref, gk_ref, o_ref):
        which = pl.program_id(0) // nh
        x = x_ref[...]
        r = lax.rsqrt(jnp.mean(x * x, axis=-1, keepdims=True) + EPS)
        g = jnp.where(which == 0, gq_ref[...], gk_ref[...])
        o_ref[...] = jnp.where(which == 2, x, (x * r) * g).astype(o_ref.dtype)

    return pl.pallas_call(
        body, name="sb_qknorm", grid=(3 * nh, T // tm),
        in_specs=[pl.BlockSpec((None, tm, HEAD), lambda ch, i: (ch // cpd, i, ch % cpd)),
                  pl.BlockSpec((1, HEAD), lambda ch, i: (0, 0)), pl.BlockSpec((1, HEAD), lambda ch, i: (0, 0))],
        out_specs=pl.BlockSpec((None, tm, HEAD), lambda ch, i: (ch, i, 0)),
        out_shape=_sds((3 * nh, T, HEAD), bf16), compiler_params=_params(("parallel", "parallel")))(qkv, g_q, g_k)


def qk_prepass_bwd(qkv, g_q, g_k, dqn, dkn, dv, nh):
    _, T, W = qkv.shape
    cpd = W // HEAD
    tm = _tile(T, 1024, 16)

    def body(x_ref, gq_ref, gk_ref, dq_ref, dk_ref, dv_ref, o_ref, dgq_ref, dgk_ref):
        ch = pl.program_id(1)
        which = ch // nh

        @pl.when((pl.program_id(0) == 0) & (ch == 0))
        def _():
            dgq_ref[...] = jnp.zeros_like(dgq_ref)
            dgk_ref[...] = jnp.zeros_like(dgk_ref)

        x = x_ref[...]
        dn = jnp.where(which == 0, dq_ref[...], dk_ref[...])
        g = jnp.where(which == 0, gq_ref[...], gk_ref[...])
        r = lax.rsqrt(jnp.mean(x * x, axis=-1, keepdims=True) + EPS)
        xh = x * r
        dxh = dn * g
        dx = r * (dxh - xh * jnp.mean(dxh * xh, axis=-1, keepdims=True))
        o_ref[...] = jnp.where(which == 2, dv_ref[...], dx).astype(o_ref.dtype)
        dg = jnp.sum(dn * xh, axis=0, keepdims=True)
        dgq_ref[...] += jnp.where(which == 0, dg, 0.0)
        dgk_ref[...] += jnp.where(which == 1, dg, 0.0)

    chunk = pl.BlockSpec((None, tm, HEAD), lambda i, ch: (ch // cpd, i, ch % cpd))
    head = lambda off: pl.BlockSpec((None, tm, HEAD), lambda i, ch: (jnp.clip(ch - off, 0, nh - 1), i, 0))
    vec = pl.BlockSpec((1, HEAD), lambda i, ch: (0, 0))
    return pl.pallas_call(
        body, name="sb_qknorm_bwd", grid=(T // tm, 3 * nh),
        in_specs=[chunk, vec, vec, head(0), head(nh), head(2 * nh)], out_specs=[chunk, vec, vec],
        out_shape=[_sds(qkv.shape, bf16), _sds((1, HEAD), f32), _sds((1, HEAD), f32)],
        compiler_params=_params(("arbitrary", "arbitrary")))(qkv, g_q, g_k, dqn, dkn, dv)


def _split_dot(a, tri):
    hi = a.astype(bf16)
    lo = (a - hi.astype(f32)).astype(bf16)
    return lax.dot_general(hi, tri, NN, preferred_element_type=f32) + lax.dot_general(lo, tri, NN, preferred_element_type=f32)


LOG2E = 1.4426950408889634


def _sb_logits(q, k, scale):
    t = lax.dot_general(q, k, NT, preferred_element_type=f32) * (scale * LOG2E)
    sp = jnp.log2(1.0 + jnp.exp2(-jnp.abs(t)))
    return jnp.minimum(t, 0.0) - sp, -(jnp.maximum(t, 0.0) + sp)


HP = 2


def attn_fwd(qkvn, nh, tq):
    _, T, _ = qkvn.shape
    nq = T // tq
    scale = 1.0 / math.sqrt(HEAD)

    def body(q_ref, k_ref, v_ref, o_ref):
        qi = pl.program_id(1)
        qs = [q_ref[h] for h in range(HP)]
        row = lax.broadcasted_iota(jnp.int32, (tq, tq), 0)
        col = lax.broadcasted_iota(jnp.int32, (tq, tq), 1)
        past = col < row
        tri = (row > col).astype(bf16)

        def block(kb, carry, acc, diag):
            ks = pl.ds(pl.multiple_of(kb * tq, tq), tq)
            new_c, new_a = [], []
            for h in range(HP):
                k, v = k_ref[h, ks, :], v_ref[h, ks, :]
                lb, lk = _sb_logits(qs[h], k, scale)
                if diag:
                    lk = jnp.where(past, lk, 0.0)
                w = jnp.exp2(lb + _split_dot(lk, tri) + carry[h])
                if diag:
                    w = jnp.where(past, w, 0.0)
                new_a.append(acc[h] + lax.dot_general(w.astype(bf16), v, NN, preferred_element_type=f32))
                new_c.append(carry[h] + jnp.sum(lk, axis=1, keepdims=True))
            return tuple(new_c), tuple(new_a)

        zero_c = tuple(jnp.zeros((tq, 1), f32) for _ in range(HP))
        zero_a = tuple(jnp.zeros((tq, HEAD), f32) for _ in range(HP))
        carry, acc = block(qi, zero_c, zero_a, True)
        carry, acc = lax.fori_loop(0, qi, lambda i, c: block(qi - 1 - i, c[0], c[1], False), (carry, acc))
        for h in range(HP):
            o_ref[:, h * HEAD:(h + 1) * HEAD] = acc[h]

    heads = lambda off: pl.BlockSpec((HP, T, HEAD), lambda hh, i: (off // HP + hh, 0, 0))
    return pl.pallas_call(
        body, name="sb_attn", grid=(nh // HP, nq),
        in_specs=[pl.BlockSpec((HP, tq, HEAD), lambda hh, i: (hh, i, 0)), heads(nh), heads(2 * nh)],
        out_specs=pl.BlockSpec((tq, HP * HEAD), lambda hh, i: (i, hh)),
        out_shape=_sds((T, nh * HEAD), f32), compiler_params=_params(("parallel", "arbitrary")))(qkvn, qkvn, qkvn)


def attn_bwd(qkvn, do, nh, tq):
    _, T, _ = qkvn.shape
    nq = T // tq
    scale = 1.0 / math.sqrt(HEAD)

    def body(q_ref, k_ref, v_ref, do_ref, dq_ref, dk_ref, dv_ref, e_scr, sg_scr):
        qi = pl.program_id(1)

        @pl.when(qi == 0)
        def _():
            dk_ref[...] = jnp.zeros_like(dk_ref)
            dv_ref[...] = jnp.zeros_like(dv_ref)

        qs = [q_ref[h] for h in range(HP)]
        dobs = [do_ref[:, h * HEAD:(h + 1) * HEAD].astype(bf16) for h in range(HP)]
        row = lax.broadcasted_iota(jnp.int32, (tq, tq), 0)
        col = lax.broadcasted_iota(jnp.int32, (tq, tq), 1)
        past = col < row
        tri_later = (row > col).astype(bf16)
        tri_before = (row < col).astype(bf16)

        def sweep1(kb, carry, diag):
            ks = pl.ds(pl.multiple_of(kb * tq, tq), tq)
            new_c = []
            for h in range(HP):
                k, v = k_ref[h, ks, :], v_ref[h, ks, :]
                lb, lk = _sb_logits(qs[h], k, scale)
                if diag:
                    lk = jnp.where(past, lk, 0.0)
                w = jnp.exp2(lb + _split_dot(lk, tri_later) + carry[h])
                if diag:
                    w = jnp.where(past, w, 0.0)
                dw = lax.dot_general(dobs[h], v, NT, preferred_element_type=f32)
                e_scr[h, kb] = dw * w
                sg_scr[h, kb] = jnp.exp2(lb)
                dv_ref[h, ks, :] += lax.dot_general(w.astype(bf16), dobs[h], TNd, preferred_element_type=f32)
                new_c.append(carry[h] + jnp.sum(lk, axis=1, keepdims=True))
            return tuple(new_c)

        zero_c = tuple(jnp.zeros((tq, 1), f32) for _ in range(HP))
        carry = sweep1(qi, zero_c, True)
        lax.fori_loop(0, qi, lambda i, c: sweep1(qi - 1 - i, c, False), carry)

        def sweep2(kb, carry, dq, diag):
            ks = pl.ds(pl.multiple_of(kb * tq, tq), tq)
            new_c, new_q = [], []
            for h in range(HP):
                k = k_ref[h, ks, :]
                e, sg = e_scr[h, kb], sg_scr[h, kb]
                big_e = _split_dot(e, tri_before) + carry[h]
                dz = (e * (1.0 - sg) - big_e * sg) * scale
                if diag:
                    dz = jnp.where(past, dz, 0.0)
                dzb = dz.astype(bf16)
                new_q.append(dq[h] + lax.dot_general(dzb, k, NN, preferred_element_type=f32))
                dk_ref[h, ks, :] += lax.dot_general(dzb, qs[h], TNd, preferred_element_type=f32)
                new_c.append(carry[h] + jnp.sum(e, axis=1, keepdims=True))
            return tuple(new_c), tuple(new_q)

        zero_q = tuple(jnp.zeros((tq, HEAD), f32) for _ in range(HP))
        carry, dq = lax.fori_loop(0, qi, lambda i, c: sweep2(i, c[0], c[1], False), (zero_c, zero_q))
        _, dq = sweep2(qi, carry, dq, True)
        for h in range(HP):
            dq_ref[h] = dq[h]

    heads = lambda off: pl.BlockSpec((HP, T, HEAD), lambda hh, i: (off // HP + hh, 0, 0))
    return pl.pallas_call(
        body, name="sb_attn_bwd", grid=(nh // HP, nq),
        in_specs=[pl.BlockSpec((HP, tq, HEAD), lambda hh, i: (hh, i, 0)), heads(nh), heads(2 * nh),
                  pl.BlockSpec((tq, HP * HEAD), lambda hh, i: (i, hh))],
        out_specs=[pl.BlockSpec((HP, tq, HEAD), lambda hh, i: (hh, i, 0)), heads(0), heads(0)],
        out_shape=[_sds((nh, T, HEAD), f32)] * 3,
        scratch_shapes=[pltpu.VMEM((HP, nq, tq, tq), f32), pltpu.VMEM((HP, nq, tq, tq), f32)],
        compiler_params=_params(("parallel", "arbitrary")))(qkvn, qkvn, qkvn, do)


def sb_fwd(x, gm, w_qkv3, w_o3, g_q, g_k):
    T, D = x.shape
    nh = D // HEAD
    h = rmsnorm_fwd("sb_norm", x, gm)
    qkv = mm_cols("sb_qkv", h, [w_qkv3])[0]
    qkvn = qk_prepass(qkv, g_q, g_k, nh)
    o = attn_fwd(qkvn, nh, _tile(T, 256, 128))
    xo = mm_rows_plain("sb_o", o, w_o3, lambda acc, x_: (x_ + acc,), [x])[0]
    return xo, (x, h, qkv, qkvn, o)


def sb_bwd(dxo, dxo16, saved, gm, w_qkv3, w_o3, g_q, g_k, send):
    x, h, qkv, qkvn, o = saved
    T, D = x.shape
    nh = D // HEAD
    do = mm_t_rows_plain("sb_do", dxo16, w_o3)[0]
    g_wo = mm_grad_rows_plain("sb_gwo", o, dxo16)
    dqn, dkn, dv = attn_bwd(qkvn, do, nh, _tile(T, 256, 128))
    dqkv, dgq, dgk = qk_prepass_bwd(qkv, g_q, g_k, dqn, dkn, dv, nh)
    g_wqkv = mm_grad_cols("sb_gwqkv", h, dqkv)
    tok = send([g_wqkv, g_wo], [0, 1])
    dh = mm_t_cols("sb_dh", [dqkv], [w_qkv3], after=tok)[0]
    dx, dx16, dgm = rmsnorm_bwd("sb_dnorm", x, gm, dh, dxo)
    return dx, dx16, dgm, dgq, dgk


def loss_head(y, target):
    D = y.shape[1]

    def fn(y_, t_):
        err = y_ - t_
        dy = err / D
        return dy, dy, jnp.sum(err * err, axis=0, keepdims=True)

    dy, dy16, sq = _ew("loss_head", fn, [y, target], [], [(D, f32), (D, bf16)], [D])
    return 0.5 * (jnp.sum(sq) / D), dy, dy16


FFN = ["w_gate", "w_up", "w_down"]
SMALL = ["norm_ffn1", "norm_mix", "s5_lam_re", "s5_lam_im", "s5_log_dt", "s5_b_re", "s5_b_im", "s5_c_re", "s5_c_im",
         "s5_d", "s5_b_glu", "sb_g_q", "sb_g_k", "norm_ffn2"]
WEIGHTS = ["norm_ffn1", "ffn1_w_gate", "ffn1_w_up", "ffn1_w_down", "norm_mix", "s5_w_in", "s5_lam_re", "s5_lam_im", "s5_log_dt",
           "s5_b_re", "s5_b_im", "s5_c_re", "s5_c_im", "s5_d", "s5_w_glu", "s5_b_glu", "s5_w_out", "sb_w_qkv", "sb_g_q", "sb_g_k",
           "sb_w_o", "norm_ffn2", "ffn2_w_gate", "ffn2_w_up", "ffn2_w_down"]
PACK_LANES = 128
PACK_ROWS = 64


def _pack(arrs):
    flat = jnp.concatenate([a.reshape(-1).astype(f32) for a in arrs])
    pad = (-flat.shape[0]) % (PACK_LANES * PACK_ROWS)
    return jnp.pad(flat, (0, pad)).reshape(-1, PACK_LANES)


def _unpack(buf, like):
    flat = buf.reshape(-1)
    out, off = [], 0
    for a in like:
        out.append(flat[off:off + a.size].reshape(a.shape))
        off += a.size
    return out


def _after(token, g):
    return g + token[0, 0]


TRANSPOSED = ("ffn1_w_gate", "ffn1_w_up", "ffn2_w_gate", "ffn2_w_up")


def _step(w, m, v, x, target):
    x = x[0]
    target = target[0]
    w, m, v = [{n: jnp.swapaxes(a, 1, 2) if n in TRANSPOSED else a for n, a in d.items()} for d in (w, m, v)]

    def shard(n, l):
        return to_bf16("cast%d_%s" % (l, n), w[n], l)

    def gain(n, l):
        return w[n][l:l + 1]

    wf1a, tok = all_gather("ag_l0f1", [shard("ffn1_" + s, 0) for s in FFN])
    h_s5 = exchange_start("ag_s5", [shard("s5_w_in", 0), shard("s5_w_glu", 0), shard("s5_w_out", 0)], True, tok)
    h_f2a = exchange_start("ag_l0f2", [shard("ffn2_" + s, 0) for s in FFN], True, h_s5["token"])
    h_f1b = exchange_start("ag_l1f1", [shard("ffn1_" + s, 1) for s in FFN], True, h_f2a["token"])
    h_sb = exchange_start("ag_sb", [shard("sb_w_qkv", 0), shard("sb_w_o", 0)], True, h_f1b["token"])
    h_f2b = exchange_start("ag_l1f2", [shard("ffn2_" + s, 1) for s in FFN], True, h_sb["token"])
    tok = h_f2b["token"]

    s5_in, mats = s5_prepare(w["s5_lam_re"][0], w["s5_lam_im"][0], w["s5_log_dt"][0], w["s5_b_re"][0], w["s5_b_im"][0],
                             w["s5_c_re"][0], w["s5_c_im"][0])

    x1, sv_f1a = ffn_fwd("l0f1", x, _after(tok, gain("norm_ffn1", 0)), *wf1a)
    ws5 = exchange_wait("agw_s5", h_s5, x1)
    x2, sv_s5 = s5_fwd(x1, gain("norm_mix", 0), *ws5, mats, w["s5_d"], w["s5_b_glu"])
    wf2a = exchange_wait("agw_l0f2", h_f2a, x2)
    x3, sv_f2a = ffn_fwd("l0f2", x2, gain("norm_ffn2", 0), *wf2a)
    wf1b = exchange_wait("agw_l1f1", h_f1b, x3)
    x4, sv_f1b = ffn_fwd("l1f1", x3, gain("norm_ffn1", 1), *wf1b)
    wsb = exchange_wait("agw_sb", h_sb, x4)
    x5, sv_sb = sb_fwd(x4, gain("norm_mix", 1), *wsb, w["sb_g_q"], w["sb_g_k"])
    wf2b = exchange_wait("agw_l1f2", h_f2b, x5)
    x6, sv_f2b = ffn_fwd("l1f2", x5, gain("norm_ffn2", 1), *wf2b)
    loss_local, dy, dy16 = loss_head(x6, target)
    loss = lax.psum(loss_local, AXES)

    rs = {}

    def sender(key):
        def send(grads, places):
            sent = rs.setdefault(key, [])
            sent.append((exchange_start("rs_%s_%d" % (key, len(sent)), grads, False), places))
            return sent[-1][0]["token"]
        return send

    d5, d5h, dn_f2b = ffn_bwd("l1f2b", dy, dy16, sv_f2b, gain("norm_ffn2", 1), *wf2b, sender("l1f2"))
    d4, d4h, dn_sb, dgq, dgk = sb_bwd(d5, d5h, sv_sb, gain("norm_mix", 1), *wsb, w["sb_g_q"], w["sb_g_k"], sender("sb"))
    d3, d3h, dn_f1b = ffn_bwd("l1f1b", d4, d4h, sv_f1b, gain("norm_ffn1", 1), *wf1b, sender("l1f1"))
    d2, d2h, dn_f2a = ffn_bwd("l0f2b", d3, d3h, sv_f2a, gain("norm_ffn2", 0), *wf2a, sender("l0f2"))
    d1, d1h, dn_s5, s5s = s5_bwd(d2, d2h, sv_s5, gain("norm_mix", 0), *ws5, mats, w["s5_d"], sender("s5"))
    d0, _, dn_f1a = ffn_bwd("l0f1b", d1, d1h, sv_f1a, gain("norm_ffn1", 0), *wf1a, sender("l0f1"), each=True)
    d_lr, d_li, d_ldt, d_brt, d_bit = disc_bwd(*s5_in, s5s["d_ar"].reshape(s5_in[0].shape), s5s["d_ai"].reshape(s5_in[0].shape),
                                               s5s["d_bbr"], s5s["d_bbi"])

    part = {
        "norm_ffn1": jnp.concatenate([dn_f1a, dn_f1b]), "norm_mix": jnp.concatenate([dn_s5, dn_sb]),
        "norm_ffn2": jnp.concatenate([dn_f2a, dn_f2b]),
        "s5_lam_re": d_lr, "s5_lam_im": d_li, "s5_log_dt": d_ldt, "s5_b_re": jnp.swapaxes(d_brt, 1, 2), "s5_b_im": jnp.swapaxes(d_bit, 1, 2),
        "s5_c_re": s5s["d_c_re"], "s5_c_im": s5s["d_c_im"], "s5_d": s5s["d_d"], "s5_b_glu": s5s["d_bglu"], "sb_g_q": dgq, "sb_g_k": dgk,
    }
    h_small = exchange_start("ag_small", [_pack([part[n].reshape(w[n].shape) for n in SMALL])], True)

    res = {}
    late = h_small["token"]
    for key, names in (("l1f2", [("ffn2_" + s, 1) for s in FFN]), ("sb", [("sb_w_qkv", None), ("sb_w_o", None)]),
                       ("l1f1", [("ffn1_" + s, 1) for s in FFN]), ("l0f2", [("ffn2_" + s, 0) for s in FFN]),
                       ("s5", [("s5_w_in", None), ("s5_w_glu", None), ("s5_w_out", None)]), ("small", None),
                       ("l0f1", [("ffn1_" + s, 0) for s in FFN])):
        if key == "small":
            packed = exchange_wait("agw_small", h_small, late)[0]
            outs = adamw("adamw_small", _pack([w[n] for n in SMALL]), _pack([m[n] for n in SMALL]), _pack([v[n] for n in SMALL]), packed)
            un = [_unpack(o, [w[n] for n in SMALL]) for o in outs]
            for i, n in enumerate(SMALL):
                res[n] = [un[k][i] for k in range(4)]
            late = outs[0]
            continue
        for k, (handle, places) in enumerate(rs[key]):
            recv = exchange_wait("rsw_%s_%d" % (key, k), handle, late)
            for place, r in zip(places, recv):
                n, l = names[place]
                if l is None:
                    res[n] = [o[None] for o in adamw("adamw_" + n, w[n][0], m[n][0], v[n][0], r)]
                else:
                    res[n] = adamw_layer("adamw%d_%s" % (l, n), w[n], m[n], v[n], r, l, res.get(n))
                late = res[n][0]

    for n in TRANSPOSED:
        res[n] = [jnp.swapaxes(o, 1, 2) for o in res[n]]
    return (loss, d0[None], *[res[n][0] for n in WEIGHTS], *[res[n][1] for n in WEIGHTS],
            *[res[n][2] for n in WEIGHTS], *[res[n][3] for n in WEIGHTS])


def kernel(x, norm_ffn1, ffn1_w_gate, ffn1_w_up, ffn1_w_down, norm_mix, s5_w_in, s5_lam_re, s5_lam_im, s5_log_dt, s5_b_re, s5_b_im, s5_c_re, s5_c_im, s5_d, s5_w_glu, s5_b_glu, s5_w_out, sb_w_qkv, sb_g_q, sb_g_k, sb_w_o, norm_ffn2, ffn2_w_gate, ffn2_w_up, ffn2_w_down, loss_target, m_norm_ffn1, m_ffn1_w_gate, m_ffn1_w_up, m_ffn1_w_down, m_norm_mix, m_s5_w_in, m_s5_lam_re, m_s5_lam_im, m_s5_log_dt, m_s5_b_re, m_s5_b_im, m_s5_c_re, m_s5_c_im, m_s5_d, m_s5_w_glu, m_s5_b_glu, m_s5_w_out, m_sb_w_qkv, m_sb_g_q, m_sb_g_k, m_sb_w_o, m_norm_ffn2, m_ffn2_w_gate, m_ffn2_w_up, m_ffn2_w_down, v_norm_ffn1, v_ffn1_w_gate, v_ffn1_w_up, v_ffn1_w_down, v_norm_mix, v_s5_w_in, v_s5_lam_re, v_s5_lam_im, v_s5_log_dt, v_s5_b_re, v_s5_b_im, v_s5_c_re, v_s5_c_im, v_s5_d, v_s5_w_glu, v_s5_b_glu, v_s5_w_out, v_sb_w_qkv, v_sb_g_q, v_sb_g_k, v_sb_w_o, v_norm_ffn2, v_ffn2_w_gate, v_ffn2_w_up, v_ffn2_w_down):
    w = dict(norm_ffn1=norm_ffn1, ffn1_w_gate=ffn1_w_gate, ffn1_w_up=ffn1_w_up, ffn1_w_down=ffn1_w_down, norm_mix=norm_mix, s5_w_in=s5_w_in, s5_lam_re=s5_lam_re, s5_lam_im=s5_lam_im, s5_log_dt=s5_log_dt, s5_b_re=s5_b_re, s5_b_im=s5_b_im, s5_c_re=s5_c_re, s5_c_im=s5_c_im, s5_d=s5_d, s5_w_glu=s5_w_glu, s5_b_glu=s5_b_glu, s5_w_out=s5_w_out, sb_w_qkv=sb_w_qkv, sb_g_q=sb_g_q, sb_g_k=sb_g_k, sb_w_o=sb_w_o, norm_ffn2=norm_ffn2, ffn2_w_gate=ffn2_w_gate, ffn2_w_up=ffn2_w_up, ffn2_w_down=ffn2_w_down)
    m = dict(norm_ffn1=m_norm_ffn1, ffn1_w_gate=m_ffn1_w_gate, ffn1_w_up=m_ffn1_w_up, ffn1_w_down=m_ffn1_w_down, norm_mix=m_norm_mix, s5_w_in=m_s5_w_in, s5_lam_re=m_s5_lam_re, s5_lam_im=m_s5_lam_im, s5_log_dt=m_s5_log_dt, s5_b_re=m_s5_b_re, s5_b_im=m_s5_b_im, s5_c_re=m_s5_c_re, s5_c_im=m_s5_c_im, s5_d=m_s5_d, s5_w_glu=m_s5_w_glu, s5_b_glu=m_s5_b_glu, s5_w_out=m_s5_w_out, sb_w_qkv=m_sb_w_qkv, sb_g_q=m_sb_g_q, sb_g_k=m_sb_g_k, sb_w_o=m_sb_w_o, norm_ffn2=m_norm_ffn2, ffn2_w_gate=m_ffn2_w_gate, ffn2_w_up=m_ffn2_w_up, ffn2_w_down=m_ffn2_w_down)
    v = dict(norm_ffn1=v_norm_ffn1, ffn1_w_gate=v_ffn1_w_gate, ffn1_w_up=v_ffn1_w_up, ffn1_w_down=v_ffn1_w_down, norm_mix=v_norm_mix, s5_w_in=v_s5_w_in, s5_lam_re=v_s5_lam_re, s5_lam_im=v_s5_lam_im, s5_log_dt=v_s5_log_dt, s5_b_re=v_s5_b_re, s5_b_im=v_s5_b_im, s5_c_re=v_s5_c_re, s5_c_im=v_s5_c_im, s5_d=v_s5_d, s5_w_glu=v_s5_w_glu, s5_b_glu=v_s5_b_glu, s5_w_out=v_s5_w_out, sb_w_qkv=v_sb_w_qkv, sb_g_q=v_sb_g_q, sb_g_k=v_sb_g_k, sb_w_o=v_sb_w_o, norm_ffn2=v_norm_ffn2, ffn2_w_gate=v_ffn2_w_gate, ffn2_w_up=v_ffn2_w_up, ffn2_w_down=v_ffn2_w_down)
    return _step(w, m, v, x, loss_target)
```

```python
import math

import jax
import jax.numpy as jnp
from jax import lax
from jax.experimental import pallas as pl
from jax.experimental.pallas import tpu as pltpu

f32 = jnp.float32
bf16 = jnp.bfloat16

NDEV = 8
AXES = ("x", "y", "c")
MESH = pl.DeviceIdType.MESH
EPS = 1e-6
HEAD = 128
S5H = 16
S5P = 64
GB = 8
FFN_RES = 0.5
ADAM_LR = 0.001
ADAM_B1 = 0.9
ADAM_B2 = 0.999
ADAM_EPS = 1e-08
ADAM_WD = 0.01
ADAM_STEP = 10
VMEM_LIMIT_V7X = 56 * 2 ** 20

NN = (((1,), (0,)), ((), ()))
NT = (((1,), (1,)), ((), ()))
TNd = (((0,), (0,)), ((), ()))


def _tile(n, target, align):
    if n <= target:
        return n
    t = (target // align) * align
    while t >= align:
        if n % t == 0:
            return t
        t -= align
    return n


def _params(sem):
    return pltpu.CompilerParams(dimension_semantics=sem, vmem_limit_bytes=VMEM_LIMIT_V7X)


def _sds(shape, dtype):
    return jax.ShapeDtypeStruct(tuple(shape), dtype)


def _mm(name, a_list, b_list, a_spec, b_spec, dims, grid, out_shapes, out_specs,
        epilogue=None, extra=(), extra_specs=(), nsub=1, merge_b=False, after=None, separate=False):
    n_a, n_p, n_e = len(a_list), len(b_list), len(extra)
    order = [] if after is None else [after]

    def body(*refs):
        a_refs = refs[:n_a]
        b_refs = refs[n_a:n_a + n_p]
        e_refs = refs[n_a + n_p:n_a + n_p + n_e]
        o_refs = refs[n_a + n_p + n_e + len(order):]
        prods = []
        for p, br in enumerate(b_refs):
            ar = a_refs[p if n_a > 1 else 0]
            if nsub > 1 and not merge_b:
                for j in range(nsub):
                    prods.append(lax.dot_general(ar[j].astype(bf16), br[j].astype(bf16), dims, preferred_element_type=f32))
            else:
                b = br[...]
                if merge_b:
                    b = b.reshape(b.shape[0] * b.shape[1], b.shape[2])
                prods.append(lax.dot_general(ar[...].astype(bf16), b.astype(bf16), dims, preferred_element_type=f32))
        if not separate:
            s = prods[0]
            for d in prods[1:]:
                s = s + d
            prods = [s]
        outs = tuple(prods) if epilogue is None else epilogue(*prods, *[e[...] for e in e_refs])
        for o, val in zip(o_refs, outs):
            o[...] = val.astype(o.dtype)

    return pl.pallas_call(
        body, name=name, grid=grid,
        in_specs=[a_spec] * n_a + [b_spec] * n_p + list(extra_specs) + [pl.BlockSpec(memory_space=pl.ANY)] * len(order),
        out_specs=list(out_specs), out_shape=list(out_shapes),
        compiler_params=_params(("parallel",) * len(grid)),
    )(*a_list, *b_list, *extra, *order)


EW_TILE_ELEMS = 1 << 19
ADAM_TILE_ELEMS = 1 << 21


def _ew(name, fn, rows, bcasts, outs, reds=(), tm=None, budget=EW_TILE_ELEMS):
    n_r, n_b, n_o, n_d = len(rows), len(bcasts), len(outs), len(reds)
    R = rows[0].shape[-2]
    if tm is None:
        widest = max([r.shape[-1] * (r.shape[0] if r.ndim == 3 else 1) for r in rows] + [c for c, _ in outs])
        tm = _tile(R, max(16, budget // widest), 16)

    def body(*refs):
        r = refs[:n_r]
        b = refs[n_r:n_r + n_b]
        o = refs[n_r + n_b:n_r + n_b + n_o]
        d = refs[n_r + n_b + n_o:]
        res = fn(*[x[...] for x in r], *[x[...] for x in b])
        for oo, val in zip(o, res[:n_o]):
            oo[...] = val.astype(oo.dtype)
        if n_d:
            @pl.when(pl.program_id(0) == 0)
            def _():
                for dd in d:
                    dd[...] = jnp.zeros_like(dd)
            for dd, val in zip(d, res[n_o:]):
                dd[...] += val

    in_specs = []
    for x in rows:
        if x.ndim == 2:
            in_specs.append(pl.BlockSpec((tm, x.shape[1]), lambda i: (i, 0)))
        else:
            in_specs.append(pl.BlockSpec((x.shape[0], tm, x.shape[2]), lambda i: (0, i, 0)))
    for x in bcasts:
        in_specs.append(pl.BlockSpec(x.shape, lambda i, nd=x.ndim: (0,) * nd))
    out_shape = [_sds((R, c), dt) for c, dt in outs] + [_sds((1, c), f32) for c in reds]
    out_specs = [pl.BlockSpec((tm, c), lambda i: (i, 0)) for c, _ in outs] + [pl.BlockSpec((1, c), lambda i: (0, 0)) for c in reds]
    return pl.pallas_call(
        body, name=name, grid=(R // tm,), in_specs=in_specs, out_specs=out_specs, out_shape=out_shape,
        compiler_params=_params(("arbitrary",)),
    )(*rows, *bcasts)


def _coords():
    return lax.axis_index("x"), lax.axis_index("y"), lax.axis_index("c")


def _me():
    x, y, c = _coords()
    return 4 * x + 2 * y + c


def _peer(k):
    x, y, c = _coords()
    return x ^ ((k >> 2) & 1), y ^ ((k >> 1) & 1), c ^ (k & 1)


def all_gather(name, shards):
    n = len(shards)

    def body(*refs):
        x_refs, out_refs, token = refs[:n], refs[n:2 * n], refs[2 * n]
        send_sems, recv_sems, local_sems = refs[2 * n + 1:]
        token[...] = jnp.zeros_like(token)
        x, y, c = _coords()
        me, sibling = (x, y, c), (x, y, 1 - c)
        chips = [(1 - x, y), (x, 1 - y), (1 - x, 1 - y)]

        def rows(a, px, py, pc):
            return out_refs[a].at[4 * px + 2 * py + pc]

        def copy(a, k, block, to, src=None):
            return pltpu.make_async_remote_copy(
                src_ref=rows(a, *block) if src is None else src, dst_ref=rows(a, *block),
                send_sem=send_sems.at[a, k], recv_sem=recv_sems.at[a, k],
                device_id=to, device_id_type=MESH)

        mine = [pltpu.make_async_copy(x_refs[a], rows(a, *me), local_sems.at[a]) for a in range(n)]
        for cp in mine:
            cp.start()
        first = []
        for a in range(n):
            first.append(copy(a, 0, me, sibling, src=x_refs[a]))
            first += [copy(a, 1 + j, me, (*chip, c), src=x_refs[a]) for j, chip in enumerate(chips)]
        for cp in first:
            cp.start()
        passed = []
        for j, chip in enumerate(chips):
            for a in range(n):
                copy(a, 1 + j, (*chip, c), me).wait_recv()
                cp = copy(a, 4 + j, (*chip, c), sibling)
                cp.start()
                passed.append(cp)
        for a in range(n):
            copy(a, 0, sibling, me).wait_recv()
            for j, chip in enumerate(chips):
                copy(a, 4 + j, (*chip, 1 - c), me).wait_recv()
        for cp in first + passed:
            cp.wait_send()
        for cp in mine:
            cp.wait()

    anyspec = pl.BlockSpec(memory_space=pl.ANY)
    outs = pl.pallas_call(
        body, name=name,
        out_shape=[_sds((NDEV,) + s.shape, s.dtype) for s in shards] + [_sds((8, 128), f32)],
        in_specs=[anyspec] * n, out_specs=[anyspec] * n + [pl.BlockSpec(memory_space=pltpu.VMEM)],
        scratch_shapes=[pltpu.SemaphoreType.DMA((n, 7)), pltpu.SemaphoreType.DMA((n, 7)), pltpu.SemaphoreType.DMA((n,))],
    )(*shards)
    return list(outs[:n]), outs[n]


def to_bf16(name, w3, l):
    _, R, C = w3.shape
    tm = _tile(R, max(16, 2 * EW_TILE_ELEMS // C), 16)

    def body(x_ref, o_ref):
        o_ref[...] = x_ref[...].astype(bf16)

    return pl.pallas_call(
        body, name=name, grid=(R // tm,), in_specs=[pl.BlockSpec((None, tm, C), lambda i: (l, i, 0))],
        out_specs=pl.BlockSpec((tm, C), lambda i: (i, 0)), out_shape=_sds((R, C), bf16),
        compiler_params=_params(("parallel",)))(w3)


HBM_SPEC = pl.BlockSpec(memory_space=pltpu.HBM)
SEM_SPEC = pl.BlockSpec(memory_space=pltpu.SEMAPHORE)
EFFECT = pltpu.SideEffectType.DATAFLOW_SIDE_EFFECTING


def exchange_start(name, srcs, gather, after=None):
    n = len(srcs)
    order = [] if after is None else [after]
    ns = n * (NDEV - 1)
    n_sem = 2 * ns + n
    lands = [lax.empty(((NDEV,) + s.shape) if gather else s.shape, s.dtype) for s in srcs]

    def body(*refs):
        src_refs, land_refs = refs[:n], refs[n:2 * n]
        first = 2 * n + len(order)
        send_sems, recv_sems = refs[first:first + ns], refs[first + ns:first + 2 * ns]
        local_sems = refs[first + 2 * ns:first + n_sem]
        token = refs[-1]
        me_ = _me()
        for k in range(1, NDEV):
            px, py, pc = _peer(k)
            p = 4 * px + 2 * py + pc
            for a in range(n):
                i = a * (NDEV - 1) + k - 1
                pltpu.make_async_remote_copy(
                    src_ref=src_refs[a] if gather else src_refs[a].at[p], dst_ref=land_refs[a].at[me_],
                    send_sem=send_sems[i], recv_sem=recv_sems[i],
                    device_id=(px, py, pc), device_id_type=MESH).start()
        for a in range(n):
            pltpu.make_async_copy(src_refs[a] if gather else src_refs[a].at[me_], land_refs[a].at[me_], local_sems[a]).start()
        token[...] = jnp.zeros_like(token)

    outs = pl.pallas_call(
        body, name=name,
        out_shape=(*[pltpu.SemaphoreType.DMA(())] * n_sem,
                   *[pltpu.HBM(s.shape, s.dtype) for s in srcs], *[pltpu.HBM(l.shape, l.dtype) for l in lands],
                   _sds((8, 128), f32)),
        in_specs=[HBM_SPEC] * (2 * n) + [pl.BlockSpec(memory_space=pl.ANY)] * len(order),
        out_specs=(*[SEM_SPEC] * n_sem, *[HBM_SPEC] * (2 * n), pl.BlockSpec(memory_space=pltpu.VMEM)),
        input_output_aliases={i: n_sem + i for i in range(2 * n)},
        compiler_params=pltpu.CompilerParams(has_side_effects=EFFECT),
    )(*[pltpu.with_memory_space_constraint(s, pltpu.HBM) for s in srcs],
      *[pltpu.with_memory_space_constraint(l, pltpu.HBM) for l in lands], *order)
    return dict(n=n, sems=outs[:n_sem], srcs=outs[n_sem:n_sem + n], lands=outs[n_sem + n:n_sem + 2 * n], token=outs[-1], gather=gather)


def exchange_wait(name, hd, after):
    n, gather = hd["n"], hd["gather"]
    ns = n * (NDEV - 1)

    def body(*refs):
        src_refs, land_refs = refs[:n], refs[n:2 * n]
        send_sems, recv_sems = refs[2 * n:2 * n + ns], refs[2 * n + ns:2 * n + 2 * ns]
        local_sems = refs[2 * n + 2 * ns:2 * n + 2 * ns + n]
        x, y, c = _coords()
        me_ = _me()
        for a in range(n):
            pltpu.make_async_copy(src_refs[a] if gather else src_refs[a].at[me_], land_refs[a].at[me_], local_sems[a]).wait()
        for k in range(1, NDEV):
            px, py, pc = _peer(k)
            p = 4 * px + 2 * py + pc
            for a in range(n):
                i = a * (NDEV - 1) + k - 1
                cp = pltpu.make_async_remote_copy(
                    src_ref=src_refs[a] if gather else src_refs[a].at[p], dst_ref=land_refs[a].at[p],
                    send_sem=send_sems[i], recv_sem=recv_sems[i],
                    device_id=(x, y, 1 - c), device_id_type=MESH)
                cp.wait_send()
                cp.wait_recv()

    outs = pl.pallas_call(
        body, name=name,
        out_shape=tuple(pltpu.HBM(s.shape, s.dtype) for s in list(hd["srcs"]) + list(hd["lands"])),
        in_specs=[HBM_SPEC] * (2 * n) + [SEM_SPEC] * (2 * ns + n) + [pl.BlockSpec(memory_space=pl.ANY)],
        out_specs=tuple([HBM_SPEC] * (2 * n)),
        input_output_aliases={i: i for i in range(2 * n)},
        compiler_params=pltpu.CompilerParams(has_side_effects=EFFECT),
    )(*hd["srcs"], *hd["lands"], *hd["sems"], after)
    return list(outs[n:])


def _adam_math(w_, m_, v_, r_):
    g = r_[0].astype(f32)
    for j in range(1, NDEV):
        g = g + r_[j].astype(f32)
    m2 = ADAM_B1 * m_ + (1.0 - ADAM_B1) * g
    v2 = ADAM_B2 * v_ + (1.0 - ADAM_B2) * jnp.square(g)
    m_hat = m2 / (1.0 - ADAM_B1 ** ADAM_STEP)
    v_hat = v2 / (1.0 - ADAM_B2 ** ADAM_STEP)
    delta = -ADAM_LR * (m_hat / (jnp.sqrt(v_hat) + ADAM_EPS) + ADAM_WD * w_)
    return g, delta, m2, v2


def adamw(name, w, m, v, recv):
    return _ew(name, _adam_math, [w, m, v, recv], [], [(w.shape[1], f32)] * 4, budget=ADAM_TILE_ELEMS)


def adamw_layer(name, w3, m3, v3, recv, l, prev):
    L, R, C = w3.shape
    tm = _tile(R, max(16, ADAM_TILE_ELEMS // (NDEV * C)), 16)
    n_prev = 0 if prev is None else 4

    def body(*refs):
        w_ref, m_ref, v_ref, r_ref = refs[:4]
        o_refs = refs[4 + n_prev:]
        res = _adam_math(w_ref[...], m_ref[...], v_ref[...], r_ref[...])
        for o, val in zip(o_refs, res):
            o[...] = val

    lay = pl.BlockSpec((None, tm, C), lambda i: (l, i, 0))
    return pl.pallas_call(
        body, name=name, grid=(R // tm,),
        in_specs=[lay, lay, lay, pl.BlockSpec((NDEV, tm, C), lambda i: (0, i, 0))] + [pl.BlockSpec(memory_space=pl.ANY)] * n_prev,
        out_specs=[lay] * 4, out_shape=[_sds((L, R, C), f32)] * 4,
        input_output_aliases={4 + i: i for i in range(n_prev)},
        compiler_params=_params(("parallel",)),
    )(w3, m3, v3, recv, *(prev or []))


def rmsnorm_fwd(name, x, g):
    D = x.shape[1]

    def fn(x_, g_):
        r = lax.rsqrt(jnp.mean(x_ * x_, axis=-1, keepdims=True) + EPS)
        return ((x_ * r) * g_,)

    return _ew(name, fn, [x], [g], [(D, bf16)])[0]


def rmsnorm_bwd(name, x, g, dh, dres):
    D = x.shape[1]

    def fn(x_, dh_, dres_, g_):
        r = lax.rsqrt(jnp.mean(x_ * x_, axis=-1, keepdims=True) + EPS)
        xh = x_ * r
        dxh = dh_ * g_
        dx = dres_ + r * (dxh - xh * jnp.mean(dxh * xh, axis=-1, keepdims=True))
        return dx, dx, jnp.sum(dh_ * xh, axis=0, keepdims=True)

    return _ew(name, fn, [x, dh, dres], [g], [(D, f32), (D, bf16)], [D])


def _silu_parts(a):
    sig = jax.nn.sigmoid(a)
    return sig, a * sig


def mm_cols(name, h, w3_list, epilogue=None, extra=(), outs=None):
    T, K = h.shape
    Nb = w3_list[0].shape[2]
    tm = _tile(T, 1024, 16)
    outs = outs or [f32]
    blk = pl.BlockSpec((None, tm, Nb), lambda j, i: (j, i, 0))
    return _mm(name, [h], w3_list, pl.BlockSpec((tm, K), lambda j, i: (i, 0)), pl.BlockSpec((None, K, Nb), lambda j, i: (j, 0, 0)),
               NN, (NDEV, T // tm), [_sds((NDEV, T, Nb), dt) for dt in outs], [blk] * len(outs), epilogue, extra, [blk] * len(extra),
               separate=True)


def mm_rows(name, p_list, w3_list, epilogue=None, extra=(), outs=None, after=None):
    _, T, Kb = p_list[0].shape
    N = w3_list[0].shape[2]
    tm, tn = _tile(T, 512, 16), _tile(N, 512 // len(p_list), 128)
    outs = outs or [f32]
    blk = pl.BlockSpec((tm, tn), lambda i, n: (i, n))
    return _mm(name, p_list, w3_list, pl.BlockSpec((NDEV, tm, Kb), lambda i, n: (0, i, 0)), pl.BlockSpec((NDEV, Kb, tn), lambda i, n: (0, 0, n)),
               NN, (T // tm, N // tn), [_sds((T, N), dt) for dt in outs], [blk] * len(outs), epilogue, extra, [blk] * len(extra), nsub=NDEV,
               after=after)


def mm_rows_plain(name, a, w3, epilogue=None, extra=(), extra_specs=None, outs=None):
    T, K = a.shape
    Kb, N = w3.shape[1], w3.shape[2]
    tm, tn = _tile(T, 512, 16), _tile(N, 1024, 128)
    outs = outs or [f32]
    blk = pl.BlockSpec((tm, tn), lambda i, n: (i, n))
    return _mm(name, [a], [w3], pl.BlockSpec((tm, K), lambda i, n: (i, 0)), pl.BlockSpec((NDEV, Kb, tn), lambda i, n: (0, 0, n)),
               NN, (T // tm, N // tn), [_sds((T, N), dt) for dt in outs], [blk] * len(outs), epilogue, extra,
               extra_specs or [blk] * len(extra), nsub=NDEV, merge_b=True)


def mm_t_rows(name, d, w3_list, epilogue=None, extra=(), outs=None, rows=512):
    T, K = d.shape
    Nb = w3_list[0].shape[1]
    tm = _tile(T, rows, 16)
    outs = outs or [f32]
    blk = pl.BlockSpec((None, tm, Nb), lambda j, i: (j, i, 0))
    return _mm(name, [d], w3_list, pl.BlockSpec((tm, K), lambda j, i: (i, 0)), pl.BlockSpec((None, Nb, K), lambda j, i: (j, 0, 0)),
               NT, (NDEV, T // tm), [_sds((NDEV, T, Nb), dt) for dt in outs], [blk] * len(outs), epilogue, extra, [blk] * len(extra),
               separate=True)


def mm_t_rows_plain(name, d, w3, epilogue=None, extra=(), outs=None, after=None):
    T, K = d.shape
    Nb = w3.shape[1]
    tm = _tile(T, 256, 16)
    outs = outs or [f32]
    blk = pl.BlockSpec((tm, NDEV * Nb), lambda i: (i, 0))
    return _mm(name, [d], [w3], pl.BlockSpec((tm, K), lambda i: (i, 0)), pl.BlockSpec((NDEV, Nb, K), lambda i: (0, 0, 0)),
               NT, (T // tm,), [_sds((T, NDEV * Nb), dt) for dt in outs], [blk] * len(outs), epilogue, extra, [blk] * len(extra),
               nsub=NDEV, merge_b=True, after=after)


def mm_t_cols(name, d_list, w3_list, epilogue=None, extra=(), outs=None, after=None):
    _, T, Kb = d_list[0].shape
    N = w3_list[0].shape[1]
    tm, tn = _tile(T, 512, 16), _tile(N, 256, 128)
    outs = outs or [f32]
    blk = pl.BlockSpec((tm, tn), lambda i, n: (i, n))
    return _mm(name, d_list, w3_list, pl.BlockSpec((NDEV, tm, Kb), lambda i, n: (0, i, 0)), pl.BlockSpec((NDEV, tn, Kb), lambda i, n: (0, n, 0)),
               NT, (T // tm, N // tn), [_sds((T, N), dt) for dt in outs], [blk] * len(outs), epilogue, extra, [blk] * len(extra), nsub=NDEV,
               after=after)


def mm_grad_rows(name, p, d, scale=1.0):
    _, T, Mb = p.shape
    N = d.shape[1]
    tn = _tile(N, 512, 128)
    return _mm(name, [p], [d], pl.BlockSpec((None, T, Mb), lambda j, n: (j, 0, 0)), pl.BlockSpec((T, tn), lambda j, n: (0, n)),
               TNd, (NDEV, N // tn), [_sds((NDEV, Mb, N), bf16)], [pl.BlockSpec((None, Mb, tn), lambda j, n: (j, 0, n))],
               (lambda acc: (acc * scale,)))[0]


def mm_grad_rows_plain(name, a, d):
    T, M = a.shape
    N = d.shape[1]
    tm, tn = _tile(M, 512, 128), _tile(N, 512, 128)
    g = _mm(name, [a], [d], pl.BlockSpec((T, tm), lambda m, n: (0, m)), pl.BlockSpec((T, tn), lambda m, n: (0, n)),
            TNd, (M // tm, N // tn), [_sds((M, N), bf16)], [pl.BlockSpec((tm, tn), lambda m, n: (m, n))])[0]
    return g.reshape(NDEV, M // NDEV, N)


def mm_grad_cols(name, h, d):
    T, M = h.shape
    Nb = d.shape[2]
    tm = _tile(M, 512, 128)
    return _mm(name, [h], [d], pl.BlockSpec((T, tm), lambda j, m: (0, m)), pl.BlockSpec((None, T, Nb), lambda j, m: (j, 0, 0)),
               TNd, (NDEV, M // tm), [_sds((NDEV, M, Nb), bf16)], [pl.BlockSpec((None, tm, Nb), lambda j, m: (j, m, 0))])[0]


def ffn_fwd(tag, x, g, wg3, wu3, wd3):
    h = rmsnorm_fwd(tag + "_norm", x, g)

    def gate_up_epi(a_, b_):
        _, sl = _silu_parts(a_)
        return a_, b_, sl * b_

    a, b, p = mm_t_rows(tag + "_gateup", h, [wg3, wu3], gate_up_epi, outs=[bf16, bf16, bf16], rows=1024)
    xo = mm_rows(tag + "_down", [p], [wd3], lambda acc, x_: (x_ + FFN_RES * acc,), [x])[0]
    return xo, (x, h, a, b, p)


def ffn_bwd(tag, dxo, dxo16, saved, g, wg3, wu3, wd3, send, each=False):
    x, h, a, b, p = saved

    def dp_epi(acc, a_, b_):
        dp = FFN_RES * acc
        a_, b_ = a_.astype(f32), b_.astype(f32)
        sig, sl = _silu_parts(a_)
        return dp * b_ * (sig * (1.0 + a_ * (1.0 - sig))), dp * sl

    da, db = mm_t_rows(tag + "_dp", dxo16, [wd3], dp_epi, [a, b], [bf16, bf16], rows=1024)
    g_wd = mm_grad_rows(tag + "_gwd", p, dxo16, FFN_RES)
    if each:
        send([g_wd], [2])
    g_wg = mm_grad_rows(tag + "_gwg", da, h)
    if each:
        send([g_wg], [0])
    g_wu = mm_grad_rows(tag + "_gwu", db, h)
    tok = send([g_wu], [1]) if each else send([g_wg, g_wu, g_wd], [0, 1, 2])
    dh = mm_rows(tag + "_dh", [da, db], [wg3, wu3], after=tok)[0]
    return rmsnorm_bwd(tag + "_dnorm", x, g, dh, dxo)


def _disc(lr, li, ldt, brt, bit):
    dt = jnp.exp(ldt)
    lr = jnp.minimum(lr, -1e-4)
    mag = jnp.exp(lr * dt)
    ab_re = mag * jnp.cos(li * dt)
    ab_im = mag * jnp.sin(li * dt)
    den = lr * lr + li * li
    n_re = ab_re - 1.0
    f_re = (n_re * lr + ab_im * li) / den
    f_im = (ab_im * lr - n_re * li) / den
    bb_re = f_re[:, None, :] * brt - f_im[:, None, :] * bit
    bb_im = f_re[:, None, :] * bit + f_im[:, None, :] * brt
    return ab_re, ab_im, bb_re, bb_im


def disc_fwd(lr, li, ldt, brt, bit):
    def body(lr_ref, li_ref, ldt_ref, brt_ref, bit_ref, ar_ref, ai_ref, bbr_ref, bbi_ref):
        ar, ai, bbr, bbi = _disc(lr_ref[...], li_ref[...], ldt_ref[...], brt_ref[...], bit_ref[...])
        ar_ref[...] = ar
        ai_ref[...] = ai
        bbr_ref[...] = bbr
        bbi_ref[...] = bbi

    return pl.pallas_call(body, name="s5_disc", out_shape=[_sds(lr.shape, f32), _sds(lr.shape, f32), _sds(brt.shape, f32), _sds(brt.shape, f32)],
                          compiler_params=pltpu.CompilerParams(vmem_limit_bytes=VMEM_LIMIT_V7X))(lr, li, ldt, brt, bit)


def disc_bwd(lr, li, ldt, brt, bit, d_ar, d_ai, d_bbr, d_bbi):
    def body(lr_ref, li_ref, ldt_ref, brt_ref, bit_ref, dar_ref, dai_ref, dbbr_ref, dbbi_ref, o_lr, o_li, o_ldt, o_brt, o_bit):
        _, vjp = jax.vjp(_disc, lr_ref[...], li_ref[...], ldt_ref[...], brt_ref[...], bit_ref[...])
        g = vjp((dar_ref[...], dai_ref[...], dbbr_ref[...], dbbi_ref[...]))
        o_lr[...] = g[0]
        o_li[...] = g[1]
        o_ldt[...] = g[2]
        o_brt[...] = g[3]
        o_bit[...] = g[4]

    return pl.pallas_call(body, name="s5_disc_bwd",
                          out_shape=[_sds(lr.shape, f32), _sds(lr.shape, f32), _sds(ldt.shape, f32), _sds(brt.shape, f32), _sds(brt.shape, f32)],
                          compiler_params=pltpu.CompilerParams(vmem_limit_bytes=VMEM_LIMIT_V7X))(lr, li, ldt, brt, bit, d_ar, d_ai, d_bbr, d_bbi)


def _blockdiag(m):
    G, R, C = m.shape
    eye = jnp.eye(GB, dtype=m.dtype)
    m5 = m.reshape(G // GB, GB, R, 1, C) * eye[None, :, None, :, None]
    return m5.reshape(G // GB, GB * R, GB * C)


def _diag_extract(M, R, C):
    nb = M.shape[0]
    eye = jnp.eye(GB, dtype=M.dtype)
    m5 = M.reshape(nb, GB, R, GB, C) * eye[None, :, None, :, None]
    return m5.sum(axis=3).reshape(nb * GB, R, C)


def mm_blockdiag(name, a_list, bd_list, dims, epilogue=None, extra=(), extra_bcast=(), outs=None):
    T = a_list[0].shape[0]
    nb, r, c = bd_list[0].shape
    ra = a_list[0].shape[1] // nb
    ca = c if dims == NN else r
    n_p, n_e, n_b = len(a_list), len(extra), len(extra_bcast)
    outs = outs or [f32]
    tm = _tile(T, 128, 16)

    def body(*refs):
        a_refs, b_refs = refs[:n_p], refs[n_p:2 * n_p]
        e_refs = refs[2 * n_p:2 * n_p + n_e + n_b]
        o_refs = refs[2 * n_p + n_e + n_b:]
        for b in range(nb):
            s = None
            for ar, br in zip(a_refs, b_refs):
                d = lax.dot_general(ar[:, b * ra:(b + 1) * ra].astype(bf16), br[b], dims, preferred_element_type=f32)
                s = d if s is None else s + d
            cols = slice(b * ca, (b + 1) * ca)
            vals = (s,) if epilogue is None else epilogue(s, *[e[:, cols] for e in e_refs])
            for o, val in zip(o_refs, vals):
                o[:, cols] = val.astype(o.dtype)

    row = lambda w: pl.BlockSpec((tm, w), lambda i: (i, 0))
    return pl.pallas_call(
        body, name=name, grid=(T // tm,),
        in_specs=[row(nb * ra)] * n_p + [pl.BlockSpec((nb, r, c), lambda i: (0, 0, 0))] * n_p + [row(nb * ca)] * n_e
        + [pl.BlockSpec((1, nb * ca), lambda i: (0, 0))] * n_b,
        out_specs=[row(nb * ca)] * len(outs), out_shape=[_sds((T, nb * ca), dt) for dt in outs],
        compiler_params=_params(("parallel",)),
    )(*a_list, *bd_list, *extra, *extra_bcast)


def mm_blockdiag_grad(name, a, d, nb):
    T = a.shape[0]
    ra, rd = a.shape[1] // nb, d.shape[1] // nb
    tk = _tile(T, 256, 16)

    def body(a_ref, d_ref, o_ref):
        @pl.when(pl.program_id(0) == 0)
        def _():
            o_ref[...] = jnp.zeros_like(o_ref)

        for b in range(nb):
            o_ref[b] += lax.dot_general(a_ref[:, b * ra:(b + 1) * ra].astype(bf16), d_ref[:, b * rd:(b + 1) * rd].astype(bf16),
                                        TNd, preferred_element_type=f32)

    return pl.pallas_call(
        body, name=name, grid=(T // tk,),
        in_specs=[pl.BlockSpec((tk, nb * ra), lambda k: (k, 0)), pl.BlockSpec((tk, nb * rd), lambda k: (k, 0))],
        out_specs=pl.BlockSpec((nb, ra, rd), lambda k: (0, 0, 0)), out_shape=_sds((nb, ra, rd), f32),
        compiler_params=_params(("arbitrary",)),
    )(a, d)


SUB = 8
SCAN_LANES = 1024
SCAN_GROUPS = 64


def s5_tables(ar, ai):
    N = ar.shape[1]

    def body(ar_ref, ai_ref, f_ref, b_ref):
        r, i = ar_ref[...], ai_ref[...]
        pw = [(r, i)]
        for _ in range(SUB - 1):
            pr, pi = pw[-1]
            pw.append((pr * r - pi * i, pr * i + pi * r))
        row = lax.broadcasted_iota(jnp.int32, (SUB, N), 0)
        for t, k in enumerate((1, 2, 4)):
            pr, pi = pw[k - 1]
            f_ref[2 * t] = jnp.where(row >= k, pr, 0.0)
            f_ref[2 * t + 1] = jnp.where(row >= k, pi, 0.0)
            b_ref[2 * t] = jnp.where(row <= SUB - 1 - k, pr, 0.0)
            b_ref[2 * t + 1] = jnp.where(row <= SUB - 1 - k, -pi, 0.0)
        fr = fi = br = bi = jnp.zeros((SUB, N), f32)
        for j in range(SUB):
            fr, fi = jnp.where(row == j, pw[j][0], fr), jnp.where(row == j, pw[j][1], fi)
            br, bi = jnp.where(row == j, pw[SUB - 1 - j][0], br), jnp.where(row == j, -pw[SUB - 1 - j][1], bi)
        f_ref[6], f_ref[7] = fr, fi
        b_ref[6], b_ref[7] = br, bi

    return pl.pallas_call(body, name="s5_tables", out_shape=[_sds((8, SUB, N), f32)] * 2,
                          compiler_params=pltpu.CompilerParams(vmem_limit_bytes=VMEM_LIMIT_V7X))(ar, ai)


def _group_scan(xr, xi, m, cr, ci, up):
    for t, k in enumerate((1, 2, 4)):
        shift = SUB - k if up else k
        pr, pi = pltpu.roll(xr, shift, 0), pltpu.roll(xi, shift, 0)
        xr, xi = xr + m[2 * t] * pr - m[2 * t + 1] * pi, xi + m[2 * t] * pi + m[2 * t + 1] * pr
    return xr + m[6] * cr - m[7] * ci, xi + m[6] * ci + m[7] * cr


def s5_scan_fwd(bu_re, bu_im, tab):
    T8, _, N = bu_re.shape
    W, tb = _tile(N, SCAN_LANES, 128), _tile(T8, SCAN_GROUPS, 1)

    def body(bur_ref, bui_ref, tab_ref, sr_ref, si_ref, st):
        @pl.when(pl.program_id(1) == 0)
        def _():
            st[...] = jnp.zeros_like(st)

        def step(g, c):
            m = [tab_ref[t] for t in range(8)]
            xr, xi = _group_scan(bur_ref[g], bui_ref[g], m, c[0], c[1], False)
            sr_ref[g] = xr
            si_ref[g] = xi
            return jnp.broadcast_to(xr[SUB - 1:SUB], (SUB, W)), jnp.broadcast_to(xi[SUB - 1:SUB], (SUB, W))

        cr, ci = lax.fori_loop(0, tb, step, (st[0], st[1]))
        st[0] = cr
        st[1] = ci

    blk = pl.BlockSpec((tb, SUB, W), lambda j, i: (i, 0, j))
    tabs = pl.BlockSpec((8, SUB, W), lambda j, i: (0, 0, j))
    return pl.pallas_call(body, name="s5_scan", grid=(N // W, T8 // tb), in_specs=[blk, blk, tabs], out_specs=[blk, blk],
                          out_shape=[_sds(bu_re.shape, f32)] * 2, scratch_shapes=[pltpu.VMEM((2, SUB, W), f32)],
                          compiler_params=_params(("parallel", "arbitrary")))(bu_re, bu_im, tab)


def s5_scan_bwd(ds_re, ds_im, s_re, s_im, tab):
    T8, _, N = ds_re.shape
    W, tb = _tile(N, SCAN_LANES, 128), _tile(T8, SCAN_GROUPS // 2, 1)
    nblk = T8 // tb

    def body(dsr_ref, dsi_ref, sr_ref, si_ref, tab_ref, qr_ref, qi_ref, dar_ref, dai_ref, st):
        @pl.when(pl.program_id(1) == 0)
        def _():
            st[...] = jnp.zeros_like(st)
            dar_ref[...] = jnp.zeros_like(dar_ref)
            dai_ref[...] = jnp.zeros_like(dai_ref)

        last_row = lax.broadcasted_iota(jnp.int32, (SUB, W), 0) == SUB - 1

        def step(i, c):
            cr, ci, dar, dai = c
            g = tb - 1 - i
            m = [tab_ref[t] for t in range(8)]
            xr, xi = _group_scan(dsr_ref[g], dsi_ref[g], m, cr, ci, True)
            qr_ref[g] = xr
            qi_ref[g] = xi
            nr = jnp.where(last_row, cr, pltpu.roll(xr, SUB - 1, 0))
            ni = jnp.where(last_row, ci, pltpu.roll(xi, SUB - 1, 0))
            sr, si = sr_ref[g], si_ref[g]
            dar = dar + nr * sr + ni * si
            dai = dai + ni * sr - nr * si
            return jnp.broadcast_to(xr[0:1], (SUB, W)), jnp.broadcast_to(xi[0:1], (SUB, W)), dar, dai

        cr, ci, dar, dai = lax.fori_loop(0, tb, step, (st[0], st[1], dar_ref[...], dai_ref[...]))
        st[0] = cr
        st[1] = ci
        dar_ref[...] = dar
        dai_ref[...] = dai

        @pl.when(pl.program_id(1) == nblk - 1)
        def _():
            dar_ref[...] = jnp.broadcast_to(jnp.sum(dar, axis=0, keepdims=True), (SUB, W))
            dai_ref[...] = jnp.broadcast_to(jnp.sum(dai, axis=0, keepdims=True), (SUB, W))

    blk = pl.BlockSpec((tb, SUB, W), lambda j, i: (nblk - 1 - i, 0, j))
    tabs = pl.BlockSpec((8, SUB, W), lambda j, i: (0, 0, j))
    acc = pl.BlockSpec((SUB, W), lambda j, i: (0, j))
    return pl.pallas_call(body, name="s5_scan_bwd", grid=(N // W, nblk), in_specs=[blk, blk, blk, blk, tabs], out_specs=[blk, blk, acc, acc],
                          out_shape=[_sds(ds_re.shape, f32)] * 2 + [_sds((SUB, N), f32)] * 2, scratch_shapes=[pltpu.VMEM((2, SUB, W), f32)],
                          compiler_params=_params(("parallel", "arbitrary")))(ds_re, ds_im, s_re, s_im, tab)


def _gelu_grad(y):
    c = math.sqrt(2.0 / math.pi)
    th = jnp.tanh(c * (y + 0.044715 * (y * y * y)))
    return 0.5 * (1.0 + th) + 0.5 * y * (1.0 - th * th) * c * (1.0 + 3.0 * 0.044715 * (y * y))


def s5_prepare(lam_re, lam_im, log_dt, b_re, b_im, c_re, c_im):
    G = lam_re.shape[0]
    ldt = log_dt.reshape(G, 1)
    brt, bit = jnp.swapaxes(b_re, 1, 2), jnp.swapaxes(b_im, 1, 2)
    ar, ai, bbr, bbi = disc_fwd(lam_re, lam_im, ldt, brt, bit)
    tab_f, tab_b = s5_tables(ar.reshape(1, -1), ai.reshape(1, -1))
    mats = dict(
        bbd_re=_blockdiag(bbr).astype(bf16), bbd_im=_blockdiag(bbi).astype(bf16),
        cd_re=_blockdiag(jnp.swapaxes(c_re, 1, 2)).astype(bf16),
        cd_imn=_blockdiag(-jnp.swapaxes(c_im, 1, 2)).astype(bf16),
        tab_f=tab_f, tab_b=tab_b)
    return (lam_re, lam_im, ldt, brt, bit), mats


def s5_fwd(x, gm, w_in3, w_glu3, w_out3, mats, d_skip, b_glu):
    T, D = x.shape
    N = mats["tab_f"].shape[2]
    h = rmsnorm_fwd("s5_norm", x, gm)
    u = mm_rows_plain("s5_in", h, w_in3)[0]
    bu_re = mm_blockdiag("s5_bu_re", [u], [mats["bbd_re"]], NN)[0]
    bu_im = mm_blockdiag("s5_bu_im", [u], [mats["bbd_im"]], NN)[0]
    grouped = (T // SUB, SUB, N)
    s_re, s_im = s5_scan_fwd(bu_re.reshape(grouped), bu_im.reshape(grouped), mats["tab_f"])
    s_re, s_im = s_re.reshape(T, N), s_im.reshape(T, N)

    def y_epi(acc, u_, d_):
        y2 = acc + d_ * u_
        return y2, jax.nn.gelu(y2)

    y2, gl = mm_blockdiag("s5_y", [s_re, s_im], [mats["cd_re"], mats["cd_imn"]], NN, y_epi, [u], [d_skip], [f32, bf16])

    def glu_epi(acc, y2_, b_):
        zg = acc + b_
        return zg, jax.nn.gelu(y2_) * jax.nn.sigmoid(zg)

    tm, tn = _tile(T, 512, 16), _tile(D, 1024, 128)
    zg, o = mm_rows_plain("s5_glu", gl, w_glu3, glu_epi, [y2, b_glu],
                          [pl.BlockSpec((tm, tn), lambda i, n: (i, n)), pl.BlockSpec((1, tn), lambda i, n: (0, n))], [f32, bf16])
    xo = mm_rows_plain("s5_out", o, w_out3, lambda acc, x_: (x_ + acc,), [x])[0]
    return xo, (x, h, u, s_re, s_im, y2, gl, zg, o)


def s5_bwd(dxo, dxo16, saved, gm, w_in3, w_glu3, w_out3, mats, d_skip, send):
    x, h, u, s_re, s_im, y2, gl, zg, o = saved
    T, D = x.shape
    N = mats["tab_b"].shape[2]
    nb = mats["cd_re"].shape[0]

    def do_epi(acc, y2_, zg_):
        sg = jax.nn.sigmoid(zg_)
        return acc * sg, acc * jax.nn.gelu(y2_) * (sg * (1.0 - sg))

    dgl_direct, dzg = mm_t_rows_plain("s5_do", dxo16, w_out3, do_epi, [y2, zg], [f32, f32])
    g_wout = mm_grad_rows_plain("s5_gwout", o, dxo16)
    g_wglu = mm_grad_rows_plain("s5_gwglu", gl, dzg)
    dy2 = mm_t_rows_plain("s5_dgl", dzg, w_glu3, lambda acc, dd_, y2_: ((acc + dd_) * _gelu_grad(y2_),), [dgl_direct, y2])[0]

    def red_fn(dy2_, u_, dzg_, d_):
        return dy2_ * d_, jnp.sum(dy2_ * u_, axis=0, keepdims=True), jnp.sum(dzg_, axis=0, keepdims=True)

    du_direct, dd, dbglu = _ew("s5_dskip", red_fn, [dy2, u, dzg], [d_skip], [(D, f32)], [D, D])
    ds_re = mm_blockdiag("s5_ds_re", [dy2], [mats["cd_re"]], NT)[0]
    ds_im = mm_blockdiag("s5_ds_im", [dy2], [mats["cd_imn"]], NT)[0]
    d_cd_re = mm_blockdiag_grad("s5_gc_re", s_re, dy2, nb)
    d_cd_imn = mm_blockdiag_grad("s5_gc_im", s_im, dy2, nb)
    sh = (T // SUB, SUB, N)
    q_re, q_im, d_ar, d_ai = s5_scan_bwd(ds_re.reshape(sh), ds_im.reshape(sh), s_re.reshape(sh), s_im.reshape(sh), mats["tab_b"])
    q_re, q_im, d_ar, d_ai = q_re.reshape(T, N), q_im.reshape(T, N), d_ar[0], d_ai[0]
    d_bbd_re = mm_blockdiag_grad("s5_gb_re", u, q_re, nb)
    d_bbd_im = mm_blockdiag_grad("s5_gb_im", u, q_im, nb)
    du = mm_blockdiag("s5_du", [q_re, q_im], [mats["bbd_re"], mats["bbd_im"]], NT, lambda acc, dd_: (acc + dd_,), [du_direct])[0]
    g_win = mm_grad_rows_plain("s5_gwin", h, du)
    tok = send([g_win, g_wglu, g_wout], [0, 1, 2])
    dh = mm_t_rows_plain("s5_dh", du, w_in3, after=tok)[0]
    dx, dx16, dgm = rmsnorm_bwd("s5_dnorm", x, gm, dh, dxo)
    small = dict(d_ar=d_ar, d_ai=d_ai, d_bbr=_diag_extract(d_bbd_re, S5H, S5P), d_bbi=_diag_extract(d_bbd_im, S5H, S5P),
                 d_c_re=jnp.swapaxes(_diag_extract(d_cd_re, S5P, S5H), 1, 2), d_c_im=-jnp.swapaxes(_diag_extract(d_cd_imn, S5P, S5H), 1, 2),
                 d_d=dd, d_bglu=dbglu)
    return dx, dx16, dgm, small


def qk_prepass(qkv, g_q, g_k, nh):
    _, T, W = qkv.shape
    cpd = W // HEAD
    tm = _tile(T, 1024, 16)

    def body(x_ref, gq_ref, gk_ref, o_ref):
        which = pl.program_id(0) // nh
        x = x_ref[...]
        r = lax.rsqrt(jnp.mean(x * x, axis=-1, keepdims=True) + EPS)
        g = jnp.where(which == 0, gq_ref[...], gk_ref[...])
        o_ref[...] = jnp.where(which == 2, x, (x * r) * g).astype(o_ref.dtype)

    return pl.pallas_call(
        body, name="sb_qknorm", grid=(3 * nh, T // tm),
        in_specs=[pl.BlockSpec((None, tm, HEAD), lambda ch, i: (ch // cpd, i, ch % cpd)),
                  pl.BlockSpec((1, HEAD), lambda ch, i: (0, 0)), pl.BlockSpec((1, HEAD), lambda ch, i: (0, 0))],
        out_specs=pl.BlockSpec((None, tm, HEAD), lambda ch, i: (ch, i, 0)),
        out_shape=_sds((3 * nh, T, HEAD), bf16), compiler_params=_params(("parallel", "parallel")))(qkv, g_q, g_k)


def qk_prepass_bwd(qkv, g_q, g_k, dqn, dkn, dv, nh):
    _, T, W = qkv.shape
    cpd = W // HEAD
    tm = _tile(T, 1024, 16)

    def body(x_ref, gq_ref, gk_ref, dq_ref, dk_ref, dv_ref, o_ref, dgq_ref, dgk_ref):
        ch = pl.program_id(1)
        which = ch // nh

        @pl.when((pl.program_id(0) == 0) & (ch == 0))
        def _():
            dgq_ref[...] = jnp.zeros_like(dgq_ref)
            dgk_ref[...] = jnp.zeros_like(dgk_ref)

        x = x_ref[...]
        dn = jnp.where(which == 0, dq_ref[...], dk_ref[...])
        g = jnp.where(which == 0, gq_ref[...], gk_ref[...])
        r = lax.rsqrt(jnp.mean(x * x, axis=-1, keepdims=True) + EPS)
        xh = x * r
        dxh = dn * g
        dx = r * (dxh - xh * jnp.mean(dxh * xh, axis=-1, keepdims=True))
        o_ref[...] = jnp.where(which == 2, dv_ref[...], dx).astype(o_ref.dtype)
        dg = jnp.sum(dn * xh, axis=0, keepdims=True)
        dgq_ref[...] += jnp.where(which == 0, dg, 0.0)
        dgk_ref[...] += jnp.where(which == 1, dg, 0.0)

    chunk = pl.BlockSpec((None, tm, HEAD), lambda i, ch: (ch // cpd, i, ch % cpd))
    head = lambda off: pl.BlockSpec((None, tm, HEAD), lambda i, ch: (jnp.clip(ch - off, 0, nh - 1), i, 0))
    vec = pl.BlockSpec((1, HEAD), lambda i, ch: (0, 0))
    return pl.pallas_call(
        body, name="sb_qknorm_bwd", grid=(T // tm, 3 * nh),
        in_specs=[chunk, vec, vec, head(0), head(nh), head(2 * nh)], out_specs=[chunk, vec, vec],
        out_shape=[_sds(qkv.shape, bf16), _sds((1, HEAD), f32), _sds((1, HEAD), f32)],
        compiler_params=_params(("arbitrary", "arbitrary")))(qkv, g_q, g_k, dqn, dkn, dv)


def _split_dot(a, tri):
    hi = a.astype(bf16)
    lo = (a - hi.astype(f32)).astype(bf16)
    return lax.dot_general(hi, tri, NN, preferred_element_type=f32) + lax.dot_general(lo, tri, NN, preferred_element_type=f32)


LOG2E = 1.4426950408889634


def _sb_logits(q, k, scale):
    t = lax.dot_general(q, k, NT, preferred_element_type=f32) * (scale * LOG2E)
    sp = jnp.log2(1.0 + jnp.exp2(-jnp.abs(t)))
    return jnp.minimum(t, 0.0) - sp, -(jnp.maximum(t, 0.0) + sp)


HP = 2


def attn_fwd(qkvn, nh, tq):
    _, T, _ = qkvn.shape
    nq = T // tq
    scale = 1.0 / math.sqrt(HEAD)

    def body(q_ref, k_ref, v_ref, o_ref):
        qi = pl.program_id(1)
        qs = [q_ref[h] for h in range(HP)]
        row = lax.broadcasted_iota(jnp.int32, (tq, tq), 0)
        col = lax.broadcasted_iota(jnp.int32, (tq, tq), 1)
        past = col < row
        tri = (row > col).astype(bf16)

        def blocks(kbs, carry, acc, diag):
            new_c, new_a = [], []
            for h in range(HP):
                parts = []
                for kb in kbs:
                    ks = pl.ds(pl.multiple_of(kb * tq, tq), tq)
                    k, v = k_ref[h, ks, :], v_ref[h, ks, :]
                    lb, lk = _sb_logits(qs[h], k, scale)
                    if diag:
                        lk = jnp.where(past, lk, 0.0)
                    parts.append((lb + _split_dot(lk, tri), jnp.sum(lk, axis=1, keepdims=True), v))
                c, a = carry[h], acc[h]
                for expo, total, v in parts:
                    w = jnp.exp2(expo + c)
                    if diag:
                        w = jnp.where(past, w, 0.0)
                    a = a + lax.dot_general(w.astype(bf16), v, NN, preferred_element_type=f32)
                    c = c + total
                new_a.append(a)
                new_c.append(c)
            return tuple(new_c), tuple(new_a)

        zero_c = tuple(jnp.zeros((tq, 1), f32) for _ in range(HP))
        zero_a = tuple(jnp.zeros((tq, HEAD), f32) for _ in range(HP))
        carry, acc = blocks([qi], zero_c, zero_a, True)
        odd = qi % 2
        carry, acc = lax.cond(odd == 1, lambda c: blocks([qi - 1], c[0], c[1], False), lambda c: c, (carry, acc))
        top = qi - 1 - odd
        carry, acc = lax.fori_loop(0, qi // 2, lambda i, c: blocks([top - 2 * i, top - 2 * i - 1], c[0], c[1], False), (carry, acc))
        for h in range(HP):
            o_ref[:, h * HEAD:(h + 1) * HEAD] = acc[h]

    heads = lambda off: pl.BlockSpec((HP, T, HEAD), lambda hh, i: (off // HP + hh, 0, 0))
    return pl.pallas_call(
        body, name="sb_attn", grid=(nh // HP, nq),
        in_specs=[pl.BlockSpec((HP, tq, HEAD), lambda hh, i: (hh, i, 0)), heads(nh), heads(2 * nh)],
        out_specs=pl.BlockSpec((tq, HP * HEAD), lambda hh, i: (i, hh)),
        out_shape=_sds((T, nh * HEAD), f32), compiler_params=_params(("parallel", "arbitrary")))(qkvn, qkvn, qkvn)


def attn_bwd(qkvn, do, nh, tq):
    _, T, _ = qkvn.shape
    nq = T // tq
    scale = 1.0 / math.sqrt(HEAD)

    def body(q_ref, k_ref, v_ref, do_ref, dq_ref, dk_ref, dv_ref, e_scr, sg_scr):
        qi = pl.program_id(1)

        @pl.when(qi == 0)
        def _():
            dk_ref[...] = jnp.zeros_like(dk_ref)
            dv_ref[...] = jnp.zeros_like(dv_ref)

        qs = [q_ref[h] for h in range(HP)]
        dobs = [do_ref[:, h * HEAD:(h + 1) * HEAD].astype(bf16) for h in range(HP)]
        row = lax.broadcasted_iota(jnp.int32, (tq, tq), 0)
        col = lax.broadcasted_iota(jnp.int32, (tq, tq), 1)
        past = col < row
        tri_later = (row > col).astype(bf16)
        tri_before = (row < col).astype(bf16)

        def sweep1(kbs, carry, diag):
            new_c = []
            for h in range(HP):
                parts = []
                for kb in kbs:
                    ks = pl.ds(pl.multiple_of(kb * tq, tq), tq)
                    k, v = k_ref[h, ks, :], v_ref[h, ks, :]
                    lb, lk = _sb_logits(qs[h], k, scale)
                    if diag:
                        lk = jnp.where(past, lk, 0.0)
                    sg_scr[h, kb] = jnp.exp2(lb)
                    dw = lax.dot_general(dobs[h], v, NT, preferred_element_type=f32)
                    parts.append((kb, ks, lb + _split_dot(lk, tri_later), jnp.sum(lk, axis=1, keepdims=True), dw))
                c = carry[h]
                for kb, ks, expo, total, dw in parts:
                    w = jnp.exp2(expo + c)
                    if diag:
                        w = jnp.where(past, w, 0.0)
                    e_scr[h, kb] = dw * w
                    dv_ref[h, ks, :] += lax.dot_general(w.astype(bf16), dobs[h], TNd, preferred_element_type=f32)
                    c = c + total
                new_c.append(c)
            return tuple(new_c)

        zero_c = tuple(jnp.zeros((tq, 1), f32) for _ in range(HP))
        carry = sweep1([qi], zero_c, True)
        odd = qi % 2
        carry = lax.cond(odd == 1, lambda c: sweep1([qi - 1], c, False), lambda c: c, carry)
        top = qi - 1 - odd
        lax.fori_loop(0, qi // 2, lambda i, c: sweep1([top - 2 * i, top - 2 * i - 1], c, False), carry)

        def sweep2(kbs, carry, dq, diag):
            new_c, new_q = [], []
            for h in range(HP):
                parts = []
                for kb in kbs:
                    e, sg = e_scr[h, kb], sg_scr[h, kb]
                    parts.append((kb, e, sg, _split_dot(e, tri_before), jnp.sum(e, axis=1, keepdims=True)))
                c, dqh = carry[h], dq[h]
                for kb, e, sg, before, total in parts:
                    ks = pl.ds(pl.multiple_of(kb * tq, tq), tq)
                    dz = (e * (1.0 - sg) - (before + c) * sg) * scale
                    if diag:
                        dz = jnp.where(past, dz, 0.0)
                    dzb = dz.astype(bf16)
                    dqh = dqh + lax.dot_general(dzb, k_ref[h, ks, :], NN, preferred_element_type=f32)
                    dk_ref[h, ks, :] += lax.dot_general(dzb, qs[h], TNd, preferred_element_type=f32)
                    c = c + total
                new_q.append(dqh)
                new_c.append(c)
            return tuple(new_c), tuple(new_q)

        zero_q = tuple(jnp.zeros((tq, HEAD), f32) for _ in range(HP))
        carry, dq = lax.fori_loop(0, qi // 2, lambda i, c: sweep2([2 * i, 2 * i + 1], c[0], c[1], False), (zero_c, zero_q))
        carry, dq = lax.cond(odd == 1, lambda c: sweep2([qi - 1], c[0], c[1], False), lambda c: c, (carry, dq))
        _, dq = sweep2([qi], carry, dq, True)
        for h in range(HP):
            dq_ref[h] = dq[h]

    heads = lambda off: pl.BlockSpec((HP, T, HEAD), lambda hh, i: (off // HP + hh, 0, 0))
    return pl.pallas_call(
        body, name="sb_attn_bwd", grid=(nh // HP, nq),
        in_specs=[pl.BlockSpec((HP, tq, HEAD), lambda hh, i: (hh, i, 0)), heads(nh), heads(2 * nh),
                  pl.BlockSpec((tq, HP * HEAD), lambda hh, i: (i, hh))],
        out_specs=[pl.BlockSpec((HP, tq, HEAD), lambda hh, i: (hh, i, 0)), heads(0), heads(0)],
        out_shape=[_sds((nh, T, HEAD), f32)] * 3,
        scratch_shapes=[pltpu.VMEM((HP, nq, tq, tq), f32), pltpu.VMEM((HP, nq, tq, tq), f32)],
        compiler_params=_params(("parallel", "arbitrary")))(qkvn, qkvn, qkvn, do)


def sb_fwd(x, gm, w_qkv3, w_o3, g_q, g_k):
    T, D = x.shape
    nh = D // HEAD
    h = rmsnorm_fwd("sb_norm", x, gm)
    qkv = mm_cols("sb_qkv", h, [w_qkv3])[0]
    qkvn = qk_prepass(qkv, g_q, g_k, nh)
    o = attn_fwd(qkvn, nh, _tile(T, 256, 128))
    xo = mm_rows_plain("sb_o", o, w_o3, lambda acc, x_: (x_ + acc,), [x])[0]
    return xo, (x, h, qkv, qkvn, o)


def sb_bwd(dxo, dxo16, saved, gm, w_qkv3, w_o3, g_q, g_k, send):
    x, h, qkv, qkvn, o = saved
    T, D = x.shape
    nh = D // HEAD
    do = mm_t_rows_plain("sb_do", dxo16, w_o3)[0]
    g_wo = mm_grad_rows_plain("sb_gwo", o, dxo16)
    dqn, dkn, dv = attn_bwd(qkvn, do, nh, _tile(T, 256, 128))
    dqkv, dgq, dgk = qk_prepass_bwd(qkv, g_q, g_k, dqn, dkn, dv, nh)
    g_wqkv = mm_grad_cols("sb_gwqkv", h, dqkv)
    tok = send([g_wqkv, g_wo], [0, 1])
    dh = mm_t_cols("sb_dh", [dqkv], [w_qkv3], after=tok)[0]
    dx, dx16, dgm = rmsnorm_bwd("sb_dnorm", x, gm, dh, dxo)
    return dx, dx16, dgm, dgq, dgk


def loss_head(y, target):
    D = y.shape[1]

    def fn(y_, t_):
        err = y_ - t_
        dy = err / D
        return dy, dy, jnp.sum(err * err, axis=0, keepdims=True)

    dy, dy16, sq = _ew("loss_head", fn, [y, target], [], [(D, f32), (D, bf16)], [D])
    return 0.5 * (jnp.sum(sq) / D), dy, dy16


FFN = ["w_gate", "w_up", "w_down"]
SMALL = ["norm_ffn1", "norm_mix", "s5_lam_re", "s5_lam_im", "s5_log_dt", "s5_b_re", "s5_b_im", "s5_c_re", "s5_c_im",
         "s5_d", "s5_b_glu", "sb_g_q", "sb_g_k", "norm_ffn2"]
WEIGHTS = ["norm_ffn1", "ffn1_w_gate", "ffn1_w_up", "ffn1_w_down", "norm_mix", "s5_w_in", "s5_lam_re", "s5_lam_im", "s5_log_dt",
           "s5_b_re", "s5_b_im", "s5_c_re", "s5_c_im", "s5_d", "s5_w_glu", "s5_b_glu", "s5_w_out", "sb_w_qkv", "sb_g_q", "sb_g_k",
           "sb_w_o", "norm_ffn2", "ffn2_w_gate", "ffn2_w_up", "ffn2_w_down"]
PACK_LANES = 128
PACK_ROWS = 64


def _pack(arrs):
    flat = jnp.concatenate([a.reshape(-1).astype(f32) for a in arrs])
    pad = (-flat.shape[0]) % (PACK_LANES * PACK_ROWS)
    return jnp.pad(flat, (0, pad)).reshape(-1, PACK_LANES)


def _unpack(buf, like):
    flat = buf.reshape(-1)
    out, off = [], 0
    for a in like:
        out.append(flat[off:off + a.size].reshape(a.shape))
        off += a.size
    return out


def _after(token, g):
    return g + token[0, 0]


TRANSPOSED = ("ffn1_w_gate", "ffn1_w_up", "ffn2_w_gate", "ffn2_w_up")


def _step(w, m, v, x, target):
    x = x[0]
    target = target[0]
    w, m, v = [{n: jnp.swapaxes(a, 1, 2) if n in TRANSPOSED else a for n, a in d.items()} for d in (w, m, v)]

    def shard(n, l):
        return to_bf16("cast%d_%s" % (l, n), w[n], l)

    def gain(n, l):
        return w[n][l:l + 1]

    wf1a, tok = all_gather("ag_l0f1", [shard("ffn1_" + s, 0) for s in FFN])
    h_s5 = exchange_start("ag_s5", [shard("s5_w_in", 0), shard("s5_w_glu", 0), shard("s5_w_out", 0)], True, tok)
    h_f2a = exchange_start("ag_l0f2", [shard("ffn2_" + s, 0) for s in FFN], True, h_s5["token"])
    h_f1b = exchange_start("ag_l1f1", [shard("ffn1_" + s, 1) for s in FFN], True, h_f2a["token"])
    h_sb = exchange_start("ag_sb", [shard("sb_w_qkv", 0), shard("sb_w_o", 0)], True, h_f1b["token"])
    h_f2b = exchange_start("ag_l1f2", [shard("ffn2_" + s, 1) for s in FFN], True, h_sb["token"])
    tok = h_f2b["token"]

    s5_in, mats = s5_prepare(w["s5_lam_re"][0], w["s5_lam_im"][0], w["s5_log_dt"][0], w["s5_b_re"][0], w["s5_b_im"][0],
                             w["s5_c_re"][0], w["s5_c_im"][0])

    x1, sv_f1a = ffn_fwd("l0f1", x, _after(tok, gain("norm_ffn1", 0)), *wf1a)
    ws5 = exchange_wait("agw_s5", h_s5, x1)
    x2, sv_s5 = s5_fwd(x1, gain("norm_mix", 0), *ws5, mats, w["s5_d"], w["s5_b_glu"])
    wf2a = exchange_wait("agw_l0f2", h_f2a, x2)
    x3, sv_f2a = ffn_fwd("l0f2", x2, gain("norm_ffn2", 0), *wf2a)
    wf1b = exchange_wait("agw_l1f1", h_f1b, x3)
    x4, sv_f1b = ffn_fwd("l1f1", x3, gain("norm_ffn1", 1), *wf1b)
    wsb = exchange_wait("agw_sb", h_sb, x4)
    x5, sv_sb = sb_fwd(x4, gain("norm_mix", 1), *wsb, w["sb_g_q"], w["sb_g_k"])
    wf2b = exchange_wait("agw_l1f2", h_f2b, x5)
    x6, sv_f2b = ffn_fwd("l1f2", x5, gain("norm_ffn2", 1), *wf2b)
    loss_local, dy, dy16 = loss_head(x6, target)
    loss = lax.psum(loss_local, AXES)

    rs = {}

    def sender(key):
        def send(grads, places):
            sent = rs.setdefault(key, [])
            sent.append((exchange_start("rs_%s_%d" % (key, len(sent)), grads, False), places))
            return sent[-1][0]["token"]
        return send

    d5, d5h, dn_f2b = ffn_bwd("l1f2b", dy, dy16, sv_f2b, gain("norm_ffn2", 1), *wf2b, sender("l1f2"))
    d4, d4h, dn_sb, dgq, dgk = sb_bwd(d5, d5h, sv_sb, gain("norm_mix", 1), *wsb, w["sb_g_q"], w["sb_g_k"], sender("sb"))
    d3, d3h, dn_f1b = ffn_bwd("l1f1b", d4, d4h, sv_f1b, gain("norm_ffn1", 1), *wf1b, sender("l1f1"))
    d2, d2h, dn_f2a = ffn_bwd("l0f2b", d3, d3h, sv_f2a, gain("norm_ffn2", 0), *wf2a, sender("l0f2"))
    d1, d1h, dn_s5, s5s = s5_bwd(d2, d2h, sv_s5, gain("norm_mix", 0), *ws5, mats, w["s5_d"], sender("s5"))
    d0, _, dn_f1a = ffn_bwd("l0f1b", d1, d1h, sv_f1a, gain("norm_ffn1", 0), *wf1a, sender("l0f1"), each=True)
    d_lr, d_li, d_ldt, d_brt, d_bit = disc_bwd(*s5_in, s5s["d_ar"].reshape(s5_in[0].shape), s5s["d_ai"].reshape(s5_in[0].shape),
                                               s5s["d_bbr"], s5s["d_bbi"])

    part = {
        "norm_ffn1": jnp.concatenate([dn_f1a, dn_f1b]), "norm_mix": jnp.concatenate([dn_s5, dn_sb]),
        "norm_ffn2": jnp.concatenate([dn_f2a, dn_f2b]),
        "s5_lam_re": d_lr, "s5_lam_im": d_li, "s5_log_dt": d_ldt, "s5_b_re": jnp.swapaxes(d_brt, 1, 2), "s5_b_im": jnp.swapaxes(d_bit, 1, 2),
        "s5_c_re": s5s["d_c_re"], "s5_c_im": s5s["d_c_im"], "s5_d": s5s["d_d"], "s5_b_glu": s5s["d_bglu"], "sb_g_q": dgq, "sb_g_k": dgk,
    }
    h_small = exchange_start("ag_small", [_pack([part[n].reshape(w[n].shape) for n in SMALL])], True)

    res = {}
    late = h_small["token"]
    for key, names in (("l1f2", [("ffn2_" + s, 1) for s in FFN]), ("sb", [("sb_w_qkv", None), ("sb_w_o", None)]),
                       ("l1f1", [("ffn1_" + s, 1) for s in FFN]), ("l0f2", [("ffn2_" + s, 0) for s in FFN]),
                       ("s5", [("s5_w_in", None), ("s5_w_glu", None), ("s5_w_out", None)]), ("small", None),
                       ("l0f1", [("ffn1_" + s, 0) for s in FFN])):
        if key == "small":
            packed = exchange_wait("agw_small", h_small, late)[0]
            outs = adamw("adamw_small", _pack([w[n] for n in SMALL]), _pack([m[n] for n in SMALL]), _pack([v[n] for n in SMALL]), packed)
            un = [_unpack(o, [w[n] for n in SMALL]) for o in outs]
            for i, n in enumerate(SMALL):
                res[n] = [un[k][i] for k in range(4)]
            late = outs[0]
            continue
        for k, (handle, places) in enumerate(rs[key]):
            recv = exchange_wait("rsw_%s_%d" % (key, k), handle, late)
            for place, r in zip(places, recv):
                n, l = names[place]
                if l is None:
                    res[n] = [o[None] for o in adamw("adamw_" + n, w[n][0], m[n][0], v[n][0], r)]
                else:
                    res[n] = adamw_layer("adamw%d_%s" % (l, n), w[n], m[n], v[n], r, l, res.get(n))
                late = res[n][0]

    for n in TRANSPOSED:
        res[n] = [jnp.swapaxes(o, 1, 2) for o in res[n]]
    return (loss, d0[None], *[res[n][0] for n in WEIGHTS], *[res[n][1] for n in WEIGHTS],
            *[res[n][2] for n in WEIGHTS], *[res[n][3] for n in WEIGHTS])


def kernel(x, norm_ffn1, ffn1_w_gate, ffn1_w_up, ffn1_w_down, norm_mix, s5_w_in, s5_lam_re, s5_lam_im, s5_log_dt, s5_b_re, s5_b_im, s5_c_re, s5_c_im, s5_d, s5_w_glu, s5_b_glu, s5_w_out, sb_w_qkv, sb_g_q, sb_g_k, sb_w_o, norm_ffn2, ffn2_w_gate, ffn2_w_up, ffn2_w_down, loss_target, m_norm_ffn1, m_ffn1_w_gate, m_ffn1_w_up, m_ffn1_w_down, m_norm_mix, m_s5_w_in, m_s5_lam_re, m_s5_lam_im, m_s5_log_dt, m_s5_b_re, m_s5_b_im, m_s5_c_re, m_s5_c_im, m_s5_d, m_s5_w_glu, m_s5_b_glu, m_s5_w_out, m_sb_w_qkv, m_sb_g_q, m_sb_g_k, m_sb_w_o, m_norm_ffn2, m_ffn2_w_gate, m_ffn2_w_up, m_ffn2_w_down, v_norm_ffn1, v_ffn1_w_gate, v_ffn1_w_up, v_ffn1_w_down, v_norm_mix, v_s5_w_in, v_s5_lam_re, v_s5_lam_im, v_s5_log_dt, v_s5_b_re, v_s5_b_im, v_s5_c_re, v_s5_c_im, v_s5_d, v_s5_w_glu, v_s5_b_glu, v_s5_w_out, v_sb_w_qkv, v_sb_g_q, v_sb_g_k, v_sb_w_o, v_norm_ffn2, v_ffn2_w_gate, v_ffn2_w_up, v_ffn2_w_down):
    w = dict(norm_ffn1=norm_ffn1, ffn1_w_gate=ffn1_w_gate, ffn1_w_up=ffn1_w_up, ffn1_w_down=ffn1_w_down, norm_mix=norm_mix, s5_w_in=s5_w_in, s5_lam_re=s5_lam_re, s5_lam_im=s5_lam_im, s5_log_dt=s5_log_dt, s5_b_re=s5_b_re, s5_b_im=s5_b_im, s5_c_re=s5_c_re, s5_c_im=s5_c_im, s5_d=s5_d, s5_w_glu=s5_w_glu, s5_b_glu=s5_b_glu, s5_w_out=s5_w_out, sb_w_qkv=sb_w_qkv, sb_g_q=sb_g_q, sb_g_k=sb_g_k, sb_w_o=sb_w_o, norm_ffn2=norm_ffn2, ffn2_w_gate=ffn2_w_gate, ffn2_w_up=ffn2_w_up, ffn2_w_down=ffn2_w_down)
    m = dict(norm_ffn1=m_norm_ffn1, ffn1_w_gate=m_ffn1_w_gate, ffn1_w_up=m_ffn1_w_up, ffn1_w_down=m_ffn1_w_down, norm_mix=m_norm_mix, s5_w_in=m_s5_w_in, s5_lam_re=m_s5_lam_re, s5_lam_im=m_s5_lam_im, s5_log_dt=m_s5_log_dt, s5_b_re=m_s5_b_re, s5_b_im=m_s5_b_im, s5_c_re=m_s5_c_re, s5_c_im=m_s5_c_im, s5_d=m_s5_d, s5_w_glu=m_s5_w_glu, s5_b_glu=m_s5_b_glu, s5_w_out=m_s5_w_out, sb_w_qkv=m_sb_w_qkv, sb_g_q=m_sb_g_q, sb_g_k=m_sb_g_k, sb_w_o=m_sb_w_o, norm_ffn2=m_norm_ffn2, ffn2_w_gate=m_ffn2_w_gate, ffn2_w_up=m_ffn2_w_up, ffn2_w_down=m_ffn2_w_down)
    v = dict(norm_ffn1=v_norm_ffn1, ffn1_w_gate=v_ffn1_w_gate, ffn1_w_up=v_ffn1_w_up, ffn1_w_down=v_ffn1_w_down, norm_mix=v_norm_mix, s5_w_in=v_s5_w_in, s5_lam_re=v_s5_lam_re, s5_lam_im=v_s5_lam_im, s5_log_dt=v_s5_log_dt, s5_b_re=v_s5_b_re, s5_b_im=v_s5_b_im, s5_c_re=v_s5_c_re, s5_c_im=v_s5_c_im, s5_d=v_s5_d, s5_w_glu=v_s5_w_glu, s5_b_glu=v_s5_b_glu, s5_w_out=v_s5_w_out, sb_w_qkv=v_sb_w_qkv, sb_g_q=v_sb_g_q, sb_g_k=v_sb_g_k, sb_w_o=v_sb_w_o, norm_ffn2=v_norm_ffn2, ffn2_w_gate=v_ffn2_w_gate, ffn2_w_up=v_ffn2_w_up, ffn2_w_down=v_ffn2_w_down)
    return _step(w, m, v, x, loss_target)
```

```python
import math

import jax
import jax.numpy as jnp
from jax import lax
from jax.experimental import pallas as pl
from jax.experimental.pallas import tpu as pltpu

f32 = jnp.float32
bf16 = jnp.bfloat16

NDEV = 8
AXES = ("x", "y", "c")
MESH = pl.DeviceIdType.MESH
EPS = 1e-6
HEAD = 128
S5H = 16
S5P = 64
GB = 8
FFN_RES = 0.5
ADAM_LR = 0.001
ADAM_B1 = 0.9
ADAM_B2 = 0.999
ADAM_EPS = 1e-08
ADAM_WD = 0.01
ADAM_STEP = 10
VMEM_LIMIT_V7X = 56 * 2 ** 20

NN = (((1,), (0,)), ((), ()))
NT = (((1,), (1,)), ((), ()))
TNd = (((0,), (0,)), ((), ()))


def _tile(n, target, align):
    if n <= target:
        return n
    t = (target // align) * align
    while t >= align:
        if n % t == 0:
            return t
        t -= align
    return n


def _params(sem):
    return pltpu.CompilerParams(dimension_semantics=sem, vmem_limit_bytes=VMEM_LIMIT_V7X)


def _sds(shape, dtype):
    return jax.ShapeDtypeStruct(tuple(shape), dtype)


def _mm(name, a_list, b_list, a_spec, b_spec, dims, grid, out_shapes, out_specs,
        epilogue=None, extra=(), extra_specs=(), nsub=1, merge_b=False, after=None, separate=False):
    n_a, n_p, n_e = len(a_list), len(b_list), len(extra)
    order = [] if after is None else [after]

    def body(*refs):
        a_refs = refs[:n_a]
        b_refs = refs[n_a:n_a + n_p]
        e_refs = refs[n_a + n_p:n_a + n_p + n_e]
        o_refs = refs[n_a + n_p + n_e + len(order):]
        prods = []
        for p, br in enumerate(b_refs):
            ar = a_refs[p if n_a > 1 else 0]
            if nsub > 1 and not merge_b:
                for j in range(nsub):
                    prods.append(lax.dot_general(ar[j].astype(bf16), br[j].astype(bf16), dims, preferred_element_type=f32))
            else:
                b = br[...]
                if merge_b:
                    b = b.reshape(b.shape[0] * b.shape[1], b.shape[2])
                prods.append(lax.dot_general(ar[...].astype(bf16), b.astype(bf16), dims, preferred_element_type=f32))
        if not separate:
            s = prods[0]
            for d in prods[1:]:
                s = s + d
            prods = [s]
        outs = tuple(prods) if epilogue is None else epilogue(*prods, *[e[...] for e in e_refs])
        for o, val in zip(o_refs, outs):
            o[...] = val.astype(o.dtype)

    return pl.pallas_call(
        body, name=name, grid=grid,
        in_specs=[a_spec] * n_a + [b_spec] * n_p + list(extra_specs) + [pl.BlockSpec(memory_space=pl.ANY)] * len(order),
        out_specs=list(out_specs), out_shape=list(out_shapes),
        compiler_params=_params(("parallel",) * len(grid)),
    )(*a_list, *b_list, *extra, *order)


EW_TILE_ELEMS = 1 << 19
ADAM_TILE_ELEMS = 1 << 21


def _ew(name, fn, rows, bcasts, outs, reds=(), tm=None, budget=EW_TILE_ELEMS):
    n_r, n_b, n_o, n_d = len(rows), len(bcasts), len(outs), len(reds)
    R = rows[0].shape[-2]
    if tm is None:
        widest = max([r.shape[-1] * (r.shape[0] if r.ndim == 3 else 1) for r in rows] + [c for c, _ in outs])
        tm = _tile(R, max(16, budget // widest), 16)

    def body(*refs):
        r = refs[:n_r]
        b = refs[n_r:n_r + n_b]
        o = refs[n_r + n_b:n_r + n_b + n_o]
        d = refs[n_r + n_b + n_o:]
        res = fn(*[x[...] for x in r], *[x[...] for x in b])
        for oo, val in zip(o, res[:n_o]):
            oo[...] = val.astype(oo.dtype)
        if n_d:
            @pl.when(pl.program_id(0) == 0)
            def _():
                for dd in d:
                    dd[...] = jnp.zeros_like(dd)
            for dd, val in zip(d, res[n_o:]):
                dd[...] += val

    in_specs = []
    for x in rows:
        if x.ndim == 2:
            in_specs.append(pl.BlockSpec((tm, x.shape[1]), lambda i: (i, 0)))
        else:
            in_specs.append(pl.BlockSpec((x.shape[0], tm, x.shape[2]), lambda i: (0, i, 0)))
    for x in bcasts:
        in_specs.append(pl.BlockSpec(x.shape, lambda i, nd=x.ndim: (0,) * nd))
    out_shape = [_sds((R, c), dt) for c, dt in outs] + [_sds((1, c), f32) for c in reds]
    out_specs = [pl.BlockSpec((tm, c), lambda i: (i, 0)) for c, _ in outs] + [pl.BlockSpec((1, c), lambda i: (0, 0)) for c in reds]
    return pl.pallas_call(
        body, name=name, grid=(R // tm,), in_specs=in_specs, out_specs=out_specs, out_shape=out_shape,
        compiler_params=_params(("arbitrary",)),
    )(*rows, *bcasts)


def _coords():
    return lax.axis_index("x"), lax.axis_index("y"), lax.axis_index("c")


def _me():
    x, y, c = _coords()
    return 4 * x + 2 * y + c


def _peer(k):
    x, y, c = _coords()
    return x ^ ((k >> 2) & 1), y ^ ((k >> 1) & 1), c ^ (k & 1)


def all_gather(name, shards):
    n = len(shards)

    def body(*refs):
        x_refs, out_refs, token = refs[:n], refs[n:2 * n], refs[2 * n]
        send_sems, recv_sems, local_sems = refs[2 * n + 1:]
        token[...] = jnp.zeros_like(token)
        x, y, c = _coords()
        me, sibling = (x, y, c), (x, y, 1 - c)
        chips = [(1 - x, y), (x, 1 - y), (1 - x, 1 - y)]

        def rows(a, px, py, pc):
            return out_refs[a].at[4 * px + 2 * py + pc]

        def copy(a, k, block, to, src=None):
            return pltpu.make_async_remote_copy(
                src_ref=rows(a, *block) if src is None else src, dst_ref=rows(a, *block),
                send_sem=send_sems.at[a, k], recv_sem=recv_sems.at[a, k],
                device_id=to, device_id_type=MESH)

        mine = [pltpu.make_async_copy(x_refs[a], rows(a, *me), local_sems.at[a]) for a in range(n)]
        for cp in mine:
            cp.start()
        first = []
        for a in range(n):
            first.append(copy(a, 0, me, sibling, src=x_refs[a]))
            first += [copy(a, 1 + j, me, (*chip, c), src=x_refs[a]) for j, chip in enumerate(chips)]
        for cp in first:
            cp.start()
        passed = []
        for j, chip in enumerate(chips):
            for a in range(n):
                copy(a, 1 + j, (*chip, c), me).wait_recv()
                cp = copy(a, 4 + j, (*chip, c), sibling)
                cp.start()
                passed.append(cp)
        for a in range(n):
            copy(a, 0, sibling, me).wait_recv()
            for j, chip in enumerate(chips):
                copy(a, 4 + j, (*chip, 1 - c), me).wait_recv()
        for cp in first + passed:
            cp.wait_send()
        for cp in mine:
            cp.wait()

    anyspec = pl.BlockSpec(memory_space=pl.ANY)
    outs = pl.pallas_call(
        body, name=name,
        out_shape=[_sds((NDEV,) + s.shape, s.dtype) for s in shards] + [_sds((8, 128), f32)],
        in_specs=[anyspec] * n, out_specs=[anyspec] * n + [pl.BlockSpec(memory_space=pltpu.VMEM)],
        scratch_shapes=[pltpu.SemaphoreType.DMA((n, 7)), pltpu.SemaphoreType.DMA((n, 7)), pltpu.SemaphoreType.DMA((n,))],
    )(*shards)
    return list(outs[:n]), outs[n]


def to_bf16(name, w3, l):
    _, R, C = w3.shape
    tm = _tile(R, max(16, 2 * EW_TILE_ELEMS // C), 16)

    def body(x_ref, o_ref):
        o_ref[...] = x_ref[...].astype(bf16)

    return pl.pallas_call(
        body, name=name, grid=(R // tm,), in_specs=[pl.BlockSpec((None, tm, C), lambda i: (l, i, 0))],
        out_specs=pl.BlockSpec((tm, C), lambda i: (i, 0)), out_shape=_sds((R, C), bf16),
        compiler_params=_params(("parallel",)))(w3)


HBM_SPEC = pl.BlockSpec(memory_space=pltpu.HBM)
SEM_SPEC = pl.BlockSpec(memory_space=pltpu.SEMAPHORE)
EFFECT = pltpu.SideEffectType.DATAFLOW_SIDE_EFFECTING


def exchange_start(name, srcs, gather, after=None):
    n = len(srcs)
    order = [] if after is None else [after]
    ns = n * (NDEV - 1)
    n_sem = 2 * ns + n
    lands = [lax.empty(((NDEV,) + s.shape) if gather else s.shape, s.dtype) for s in srcs]

    def body(*refs):
        src_refs, land_refs = refs[:n], refs[n:2 * n]
        first = 2 * n + len(order)
        send_sems, recv_sems = refs[first:first + ns], refs[first + ns:first + 2 * ns]
        local_sems = refs[first + 2 * ns:first + n_sem]
        token = refs[-1]
        me_ = _me()
        for k in range(1, NDEV):
            px, py, pc = _peer(k)
            p = 4 * px + 2 * py + pc
            for a in range(n):
                i = a * (NDEV - 1) + k - 1
                pltpu.make_async_remote_copy(
                    src_ref=src_refs[a] if gather else src_refs[a].at[p], dst_ref=land_refs[a].at[me_],
                    send_sem=send_sems[i], recv_sem=recv_sems[i],
                    device_id=(px, py, pc), device_id_type=MESH).start()
        for a in range(n):
            pltpu.make_async_copy(src_refs[a] if gather else src_refs[a].at[me_], land_refs[a].at[me_], local_sems[a]).start()
        token[...] = jnp.zeros_like(token)

    outs = pl.pallas_call(
        body, name=name,
        out_shape=(*[pltpu.SemaphoreType.DMA(())] * n_sem,
                   *[pltpu.HBM(s.shape, s.dtype) for s in srcs], *[pltpu.HBM(l.shape, l.dtype) for l in lands],
                   _sds((8, 128), f32)),
        in_specs=[HBM_SPEC] * (2 * n) + [pl.BlockSpec(memory_space=pl.ANY)] * len(order),
        out_specs=(*[SEM_SPEC] * n_sem, *[HBM_SPEC] * (2 * n), pl.BlockSpec(memory_space=pltpu.VMEM)),
        input_output_aliases={i: n_sem + i for i in range(2 * n)},
        compiler_params=pltpu.CompilerParams(has_side_effects=EFFECT),
    )(*[pltpu.with_memory_space_constraint(s, pltpu.HBM) for s in srcs],
      *[pltpu.with_memory_space_constraint(l, pltpu.HBM) for l in lands], *order)
    return dict(n=n, sems=outs[:n_sem], srcs=outs[n_sem:n_sem + n], lands=outs[n_sem + n:n_sem + 2 * n], token=outs[-1], gather=gather)


def exchange_wait(name, hd, after):
    n, gather = hd["n"], hd["gather"]
    ns = n * (NDEV - 1)

    def body(*refs):
        src_refs, land_refs = refs[:n], refs[n:2 * n]
        send_sems, recv_sems = refs[2 * n:2 * n + ns], refs[2 * n + ns:2 * n + 2 * ns]
        local_sems = refs[2 * n + 2 * ns:2 * n + 2 * ns + n]
        x, y, c = _coords()
        me_ = _me()
        for a in range(n):
            pltpu.make_async_copy(src_refs[a] if gather else src_refs[a].at[me_], land_refs[a].at[me_], local_sems[a]).wait()
        for k in range(1, NDEV):
            px, py, pc = _peer(k)
            p = 4 * px + 2 * py + pc
            for a in range(n):
                i = a * (NDEV - 1) + k - 1
                cp = pltpu.make_async_remote_copy(
                    src_ref=src_refs[a] if gather else src_refs[a].at[p], dst_ref=land_refs[a].at[p],
                    send_sem=send_sems[i], recv_sem=recv_sems[i],
                    device_id=(x, y, 1 - c), device_id_type=MESH)
                cp.wait_send()
                cp.wait_recv()

    outs = pl.pallas_call(
        body, name=name,
        out_shape=tuple(pltpu.HBM(s.shape, s.dtype) for s in list(hd["srcs"]) + list(hd["lands"])),
        in_specs=[HBM_SPEC] * (2 * n) + [SEM_SPEC] * (2 * ns + n) + [pl.BlockSpec(memory_space=pl.ANY)],
        out_specs=tuple([HBM_SPEC] * (2 * n)),
        input_output_aliases={i: i for i in range(2 * n)},
        compiler_params=pltpu.CompilerParams(has_side_effects=EFFECT),
    )(*hd["srcs"], *hd["lands"], *hd["sems"], after)
    return list(outs[n:])


def _adam_math(w_, m_, v_, r_):
    g = r_[0].astype(f32)
    for j in range(1, NDEV):
        g = g + r_[j].astype(f32)
    m2 = ADAM_B1 * m_ + (1.0 - ADAM_B1) * g
    v2 = ADAM_B2 * v_ + (1.0 - ADAM_B2) * jnp.square(g)
    m_hat = m2 / (1.0 - ADAM_B1 ** ADAM_STEP)
    v_hat = v2 / (1.0 - ADAM_B2 ** ADAM_STEP)
    delta = -ADAM_LR * (m_hat / (jnp.sqrt(v_hat) + ADAM_EPS) + ADAM_WD * w_)
    return g, delta, m2, v2


def adamw(name, w, m, v, recv):
    return _ew(name, _adam_math, [w, m, v, recv], [], [(w.shape[1], f32)] * 4, budget=ADAM_TILE_ELEMS)


def adamw_layer(name, w3, m3, v3, recv, l, prev):
    L, R, C = w3.shape
    tm = _tile(R, max(16, ADAM_TILE_ELEMS // (NDEV * C)), 16)
    n_prev = 0 if prev is None else 4

    def body(*refs):
        w_ref, m_ref, v_ref, r_ref = refs[:4]
        o_refs = refs[4 + n_prev:]
        res = _adam_math(w_ref[...], m_ref[...], v_ref[...], r_ref[...])
        for o, val in zip(o_refs, res):
            o[...] = val

    lay = pl.BlockSpec((None, tm, C), lambda i: (l, i, 0))
    return pl.pallas_call(
        body, name=name, grid=(R // tm,),
        in_specs=[lay, lay, lay, pl.BlockSpec((NDEV, tm, C), lambda i: (0, i, 0))] + [pl.BlockSpec(memory_space=pl.ANY)] * n_prev,
        out_specs=[lay] * 4, out_shape=[_sds((L, R, C), f32)] * 4,
        input_output_aliases={4 + i: i for i in range(n_prev)},
        compiler_params=_params(("parallel",)),
    )(w3, m3, v3, recv, *(prev or []))


def rmsnorm_fwd(name, x, g):
    D = x.shape[1]

    def fn(x_, g_):
        r = lax.rsqrt(jnp.mean(x_ * x_, axis=-1, keepdims=True) + EPS)
        return ((x_ * r) * g_,)

    return _ew(name, fn, [x], [g], [(D, bf16)])[0]


def rmsnorm_bwd(name, x, g, dh, dres):
    D = x.shape[1]

    def fn(x_, dh_, dres_, g_):
        r = lax.rsqrt(jnp.mean(x_ * x_, axis=-1, keepdims=True) + EPS)
        xh = x_ * r
        dxh = dh_ * g_
        dx = dres_ + r * (dxh - xh * jnp.mean(dxh * xh, axis=-1, keepdims=True))
        return dx, dx, jnp.sum(dh_ * xh, axis=0, keepdims=True)

    return _ew(name, fn, [x, dh, dres], [g], [(D, f32), (D, bf16)], [D])


def _silu_parts(a):
    sig = jax.nn.sigmoid(a)
    return sig, a * sig


def mm_cols(name, h, w3_list, epilogue=None, extra=(), outs=None):
    T, K = h.shape
    Nb = w3_list[0].shape[2]
    tm = _tile(T, 1024, 16)
    outs = outs or [f32]
    blk = pl.BlockSpec((None, tm, Nb), lambda j, i: (j, i, 0))
    return _mm(name, [h], w3_list, pl.BlockSpec((tm, K), lambda j, i: (i, 0)), pl.BlockSpec((None, K, Nb), lambda j, i: (j, 0, 0)),
               NN, (NDEV, T // tm), [_sds((NDEV, T, Nb), dt) for dt in outs], [blk] * len(outs), epilogue, extra, [blk] * len(extra),
               separate=True)


def mm_rows(name, p_list, w3_list, epilogue=None, extra=(), outs=None, after=None):
    _, T, Kb = p_list[0].shape
    N = w3_list[0].shape[2]
    tm, tn = _tile(T, 512, 16), _tile(N, 512 // len(p_list), 128)
    outs = outs or [f32]
    blk = pl.BlockSpec((tm, tn), lambda i, n: (i, n))
    return _mm(name, p_list, w3_list, pl.BlockSpec((NDEV, tm, Kb), lambda i, n: (0, i, 0)), pl.BlockSpec((NDEV, Kb, tn), lambda i, n: (0, 0, n)),
               NN, (T // tm, N // tn), [_sds((T, N), dt) for dt in outs], [blk] * len(outs), epilogue, extra, [blk] * len(extra), nsub=NDEV,
               after=after)


def mm_rows_plain(name, a, w3, epilogue=None, extra=(), extra_specs=None, outs=None):
    T, K = a.shape
    Kb, N = w3.shape[1], w3.shape[2]
    tm, tn = _tile(T, 512, 16), _tile(N, 1024, 128)
    outs = outs or [f32]
    blk = pl.BlockSpec((tm, tn), lambda i, n: (i, n))
    return _mm(name, [a], [w3], pl.BlockSpec((tm, K), lambda i, n: (i, 0)), pl.BlockSpec((NDEV, Kb, tn), lambda i, n: (0, 0, n)),
               NN, (T // tm, N // tn), [_sds((T, N), dt) for dt in outs], [blk] * len(outs), epilogue, extra,
               extra_specs or [blk] * len(extra), nsub=NDEV, merge_b=True)


def mm_t_rows(name, d, w3_list, epilogue=None, extra=(), outs=None, rows=512):
    T, K = d.shape
    Nb = w3_list[0].shape[1]
    tm = _tile(T, rows, 16)
    outs = outs or [f32]
    blk = pl.BlockSpec((None, tm, Nb), lambda j, i: (j, i, 0))
    return _mm(name, [d], w3_list, pl.BlockSpec((tm, K), lambda j, i: (i, 0)), pl.BlockSpec((None, Nb, K), lambda j, i: (j, 0, 0)),
               NT, (NDEV, T // tm), [_sds((NDEV, T, Nb), dt) for dt in outs], [blk] * len(outs), epilogue, extra, [blk] * len(extra),
               separate=True)


def mm_t_rows_plain(name, d, w3, epilogue=None, extra=(), outs=None, after=None):
    T, K = d.shape
    Nb = w3.shape[1]
    tm = _tile(T, 256, 16)
    outs = outs or [f32]
    blk = pl.BlockSpec((tm, NDEV * Nb), lambda i: (i, 0))
    return _mm(name, [d], [w3], pl.BlockSpec((tm, K), lambda i: (i, 0)), pl.BlockSpec((NDEV, Nb, K), lambda i: (0, 0, 0)),
               NT, (T // tm,), [_sds((T, NDEV * Nb), dt) for dt in outs], [blk] * len(outs), epilogue, extra, [blk] * len(extra),
               nsub=NDEV, merge_b=True, after=after)


def mm_t_cols(name, d_list, w3_list, epilogue=None, extra=(), outs=None, after=None):
    _, T, Kb = d_list[0].shape
    N = w3_list[0].shape[1]
    tm, tn = _tile(T, 512, 16), _tile(N, 256, 128)
    outs = outs or [f32]
    blk = pl.BlockSpec((tm, tn), lambda i, n: (i, n))
    return _mm(name, d_list, w3_list, pl.BlockSpec((NDEV, tm, Kb), lambda i, n: (0, i, 0)), pl.BlockSpec((NDEV, tn, Kb), lambda i, n: (0, n, 0)),
               NT, (T // tm, N // tn), [_sds((T, N), dt) for dt in outs], [blk] * len(outs), epilogue, extra, [blk] * len(extra), nsub=NDEV,
               after=after)


def mm_grad_rows(name, p, d, scale=1.0):
    _, T, Mb = p.shape
    N = d.shape[1]
    tn = _tile(N, 512, 128)
    return _mm(name, [p], [d], pl.BlockSpec((None, T, Mb), lambda j, n: (j, 0, 0)), pl.BlockSpec((T, tn), lambda j, n: (0, n)),
               TNd, (NDEV, N // tn), [_sds((NDEV, Mb, N), bf16)], [pl.BlockSpec((None, Mb, tn), lambda j, n: (j, 0, n))],
               (lambda acc: (acc * scale,)))[0]


def mm_grad_rows_plain(name, a, d):
    T, M = a.shape
    N = d.shape[1]
    tm, tn = _tile(M, 512, 128), _tile(N, 512, 128)
    g = _mm(name, [a], [d], pl.BlockSpec((T, tm), lambda m, n: (0, m)), pl.BlockSpec((T, tn), lambda m, n: (0, n)),
            TNd, (M // tm, N // tn), [_sds((M, N), bf16)], [pl.BlockSpec((tm, tn), lambda m, n: (m, n))])[0]
    return g.reshape(NDEV, M // NDEV, N)


def mm_grad_cols(name, h, d):
    T, M = h.shape
    Nb = d.shape[2]
    tm = _tile(M, 512, 128)
    return _mm(name, [h], [d], pl.BlockSpec((T, tm), lambda j, m: (0, m)), pl.BlockSpec((None, T, Nb), lambda j, m: (j, 0, 0)),
               TNd, (NDEV, M // tm), [_sds((NDEV, M, Nb), bf16)], [pl.BlockSpec((None, tm, Nb), lambda j, m: (j, m, 0))])[0]


def ffn_fwd(tag, x, g, wg3, wu3, wd3):
    h = rmsnorm_fwd(tag + "_norm", x, g)

    def gate_up_epi(a_, b_):
        _, sl = _silu_parts(a_)
        return a_, b_, sl * b_

    a, b, p = mm_t_rows(tag + "_gateup", h, [wg3, wu3], gate_up_epi, outs=[bf16, bf16, bf16], rows=1024)
    xo = mm_rows(tag + "_down", [p], [wd3], lambda acc, x_: (x_ + FFN_RES * acc,), [x])[0]
    return xo, (x, h, a, b, p)


def ffn_bwd(tag, dxo, dxo16, saved, g, wg3, wu3, wd3, send, each=False):
    x, h, a, b, p = saved

    def dp_epi(acc, a_, b_):
        dp = FFN_RES * acc
        a_, b_ = a_.astype(f32), b_.astype(f32)
        sig, sl = _silu_parts(a_)
        return dp * b_ * (sig * (1.0 + a_ * (1.0 - sig))), dp * sl

    da, db = mm_t_rows(tag + "_dp", dxo16, [wd3], dp_epi, [a, b], [bf16, bf16], rows=1024)
    g_wd = mm_grad_rows(tag + "_gwd", p, dxo16, FFN_RES)
    if each:
        send([g_wd], [2])
    g_wg = mm_grad_rows(tag + "_gwg", da, h)
    if each:
        send([g_wg], [0])
    g_wu = mm_grad_rows(tag + "_gwu", db, h)
    tok = send([g_wu], [1]) if each else send([g_wg, g_wu, g_wd], [0, 1, 2])
    dh = mm_rows(tag + "_dh", [da, db], [wg3, wu3], after=tok)[0]
    return rmsnorm_bwd(tag + "_dnorm", x, g, dh, dxo)


def _disc(lr, li, ldt, brt, bit):
    dt = jnp.exp(ldt)
    lr = jnp.minimum(lr, -1e-4)
    mag = jnp.exp(lr * dt)
    ab_re = mag * jnp.cos(li * dt)
    ab_im = mag * jnp.sin(li * dt)
    den = lr * lr + li * li
    n_re = ab_re - 1.0
    f_re = (n_re * lr + ab_im * li) / den
    f_im = (ab_im * lr - n_re * li) / den
    bb_re = f_re[:, None, :] * brt - f_im[:, None, :] * bit
    bb_im = f_re[:, None, :] * bit + f_im[:, None, :] * brt
    return ab_re, ab_im, bb_re, bb_im


def disc_fwd(lr, li, ldt, brt, bit):
    def body(lr_ref, li_ref, ldt_ref, brt_ref, bit_ref, ar_ref, ai_ref, bbr_ref, bbi_ref):
        ar, ai, bbr, bbi = _disc(lr_ref[...], li_ref[...], ldt_ref[...], brt_ref[...], bit_ref[...])
        ar_ref[...] = ar
        ai_ref[...] = ai
        bbr_ref[...] = bbr
        bbi_ref[...] = bbi

    return pl.pallas_call(body, name="s5_disc", out_shape=[_sds(lr.shape, f32), _sds(lr.shape, f32), _sds(brt.shape, f32), _sds(brt.shape, f32)],
                          compiler_params=pltpu.CompilerParams(vmem_limit_bytes=VMEM_LIMIT_V7X))(lr, li, ldt, brt, bit)


def disc_bwd(lr, li, ldt, brt, bit, d_ar, d_ai, d_bbr, d_bbi):
    def body(lr_ref, li_ref, ldt_ref, brt_ref, bit_ref, dar_ref, dai_ref, dbbr_ref, dbbi_ref, o_lr, o_li, o_ldt, o_brt, o_bit):
        _, vjp = jax.vjp(_disc, lr_ref[...], li_ref[...], ldt_ref[...], brt_ref[...], bit_ref[...])
        g = vjp((dar_ref[...], dai_ref[...], dbbr_ref[...], dbbi_ref[...]))
        o_lr[...] = g[0]
        o_li[...] = g[1]
        o_ldt[...] = g[2]
        o_brt[...] = g[3]
        o_bit[...] = g[4]

    return pl.pallas_call(body, name="s5_disc_bwd",
                          out_shape=[_sds(lr.shape, f32), _sds(lr.shape, f32), _sds(ldt.shape, f32), _sds(brt.shape, f32), _sds(brt.shape, f32)],
                          compiler_params=pltpu.CompilerParams(vmem_limit_bytes=VMEM_LIMIT_V7X))(lr, li, ldt, brt, bit, d_ar, d_ai, d_bbr, d_bbi)


def _blockdiag(m):
    G, R, C = m.shape
    eye = jnp.eye(GB, dtype=m.dtype)
    m5 = m.reshape(G // GB, GB, R, 1, C) * eye[None, :, None, :, None]
    return m5.reshape(G // GB, GB * R, GB * C)


def _diag_extract(M, R, C):
    nb = M.shape[0]
    eye = jnp.eye(GB, dtype=M.dtype)
    m5 = M.reshape(nb, GB, R, GB, C) * eye[None, :, None, :, None]
    return m5.sum(axis=3).reshape(nb * GB, R, C)


def mm_blockdiag(name, a_list, bd_list, dims, epilogue=None, extra=(), extra_bcast=(), outs=None):
    T = a_list[0].shape[0]
    nb, r, c = bd_list[0].shape
    ra = a_list[0].shape[1] // nb
    ca = c if dims == NN else r
    n_p, n_e, n_b = len(a_list), len(extra), len(extra_bcast)
    outs = outs or [f32]
    tm = _tile(T, 128, 16)

    def body(*refs):
        a_refs, b_refs = refs[:n_p], refs[n_p:2 * n_p]
        e_refs = refs[2 * n_p:2 * n_p + n_e + n_b]
        o_refs = refs[2 * n_p + n_e + n_b:]
        for b in range(nb):
            s = None
            for ar, br in zip(a_refs, b_refs):
                d = lax.dot_general(ar[:, b * ra:(b + 1) * ra].astype(bf16), br[b], dims, preferred_element_type=f32)
                s = d if s is None else s + d
            cols = slice(b * ca, (b + 1) * ca)
            vals = (s,) if epilogue is None else epilogue(s, *[e[:, cols] for e in e_refs])
            for o, val in zip(o_refs, vals):
                o[:, cols] = val.astype(o.dtype)

    row = lambda w: pl.BlockSpec((tm, w), lambda i: (i, 0))
    return pl.pallas_call(
        body, name=name, grid=(T // tm,),
        in_specs=[row(nb * ra)] * n_p + [pl.BlockSpec((nb, r, c), lambda i: (0, 0, 0))] * n_p + [row(nb * ca)] * n_e
        + [pl.BlockSpec((1, nb * ca), lambda i: (0, 0))] * n_b,
        out_specs=[row(nb * ca)] * len(outs), out_shape=[_sds((T, nb * ca), dt) for dt in outs],
        compiler_params=_params(("parallel",)),
    )(*a_list, *bd_list, *extra, *extra_bcast)


def mm_blockdiag_grad(name, a, d, nb):
    T = a.shape[0]
    ra, rd = a.shape[1] // nb, d.shape[1] // nb
    tk = _tile(T, 256, 16)

    def body(a_ref, d_ref, o_ref):
        @pl.when(pl.program_id(0) == 0)
        def _():
            o_ref[...] = jnp.zeros_like(o_ref)

        for b in range(nb):
            o_ref[b] += lax.dot_general(a_ref[:, b * ra:(b + 1) * ra].astype(bf16), d_ref[:, b * rd:(b + 1) * rd].astype(bf16),
                                        TNd, preferred_element_type=f32)

    return pl.pallas_call(
        body, name=name, grid=(T // tk,),
        in_specs=[pl.BlockSpec((tk, nb * ra), lambda k: (k, 0)), pl.BlockSpec((tk, nb * rd), lambda k: (k, 0))],
        out_specs=pl.BlockSpec((nb, ra, rd), lambda k: (0, 0, 0)), out_shape=_sds((nb, ra, rd), f32),
        compiler_params=_params(("arbitrary",)),
    )(a, d)


SUB = 8
SCAN_LANES = 1024
SCAN_GROUPS = 64


def s5_tables(ar, ai):
    N = ar.shape[1]

    def body(ar_ref, ai_ref, f_ref, b_ref):
        r, i = ar_ref[...], ai_ref[...]
        pw = [(r, i)]
        for _ in range(SUB - 1):
            pr, pi = pw[-1]
            pw.append((pr * r - pi * i, pr * i + pi * r))
        row = lax.broadcasted_iota(jnp.int32, (SUB, N), 0)
        for t, k in enumerate((1, 2, 4)):
            pr, pi = pw[k - 1]
            f_ref[2 * t] = jnp.where(row >= k, pr, 0.0)
            f_ref[2 * t + 1] = jnp.where(row >= k, pi, 0.0)
            b_ref[2 * t] = jnp.where(row <= SUB - 1 - k, pr, 0.0)
            b_ref[2 * t + 1] = jnp.where(row <= SUB - 1 - k, -pi, 0.0)
        fr = fi = br = bi = jnp.zeros((SUB, N), f32)
        for j in range(SUB):
            fr, fi = jnp.where(row == j, pw[j][0], fr), jnp.where(row == j, pw[j][1], fi)
            br, bi = jnp.where(row == j, pw[SUB - 1 - j][0], br), jnp.where(row == j, -pw[SUB - 1 - j][1], bi)
        f_ref[6], f_ref[7] = fr, fi
        b_ref[6], b_ref[7] = br, bi

    return pl.pallas_call(body, name="s5_tables", out_shape=[_sds((8, SUB, N), f32)] * 2,
                          compiler_params=pltpu.CompilerParams(vmem_limit_bytes=VMEM_LIMIT_V7X))(ar, ai)


def _group_scan(xr, xi, m, cr, ci, up):
    for t, k in enumerate((1, 2, 4)):
        shift = SUB - k if up else k
        pr, pi = pltpu.roll(xr, shift, 0), pltpu.roll(xi, shift, 0)
        xr, xi = xr + m[2 * t] * pr - m[2 * t + 1] * pi, xi + m[2 * t] * pi + m[2 * t + 1] * pr
    return xr + m[6] * cr - m[7] * ci, xi + m[6] * ci + m[7] * cr


def s5_scan_fwd(bu_re, bu_im, tab):
    T8, _, N = bu_re.shape
    W, tb = _tile(N, SCAN_LANES, 128), _tile(T8, SCAN_GROUPS, 1)

    def body(bur_ref, bui_ref, tab_ref, sr_ref, si_ref, st):
        @pl.when(pl.program_id(1) == 0)
        def _():
            st[...] = jnp.zeros_like(st)

        def step(g, c):
            m = [tab_ref[t] for t in range(8)]
            xr, xi = _group_scan(bur_ref[g], bui_ref[g], m, c[0], c[1], False)
            sr_ref[g] = xr
            si_ref[g] = xi
            return jnp.broadcast_to(xr[SUB - 1:SUB], (SUB, W)), jnp.broadcast_to(xi[SUB - 1:SUB], (SUB, W))

        cr, ci = lax.fori_loop(0, tb, step, (st[0], st[1]))
        st[0] = cr
        st[1] = ci

    blk = pl.BlockSpec((tb, SUB, W), lambda j, i: (i, 0, j))
    tabs = pl.BlockSpec((8, SUB, W), lambda j, i: (0, 0, j))
    return pl.pallas_call(body, name="s5_scan", grid=(N // W, T8 // tb), in_specs=[blk, blk, tabs], out_specs=[blk, blk],
                          out_shape=[_sds(bu_re.shape, f32)] * 2, scratch_shapes=[pltpu.VMEM((2, SUB, W), f32)],
                          compiler_params=_params(("parallel", "arbitrary")))(bu_re, bu_im, tab)


def s5_scan_bwd(ds_re, ds_im, s_re, s_im, tab):
    T8, _, N = ds_re.shape
    W, tb = _tile(N, SCAN_LANES, 128), _tile(T8, SCAN_GROUPS // 2, 1)
    nblk = T8 // tb

    def body(dsr_ref, dsi_ref, sr_ref, si_ref, tab_ref, qr_ref, qi_ref, dar_ref, dai_ref, st):
        @pl.when(pl.program_id(1) == 0)
        def _():
            st[...] = jnp.zeros_like(st)
            dar_ref[...] = jnp.zeros_like(dar_ref)
            dai_ref[...] = jnp.zeros_like(dai_ref)

        last_row = lax.broadcasted_iota(jnp.int32, (SUB, W), 0) == SUB - 1

        def step(i, c):
            cr, ci, dar, dai = c
            g = tb - 1 - i
            m = [tab_ref[t] for t in range(8)]
            xr, xi = _group_scan(dsr_ref[g], dsi_ref[g], m, cr, ci, True)
            qr_ref[g] = xr
            qi_ref[g] = xi
            nr = jnp.where(last_row, cr, pltpu.roll(xr, SUB - 1, 0))
            ni = jnp.where(last_row, ci, pltpu.roll(xi, SUB - 1, 0))
            sr, si = sr_ref[g], si_ref[g]
            dar = dar + nr * sr + ni * si
            dai = dai + ni * sr - nr * si
            return jnp.broadcast_to(xr[0:1], (SUB, W)), jnp.broadcast_to(xi[0:1], (SUB, W)), dar, dai

        cr, ci, dar, dai = lax.fori_loop(0, tb, step, (st[0], st[1], dar_ref[...], dai_ref[...]))
        st[0] = cr
        st[1] = ci
        dar_ref[...] = dar
        dai_ref[...] = dai

        @pl.when(pl.program_id(1) == nblk - 1)
        def _():
            dar_ref[...] = jnp.broadcast_to(jnp.sum(dar, axis=0, keepdims=True), (SUB, W))
            dai_ref[...] = jnp.broadcast_to(jnp.sum(dai, axis=0, keepdims=True), (SUB, W))

    blk = pl.BlockSpec((tb, SUB, W), lambda j, i: (nblk - 1 - i, 0, j))
    tabs = pl.BlockSpec((8, SUB, W), lambda j, i: (0, 0, j))
    acc = pl.BlockSpec((SUB, W), lambda j, i: (0, j))
    return pl.pallas_call(body, name="s5_scan_bwd", grid=(N // W, nblk), in_specs=[blk, blk, blk, blk, tabs], out_specs=[blk, blk, acc, acc],
                          out_shape=[_sds(ds_re.shape, f32)] * 2 + [_sds((SUB, N), f32)] * 2, scratch_shapes=[pltpu.VMEM((2, SUB, W), f32)],
                          compiler_params=_params(("parallel", "arbitrary")))(ds_re, ds_im, s_re, s_im, tab)


def _gelu_grad(y):
    c = math.sqrt(2.0 / math.pi)
    th = jnp.tanh(c * (y + 0.044715 * (y * y * y)))
    return 0.5 * (1.0 + th) + 0.5 * y * (1.0 - th * th) * c * (1.0 + 3.0 * 0.044715 * (y * y))


def s5_prepare(lam_re, lam_im, log_dt, b_re, b_im, c_re, c_im):
    G = lam_re.shape[0]
    ldt = log_dt.reshape(G, 1)
    brt, bit = jnp.swapaxes(b_re, 1, 2), jnp.swapaxes(b_im, 1, 2)
    ar, ai, bbr, bbi = disc_fwd(lam_re, lam_im, ldt, brt, bit)
    tab_f, tab_b = s5_tables(ar.reshape(1, -1), ai.reshape(1, -1))
    mats = dict(
        bbd_re=_blockdiag(bbr).astype(bf16), bbd_im=_blockdiag(bbi).astype(bf16),
        cd_re=_blockdiag(jnp.swapaxes(c_re, 1, 2)).astype(bf16),
        cd_imn=_blockdiag(-jnp.swapaxes(c_im, 1, 2)).astype(bf16),
        tab_f=tab_f, tab_b=tab_b)
    return (lam_re, lam_im, ldt, brt, bit), mats


def s5_fwd(x, gm, w_in3, w_glu3, w_out3, mats, d_skip, b_glu):
    T, D = x.shape
    N = mats["tab_f"].shape[2]
    h = rmsnorm_fwd("s5_norm", x, gm)
    u = mm_rows_plain("s5_in", h, w_in3)[0]
    bu_re = mm_blockdiag("s5_bu_re", [u], [mats["bbd_re"]], NN)[0]
    bu_im = mm_blockdiag("s5_bu_im", [u], [mats["bbd_im"]], NN)[0]
    grouped = (T // SUB, SUB, N)
    s_re, s_im = s5_scan_fwd(bu_re.reshape(grouped), bu_im.reshape(grouped), mats["tab_f"])
    s_re, s_im = s_re.reshape(T, N), s_im.reshape(T, N)

    def y_epi(acc, u_, d_):
        y2 = acc + d_ * u_
        return y2, jax.nn.gelu(y2)

    y2, gl = mm_blockdiag("s5_y", [s_re, s_im], [mats["cd_re"], mats["cd_imn"]], NN, y_epi, [u], [d_skip], [f32, bf16])

    def glu_epi(acc, y2_, b_):
        zg = acc + b_
        return zg, jax.nn.gelu(y2_) * jax.nn.sigmoid(zg)

    tm, tn = _tile(T, 512, 16), _tile(D, 1024, 128)
    zg, o = mm_rows_plain("s5_glu", gl, w_glu3, glu_epi, [y2, b_glu],
                          [pl.BlockSpec((tm, tn), lambda i, n: (i, n)), pl.BlockSpec((1, tn), lambda i, n: (0, n))], [f32, bf16])
    xo = mm_rows_plain("s5_out", o, w_out3, lambda acc, x_: (x_ + acc,), [x])[0]
    return xo, (x, h, u, s_re, s_im, y2, gl, zg, o)


def s5_bwd(dxo, dxo16, saved, gm, w_in3, w_glu3, w_out3, mats, d_skip, send):
    x, h, u, s_re, s_im, y2, gl, zg, o = saved
    T, D = x.shape
    N = mats["tab_b"].shape[2]
    nb = mats["cd_re"].shape[0]

    def do_epi(acc, y2_, zg_):
        sg = jax.nn.sigmoid(zg_)
        return acc * sg, acc * jax.nn.gelu(y2_) * (sg * (1.0 - sg))

    dgl_direct, dzg = mm_t_rows_plain("s5_do", dxo16, w_out3, do_epi, [y2, zg], [f32, f32])
    g_wout = mm_grad_rows_plain("s5_gwout", o, dxo16)
    g_wglu = mm_grad_rows_plain("s5_gwglu", gl, dzg)
    dy2 = mm_t_rows_plain("s5_dgl", dzg, w_glu3, lambda acc, dd_, y2_: ((acc + dd_) * _gelu_grad(y2_),), [dgl_direct, y2])[0]

    def red_fn(dy2_, u_, dzg_, d_):
        return dy2_ * d_, jnp.sum(dy2_ * u_, axis=0, keepdims=True), jnp.sum(dzg_, axis=0, keepdims=True)

    du_direct, dd, dbglu = _ew("s5_dskip", red_fn, [dy2, u, dzg], [d_skip], [(D, f32)], [D, D])
    ds_re = mm_blockdiag("s5_ds_re", [dy2], [mats["cd_re"]], NT)[0]
    ds_im = mm_blockdiag("s5_ds_im", [dy2], [mats["cd_imn"]], NT)[0]
    d_cd_re = mm_blockdiag_grad("s5_gc_re", s_re, dy2, nb)
    d_cd_imn = mm_blockdiag_grad("s5_gc_im", s_im, dy2, nb)
    sh = (T // SUB, SUB, N)
    q_re, q_im, d_ar, d_ai = s5_scan_bwd(ds_re.reshape(sh), ds_im.reshape(sh), s_re.reshape(sh), s_im.reshape(sh), mats["tab_b"])
    q_re, q_im, d_ar, d_ai = q_re.reshape(T, N), q_im.reshape(T, N), d_ar[0], d_ai[0]
    d_bbd_re = mm_blockdiag_grad("s5_gb_re", u, q_re, nb)
    d_bbd_im = mm_blockdiag_grad("s5_gb_im", u, q_im, nb)
    du = mm_blockdiag("s5_du", [q_re, q_im], [mats["bbd_re"], mats["bbd_im"]], NT, lambda acc, dd_: (acc + dd_,), [du_direct])[0]
    g_win = mm_grad_rows_plain("s5_gwin", h, du)
    tok = send([g_win, g_wglu, g_wout], [0, 1, 2])
    dh = mm_t_rows_plain("s5_dh", du, w_in3, after=tok)[0]
    dx, dx16, dgm = rmsnorm_bwd("s5_dnorm", x, gm, dh, dxo)
    small = dict(d_ar=d_ar, d_ai=d_ai, d_bbr=_diag_extract(d_bbd_re, S5H, S5P), d_bbi=_diag_extract(d_bbd_im, S5H, S5P),
                 d_c_re=jnp.swapaxes(_diag_extract(d_cd_re, S5P, S5H), 1, 2), d_c_im=-jnp.swapaxes(_diag_extract(d_cd_imn, S5P, S5H), 1, 2),
                 d_d=dd, d_bglu=dbglu)
    return dx, dx16, dgm, small


def qk_prepass(qkv, g_q, g_k, nh):
    _, T, W = qkv.shape
    cpd = W // HEAD
    tm = _tile(T, 1024, 16)

    def body(x_ref, gq_ref, gk_ref, o_ref):
        which = pl.program_id(0) // nh
        x = x_ref[...]
        r = lax.rsqrt(jnp.mean(x * x, axis=-1, keepdims=True) + EPS)
        g = jnp.where(which == 0, gq_ref[...], gk_ref[...])
        o_ref[...] = jnp.where(which == 2, x, (x * r) * g).astype(o_ref.dtype)

    return pl.pallas_call(
        body, name="sb_qknorm", grid=(3 * nh, T // tm),
        in_specs=[pl.BlockSpec((None, tm, HEAD), lambda ch, i: (ch // cpd, i, ch % cpd)),
                  pl.BlockSpec((1, HEAD), lambda ch, i: (0, 0)), pl.BlockSpec((1, HEAD), lambda ch, i: (0, 0))],
        out_specs=pl.BlockSpec((None, tm, HEAD), lambda ch, i: (ch, i, 0)),
        out_shape=_sds((3 * nh, T, HEAD), bf16), compiler_params=_params(("parallel", "parallel")))(qkv, g_q, g_k)


def qk_prepass_bwd(qkv, g_q, g_k, dqn, dkn, dv, nh):
    _, T, W = qkv.shape
    cpd = W // HEAD
    tm = _tile(T, 1024, 16)

    def body(x_ref, gq_ref, gk_ref, dq_ref, dk_ref, dv_ref, o_ref, dgq_ref, dgk_ref):
        ch = pl.program_id(1)
        which = ch // nh

        @pl.when((pl.program_id(0) == 0) & (ch == 0))
        def _():
            dgq_ref[...] = jnp.zeros_like(dgq_ref)
            dgk_ref[...] = jnp.zeros_like(dgk_ref)

        x = x_ref[...]
        dn = jnp.where(which == 0, dq_ref[...], dk_ref[...])
        g = jnp.where(which == 0, gq_ref[...], gk_ref[...])
        r = lax.rsqrt(jnp.mean(x * x, axis=-1, keepdims=True) + EPS)
        xh = x * r
        dxh = dn * g
        dx = r * (dxh - xh * jnp.mean(dxh * xh, axis=-1, keepdims=True))
        o_ref[...] = jnp.where(which == 2, dv_ref[...], dx).astype(o_ref.dtype)
        dg = jnp.sum(dn * xh, axis=0, keepdims=True)
        dgq_ref[...] += jnp.where(which == 0, dg, 0.0)
        dgk_ref[...] += jnp.where(which == 1, dg, 0.0)

    chunk = pl.BlockSpec((None, tm, HEAD), lambda i, ch: (ch // cpd, i, ch % cpd))
    head = lambda off: pl.BlockSpec((None, tm, HEAD), lambda i, ch: (jnp.clip(ch - off, 0, nh - 1), i, 0))
    vec = pl.BlockSpec((1, HEAD), lambda i, ch: (0, 0))
    return pl.pallas_call(
        body, name="sb_qknorm_bwd", grid=(T // tm, 3 * nh),
        in_specs=[chunk, vec, vec, head(0), head(nh), head(2 * nh)], out_specs=[chunk, vec, vec],
        out_shape=[_sds(qkv.shape, bf16), _sds((1, HEAD), f32), _sds((1, HEAD), f32)],
        compiler_params=_params(("arbitrary", "arbitrary")))(qkv, g_q, g_k, dqn, dkn, dv)


def _split_dot(a, tri):
    hi = a.astype(bf16)
    lo = (a - hi.astype(f32)).astype(bf16)
    return lax.dot_general(hi, tri, NN, preferred_element_type=f32) + lax.dot_general(lo, tri, NN, preferred_element_type=f32)


LOG2E = 1.4426950408889634


def _sb_logits(q, k, scale):
    t = lax.dot_general(q, k, NT, preferred_element_type=f32) * (scale * LOG2E)
    sp = jnp.log2(1.0 + jnp.exp2(-jnp.abs(t)))
    return jnp.minimum(t, 0.0) - sp, -(jnp.maximum(t, 0.0) + sp)


HP = 2
KB = 4


def attn_fwd(qkvn, nh, tq):
    _, T, _ = qkvn.shape
    nq = T // tq
    scale = 1.0 / math.sqrt(HEAD)

    def body(q_ref, k_ref, v_ref, o_ref):
        qi = pl.program_id(1)
        qs = [q_ref[h] for h in range(HP)]
        row = lax.broadcasted_iota(jnp.int32, (tq, tq), 0)
        col = lax.broadcasted_iota(jnp.int32, (tq, tq), 1)
        past = col < row
        tri = (row > col).astype(bf16)

        def blocks(kbs, carry, acc, diag):
            new_c, new_a = [], []
            for h in range(HP):
                parts = []
                for kb in kbs:
                    ks = pl.ds(pl.multiple_of(kb * tq, tq), tq)
                    k, v = k_ref[h, ks, :], v_ref[h, ks, :]
                    lb, lk = _sb_logits(qs[h], k, scale)
                    if diag:
                        lk = jnp.where(past, lk, 0.0)
                    parts.append((lb + _split_dot(lk, tri), jnp.sum(lk, axis=1, keepdims=True), v))
                c, a = carry[h], acc[h]
                for expo, total, v in parts:
                    w = jnp.exp2(expo + c)
                    if diag:
                        w = jnp.where(past, w, 0.0)
                    a = a + lax.dot_general(w.astype(bf16), v, NN, preferred_element_type=f32)
                    c = c + total
                new_a.append(a)
                new_c.append(c)
            return tuple(new_c), tuple(new_a)

        zero_c = tuple(jnp.zeros((tq, 1), f32) for _ in range(HP))
        zero_a = tuple(jnp.zeros((tq, HEAD), f32) for _ in range(HP))
        carry, acc = blocks([qi], zero_c, zero_a, True)
        rest = qi % KB
        carry, acc = lax.fori_loop(0, rest % 2, lambda i, c: blocks([qi - 1], c[0], c[1], False), (carry, acc))
        top = qi - 1 - rest % 2
        carry, acc = lax.fori_loop(0, rest // 2, lambda i, c: blocks([top, top - 1], c[0], c[1], False), (carry, acc))
        top = qi - 1 - rest
        carry, acc = lax.fori_loop(0, qi // KB, lambda i, c: blocks([top - KB * i - j for j in range(KB)], c[0], c[1], False), (carry, acc))
        for h in range(HP):
            o_ref[:, h * HEAD:(h + 1) * HEAD] = acc[h]

    heads = lambda off: pl.BlockSpec((HP, T, HEAD), lambda hh, i: (off // HP + hh, 0, 0))
    return pl.pallas_call(
        body, name="sb_attn", grid=(nh // HP, nq),
        in_specs=[pl.BlockSpec((HP, tq, HEAD), lambda hh, i: (hh, i, 0)), heads(nh), heads(2 * nh)],
        out_specs=pl.BlockSpec((tq, HP * HEAD), lambda hh, i: (i, hh)),
        out_shape=_sds((T, nh * HEAD), f32), compiler_params=_params(("parallel", "arbitrary")))(qkvn, qkvn, qkvn)


def attn_bwd(qkvn, do, nh, tq):
    _, T, _ = qkvn.shape
    nq = T // tq
    scale = 1.0 / math.sqrt(HEAD)

    def body(q_ref, k_ref, v_ref, do_ref, dq_ref, dk_ref, dv_ref, e_scr, sg_scr):
        qi = pl.program_id(1)

        @pl.when(qi == 0)
        def _():
            dk_ref[...] = jnp.zeros_like(dk_ref)
            dv_ref[...] = jnp.zeros_like(dv_ref)

        qs = [q_ref[h] for h in range(HP)]
        dobs = [do_ref[:, h * HEAD:(h + 1) * HEAD].astype(bf16) for h in range(HP)]
        row = lax.broadcasted_iota(jnp.int32, (tq, tq), 0)
        col = lax.broadcasted_iota(jnp.int32, (tq, tq), 1)
        past = col < row
        tri_later = (row > col).astype(bf16)
        tri_before = (row < col).astype(bf16)

        def sweep1(kbs, carry, diag):
            new_c = []
            for h in range(HP):
                parts = []
                for kb in kbs:
                    ks = pl.ds(pl.multiple_of(kb * tq, tq), tq)
                    k, v = k_ref[h, ks, :], v_ref[h, ks, :]
                    lb, lk = _sb_logits(qs[h], k, scale)
                    if diag:
                        lk = jnp.where(past, lk, 0.0)
                    sg_scr[h, kb] = jnp.exp2(lb)
                    dw = lax.dot_general(dobs[h], v, NT, preferred_element_type=f32)
                    parts.append((kb, ks, lb + _split_dot(lk, tri_later), jnp.sum(lk, axis=1, keepdims=True), dw))
                c = carry[h]
                for kb, ks, expo, total, dw in parts:
                    w = jnp.exp2(expo + c)
                    if diag:
                        w = jnp.where(past, w, 0.0)
                    e_scr[h, kb] = dw * w
                    dv_ref[h, ks, :] += lax.dot_general(w.astype(bf16), dobs[h], TNd, preferred_element_type=f32)
                    c = c + total
                new_c.append(c)
            return tuple(new_c)

        zero_c = tuple(jnp.zeros((tq, 1), f32) for _ in range(HP))
        carry = sweep1([qi], zero_c, True)
        rest = qi % KB
        carry = lax.fori_loop(0, rest % 2, lambda i, c: sweep1([qi - 1], c, False), carry)
        top2 = qi - 1 - rest % 2
        carry = lax.fori_loop(0, rest // 2, lambda i, c: sweep1([top2, top2 - 1], c, False), carry)
        top = qi - 1 - rest
        lax.fori_loop(0, qi // KB, lambda i, c: sweep1([top - KB * i - j for j in range(KB)], c, False), carry)

        def sweep2(kbs, carry, dq, diag):
            new_c, new_q = [], []
            for h in range(HP):
                parts = []
                for kb in kbs:
                    e, sg = e_scr[h, kb], sg_scr[h, kb]
                    parts.append((kb, e, sg, _split_dot(e, tri_before), jnp.sum(e, axis=1, keepdims=True)))
                c, dqh = carry[h], dq[h]
                for kb, e, sg, before, total in parts:
                    ks = pl.ds(pl.multiple_of(kb * tq, tq), tq)
                    dz = (e * (1.0 - sg) - (before + c) * sg) * scale
                    if diag:
                        dz = jnp.where(past, dz, 0.0)
                    dzb = dz.astype(bf16)
                    dqh = dqh + lax.dot_general(dzb, k_ref[h, ks, :], NN, preferred_element_type=f32)
                    dk_ref[h, ks, :] += lax.dot_general(dzb, qs[h], TNd, preferred_element_type=f32)
                    c = c + total
                new_q.append(dqh)
                new_c.append(c)
            return tuple(new_c), tuple(new_q)

        zero_q = tuple(jnp.zeros((tq, HEAD), f32) for _ in range(HP))
        carry, dq = lax.fori_loop(0, qi // KB, lambda i, c: sweep2([KB * i + j for j in range(KB)], c[0], c[1], False), (zero_c, zero_q))
        carry, dq = lax.fori_loop(0, rest // 2, lambda i, c: sweep2([qi - rest, qi - rest + 1], c[0], c[1], False), (carry, dq))
        carry, dq = lax.fori_loop(0, rest % 2, lambda i, c: sweep2([qi - 1], c[0], c[1], False), (carry, dq))
        _, dq = sweep2([qi], carry, dq, True)
        for h in range(HP):
            dq_ref[h] = dq[h]

    heads = lambda off: pl.BlockSpec((HP, T, HEAD), lambda hh, i: (off // HP + hh, 0, 0))
    return pl.pallas_call(
        body, name="sb_attn_bwd", grid=(nh // HP, nq),
        in_specs=[pl.BlockSpec((HP, tq, HEAD), lambda hh, i: (hh, i, 0)), heads(nh), heads(2 * nh),
                  pl.BlockSpec((tq, HP * HEAD), lambda hh, i: (i, hh))],
        out_specs=[pl.BlockSpec((HP, tq, HEAD), lambda hh, i: (hh, i, 0)), heads(0), heads(0)],
        out_shape=[_sds((nh, T, HEAD), f32)] * 3,
        scratch_shapes=[pltpu.VMEM((HP, nq, tq, tq), f32), pltpu.VMEM((HP, nq, tq, tq), f32)],
        compiler_params=_params(("parallel", "arbitrary")))(qkvn, qkvn, qkvn, do)


def sb_fwd(x, gm, w_qkv3, w_o3, g_q, g_k):
    T, D = x.shape
    nh = D // HEAD
    h = rmsnorm_fwd("sb_norm", x, gm)
    qkv = mm_cols("sb_qkv", h, [w_qkv3])[0]
    qkvn = qk_prepass(qkv, g_q, g_k, nh)
    o = attn_fwd(qkvn, nh, _tile(T, 256, 128))
    xo = mm_rows_plain("sb_o", o, w_o3, lambda acc, x_: (x_ + acc,), [x])[0]
    return xo, (x, h, qkv, qkvn, o)


def sb_bwd(dxo, dxo16, saved, gm, w_qkv3, w_o3, g_q, g_k, send):
    x, h, qkv, qkvn, o = saved
    T, D = x.shape
    nh = D // HEAD
    do = mm_t_rows_plain("sb_do", dxo16, w_o3)[0]
    g_wo = mm_grad_rows_plain("sb_gwo", o, dxo16)
    dqn, dkn, dv = attn_bwd(qkvn, do, nh, _tile(T, 256, 128))
    dqkv, dgq, dgk = qk_prepass_bwd(qkv, g_q, g_k, dqn, dkn, dv, nh)
    g_wqkv = mm_grad_cols("sb_gwqkv", h, dqkv)
    tok = send([g_wqkv, g_wo], [0, 1])
    dh = mm_t_cols("sb_dh", [dqkv], [w_qkv3], after=tok)[0]
    dx, dx16, dgm = rmsnorm_bwd("sb_dnorm", x, gm, dh, dxo)
    return dx, dx16, dgm, dgq, dgk


def loss_head(y, target):
    D = y.shape[1]

    def fn(y_, t_):
        err = y_ - t_
        dy = err / D
        return dy, dy, jnp.sum(err * err, axis=0, keepdims=True)

    dy, dy16, sq = _ew("loss_head", fn, [y, target], [], [(D, f32), (D, bf16)], [D])
    return 0.5 * (jnp.sum(sq) / D), dy, dy16


FFN = ["w_gate", "w_up", "w_down"]
SMALL = ["norm_ffn1", "norm_mix", "s5_lam_re", "s5_lam_im", "s5_log_dt", "s5_b_re", "s5_b_im", "s5_c_re", "s5_c_im",
         "s5_d", "s5_b_glu", "sb_g_q", "sb_g_k", "norm_ffn2"]
WEIGHTS = ["norm_ffn1", "ffn1_w_gate", "ffn1_w_up", "ffn1_w_down", "norm_mix", "s5_w_in", "s5_lam_re", "s5_lam_im", "s5_log_dt",
           "s5_b_re", "s5_b_im", "s5_c_re", "s5_c_im", "s5_d", "s5_w_glu", "s5_b_glu", "s5_w_out", "sb_w_qkv", "sb_g_q", "sb_g_k",
           "sb_w_o", "norm_ffn2", "ffn2_w_gate", "ffn2_w_up", "ffn2_w_down"]
PACK_LANES = 128
PACK_ROWS = 64


def _pack(arrs):
    flat = jnp.concatenate([a.reshape(-1).astype(f32) for a in arrs])
    pad = (-flat.shape[0]) % (PACK_LANES * PACK_ROWS)
    return jnp.pad(flat, (0, pad)).reshape(-1, PACK_LANES)


def _unpack(buf, like):
    flat = buf.reshape(-1)
    out, off = [], 0
    for a in like:
        out.append(flat[off:off + a.size].reshape(a.shape))
        off += a.size
    return out


def _after(token, g):
    return g + token[0, 0]


TRANSPOSED = ("ffn1_w_gate", "ffn1_w_up", "ffn2_w_gate", "ffn2_w_up")


def _step(w, m, v, x, target):
    x = x[0]
    target = target[0]
    w, m, v = [{n: jnp.swapaxes(a, 1, 2) if n in TRANSPOSED else a for n, a in d.items()} for d in (w, m, v)]

    def shard(n, l):
        return to_bf16("cast%d_%s" % (l, n), w[n], l)

    def gain(n, l):
        return w[n][l:l + 1]

    wf1a, tok = all_gather("ag_l0f1", [shard("ffn1_" + s, 0) for s in FFN])
    h_s5 = exchange_start("ag_s5", [shard("s5_w_in", 0), shard("s5_w_glu", 0), shard("s5_w_out", 0)], True, tok)
    h_f2a = exchange_start("ag_l0f2", [shard("ffn2_" + s, 0) for s in FFN], True, h_s5["token"])
    h_f1b = exchange_start("ag_l1f1", [shard("ffn1_" + s, 1) for s in FFN], True, h_f2a["token"])
    h_sb = exchange_start("ag_sb", [shard("sb_w_qkv", 0), shard("sb_w_o", 0)], True, h_f1b["token"])
    h_f2b = exchange_start("ag_l1f2", [shard("ffn2_" + s, 1) for s in FFN], True, h_sb["token"])
    tok = h_f2b["token"]

    s5_in, mats = s5_prepare(w["s5_lam_re"][0], w["s5_lam_im"][0], w["s5_log_dt"][0], w["s5_b_re"][0], w["s5_b_im"][0],
                             w["s5_c_re"][0], w["s5_c_im"][0])

    x1, sv_f1a = ffn_fwd("l0f1", x, _after(tok, gain("norm_ffn1", 0)), *wf1a)
    ws5 = exchange_wait("agw_s5", h_s5, x1)
    x2, sv_s5 = s5_fwd(x1, gain("norm_mix", 0), *ws5, mats, w["s5_d"], w["s5_b_glu"])
    wf2a = exchange_wait("agw_l0f2", h_f2a, x2)
    x3, sv_f2a = ffn_fwd("l0f2", x2, gain("norm_ffn2", 0), *wf2a)
    wf1b = exchange_wait("agw_l1f1", h_f1b, x3)
    x4, sv_f1b = ffn_fwd("l1f1", x3, gain("norm_ffn1", 1), *wf1b)
    wsb = exchange_wait("agw_sb", h_sb, x4)
    x5, sv_sb = sb_fwd(x4, gain("norm_mix", 1), *wsb, w["sb_g_q"], w["sb_g_k"])
    wf2b = exchange_wait("agw_l1f2", h_f2b, x5)
    x6, sv_f2b = ffn_fwd("l1f2", x5, gain("norm_ffn2", 1), *wf2b)
    loss_local, dy, dy16 = loss_head(x6, target)
    loss = lax.psum(loss_local, AXES)

    rs = {}

    def sender(key):
        def send(grads, places):
            sent = rs.setdefault(key, [])
            sent.append((exchange_start("rs_%s_%d" % (key, len(sent)), grads, False), places))
            return sent[-1][0]["token"]
        return send

    d5, d5h, dn_f2b = ffn_bwd("l1f2b", dy, dy16, sv_f2b, gain("norm_ffn2", 1), *wf2b, sender("l1f2"))
    d4, d4h, dn_sb, dgq, dgk = sb_bwd(d5, d5h, sv_sb, gain("norm_mix", 1), *wsb, w["sb_g_q"], w["sb_g_k"], sender("sb"))
    d3, d3h, dn_f1b = ffn_bwd("l1f1b", d4, d4h, sv_f1b, gain("norm_ffn1", 1), *wf1b, sender("l1f1"))
    d2, d2h, dn_f2a = ffn_bwd("l0f2b", d3, d3h, sv_f2a, gain("norm_ffn2", 0), *wf2a, sender("l0f2"))
    d1, d1h, dn_s5, s5s = s5_bwd(d2, d2h, sv_s5, gain("norm_mix", 0), *ws5, mats, w["s5_d"], sender("s5"))
    d0, _, dn_f1a = ffn_bwd("l0f1b", d1, d1h, sv_f1a, gain("norm_ffn1", 0), *wf1a, sender("l0f1"), each=True)
    d_lr, d_li, d_ldt, d_brt, d_bit = disc_bwd(*s5_in, s5s["d_ar"].reshape(s5_in[0].shape), s5s["d_ai"].reshape(s5_in[0].shape),
                                               s5s["d_bbr"], s5s["d_bbi"])

    part = {
        "norm_ffn1": jnp.concatenate([dn_f1a, dn_f1b]), "norm_mix": jnp.concatenate([dn_s5, dn_sb]),
        "norm_ffn2": jnp.concatenate([dn_f2a, dn_f2b]),
        "s5_lam_re": d_lr, "s5_lam_im": d_li, "s5_log_dt": d_ldt, "s5_b_re": jnp.swapaxes(d_brt, 1, 2), "s5_b_im": jnp.swapaxes(d_bit, 1, 2),
        "s5_c_re": s5s["d_c_re"], "s5_c_im": s5s["d_c_im"], "s5_d": s5s["d_d"], "s5_b_glu": s5s["d_bglu"], "sb_g_q": dgq, "sb_g_k": dgk,
    }
    h_small = exchange_start("ag_small", [_pack([part[n].reshape(w[n].shape) for n in SMALL])], True)

    res = {}
    late = h_small["token"]
    for key, names in (("l1f2", [("ffn2_" + s, 1) for s in FFN]), ("sb", [("sb_w_qkv", None), ("sb_w_o", None)]),
                       ("l1f1", [("ffn1_" + s, 1) for s in FFN]), ("l0f2", [("ffn2_" + s, 0) for s in FFN]),
                       ("s5", [("s5_w_in", None), ("s5_w_glu", None), ("s5_w_out", None)]), ("small", None),
                       ("l0f1", [("ffn1_" + s, 0) for s in FFN])):
        if key == "small":
            packed = exchange_wait("agw_small", h_small, late)[0]
            outs = adamw("adamw_small", _pack([w[n] for n in SMALL]), _pack([m[n] for n in SMALL]), _pack([v[n] for n in SMALL]), packed)
            un = [_unpack(o, [w[n] for n in SMALL]) for o in outs]
            for i, n in enumerate(SMALL):
                res[n] = [un[k][i] for k in range(4)]
            late = outs[0]
            continue
        for k, (handle, places) in enumerate(rs[key]):
            recv = exchange_wait("rsw_%s_%d" % (key, k), handle, late)
            for place, r in zip(places, recv):
                n, l = names[place]
                if l is None:
                    res[n] = [o[None] for o in adamw("adamw_" + n, w[n][0], m[n][0], v[n][0], r)]
                else:
                    res[n] = adamw_layer("adamw%d_%s" % (l, n), w[n], m[n], v[n], r, l, res.get(n))
                late = res[n][0]

    for n in TRANSPOSED:
        res[n] = [jnp.swapaxes(o, 1, 2) for o in res[n]]
    return (loss, d0[None], *[res[n][0] for n in WEIGHTS], *[res[n][1] for n in WEIGHTS],
            *[res[n][2] for n in WEIGHTS], *[res[n][3] for n in WEIGHTS])


def kernel(x, norm_ffn1, ffn1_w_gate, ffn1_w_up, ffn1_w_down, norm_mix, s5_w_in, s5_lam_re, s5_lam_im, s5_log_dt, s5_b_re, s5_b_im, s5_c_re, s5_c_im, s5_d, s5_w_glu, s5_b_glu, s5_w_out, sb_w_qkv, sb_g_q, sb_g_k, sb_w_o, norm_ffn2, ffn2_w_gate, ffn2_w_up, ffn2_w_down, loss_target, m_norm_ffn1, m_ffn1_w_gate, m_ffn1_w_up, m_ffn1_w_down, m_norm_mix, m_s5_w_in, m_s5_lam_re, m_s5_lam_im, m_s5_log_dt, m_s5_b_re, m_s5_b_im, m_s5_c_re, m_s5_c_im, m_s5_d, m_s5_w_glu, m_s5_b_glu, m_s5_w_out, m_sb_w_qkv, m_sb_g_q, m_sb_g_k, m_sb_w_o, m_norm_ffn2, m_ffn2_w_gate, m_ffn2_w_up, m_ffn2_w_down, v_norm_ffn1, v_ffn1_w_gate, v_ffn1_w_up, v_ffn1_w_down, v_norm_mix, v_s5_w_in, v_s5_lam_re, v_s5_lam_im, v_s5_log_dt, v_s5_b_re, v_s5_b_im, v_s5_c_re, v_s5_c_im, v_s5_d, v_s5_w_glu, v_s5_b_glu, v_s5_w_out, v_sb_w_qkv, v_sb_g_q, v_sb_g_k, v_sb_w_o, v_norm_ffn2, v_ffn2_w_gate, v_ffn2_w_up, v_ffn2_w_down):
    w = dict(norm_ffn1=norm_ffn1, ffn1_w_gate=ffn1_w_gate, ffn1_w_up=ffn1_w_up, ffn1_w_down=ffn1_w_down, norm_mix=norm_mix, s5_w_in=s5_w_in, s5_lam_re=s5_lam_re, s5_lam_im=s5_lam_im, s5_log_dt=s5_log_dt, s5_b_re=s5_b_re, s5_b_im=s5_b_im, s5_c_re=s5_c_re, s5_c_im=s5_c_im, s5_d=s5_d, s5_w_glu=s5_w_glu, s5_b_glu=s5_b_glu, s5_w_out=s5_w_out, sb_w_qkv=sb_w_qkv, sb_g_q=sb_g_q, sb_g_k=sb_g_k, sb_w_o=sb_w_o, norm_ffn2=norm_ffn2, ffn2_w_gate=ffn2_w_gate, ffn2_w_up=ffn2_w_up, ffn2_w_down=ffn2_w_down)
    m = dict(norm_ffn1=m_norm_ffn1, ffn1_w_gate=m_ffn1_w_gate, ffn1_w_up=m_ffn1_w_up, ffn1_w_down=m_ffn1_w_down, norm_mix=m_norm_mix, s5_w_in=m_s5_w_in, s5_lam_re=m_s5_lam_re, s5_lam_im=m_s5_lam_im, s5_log_dt=m_s5_log_dt, s5_b_re=m_s5_b_re, s5_b_im=m_s5_b_im, s5_c_re=m_s5_c_re, s5_c_im=m_s5_c_im, s5_d=m_s5_d, s5_w_glu=m_s5_w_glu, s5_b_glu=m_s5_b_glu, s5_w_out=m_s5_w_out, sb_w_qkv=m_sb_w_qkv, sb_g_q=m_sb_g_q, sb_g_k=m_sb_g_k, sb_w_o=m_sb_w_o, norm_ffn2=m_norm_ffn2, ffn2_w_gate=m_ffn2_w_gate, ffn2_w_up=m_ffn2_w_up, ffn2_w_down=m_ffn2_w_down)
    v = dict(norm_ffn1=v_norm_ffn1, ffn1_w_gate=v_ffn1_w_gate, ffn1_w_up=v_ffn1_w_up, ffn1_w_down=v_ffn1_w_down, norm_mix=v_norm_mix, s5_w_in=v_s5_w_in, s5_lam_re=v_s5_lam_re, s5_lam_im=v_s5_lam_im, s5_log_dt=v_s5_log_dt, s5_b_re=v_s5_b_re, s5_b_im=v_s5_b_im, s5_c_re=v_s5_c_re, s5_c_im=v_s5_c_im, s5_d=v_s5_d, s5_w_glu=v_s5_w_glu, s5_b_glu=v_s5_b_glu, s5_w_out=v_s5_w_out, sb_w_qkv=v_sb_w_qkv, sb_g_q=v_sb_g_q, sb_g_k=v_sb_g_k, sb_w_o=v_sb_w_o, norm_ffn2=v_norm_ffn2, ffn2_w_gate=v_ffn2_w_gate, ffn2_w_up=v_ffn2_w_up, ffn2_w_down=v_ffn2_w_down)
    return _step(w, m, v, x, loss_target)
```

```python
import math

import jax
import jax.numpy as jnp
from jax import lax
from jax.experimental import pallas as pl
from jax.experimental.pallas import tpu as pltpu

f32 = jnp.float32
bf16 = jnp.bfloat16

NDEV = 8
AXES = ("x", "y", "c")
MESH = pl.DeviceIdType.MESH
EPS = 1e-6
HEAD = 128
S5H = 16
S5P = 64
GB = 8
FFN_RES = 0.5
ADAM_LR = 0.001
ADAM_B1 = 0.9
ADAM_B2 = 0.999
ADAM_EPS = 1e-08
ADAM_WD = 0.01
ADAM_STEP = 10
VMEM_LIMIT_V7X = 56 * 2 ** 20

NN = (((1,), (0,)), ((), ()))
NT = (((1,), (1,)), ((), ()))
TNd = (((0,), (0,)), ((), ()))


def _tile(n, target, align):
    if n <= target:
        return n
    t = (target // align) * align
    while t >= align:
        if n % t == 0:
            return t
        t -= align
    return n


def _params(sem):
    return pltpu.CompilerParams(dimension_semantics=sem, vmem_limit_bytes=VMEM_LIMIT_V7X)


def _sds(shape, dtype):
    return jax.ShapeDtypeStruct(tuple(shape), dtype)


def _mm(name, a_list, b_list, a_spec, b_spec, dims, grid, out_shapes, out_specs,
        epilogue=None, extra=(), extra_specs=(), nsub=1, merge_b=False, after=None, separate=False):
    n_a, n_p, n_e = len(a_list), len(b_list), len(extra)
    order = [] if after is None else [after]

    def body(*refs):
        a_refs = refs[:n_a]
        b_refs = refs[n_a:n_a + n_p]
        e_refs = refs[n_a + n_p:n_a + n_p + n_e]
        o_refs = refs[n_a + n_p + n_e + len(order):]
        prods = []
        for p, br in enumerate(b_refs):
            ar = a_refs[p if n_a > 1 else 0]
            if nsub > 1 and not merge_b:
                for j in range(nsub):
                    prods.append(lax.dot_general(ar[j].astype(bf16), br[j].astype(bf16), dims, preferred_element_type=f32))
            else:
                b = br[...]
                if merge_b:
                    b = b.reshape(b.shape[0] * b.shape[1], b.shape[2])
                prods.append(lax.dot_general(ar[...].astype(bf16), b.astype(bf16), dims, preferred_element_type=f32))
        if not separate:
            s = prods[0]
            for d in prods[1:]:
                s = s + d
            prods = [s]
        outs = tuple(prods) if epilogue is None else epilogue(*prods, *[e[...] for e in e_refs])
        for o, val in zip(o_refs, outs):
            o[...] = val.astype(o.dtype)

    return pl.pallas_call(
        body, name=name, grid=grid,
        in_specs=[a_spec] * n_a + [b_spec] * n_p + list(extra_specs) + [pl.BlockSpec(memory_space=pl.ANY)] * len(order),
        out_specs=list(out_specs), out_shape=list(out_shapes),
        compiler_params=_params(("parallel",) * len(grid)),
    )(*a_list, *b_list, *extra, *order)


EW_TILE_ELEMS = 1 << 19
ADAM_TILE_ELEMS = 1 << 21


def _ew(name, fn, rows, bcasts, outs, reds=(), tm=None, budget=EW_TILE_ELEMS):
    n_r, n_b, n_o, n_d = len(rows), len(bcasts), len(outs), len(reds)
    R = rows[0].shape[-2]
    if tm is None:
        widest = max([r.shape[-1] * (r.shape[0] if r.ndim == 3 else 1) for r in rows] + [c for c, _ in outs])
        tm = _tile(R, max(16, budget // widest), 16)

    def body(*refs):
        r = refs[:n_r]
        b = refs[n_r:n_r + n_b]
        o = refs[n_r + n_b:n_r + n_b + n_o]
        d = refs[n_r + n_b + n_o:]
        res = fn(*[x[...] for x in r], *[x[...] for x in b])
        for oo, val in zip(o, res[:n_o]):
            oo[...] = val.astype(oo.dtype)
        if n_d:
            @pl.when(pl.program_id(0) == 0)
            def _():
                for dd in d:
                    dd[...] = jnp.zeros_like(dd)
            for dd, val in zip(d, res[n_o:]):
                dd[...] += val

    in_specs = []
    for x in rows:
        if x.ndim == 2:
            in_specs.append(pl.BlockSpec((tm, x.shape[1]), lambda i: (i, 0)))
        else:
            in_specs.append(pl.BlockSpec((x.shape[0], tm, x.shape[2]), lambda i: (0, i, 0)))
    for x in bcasts:
        in_specs.append(pl.BlockSpec(x.shape, lambda i, nd=x.ndim: (0,) * nd))
    out_shape = [_sds((R, c), dt) for c, dt in outs] + [_sds((1, c), f32) for c in reds]
    out_specs = [pl.BlockSpec((tm, c), lambda i: (i, 0)) for c, _ in outs] + [pl.BlockSpec((1, c), lambda i: (0, 0)) for c in reds]
    return pl.pallas_call(
        body, name=name, grid=(R // tm,), in_specs=in_specs, out_specs=out_specs, out_shape=out_shape,
        compiler_params=_params(("arbitrary",)),
    )(*rows, *bcasts)


def _coords():
    return lax.axis_index("x"), lax.axis_index("y"), lax.axis_index("c")


def _me():
    x, y, c = _coords()
    return 4 * x + 2 * y + c


def _peer(k):
    x, y, c = _coords()
    return x ^ ((k >> 2) & 1), y ^ ((k >> 1) & 1), c ^ (k & 1)


def all_gather(name, shards):
    n = len(shards)

    def body(*refs):
        x_refs, out_refs, token = refs[:n], refs[n:2 * n], refs[2 * n]
        send_sems, recv_sems, local_sems = refs[2 * n + 1:]
        token[...] = jnp.zeros_like(token)
        x, y, c = _coords()
        me, sibling = (x, y, c), (x, y, 1 - c)
        chips = [(1 - x, y), (x, 1 - y), (1 - x, 1 - y)]

        def rows(a, px, py, pc):
            return out_refs[a].at[4 * px + 2 * py + pc]

        def copy(a, k, block, to, src=None):
            return pltpu.make_async_remote_copy(
                src_ref=rows(a, *block) if src is None else src, dst_ref=rows(a, *block),
                send_sem=send_sems.at[a, k], recv_sem=recv_sems.at[a, k],
                device_id=to, device_id_type=MESH)

        mine = [pltpu.make_async_copy(x_refs[a], rows(a, *me), local_sems.at[a]) for a in range(n)]
        for cp in mine:
            cp.start()
        first = []
        for a in range(n):
            first.append(copy(a, 0, me, sibling, src=x_refs[a]))
            first += [copy(a, 1 + j, me, (*chip, c), src=x_refs[a]) for j, chip in enumerate(chips)]
        for cp in first:
            cp.start()
        passed = []
        for j, chip in enumerate(chips):
            for a in range(n):
                copy(a, 1 + j, (*chip, c), me).wait_recv()
                cp = copy(a, 4 + j, (*chip, c), sibling)
                cp.start()
                passed.append(cp)
        for a in range(n):
            copy(a, 0, sibling, me).wait_recv()
            for j, chip in enumerate(chips):
                copy(a, 4 + j, (*chip, 1 - c), me).wait_recv()
        for cp in first + passed:
            cp.wait_send()
        for cp in mine:
            cp.wait()

    anyspec = pl.BlockSpec(memory_space=pl.ANY)
    outs = pl.pallas_call(
        body, name=name,
        out_shape=[_sds((NDEV,) + s.shape, s.dtype) for s in shards] + [_sds((8, 128), f32)],
        in_specs=[anyspec] * n, out_specs=[anyspec] * n + [pl.BlockSpec(memory_space=pltpu.VMEM)],
        scratch_shapes=[pltpu.SemaphoreType.DMA((n, 7)), pltpu.SemaphoreType.DMA((n, 7)), pltpu.SemaphoreType.DMA((n,))],
    )(*shards)
    return list(outs[:n]), outs[n]


def to_bf16(name, w3, l):
    _, R, C = w3.shape
    tm = _tile(R, max(16, 2 * EW_TILE_ELEMS // C), 16)

    def body(x_ref, o_ref):
        o_ref[...] = x_ref[...].astype(bf16)

    return pl.pallas_call(
        body, name=name, grid=(R // tm,), in_specs=[pl.BlockSpec((None, tm, C), lambda i: (l, i, 0))],
        out_specs=pl.BlockSpec((tm, C), lambda i: (i, 0)), out_shape=_sds((R, C), bf16),
        compiler_params=_params(("parallel",)))(w3)


HBM_SPEC = pl.BlockSpec(memory_space=pltpu.HBM)
SEM_SPEC = pl.BlockSpec(memory_space=pltpu.SEMAPHORE)
EFFECT = pltpu.SideEffectType.DATAFLOW_SIDE_EFFECTING


def exchange_start(name, srcs, gather, after=None):
    n = len(srcs)
    order = [] if after is None else [after]
    ns = n * (NDEV - 1)
    n_sem = 2 * ns + n
    lands = [lax.empty(((NDEV,) + s.shape) if gather else s.shape, s.dtype) for s in srcs]

    def body(*refs):
        src_refs, land_refs = refs[:n], refs[n:2 * n]
        first = 2 * n + len(order)
        send_sems, recv_sems = refs[first:first + ns], refs[first + ns:first + 2 * ns]
        local_sems = refs[first + 2 * ns:first + n_sem]
        token = refs[-1]
        me_ = _me()
        for k in range(1, NDEV):
            px, py, pc = _peer(k)
            p = 4 * px + 2 * py + pc
            for a in range(n):
                i = a * (NDEV - 1) + k - 1
                pltpu.make_async_remote_copy(
                    src_ref=src_refs[a] if gather else src_refs[a].at[p], dst_ref=land_refs[a].at[me_],
                    send_sem=send_sems[i], recv_sem=recv_sems[i],
                    device_id=(px, py, pc), device_id_type=MESH).start()
        for a in range(n):
            pltpu.make_async_copy(src_refs[a] if gather else src_refs[a].at[me_], land_refs[a].at[me_], local_sems[a]).start()
        token[...] = jnp.zeros_like(token)

    outs = pl.pallas_call(
        body, name=name,
        out_shape=(*[pltpu.SemaphoreType.DMA(())] * n_sem,
                   *[pltpu.HBM(s.shape, s.dtype) for s in srcs], *[pltpu.HBM(l.shape, l.dtype) for l in lands],
                   _sds((8, 128), f32)),
        in_specs=[HBM_SPEC] * (2 * n) + [pl.BlockSpec(memory_space=pl.ANY)] * len(order),
        out_specs=(*[SEM_SPEC] * n_sem, *[HBM_SPEC] * (2 * n), pl.BlockSpec(memory_space=pltpu.VMEM)),
        input_output_aliases={i: n_sem + i for i in range(2 * n)},
        compiler_params=pltpu.CompilerParams(has_side_effects=EFFECT),
    )(*[pltpu.with_memory_space_constraint(s, pltpu.HBM) for s in srcs],
      *[pltpu.with_memory_space_constraint(l, pltpu.HBM) for l in lands], *order)
    return dict(n=n, sems=outs[:n_sem], srcs=outs[n_sem:n_sem + n], lands=outs[n_sem + n:n_sem + 2 * n], token=outs[-1], gather=gather)


def exchange_wait(name, hd, after):
    n, gather = hd["n"], hd["gather"]
    ns = n * (NDEV - 1)

    def body(*refs):
        src_refs, land_refs = refs[:n], refs[n:2 * n]
        send_sems, recv_sems = refs[2 * n:2 * n + ns], refs[2 * n + ns:2 * n + 2 * ns]
        local_sems = refs[2 * n + 2 * ns:2 * n + 2 * ns + n]
        x, y, c = _coords()
        me_ = _me()
        for a in range(n):
            pltpu.make_async_copy(src_refs[a] if gather else src_refs[a].at[me_], land_refs[a].at[me_], local_sems[a]).wait()
        for k in range(1, NDEV):
            px, py, pc = _peer(k)
            p = 4 * px + 2 * py + pc
            for a in range(n):
                i = a * (NDEV - 1) + k - 1
                cp = pltpu.make_async_remote_copy(
                    src_ref=src_refs[a] if gather else src_refs[a].at[p], dst_ref=land_refs[a].at[p],
                    send_sem=send_sems[i], recv_sem=recv_sems[i],
                    device_id=(x, y, 1 - c), device_id_type=MESH)
                cp.wait_send()
                cp.wait_recv()

    outs = pl.pallas_call(
        body, name=name,
        out_shape=tuple(pltpu.HBM(s.shape, s.dtype) for s in list(hd["srcs"]) + list(hd["lands"])),
        in_specs=[HBM_SPEC] * (2 * n) + [SEM_SPEC] * (2 * ns + n) + [pl.BlockSpec(memory_space=pl.ANY)],
        out_specs=tuple([HBM_SPEC] * (2 * n)),
        input_output_aliases={i: i for i in range(2 * n)},
        compiler_params=pltpu.CompilerParams(has_side_effects=EFFECT),
    )(*hd["srcs"], *hd["lands"], *hd["sems"], after)
    return list(outs[n:])


def _adam_math(w_, m_, v_, r_):
    g = r_[0].astype(f32)
    for j in range(1, NDEV):
        g = g + r_[j].astype(f32)
    m2 = ADAM_B1 * m_ + (1.0 - ADAM_B1) * g
    v2 = ADAM_B2 * v_ + (1.0 - ADAM_B2) * jnp.square(g)
    m_hat = m2 / (1.0 - ADAM_B1 ** ADAM_STEP)
    v_hat = v2 / (1.0 - ADAM_B2 ** ADAM_STEP)
    delta = -ADAM_LR * (m_hat / (jnp.sqrt(v_hat) + ADAM_EPS) + ADAM_WD * w_)
    return g, delta, m2, v2


def adamw(name, w, m, v, recv):
    return _ew(name, _adam_math, [w, m, v, recv], [], [(w.shape[1], f32)] * 4, budget=ADAM_TILE_ELEMS)


def adamw_layer(name, w3, m3, v3, recv, l, prev):
    L, R, C = w3.shape
    tm = _tile(R, max(16, ADAM_TILE_ELEMS // (NDEV * C)), 16)
    n_prev = 0 if prev is None else 4

    def body(*refs):
        w_ref, m_ref, v_ref, r_ref = refs[:4]
        o_refs = refs[4 + n_prev:]
        res = _adam_math(w_ref[...], m_ref[...], v_ref[...], r_ref[...])
        for o, val in zip(o_refs, res):
            o[...] = val

    lay = pl.BlockSpec((None, tm, C), lambda i: (l, i, 0))
    return pl.pallas_call(
        body, name=name, grid=(R // tm,),
        in_specs=[lay, lay, lay, pl.BlockSpec((NDEV, tm, C), lambda i: (0, i, 0))] + [pl.BlockSpec(memory_space=pl.ANY)] * n_prev,
        out_specs=[lay] * 4, out_shape=[_sds((L, R, C), f32)] * 4,
        input_output_aliases={4 + i: i for i in range(n_prev)},
        compiler_params=_params(("parallel",)),
    )(w3, m3, v3, recv, *(prev or []))


def rmsnorm_fwd(name, x, g):
    D = x.shape[1]

    def fn(x_, g_):
        r = lax.rsqrt(jnp.mean(x_ * x_, axis=-1, keepdims=True) + EPS)
        return ((x_ * r) * g_,)

    return _ew(name, fn, [x], [g], [(D, bf16)])[0]


def rmsnorm_bwd(name, x, g, dh, dres):
    D = x.shape[1]

    def fn(x_, dh_, dres_, g_):
        r = lax.rsqrt(jnp.mean(x_ * x_, axis=-1, keepdims=True) + EPS)
        xh = x_ * r
        dxh = dh_ * g_
        dx = dres_ + r * (dxh - xh * jnp.mean(dxh * xh, axis=-1, keepdims=True))
        return dx, dx, jnp.sum(dh_ * xh, axis=0, keepdims=True)

    return _ew(name, fn, [x, dh, dres], [g], [(D, f32), (D, bf16)], [D])


def _silu_parts(a):
    sig = jax.nn.sigmoid(a)
    return sig, a * sig


def mm_cols(name, h, w3_list, epilogue=None, extra=(), outs=None):
    T, K = h.shape
    Nb = w3_list[0].shape[2]
    tm = _tile(T, 1024, 16)
    outs = outs or [f32]
    blk = pl.BlockSpec((None, tm, Nb), lambda j, i: (j, i, 0))
    return _mm(name, [h], w3_list, pl.BlockSpec((tm, K), lambda j, i: (i, 0)), pl.BlockSpec((None, K, Nb), lambda j, i: (j, 0, 0)),
               NN, (NDEV, T // tm), [_sds((NDEV, T, Nb), dt) for dt in outs], [blk] * len(outs), epilogue, extra, [blk] * len(extra),
               separate=True)


def mm_rows(name, p_list, w3_list, epilogue=None, extra=(), outs=None, after=None):
    _, T, Kb = p_list[0].shape
    N = w3_list[0].shape[2]
    tm, tn = _tile(T, 512, 16), _tile(N, 512 // len(p_list), 128)
    outs = outs or [f32]
    blk = pl.BlockSpec((tm, tn), lambda i, n: (i, n))
    return _mm(name, p_list, w3_list, pl.BlockSpec((NDEV, tm, Kb), lambda i, n: (0, i, 0)), pl.BlockSpec((NDEV, Kb, tn), lambda i, n: (0, 0, n)),
               NN, (T // tm, N // tn), [_sds((T, N), dt) for dt in outs], [blk] * len(outs), epilogue, extra, [blk] * len(extra), nsub=NDEV,
               after=after)


def mm_rows_plain(name, a, w3, epilogue=None, extra=(), extra_specs=None, outs=None):
    T, K = a.shape
    Kb, N = w3.shape[1], w3.shape[2]
    tm, tn = _tile(T, 512, 16), _tile(N, 1024, 128)
    outs = outs or [f32]
    blk = pl.BlockSpec((tm, tn), lambda i, n: (i, n))
    return _mm(name, [a], [w3], pl.BlockSpec((tm, K), lambda i, n: (i, 0)), pl.BlockSpec((NDEV, Kb, tn), lambda i, n: (0, 0, n)),
               NN, (T // tm, N // tn), [_sds((T, N), dt) for dt in outs], [blk] * len(outs), epilogue, extra,
               extra_specs or [blk] * len(extra), nsub=NDEV, merge_b=True)


def mm_t_rows(name, d, w3_list, epilogue=None, extra=(), outs=None, rows=512):
    T, K = d.shape
    Nb = w3_list[0].shape[1]
    tm = _tile(T, rows, 16)
    outs = outs or [f32]
    blk = pl.BlockSpec((None, tm, Nb), lambda j, i: (j, i, 0))
    return _mm(name, [d], w3_list, pl.BlockSpec((tm, K), lambda j, i: (i, 0)), pl.BlockSpec((None, Nb, K), lambda j, i: (j, 0, 0)),
               NT, (NDEV, T // tm), [_sds((NDEV, T, Nb), dt) for dt in outs], [blk] * len(outs), epilogue, extra, [blk] * len(extra),
               separate=True)


def mm_t_rows_plain(name, d, w3, epilogue=None, extra=(), outs=None, after=None):
    T, K = d.shape
    Nb = w3.shape[1]
    tm = _tile(T, 256, 16)
    outs = outs or [f32]
    blk = pl.BlockSpec((tm, NDEV * Nb), lambda i: (i, 0))
    return _mm(name, [d], [w3], pl.BlockSpec((tm, K), lambda i: (i, 0)), pl.BlockSpec((NDEV, Nb, K), lambda i: (0, 0, 0)),
               NT, (T // tm,), [_sds((T, NDEV * Nb), dt) for dt in outs], [blk] * len(outs), epilogue, extra, [blk] * len(extra),
               nsub=NDEV, merge_b=True, after=after)


def mm_t_cols(name, d_list, w3_list, epilogue=None, extra=(), outs=None, after=None):
    _, T, Kb = d_list[0].shape
    N = w3_list[0].shape[1]
    tm, tn = _tile(T, 512, 16), _tile(N, 256, 128)
    outs = outs or [f32]
    blk = pl.BlockSpec((tm, tn), lambda i, n: (i, n))
    return _mm(name, d_list, w3_list, pl.BlockSpec((NDEV, tm, Kb), lambda i, n: (0, i, 0)), pl.BlockSpec((NDEV, tn, Kb), lambda i, n: (0, n, 0)),
               NT, (T // tm, N // tn), [_sds((T, N), dt) for dt in outs], [blk] * len(outs), epilogue, extra, [blk] * len(extra), nsub=NDEV,
               after=after)


def mm_grad_rows(name, p, d, scale=1.0):
    _, T, Mb = p.shape
    N = d.shape[1]
    tn = _tile(N, 512, 128)
    return _mm(name, [p], [d], pl.BlockSpec((None, T, Mb), lambda j, n: (j, 0, 0)), pl.BlockSpec((T, tn), lambda j, n: (0, n)),
               TNd, (NDEV, N // tn), [_sds((NDEV, Mb, N), bf16)], [pl.BlockSpec((None, Mb, tn), lambda j, n: (j, 0, n))],
               (lambda acc: (acc * scale,)))[0]


def mm_grad_rows_plain(name, a, d):
    T, M = a.shape
    N = d.shape[1]
    tm, tn = _tile(M, 512, 128), _tile(N, 512, 128)
    g = _mm(name, [a], [d], pl.BlockSpec((T, tm), lambda m, n: (0, m)), pl.BlockSpec((T, tn), lambda m, n: (0, n)),
            TNd, (M // tm, N // tn), [_sds((M, N), bf16)], [pl.BlockSpec((tm, tn), lambda m, n: (m, n))])[0]
    return g.reshape(NDEV, M // NDEV, N)


def mm_grad_cols(name, h, d):
    T, M = h.shape
    Nb = d.shape[2]
    tm = _tile(M, 512, 128)
    return _mm(name, [h], [d], pl.BlockSpec((T, tm), lambda j, m: (0, m)), pl.BlockSpec((None, T, Nb), lambda j, m: (j, 0, 0)),
               TNd, (NDEV, M // tm), [_sds((NDEV, M, Nb), bf16)], [pl.BlockSpec((None, tm, Nb), lambda j, m: (j, m, 0))])[0]


def ffn_fwd(tag, x, g, wg3, wu3, wd3):
    h = rmsnorm_fwd(tag + "_norm", x, g)

    def gate_up_epi(a_, b_):
        _, sl = _silu_parts(a_)
        return a_, b_, sl * b_

    a, b, p = mm_t_rows(tag + "_gateup", h, [wg3, wu3], gate_up_epi, outs=[bf16, bf16, bf16], rows=1024)
    xo = mm_rows(tag + "_down", [p], [wd3], lambda acc, x_: (x_ + FFN_RES * acc,), [x])[0]
    return xo, (x, h, a, b, p)


def ffn_bwd(tag, dxo, dxo16, saved, g, wg3, wu3, wd3, send, each=False):
    x, h, a, b, p = saved

    def dp_epi(acc, a_, b_):
        dp = FFN_RES * acc
        a_, b_ = a_.astype(f32), b_.astype(f32)
        sig, sl = _silu_parts(a_)
        return dp * b_ * (sig * (1.0 + a_ * (1.0 - sig))), dp * sl

    da, db = mm_t_rows(tag + "_dp", dxo16, [wd3], dp_epi, [a, b], [bf16, bf16], rows=1024)
    g_wd = mm_grad_rows(tag + "_gwd", p, dxo16, FFN_RES)
    if each:
        send([g_wd], [2])
    g_wg = mm_grad_rows(tag + "_gwg", da, h)
    if each:
        send([g_wg], [0])
    g_wu = mm_grad_rows(tag + "_gwu", db, h)
    tok = send([g_wu], [1]) if each else send([g_wg, g_wu, g_wd], [0, 1, 2])
    dh = mm_rows(tag + "_dh", [da, db], [wg3, wu3], after=tok)[0]
    return rmsnorm_bwd(tag + "_dnorm", x, g, dh, dxo)


def _disc(lr, li, ldt, brt, bit):
    dt = jnp.exp(ldt)
    lr = jnp.minimum(lr, -1e-4)
    mag = jnp.exp(lr * dt)
    ab_re = mag * jnp.cos(li * dt)
    ab_im = mag * jnp.sin(li * dt)
    den = lr * lr + li * li
    n_re = ab_re - 1.0
    f_re = (n_re * lr + ab_im * li) / den
    f_im = (ab_im * lr - n_re * li) / den
    bb_re = f_re[:, None, :] * brt - f_im[:, None, :] * bit
    bb_im = f_re[:, None, :] * bit + f_im[:, None, :] * brt
    return ab_re, ab_im, bb_re, bb_im


def disc_fwd(lr, li, ldt, brt, bit):
    def body(lr_ref, li_ref, ldt_ref, brt_ref, bit_ref, ar_ref, ai_ref, bbr_ref, bbi_ref):
        ar, ai, bbr, bbi = _disc(lr_ref[...], li_ref[...], ldt_ref[...], brt_ref[...], bit_ref[...])
        ar_ref[...] = ar
        ai_ref[...] = ai
        bbr_ref[...] = bbr
        bbi_ref[...] = bbi

    return pl.pallas_call(body, name="s5_disc", out_shape=[_sds(lr.shape, f32), _sds(lr.shape, f32), _sds(brt.shape, f32), _sds(brt.shape, f32)],
                          compiler_params=pltpu.CompilerParams(vmem_limit_bytes=VMEM_LIMIT_V7X))(lr, li, ldt, brt, bit)


def disc_bwd(lr, li, ldt, brt, bit, d_ar, d_ai, d_bbr, d_bbi):
    def body(lr_ref, li_ref, ldt_ref, brt_ref, bit_ref, dar_ref, dai_ref, dbbr_ref, dbbi_ref, o_lr, o_li, o_ldt, o_brt, o_bit):
        _, vjp = jax.vjp(_disc, lr_ref[...], li_ref[...], ldt_ref[...], brt_ref[...], bit_ref[...])
        g = vjp((dar_ref[...], dai_ref[...], dbbr_ref[...], dbbi_ref[...]))
        o_lr[...] = g[0]
        o_li[...] = g[1]
        o_ldt[...] = g[2]
        o_brt[...] = g[3]
        o_bit[...] = g[4]

    return pl.pallas_call(body, name="s5_disc_bwd",
                          out_shape=[_sds(lr.shape, f32), _sds(lr.shape, f32), _sds(ldt.shape, f32), _sds(brt.shape, f32), _sds(brt.shape, f32)],
                          compiler_params=pltpu.CompilerParams(vmem_limit_bytes=VMEM_LIMIT_V7X))(lr, li, ldt, brt, bit, d_ar, d_ai, d_bbr, d_bbi)


def _blockdiag(m):
    G, R, C = m.shape
    eye = jnp.eye(GB, dtype=m.dtype)
    m5 = m.reshape(G // GB, GB, R, 1, C) * eye[None, :, None, :, None]
    return m5.reshape(G // GB, GB * R, GB * C)


def _diag_extract(M, R, C):
    nb = M.shape[0]
    eye = jnp.eye(GB, dtype=M.dtype)
    m5 = M.reshape(nb, GB, R, GB, C) * eye[None, :, None, :, None]
    return m5.sum(axis=3).reshape(nb * GB, R, C)


def mm_blockdiag(name, a_list, bd_list, dims, epilogue=None, extra=(), extra_bcast=(), outs=None):
    T = a_list[0].shape[0]
    nb, r, c = bd_list[0].shape
    ra = a_list[0].shape[1] // nb
    ca = c if dims == NN else r
    n_p, n_e, n_b = len(a_list), len(extra), len(extra_bcast)
    outs = outs or [f32]
    tm = _tile(T, 128, 16)

    def body(*refs):
        a_refs, b_refs = refs[:n_p], refs[n_p:2 * n_p]
        e_refs = refs[2 * n_p:2 * n_p + n_e + n_b]
        o_refs = refs[2 * n_p + n_e + n_b:]
        for b in range(nb):
            s = None
            for ar, br in zip(a_refs, b_refs):
                d = lax.dot_general(ar[:, b * ra:(b + 1) * ra].astype(bf16), br[b], dims, preferred_element_type=f32)
                s = d if s is None else s + d
            cols = slice(b * ca, (b + 1) * ca)
            vals = (s,) if epilogue is None else epilogue(s, *[e[:, cols] for e in e_refs])
            for o, val in zip(o_refs, vals):
                o[:, cols] = val.astype(o.dtype)

    row = lambda w: pl.BlockSpec((tm, w), lambda i: (i, 0))
    return pl.pallas_call(
        body, name=name, grid=(T // tm,),
        in_specs=[row(nb * ra)] * n_p + [pl.BlockSpec((nb, r, c), lambda i: (0, 0, 0))] * n_p + [row(nb * ca)] * n_e
        + [pl.BlockSpec((1, nb * ca), lambda i: (0, 0))] * n_b,
        out_specs=[row(nb * ca)] * len(outs), out_shape=[_sds((T, nb * ca), dt) for dt in outs],
        compiler_params=_params(("parallel",)),
    )(*a_list, *bd_list, *extra, *extra_bcast)


def mm_blockdiag_grad(name, a, d, nb):
    T = a.shape[0]
    ra, rd = a.shape[1] // nb, d.shape[1] // nb
    tk = _tile(T, 256, 16)

    def body(a_ref, d_ref, o_ref):
        @pl.when(pl.program_id(0) == 0)
        def _():
            o_ref[...] = jnp.zeros_like(o_ref)

        for b in range(nb):
            o_ref[b] += lax.dot_general(a_ref[:, b * ra:(b + 1) * ra].astype(bf16), d_ref[:, b * rd:(b + 1) * rd].astype(bf16),
                                        TNd, preferred_element_type=f32)

    return pl.pallas_call(
        body, name=name, grid=(T // tk,),
        in_specs=[pl.BlockSpec((tk, nb * ra), lambda k: (k, 0)), pl.BlockSpec((tk, nb * rd), lambda k: (k, 0))],
        out_specs=pl.BlockSpec((nb, ra, rd), lambda k: (0, 0, 0)), out_shape=_sds((nb, ra, rd), f32),
        compiler_params=_params(("arbitrary",)),
    )(a, d)


SUB = 8
SCAN_LANES = 1024
SCAN_GROUPS = 64


def s5_tables(ar, ai):
    N = ar.shape[1]

    def body(ar_ref, ai_ref, f_ref, b_ref):
        r, i = ar_ref[...], ai_ref[...]
        pw = [(r, i)]
        for _ in range(SUB - 1):
            pr, pi = pw[-1]
            pw.append((pr * r - pi * i, pr * i + pi * r))
        row = lax.broadcasted_iota(jnp.int32, (SUB, N), 0)
        for t, k in enumerate((1, 2, 4)):
            pr, pi = pw[k - 1]
            f_ref[2 * t] = jnp.where(row >= k, pr, 0.0)
            f_ref[2 * t + 1] = jnp.where(row >= k, pi, 0.0)
            b_ref[2 * t] = jnp.where(row <= SUB - 1 - k, pr, 0.0)
            b_ref[2 * t + 1] = jnp.where(row <= SUB - 1 - k, -pi, 0.0)
        fr = fi = br = bi = jnp.zeros((SUB, N), f32)
        for j in range(SUB):
            fr, fi = jnp.where(row == j, pw[j][0], fr), jnp.where(row == j, pw[j][1], fi)
            br, bi = jnp.where(row == j, pw[SUB - 1 - j][0], br), jnp.where(row == j, -pw[SUB - 1 - j][1], bi)
        f_ref[6], f_ref[7] = fr, fi
        b_ref[6], b_ref[7] = br, bi

    return pl.pallas_call(body, name="s5_tables", out_shape=[_sds((8, SUB, N), f32)] * 2,
                          compiler_params=pltpu.CompilerParams(vmem_limit_bytes=VMEM_LIMIT_V7X))(ar, ai)


def _group_scan(xr, xi, m, cr, ci, up):
    for t, k in enumerate((1, 2, 4)):
        shift = SUB - k if up else k
        pr, pi = pltpu.roll(xr, shift, 0), pltpu.roll(xi, shift, 0)
        xr, xi = xr + m[2 * t] * pr - m[2 * t + 1] * pi, xi + m[2 * t] * pi + m[2 * t + 1] * pr
    return xr + m[6] * cr - m[7] * ci, xi + m[6] * ci + m[7] * cr


def s5_scan_fwd(bu_re, bu_im, tab):
    T8, _, N = bu_re.shape
    W, tb = _tile(N, SCAN_LANES, 128), _tile(T8, SCAN_GROUPS, 1)

    def body(bur_ref, bui_ref, tab_ref, sr_ref, si_ref, st):
        @pl.when(pl.program_id(1) == 0)
        def _():
            st[...] = jnp.zeros_like(st)

        def step(g, c):
            m = [tab_ref[t] for t in range(8)]
            xr, xi = _group_scan(bur_ref[g], bui_ref[g], m, c[0], c[1], False)
            sr_ref[g] = xr
            si_ref[g] = xi
            return jnp.broadcast_to(xr[SUB - 1:SUB], (SUB, W)), jnp.broadcast_to(xi[SUB - 1:SUB], (SUB, W))

        cr, ci = lax.fori_loop(0, tb, step, (st[0], st[1]))
        st[0] = cr
        st[1] = ci

    blk = pl.BlockSpec((tb, SUB, W), lambda j, i: (i, 0, j))
    tabs = pl.BlockSpec((8, SUB, W), lambda j, i: (0, 0, j))
    return pl.pallas_call(body, name="s5_scan", grid=(N // W, T8 // tb), in_specs=[blk, blk, tabs], out_specs=[blk, blk],
                          out_shape=[_sds(bu_re.shape, f32)] * 2, scratch_shapes=[pltpu.VMEM((2, SUB, W), f32)],
                          compiler_params=_params(("parallel", "arbitrary")))(bu_re, bu_im, tab)


def s5_scan_bwd(ds_re, ds_im, s_re, s_im, tab):
    T8, _, N = ds_re.shape
    W, tb = _tile(N, SCAN_LANES, 128), _tile(T8, SCAN_GROUPS // 2, 1)
    nblk = T8 // tb

    def body(dsr_ref, dsi_ref, sr_ref, si_ref, tab_ref, qr_ref, qi_ref, dar_ref, dai_ref, st):
        @pl.when(pl.program_id(1) == 0)
        def _():
            st[...] = jnp.zeros_like(st)
            dar_ref[...] = jnp.zeros_like(dar_ref)
            dai_ref[...] = jnp.zeros_like(dai_ref)

        last_row = lax.broadcasted_iota(jnp.int32, (SUB, W), 0) == SUB - 1

        def step(i, c):
            cr, ci, dar, dai = c
            g = tb - 1 - i
            m = [tab_ref[t] for t in range(8)]
            xr, xi = _group_scan(dsr_ref[g], dsi_ref[g], m, cr, ci, True)
            qr_ref[g] = xr
            qi_ref[g] = xi
            nr = jnp.where(last_row, cr, pltpu.roll(xr, SUB - 1, 0))
            ni = jnp.where(last_row, ci, pltpu.roll(xi, SUB - 1, 0))
            sr, si = sr_ref[g], si_ref[g]
            dar = dar + nr * sr + ni * si
            dai = dai + ni * sr - nr * si
            return jnp.broadcast_to(xr[0:1], (SUB, W)), jnp.broadcast_to(xi[0:1], (SUB, W)), dar, dai

        cr, ci, dar, dai = lax.fori_loop(0, tb, step, (st[0], st[1], dar_ref[...], dai_ref[...]))
        st[0] = cr
        st[1] = ci
        dar_ref[...] = dar
        dai_ref[...] = dai

        @pl.when(pl.program_id(1) == nblk - 1)
        def _():
            dar_ref[...] = jnp.broadcast_to(jnp.sum(dar, axis=0, keepdims=True), (SUB, W))
            dai_ref[...] = jnp.broadcast_to(jnp.sum(dai, axis=0, keepdims=True), (SUB, W))

    blk = pl.BlockSpec((tb, SUB, W), lambda j, i: (nblk - 1 - i, 0, j))
    tabs = pl.BlockSpec((8, SUB, W), lambda j, i: (0, 0, j))
    acc = pl.BlockSpec((SUB, W), lambda j, i: (0, j))
    return pl.pallas_call(body, name="s5_scan_bwd", grid=(N // W, nblk), in_specs=[blk, blk, blk, blk, tabs], out_specs=[blk, blk, acc, acc],
                          out_shape=[_sds(ds_re.shape, f32)] * 2 + [_sds((SUB, N), f32)] * 2, scratch_shapes=[pltpu.VMEM((2, SUB, W), f32)],
                          compiler_params=_params(("parallel", "arbitrary")))(ds_re, ds_im, s_re, s_im, tab)


def _gelu_grad(y):
    c = math.sqrt(2.0 / math.pi)
    th = jnp.tanh(c * (y + 0.044715 * (y * y * y)))
    return 0.5 * (1.0 + th) + 0.5 * y * (1.0 - th * th) * c * (1.0 + 3.0 * 0.044715 * (y * y))


def s5_prepare(lam_re, lam_im, log_dt, b_re, b_im, c_re, c_im):
    G = lam_re.shape[0]
    ldt = log_dt.reshape(G, 1)
    brt, bit = jnp.swapaxes(b_re, 1, 2), jnp.swapaxes(b_im, 1, 2)
    ar, ai, bbr, bbi = disc_fwd(lam_re, lam_im, ldt, brt, bit)
    tab_f, tab_b = s5_tables(ar.reshape(1, -1), ai.reshape(1, -1))
    mats = dict(
        bbd_re=_blockdiag(bbr).astype(bf16), bbd_im=_blockdiag(bbi).astype(bf16),
        cd_re=_blockdiag(jnp.swapaxes(c_re, 1, 2)).astype(bf16),
        cd_imn=_blockdiag(-jnp.swapaxes(c_im, 1, 2)).astype(bf16),
        tab_f=tab_f, tab_b=tab_b)
    return (lam_re, lam_im, ldt, brt, bit), mats


def s5_fwd(x, gm, w_in3, w_glu3, w_out3, mats, d_skip, b_glu):
    T, D = x.shape
    N = mats["tab_f"].shape[2]
    h = rmsnorm_fwd("s5_norm", x, gm)
    u = mm_rows_plain("s5_in", h, w_in3)[0]
    bu_re = mm_blockdiag("s5_bu_re", [u], [mats["bbd_re"]], NN)[0]
    bu_im = mm_blockdiag("s5_bu_im", [u], [mats["bbd_im"]], NN)[0]
    grouped = (T // SUB, SUB, N)
    s_re, s_im = s5_scan_fwd(bu_re.reshape(grouped), bu_im.reshape(grouped), mats["tab_f"])
    s_re, s_im = s_re.reshape(T, N), s_im.reshape(T, N)

    def y_epi(acc, u_, d_):
        y2 = acc + d_ * u_
        return y2, jax.nn.gelu(y2)

    y2, gl = mm_blockdiag("s5_y", [s_re, s_im], [mats["cd_re"], mats["cd_imn"]], NN, y_epi, [u], [d_skip], [f32, bf16])

    def glu_epi(acc, y2_, b_):
        zg = acc + b_
        return zg, jax.nn.gelu(y2_) * jax.nn.sigmoid(zg)

    tm, tn = _tile(T, 512, 16), _tile(D, 1024, 128)
    zg, o = mm_rows_plain("s5_glu", gl, w_glu3, glu_epi, [y2, b_glu],
                          [pl.BlockSpec((tm, tn), lambda i, n: (i, n)), pl.BlockSpec((1, tn), lambda i, n: (0, n))], [f32, bf16])
    xo = mm_rows_plain("s5_out", o, w_out3, lambda acc, x_: (x_ + acc,), [x])[0]
    return xo, (x, h, u, s_re, s_im, y2, gl, zg, o)


def s5_bwd(dxo, dxo16, saved, gm, w_in3, w_glu3, w_out3, mats, d_skip, send):
    x, h, u, s_re, s_im, y2, gl, zg, o = saved
    T, D = x.shape
    N = mats["tab_b"].shape[2]
    nb = mats["cd_re"].shape[0]

    def do_epi(acc, y2_, zg_):
        sg = jax.nn.sigmoid(zg_)
        return acc * sg, acc * jax.nn.gelu(y2_) * (sg * (1.0 - sg))

    dgl_direct, dzg = mm_t_rows_plain("s5_do", dxo16, w_out3, do_epi, [y2, zg], [f32, f32])
    g_wout = mm_grad_rows_plain("s5_gwout", o, dxo16)
    g_wglu = mm_grad_rows_plain("s5_gwglu", gl, dzg)
    dy2 = mm_t_rows_plain("s5_dgl", dzg, w_glu3, lambda acc, dd_, y2_: ((acc + dd_) * _gelu_grad(y2_),), [dgl_direct, y2])[0]

    def red_fn(dy2_, u_, dzg_, d_):
        return dy2_ * d_, jnp.sum(dy2_ * u_, axis=0, keepdims=True), jnp.sum(dzg_, axis=0, keepdims=True)

    du_direct, dd, dbglu = _ew("s5_dskip", red_fn, [dy2, u, dzg], [d_skip], [(D, f32)], [D, D])
    ds_re = mm_blockdiag("s5_ds_re", [dy2], [mats["cd_re"]], NT)[0]
    ds_im = mm_blockdiag("s5_ds_im", [dy2], [mats["cd_imn"]], NT)[0]
    d_cd_re = mm_blockdiag_grad("s5_gc_re", s_re, dy2, nb)
    d_cd_imn = mm_blockdiag_grad("s5_gc_im", s_im, dy2, nb)
    sh = (T // SUB, SUB, N)
    q_re, q_im, d_ar, d_ai = s5_scan_bwd(ds_re.reshape(sh), ds_im.reshape(sh), s_re.reshape(sh), s_im.reshape(sh), mats["tab_b"])
    q_re, q_im, d_ar, d_ai = q_re.reshape(T, N), q_im.reshape(T, N), d_ar[0], d_ai[0]
    d_bbd_re = mm_blockdiag_grad("s5_gb_re", u, q_re, nb)
    d_bbd_im = mm_blockdiag_grad("s5_gb_im", u, q_im, nb)
    du = mm_blockdiag("s5_du", [q_re, q_im], [mats["bbd_re"], mats["bbd_im"]], NT, lambda acc, dd_: (acc + dd_,), [du_direct])[0]
    g_win = mm_grad_rows_plain("s5_gwin", h, du)
    tok = send([g_win, g_wglu, g_wout], [0, 1, 2])
    dh = mm_t_rows_plain("s5_dh", du, w_in3, after=tok)[0]
    dx, dx16, dgm = rmsnorm_bwd("s5_dnorm", x, gm, dh, dxo)
    small = dict(d_ar=d_ar, d_ai=d_ai, d_bbr=_diag_extract(d_bbd_re, S5H, S5P), d_bbi=_diag_extract(d_bbd_im, S5H, S5P),
                 d_c_re=jnp.swapaxes(_diag_extract(d_cd_re, S5P, S5H), 1, 2), d_c_im=-jnp.swapaxes(_diag_extract(d_cd_imn, S5P, S5H), 1, 2),
                 d_d=dd, d_bglu=dbglu)
    return dx, dx16, dgm, small


def qk_prepass(qkv, g_q, g_k, nh):
    _, T, W = qkv.shape
    cpd = W // HEAD
    tm = _tile(T, 512, 16)

    def body(x_ref, gq_ref, gk_ref, o_ref):
        j = pl.program_id(0)
        for c in range(cpd):
            which = (j * cpd + c) // nh
            x = x_ref[:, c * HEAD:(c + 1) * HEAD]
            r = lax.rsqrt(jnp.mean(x * x, axis=-1, keepdims=True) + EPS)
            g = jnp.where(which == 0, gq_ref[...], gk_ref[...])
            o_ref[c] = jnp.where(which == 2, x, (x * r) * g).astype(o_ref.dtype)

    vec = pl.BlockSpec((1, HEAD), lambda j, i: (0, 0))
    return pl.pallas_call(
        body, name="sb_qknorm", grid=(NDEV, T // tm),
        in_specs=[pl.BlockSpec((None, tm, W), lambda j, i: (j, i, 0)), vec, vec],
        out_specs=pl.BlockSpec((cpd, tm, HEAD), lambda j, i: (j, i, 0)),
        out_shape=_sds((3 * nh, T, HEAD), bf16), compiler_params=_params(("parallel", "parallel")))(qkv, g_q, g_k)


def qk_prepass_bwd(qkv, g_q, g_k, dqn, dkn, dv, nh):
    _, T, W = qkv.shape
    cpd = W // HEAD
    tm = _tile(T, 1024, 16)

    def body(x_ref, gq_ref, gk_ref, dq_ref, dk_ref, dv_ref, o_ref, dgq_ref, dgk_ref):
        ch = pl.program_id(1)
        which = ch // nh

        @pl.when((pl.program_id(0) == 0) & (ch == 0))
        def _():
            dgq_ref[...] = jnp.zeros_like(dgq_ref)
            dgk_ref[...] = jnp.zeros_like(dgk_ref)

        x = x_ref[...]
        dn = jnp.where(which == 0, dq_ref[...], dk_ref[...])
        g = jnp.where(which == 0, gq_ref[...], gk_ref[...])
        r = lax.rsqrt(jnp.mean(x * x, axis=-1, keepdims=True) + EPS)
        xh = x * r
        dxh = dn * g
        dx = r * (dxh - xh * jnp.mean(dxh * xh, axis=-1, keepdims=True))
        o_ref[...] = jnp.where(which == 2, dv_ref[...], dx).astype(o_ref.dtype)
        dg = jnp.sum(dn * xh, axis=0, keepdims=True)
        dgq_ref[...] += jnp.where(which == 0, dg, 0.0)
        dgk_ref[...] += jnp.where(which == 1, dg, 0.0)

    chunk = pl.BlockSpec((None, tm, HEAD), lambda i, ch: (ch // cpd, i, ch % cpd))
    head = lambda off: pl.BlockSpec((None, tm, HEAD), lambda i, ch: (jnp.clip(ch - off, 0, nh - 1), i, 0))
    vec = pl.BlockSpec((1, HEAD), lambda i, ch: (0, 0))
    return pl.pallas_call(
        body, name="sb_qknorm_bwd", grid=(T // tm, 3 * nh),
        in_specs=[chunk, vec, vec, head(0), head(nh), head(2 * nh)], out_specs=[chunk, vec, vec],
        out_shape=[_sds(qkv.shape, bf16), _sds((1, HEAD), f32), _sds((1, HEAD), f32)],
        compiler_params=_params(("arbitrary", "arbitrary")))(qkv, g_q, g_k, dqn, dkn, dv)


def _split_dot(a, tri):
    hi = a.astype(bf16)
    lo = (a - hi.astype(f32)).astype(bf16)
    return lax.dot_general(hi, tri, NN, preferred_element_type=f32) + lax.dot_general(lo, tri, NN, preferred_element_type=f32)


LOG2E = 1.4426950408889634


def _sb_logits(q, k, scale):
    t = lax.dot_general(q, k, NT, preferred_element_type=f32) * (scale * LOG2E)
    sp = jnp.log2(1.0 + jnp.exp2(-jnp.abs(t)))
    return jnp.minimum(t, 0.0) - sp, -(jnp.maximum(t, 0.0) + sp)


HP = 2
KB = 4


def attn_fwd(qkvn, nh, tq):
    _, T, _ = qkvn.shape
    nq = T // tq
    scale = 1.0 / math.sqrt(HEAD)

    def body(q_ref, k_ref, v_ref, o_ref):
        qi = pl.program_id(1)
        qs = [q_ref[h] for h in range(HP)]
        row = lax.broadcasted_iota(jnp.int32, (tq, tq), 0)
        col = lax.broadcasted_iota(jnp.int32, (tq, tq), 1)
        past = col < row
        tri = (row > col).astype(bf16)

        def blocks(kbs, carry, acc, diag):
            new_c, new_a = [], []
            for h in range(HP):
                parts = []
                for kb in kbs:
                    ks = pl.ds(pl.multiple_of(kb * tq, tq), tq)
                    k, v = k_ref[h, ks, :], v_ref[h, ks, :]
                    lb, lk = _sb_logits(qs[h], k, scale)
                    if diag:
                        lk = jnp.where(past, lk, 0.0)
                    parts.append((lb + _split_dot(lk, tri), jnp.sum(lk, axis=1, keepdims=True), v))
                c, a = carry[h], acc[h]
                for expo, total, v in parts:
                    w = jnp.exp2(expo + c)
                    if diag:
                        w = jnp.where(past, w, 0.0)
                    a = a + lax.dot_general(w.astype(bf16), v, NN, preferred_element_type=f32)
                    c = c + total
                new_a.append(a)
                new_c.append(c)
            return tuple(new_c), tuple(new_a)

        zero_c = tuple(jnp.zeros((tq, 1), f32) for _ in range(HP))
        zero_a = tuple(jnp.zeros((tq, HEAD), f32) for _ in range(HP))
        carry, acc = blocks([qi], zero_c, zero_a, True)
        rest = qi % KB
        carry, acc = lax.fori_loop(0, rest % 2, lambda i, c: blocks([qi - 1], c[0], c[1], False), (carry, acc))
        top = qi - 1 - rest % 2
        carry, acc = lax.fori_loop(0, rest // 2, lambda i, c: blocks([top, top - 1], c[0], c[1], False), (carry, acc))
        top = qi - 1 - rest
        carry, acc = lax.fori_loop(0, qi // KB, lambda i, c: blocks([top - KB * i - j for j in range(KB)], c[0], c[1], False), (carry, acc))
        for h in range(HP):
            o_ref[:, h * HEAD:(h + 1) * HEAD] = acc[h]

    heads = lambda off: pl.BlockSpec((HP, T, HEAD), lambda hh, i: (off // HP + hh, 0, 0))
    return pl.pallas_call(
        body, name="sb_attn", grid=(nh // HP, nq),
        in_specs=[pl.BlockSpec((HP, tq, HEAD), lambda hh, i: (hh, i, 0)), heads(nh), heads(2 * nh)],
        out_specs=pl.BlockSpec((tq, HP * HEAD), lambda hh, i: (i, hh)),
        out_shape=_sds((T, nh * HEAD), f32), compiler_params=_params(("parallel", "arbitrary")))(qkvn, qkvn, qkvn)


def attn_bwd(qkvn, do, nh, tq):
    _, T, _ = qkvn.shape
    nq = T // tq
    scale = 1.0 / math.sqrt(HEAD)

    def body(q_ref, k_ref, v_ref, do_ref, dq_ref, dk_ref, dv_ref, e_scr, sg_scr):
        qi = pl.program_id(1)

        @pl.when(qi == 0)
        def _():
            dk_ref[...] = jnp.zeros_like(dk_ref)
            dv_ref[...] = jnp.zeros_like(dv_ref)

        qs = [q_ref[h] for h in range(HP)]
        dobs = [do_ref[:, h * HEAD:(h + 1) * HEAD].astype(bf16) for h in range(HP)]
        row = lax.broadcasted_iota(jnp.int32, (tq, tq), 0)
        col = lax.broadcasted_iota(jnp.int32, (tq, tq), 1)
        past = col < row
        tri_later = (row > col).astype(bf16)
        tri_before = (row < col).astype(bf16)

        def sweep1(kbs, carry, diag):
            new_c = []
            for h in range(HP):
                parts = []
                for kb in kbs:
                    ks = pl.ds(pl.multiple_of(kb * tq, tq), tq)
                    k, v = k_ref[h, ks, :], v_ref[h, ks, :]
                    lb, lk = _sb_logits(qs[h], k, scale)
                    if diag:
                        lk = jnp.where(past, lk, 0.0)
                    sg_scr[h, kb] = jnp.exp2(lb)
                    dw = lax.dot_general(dobs[h], v, NT, preferred_element_type=f32)
                    parts.append((kb, ks, lb + _split_dot(lk, tri_later), jnp.sum(lk, axis=1, keepdims=True), dw))
                c = carry[h]
                for kb, ks, expo, total, dw in parts:
                    w = jnp.exp2(expo + c)
                    if diag:
                        w = jnp.where(past, w, 0.0)
                    e_scr[h, kb] = dw * w
                    dv_ref[h, ks, :] += lax.dot_general(w.astype(bf16), dobs[h], TNd, preferred_element_type=f32)
                    c = c + total
                new_c.append(c)
            return tuple(new_c)

        zero_c = tuple(jnp.zeros((tq, 1), f32) for _ in range(HP))
        carry = sweep1([qi], zero_c, True)
        rest = qi % KB
        carry = lax.fori_loop(0, rest % 2, lambda i, c: sweep1([qi - 1], c, False), carry)
        top2 = qi - 1 - rest % 2
        carry = lax.fori_loop(0, rest // 2, lambda i, c: sweep1([top2, top2 - 1], c, False), carry)
        top = qi - 1 - rest
        lax.fori_loop(0, qi // KB, lambda i, c: sweep1([top - KB * i - j for j in range(KB)], c, False), carry)

        def sweep2(kbs, carry, dq, diag):
            new_c, new_q = [], []
            for h in range(HP):
                parts = []
                for kb in kbs:
                    e, sg = e_scr[h, kb], sg_scr[h, kb]
                    parts.append((kb, e, sg, _split_dot(e, tri_before), jnp.sum(e, axis=1, keepdims=True)))
                c, dqh = carry[h], dq[h]
                for kb, e, sg, before, total in parts:
                    ks = pl.ds(pl.multiple_of(kb * tq, tq), tq)
                    dz = (e * (1.0 - sg) - (before + c) * sg) * scale
                    if diag:
                        dz = jnp.where(past, dz, 0.0)
                    dzb = dz.astype(bf16)
                    dqh = dqh + lax.dot_general(dzb, k_ref[h, ks, :], NN, preferred_element_type=f32)
                    dk_ref[h, ks, :] += lax.dot_general(dzb, qs[h], TNd, preferred_element_type=f32)
                    c = c + total
                new_q.append(dqh)
                new_c.append(c)
            return tuple(new_c), tuple(new_q)

        zero_q = tuple(jnp.zeros((tq, HEAD), f32) for _ in range(HP))
        carry, dq = lax.fori_loop(0, qi // KB, lambda i, c: sweep2([KB * i + j for j in range(KB)], c[0], c[1], False), (zero_c, zero_q))
        carry, dq = lax.fori_loop(0, rest // 2, lambda i, c: sweep2([qi - rest, qi - rest + 1], c[0], c[1], False), (carry, dq))
        carry, dq = lax.fori_loop(0, rest % 2, lambda i, c: sweep2([qi - 1], c[0], c[1], False), (carry, dq))
        _, dq = sweep2([qi], carry, dq, True)
        for h in range(HP):
            dq_ref[h] = dq[h]

    heads = lambda off: pl.BlockSpec((HP, T, HEAD), lambda hh, i: (off // HP + hh, 0, 0))
    return pl.pallas_call(
        body, name="sb_attn_bwd", grid=(nh // HP, nq),
        in_specs=[pl.BlockSpec((HP, tq, HEAD), lambda hh, i: (hh, i, 0)), heads(nh), heads(2 * nh),
                  pl.BlockSpec((tq, HP * HEAD), lambda hh, i: (i, hh))],
        out_specs=[pl.BlockSpec((HP, tq, HEAD), lambda hh, i: (hh, i, 0)), heads(0), heads(0)],
        out_shape=[_sds((nh, T, HEAD), f32)] * 3,
        scratch_shapes=[pltpu.VMEM((HP, nq, tq, tq), f32), pltpu.VMEM((HP, nq, tq, tq), f32)],
        compiler_params=_params(("parallel", "arbitrary")))(qkvn, qkvn, qkvn, do)


def sb_fwd(x, gm, w_qkv3, w_o3, g_q, g_k):
    T, D = x.shape
    nh = D // HEAD
    h = rmsnorm_fwd("sb_norm", x, gm)
    qkv = mm_cols("sb_qkv", h, [w_qkv3])[0]
    qkvn = qk_prepass(qkv, g_q, g_k, nh)
    o = attn_fwd(qkvn, nh, _tile(T, 256, 128))
    xo = mm_rows_plain("sb_o", o, w_o3, lambda acc, x_: (x_ + acc,), [x])[0]
    return xo, (x, h, qkv, qkvn, o)


def sb_bwd(dxo, dxo16, saved, gm, w_qkv3, w_o3, g_q, g_k, send):
    x, h, qkv, qkvn, o = saved
    T, D = x.shape
    nh = D // HEAD
    do = mm_t_rows_plain("sb_do", dxo16, w_o3)[0]
    g_wo = mm_grad_rows_plain("sb_gwo", o, dxo16)
    dqn, dkn, dv = attn_bwd(qkvn, do, nh, _tile(T, 256, 128))
    dqkv, dgq, dgk = qk_prepass_bwd(qkv, g_q, g_k, dqn, dkn, dv, nh)
    g_wqkv = mm_grad_cols("sb_gwqkv", h, dqkv)
    tok = send([g_wqkv, g_wo], [0, 1])
    dh = mm_t_cols("sb_dh", [dqkv], [w_qkv3], after=tok)[0]
    dx, dx16, dgm = rmsnorm_bwd("sb_dnorm", x, gm, dh, dxo)
    return dx, dx16, dgm, dgq, dgk


def loss_head(y, target):
    D = y.shape[1]

    def fn(y_, t_):
        err = y_ - t_
        dy = err / D
        return dy, dy, jnp.sum(err * err, axis=0, keepdims=True)

    dy, dy16, sq = _ew("loss_head", fn, [y, target], [], [(D, f32), (D, bf16)], [D])
    return 0.5 * (jnp.sum(sq) / D), dy, dy16


FFN = ["w_gate", "w_up", "w_down"]
SMALL = ["norm_ffn1", "norm_mix", "s5_lam_re", "s5_lam_im", "s5_log_dt", "s5_b_re", "s5_b_im", "s5_c_re", "s5_c_im",
         "s5_d", "s5_b_glu", "sb_g_q", "sb_g_k", "norm_ffn2"]
WEIGHTS = ["norm_ffn1", "ffn1_w_gate", "ffn1_w_up", "ffn1_w_down", "norm_mix", "s5_w_in", "s5_lam_re", "s5_lam_im", "s5_log_dt",
           "s5_b_re", "s5_b_im", "s5_c_re", "s5_c_im", "s5_d", "s5_w_glu", "s5_b_glu", "s5_w_out", "sb_w_qkv", "sb_g_q", "sb_g_k",
           "sb_w_o", "norm_ffn2", "ffn2_w_gate", "ffn2_w_up", "ffn2_w_down"]
PACK_LANES = 128
PACK_ROWS = 64


def _pack(arrs):
    flat = jnp.concatenate([a.reshape(-1).astype(f32) for a in arrs])
    pad = (-flat.shape[0]) % (PACK_LANES * PACK_ROWS)
    return jnp.pad(flat, (0, pad)).reshape(-1, PACK_LANES)


def _unpack(buf, like):
    flat = buf.reshape(-1)
    out, off = [], 0
    for a in like:
        out.append(flat[off:off + a.size].reshape(a.shape))
        off += a.size
    return out


def _after(token, g):
    return g + token[0, 0]


TRANSPOSED = ("ffn1_w_gate", "ffn1_w_up", "ffn2_w_gate", "ffn2_w_up")


def _step(w, m, v, x, target):
    x = x[0]
    target = target[0]
    w, m, v = [{n: jnp.swapaxes(a, 1, 2) if n in TRANSPOSED else a for n, a in d.items()} for d in (w, m, v)]

    def shard(n, l):
        return to_bf16("cast%d_%s" % (l, n), w[n], l)

    def gain(n, l):
        return w[n][l:l + 1]

    wf1a, tok = all_gather("ag_l0f1", [shard("ffn1_" + s, 0) for s in FFN])
    h_s5 = exchange_start("ag_s5", [shard("s5_w_in", 0), shard("s5_w_glu", 0), shard("s5_w_out", 0)], True, tok)
    h_f2a = exchange_start("ag_l0f2", [shard("ffn2_" + s, 0) for s in FFN], True, h_s5["token"])
    h_f1b = exchange_start("ag_l1f1", [shard("ffn1_" + s, 1) for s in FFN], True, h_f2a["token"])
    h_sb = exchange_start("ag_sb", [shard("sb_w_qkv", 0), shard("sb_w_o", 0)], True, h_f1b["token"])
    h_f2b = exchange_start("ag_l1f2", [shard("ffn2_" + s, 1) for s in FFN], True, h_sb["token"])
    tok = h_f2b["token"]

    s5_in, mats = s5_prepare(w["s5_lam_re"][0], w["s5_lam_im"][0], w["s5_log_dt"][0], w["s5_b_re"][0], w["s5_b_im"][0],
                             w["s5_c_re"][0], w["s5_c_im"][0])

    x1, sv_f1a = ffn_fwd("l0f1", x, _after(tok, gain("norm_ffn1", 0)), *wf1a)
    ws5 = exchange_wait("agw_s5", h_s5, x1)
    x2, sv_s5 = s5_fwd(x1, gain("norm_mix", 0), *ws5, mats, w["s5_d"], w["s5_b_glu"])
    wf2a = exchange_wait("agw_l0f2", h_f2a, x2)
    x3, sv_f2a = ffn_fwd("l0f2", x2, gain("norm_ffn2", 0), *wf2a)
    wf1b = exchange_wait("agw_l1f1", h_f1b, x3)
    x4, sv_f1b = ffn_fwd("l1f1", x3, gain("norm_ffn1", 1), *wf1b)
    wsb = exchange_wait("agw_sb", h_sb, x4)
    x5, sv_sb = sb_fwd(x4, gain("norm_mix", 1), *wsb, w["sb_g_q"], w["sb_g_k"])
    wf2b = exchange_wait("agw_l1f2", h_f2b, x5)
    x6, sv_f2b = ffn_fwd("l1f2", x5, gain("norm_ffn2", 1), *wf2b)
    loss_local, dy, dy16 = loss_head(x6, target)
    loss = lax.psum(loss_local, AXES)

    rs = {}

    def sender(key):
        def send(grads, places):
            sent = rs.setdefault(key, [])
            sent.append((exchange_start("rs_%s_%d" % (key, len(sent)), grads, False), places))
            return sent[-1][0]["token"]
        return send

    d5, d5h, dn_f2b = ffn_bwd("l1f2b", dy, dy16, sv_f2b, gain("norm_ffn2", 1), *wf2b, sender("l1f2"))
    d4, d4h, dn_sb, dgq, dgk = sb_bwd(d5, d5h, sv_sb, gain("norm_mix", 1), *wsb, w["sb_g_q"], w["sb_g_k"], sender("sb"))
    d3, d3h, dn_f1b = ffn_bwd("l1f1b", d4, d4h, sv_f1b, gain("norm_ffn1", 1), *wf1b, sender("l1f1"))
    d2, d2h, dn_f2a = ffn_bwd("l0f2b", d3, d3h, sv_f2a, gain("norm_ffn2", 0), *wf2a, sender("l0f2"))
    d1, d1h, dn_s5, s5s = s5_bwd(d2, d2h, sv_s5, gain("norm_mix", 0), *ws5, mats, w["s5_d"], sender("s5"))
    d0, _, dn_f1a = ffn_bwd("l0f1b", d1, d1h, sv_f1a, gain("norm_ffn1", 0), *wf1a, sender("l0f1"), each=True)
    d_lr, d_li, d_ldt, d_brt, d_bit = disc_bwd(*s5_in, s5s["d_ar"].reshape(s5_in[0].shape), s5s["d_ai"].reshape(s5_in[0].shape),
                                               s5s["d_bbr"], s5s["d_bbi"])

    part = {
        "norm_ffn1": jnp.concatenate([dn_f1a, dn_f1b]), "norm_mix": jnp.concatenate([dn_s5, dn_sb]),
        "norm_ffn2": jnp.concatenate([dn_f2a, dn_f2b]),
        "s5_lam_re": d_lr, "s5_lam_im": d_li, "s5_log_dt": d_ldt, "s5_b_re": jnp.swapaxes(d_brt, 1, 2), "s5_b_im": jnp.swapaxes(d_bit, 1, 2),
        "s5_c_re": s5s["d_c_re"], "s5_c_im": s5s["d_c_im"], "s5_d": s5s["d_d"], "s5_b_glu": s5s["d_bglu"], "sb_g_q": dgq, "sb_g_k": dgk,
    }
    h_small = exchange_start("ag_small", [_pack([part[n].reshape(w[n].shape) for n in SMALL])], True)

    res = {}
    late = h_small["token"]
    for key, names in (("l1f2", [("ffn2_" + s, 1) for s in FFN]), ("sb", [("sb_w_qkv", None), ("sb_w_o", None)]),
                       ("l1f1", [("ffn1_" + s, 1) for s in FFN]), ("l0f2", [("ffn2_" + s, 0) for s in FFN]),
                       ("s5", [("s5_w_in", None), ("s5_w_glu", None), ("s5_w_out", None)]), ("small", None),
                       ("l0f1", [("ffn1_" + s, 0) for s in FFN])):
        if key == "small":
            packed = exchange_wait("agw_small", h_small, late)[0]
            outs = adamw("adamw_small", _pack([w[n] for n in SMALL]), _pack([m[n] for n in SMALL]), _pack([v[n] for n in SMALL]), packed)
            un = [_unpack(o, [w[n] for n in SMALL]) for o in outs]
            for i, n in enumerate(SMALL):
                res[n] = [un[k][i] for k in range(4)]
            late = outs[0]
            continue
        for k, (handle, places) in enumerate(rs[key]):
            recv = exchange_wait("rsw_%s_%d" % (key, k), handle, late)
            for place, r in zip(places, recv):
                n, l = names[place]
                if l is None:
                    res[n] = [o[None] for o in adamw("adamw_" + n, w[n][0], m[n][0], v[n][0], r)]
                else:
                    res[n] = adamw_layer("adamw%d_%s" % (l, n), w[n], m[n], v[n], r, l, res.get(n))
                late = res[n][0]

    for n in TRANSPOSED:
        res[n] = [jnp.swapaxes(o, 1, 2) for o in res[n]]
    return (loss, d0[None], *[res[n][0] for n in WEIGHTS], *[res[n][1] for n in WEIGHTS],
            *[res[n][2] for n in WEIGHTS], *[res[n][3] for n in WEIGHTS])


def kernel(x, norm_ffn1, ffn1_w_gate, ffn1_w_up, ffn1_w_down, norm_mix, s5_w_in, s5_lam_re, s5_lam_im, s5_log_dt, s5_b_re, s5_b_im, s5_c_re, s5_c_im, s5_d, s5_w_glu, s5_b_glu, s5_w_out, sb_w_qkv, sb_g_q, sb_g_k, sb_w_o, norm_ffn2, ffn2_w_gate, ffn2_w_up, ffn2_w_down, loss_target, m_norm_ffn1, m_ffn1_w_gate, m_ffn1_w_up, m_ffn1_w_down, m_norm_mix, m_s5_w_in, m_s5_lam_re, m_s5_lam_im, m_s5_log_dt, m_s5_b_re, m_s5_b_im, m_s5_c_re, m_s5_c_im, m_s5_d, m_s5_w_glu, m_s5_b_glu, m_s5_w_out, m_sb_w_qkv, m_sb_g_q, m_sb_g_k, m_sb_w_o, m_norm_ffn2, m_ffn2_w_gate, m_ffn2_w_up, m_ffn2_w_down, v_norm_ffn1, v_ffn1_w_gate, v_ffn1_w_up, v_ffn1_w_down, v_norm_mix, v_s5_w_in, v_s5_lam_re, v_s5_lam_im, v_s5_log_dt, v_s5_b_re, v_s5_b_im, v_s5_c_re, v_s5_c_im, v_s5_d, v_s5_w_glu, v_s5_b_glu, v_s5_w_out, v_sb_w_qkv, v_sb_g_q, v_sb_g_k, v_sb_w_o, v_norm_ffn2, v_ffn2_w_gate, v_ffn2_w_up, v_ffn2_w_down):
    w = dict(norm_ffn1=norm_ffn1, ffn1_w_gate=ffn1_w_gate, ffn1_w_up=ffn1_w_up, ffn1_w_down=ffn1_w_down, norm_mix=norm_mix, s5_w_in=s5_w_in, s5_lam_re=s5_lam_re, s5_lam_im=s5_lam_im, s5_log_dt=s5_log_dt, s5_b_re=s5_b_re, s5_b_im=s5_b_im, s5_c_re=s5_c_re, s5_c_im=s5_c_im, s5_d=s5_d, s5_w_glu=s5_w_glu, s5_b_glu=s5_b_glu, s5_w_out=s5_w_out, sb_w_qkv=sb_w_qkv, sb_g_q=sb_g_q, sb_g_k=sb_g_k, sb_w_o=sb_w_o, norm_ffn2=norm_ffn2, ffn2_w_gate=ffn2_w_gate, ffn2_w_up=ffn2_w_up, ffn2_w_down=ffn2_w_down)
    m = dict(norm_ffn1=m_norm_ffn1, ffn1_w_gate=m_ffn1_w_gate, ffn1_w_up=m_ffn1_w_up, ffn1_w_down=m_ffn1_w_down, norm_mix=m_norm_mix, s5_w_in=m_s5_w_in, s5_lam_re=m_s5_lam_re, s5_lam_im=m_s5_lam_im, s5_log_dt=m_s5_log_dt, s5_b_re=m_s5_b_re, s5_b_im=m_s5_b_im, s5_c_re=m_s5_c_re, s5_c_im=m_s5_c_im, s5_d=m_s5_d, s5_w_glu=m_s5_w_glu, s5_b_glu=m_s5_b_glu, s5_w_out=m_s5_w_out, sb_w_qkv=m_sb_w_qkv, sb_g_q=m_sb_g_q, sb_g_k=m_sb_g_k, sb_w_o=m_sb_w_o, norm_ffn2=m_norm_ffn2, ffn2_w_gate=m_ffn2_w_gate, ffn2_w_up=m_ffn2_w_up, ffn2_w_down=m_ffn2_w_down)
    v = dict(norm_ffn1=v_norm_ffn1, ffn1_w_gate=v_ffn1_w_gate, ffn1_w_up=v_ffn1_w_up, ffn1_w_down=v_ffn1_w_down, norm_mix=v_norm_mix, s5_w_in=v_s5_w_in, s5_lam_re=v_s5_lam_re, s5_lam_im=v_s5_lam_im, s5_log_dt=v_s5_log_dt, s5_b_re=v_s5_b_re, s5_b_im=v_s5_b_im, s5_c_re=v_s5_c_re, s5_c_im=v_s5_c_im, s5_d=v_s5_d, s5_w_glu=v_s5_w_glu, s5_b_glu=v_s5_b_glu, s5_w_out=v_s5_w_out, sb_w_qkv=v_sb_w_qkv, sb_g_q=v_sb_g_q, sb_g_k=v_sb_g_k, sb_w_o=v_sb_w_o, norm_ffn2=v_norm_ffn2, ffn2_w_gate=v_ffn2_w_gate, ffn2_w_up=v_ffn2_w_up, ffn2_w_down=v_ffn2_w_down)
    return _step(w, m, v, x, loss_target)
```

```python
import math

import jax
import jax.numpy as jnp
from jax import lax
from jax.experimental import pallas as pl
from jax.experimental.pallas import tpu as pltpu

f32 = jnp.float32
bf16 = jnp.bfloat16

NDEV = 8
AXES = ("x", "y", "c")
MESH = pl.DeviceIdType.MESH
EPS = 1e-6
HEAD = 128
S5H = 16
S5P = 64
GB = 8
FFN_RES = 0.5
ADAM_LR = 0.001
ADAM_B1 = 0.9
ADAM_B2 = 0.999
ADAM_EPS = 1e-08
ADAM_WD = 0.01
ADAM_STEP = 10
VMEM_LIMIT_V7X = 56 * 2 ** 20

NN = (((1,), (0,)), ((), ()))
NT = (((1,), (1,)), ((), ()))
TNd = (((0,), (0,)), ((), ()))


def _tile(n, target, align):
    if n <= target:
        return n
    t = (target // align) * align
    while t >= align:
        if n % t == 0:
            return t
        t -= align
    return n


def _params(sem):
    return pltpu.CompilerParams(dimension_semantics=sem, vmem_limit_bytes=VMEM_LIMIT_V7X)


def _sds(shape, dtype):
    return jax.ShapeDtypeStruct(tuple(shape), dtype)


def _mm(name, a_list, b_list, a_spec, b_spec, dims, grid, out_shapes, out_specs,
        epilogue=None, extra=(), extra_specs=(), nsub=1, merge_b=False, after=None, separate=False):
    n_a, n_p, n_e = len(a_list), len(b_list), len(extra)
    order = [] if after is None else [after]

    def body(*refs):
        a_refs = refs[:n_a]
        b_refs = refs[n_a:n_a + n_p]
        e_refs = refs[n_a + n_p:n_a + n_p + n_e]
        o_refs = refs[n_a + n_p + n_e + len(order):]
        prods = []
        for p, br in enumerate(b_refs):
            ar = a_refs[p if n_a > 1 else 0]
            if nsub > 1 and not merge_b:
                for j in range(nsub):
                    prods.append(lax.dot_general(ar[j].astype(bf16), br[j].astype(bf16), dims, preferred_element_type=f32))
            else:
                b = br[...]
                if merge_b:
                    b = b.reshape(b.shape[0] * b.shape[1], b.shape[2])
                prods.append(lax.dot_general(ar[...].astype(bf16), b.astype(bf16), dims, preferred_element_type=f32))
        if not separate:
            s = prods[0]
            for d in prods[1:]:
                s = s + d
            prods = [s]
        outs = tuple(prods) if epilogue is None else epilogue(*prods, *[e[...] for e in e_refs])
        for o, val in zip(o_refs, outs):
            o[...] = val.astype(o.dtype)

    return pl.pallas_call(
        body, name=name, grid=grid,
        in_specs=[a_spec] * n_a + [b_spec] * n_p + list(extra_specs) + [pl.BlockSpec(memory_space=pl.ANY)] * len(order),
        out_specs=list(out_specs), out_shape=list(out_shapes),
        compiler_params=_params(("parallel",) * len(grid)),
    )(*a_list, *b_list, *extra, *order)


EW_TILE_ELEMS = 1 << 19
ADAM_TILE_ELEMS = 3 << 20


def _ew(name, fn, rows, bcasts, outs, reds=(), tm=None, budget=EW_TILE_ELEMS):
    n_r, n_b, n_o, n_d = len(rows), len(bcasts), len(outs), len(reds)
    R = rows[0].shape[-2]
    if tm is None:
        widest = max([r.shape[-1] * (r.shape[0] if r.ndim == 3 else 1) for r in rows] + [c for c, _ in outs])
        tm = _tile(R, max(16, budget // widest), 16)

    def body(*refs):
        r = refs[:n_r]
        b = refs[n_r:n_r + n_b]
        o = refs[n_r + n_b:n_r + n_b + n_o]
        d = refs[n_r + n_b + n_o:]
        res = fn(*[x[...] for x in r], *[x[...] for x in b])
        for oo, val in zip(o, res[:n_o]):
            oo[...] = val.astype(oo.dtype)
        if n_d:
            @pl.when(pl.program_id(0) == 0)
            def _():
                for dd in d:
                    dd[...] = jnp.zeros_like(dd)
            for dd, val in zip(d, res[n_o:]):
                dd[...] += val

    in_specs = []
    for x in rows:
        if x.ndim == 2:
            in_specs.append(pl.BlockSpec((tm, x.shape[1]), lambda i: (i, 0)))
        else:
            in_specs.append(pl.BlockSpec((x.shape[0], tm, x.shape[2]), lambda i: (0, i, 0)))
    for x in bcasts:
        in_specs.append(pl.BlockSpec(x.shape, lambda i, nd=x.ndim: (0,) * nd))
    out_shape = [_sds((R, c), dt) for c, dt in outs] + [_sds((1, c), f32) for c in reds]
    out_specs = [pl.BlockSpec((tm, c), lambda i: (i, 0)) for c, _ in outs] + [pl.BlockSpec((1, c), lambda i: (0, 0)) for c in reds]
    return pl.pallas_call(
        body, name=name, grid=(R // tm,), in_specs=in_specs, out_specs=out_specs, out_shape=out_shape,
        compiler_params=_params(("arbitrary",)),
    )(*rows, *bcasts)


def _coords():
    return lax.axis_index("x"), lax.axis_index("y"), lax.axis_index("c")


def _me():
    x, y, c = _coords()
    return 4 * x + 2 * y + c


def _peer(k):
    x, y, c = _coords()
    return x ^ ((k >> 2) & 1), y ^ ((k >> 1) & 1), c ^ (k & 1)


def all_gather(name, shards):
    n = len(shards)

    def body(*refs):
        x_refs, out_refs, token = refs[:n], refs[n:2 * n], refs[2 * n]
        send_sems, recv_sems, local_sems = refs[2 * n + 1:]
        token[...] = jnp.zeros_like(token)
        x, y, c = _coords()
        me, sibling = (x, y, c), (x, y, 1 - c)
        chips = [(1 - x, y), (x, 1 - y), (1 - x, 1 - y)]

        def rows(a, px, py, pc):
            return out_refs[a].at[4 * px + 2 * py + pc]

        def copy(a, k, block, to, src=None):
            return pltpu.make_async_remote_copy(
                src_ref=rows(a, *block) if src is None else src, dst_ref=rows(a, *block),
                send_sem=send_sems.at[a, k], recv_sem=recv_sems.at[a, k],
                device_id=to, device_id_type=MESH)

        mine = [pltpu.make_async_copy(x_refs[a], rows(a, *me), local_sems.at[a]) for a in range(n)]
        for cp in mine:
            cp.start()
        first = []
        for a in range(n):
            first.append(copy(a, 0, me, sibling, src=x_refs[a]))
            first += [copy(a, 1 + j, me, (*chip, c), src=x_refs[a]) for j, chip in enumerate(chips)]
        for cp in first:
            cp.start()
        passed = []
        for j, chip in enumerate(chips):
            for a in range(n):
                copy(a, 1 + j, (*chip, c), me).wait_recv()
                cp = copy(a, 4 + j, (*chip, c), sibling)
                cp.start()
                passed.append(cp)
        for a in range(n):
            copy(a, 0, sibling, me).wait_recv()
            for j, chip in enumerate(chips):
                copy(a, 4 + j, (*chip, 1 - c), me).wait_recv()
        for cp in first + passed:
            cp.wait_send()
        for cp in mine:
            cp.wait()

    anyspec = pl.BlockSpec(memory_space=pl.ANY)
    outs = pl.pallas_call(
        body, name=name,
        out_shape=[_sds((NDEV,) + s.shape, s.dtype) for s in shards] + [_sds((8, 128), f32)],
        in_specs=[anyspec] * n, out_specs=[anyspec] * n + [pl.BlockSpec(memory_space=pltpu.VMEM)],
        scratch_shapes=[pltpu.SemaphoreType.DMA((n, 7)), pltpu.SemaphoreType.DMA((n, 7)), pltpu.SemaphoreType.DMA((n,))],
    )(*shards)
    return list(outs[:n]), outs[n]


def to_bf16(name, w3, l):
    _, R, C = w3.shape
    tm = _tile(R, max(16, 2 * EW_TILE_ELEMS // C), 16)

    def body(x_ref, o_ref):
        o_ref[...] = x_ref[...].astype(bf16)

    return pl.pallas_call(
        body, name=name, grid=(R // tm,), in_specs=[pl.BlockSpec((None, tm, C), lambda i: (l, i, 0))],
        out_specs=pl.BlockSpec((tm, C), lambda i: (i, 0)), out_shape=_sds((R, C), bf16),
        compiler_params=_params(("parallel",)))(w3)


HBM_SPEC = pl.BlockSpec(memory_space=pltpu.HBM)
SEM_SPEC = pl.BlockSpec(memory_space=pltpu.SEMAPHORE)
EFFECT = pltpu.SideEffectType.DATAFLOW_SIDE_EFFECTING


def exchange_start(name, srcs, gather, after=None):
    n = len(srcs)
    order = [] if after is None else [after]
    ns = n * (NDEV - 1)
    n_sem = 2 * ns + n
    lands = [lax.empty(((NDEV,) + s.shape) if gather else s.shape, s.dtype) for s in srcs]

    def body(*refs):
        src_refs, land_refs = refs[:n], refs[n:2 * n]
        first = 2 * n + len(order)
        send_sems, recv_sems = refs[first:first + ns], refs[first + ns:first + 2 * ns]
        local_sems = refs[first + 2 * ns:first + n_sem]
        token = refs[-1]
        me_ = _me()
        for k in range(1, NDEV):
            px, py, pc = _peer(k)
            p = 4 * px + 2 * py + pc
            for a in range(n):
                i = a * (NDEV - 1) + k - 1
                pltpu.make_async_remote_copy(
                    src_ref=src_refs[a] if gather else src_refs[a].at[p], dst_ref=land_refs[a].at[me_],
                    send_sem=send_sems[i], recv_sem=recv_sems[i],
                    device_id=(px, py, pc), device_id_type=MESH).start()
        for a in range(n):
            pltpu.make_async_copy(src_refs[a] if gather else src_refs[a].at[me_], land_refs[a].at[me_], local_sems[a]).start()
        token[...] = jnp.zeros_like(token)

    outs = pl.pallas_call(
        body, name=name,
        out_shape=(*[pltpu.SemaphoreType.DMA(())] * n_sem,
                   *[pltpu.HBM(s.shape, s.dtype) for s in srcs], *[pltpu.HBM(l.shape, l.dtype) for l in lands],
                   _sds((8, 128), f32)),
        in_specs=[HBM_SPEC] * (2 * n) + [pl.BlockSpec(memory_space=pl.ANY)] * len(order),
        out_specs=(*[SEM_SPEC] * n_sem, *[HBM_SPEC] * (2 * n), pl.BlockSpec(memory_space=pltpu.VMEM)),
        input_output_aliases={i: n_sem + i for i in range(2 * n)},
        compiler_params=pltpu.CompilerParams(has_side_effects=EFFECT),
    )(*[pltpu.with_memory_space_constraint(s, pltpu.HBM) for s in srcs],
      *[pltpu.with_memory_space_constraint(l, pltpu.HBM) for l in lands], *order)
    return dict(n=n, sems=outs[:n_sem], srcs=outs[n_sem:n_sem + n], lands=outs[n_sem + n:n_sem + 2 * n], token=outs[-1], gather=gather)


def exchange_wait(name, hd, after):
    n, gather = hd["n"], hd["gather"]
    ns = n * (NDEV - 1)

    def body(*refs):
        src_refs, land_refs = refs[:n], refs[n:2 * n]
        send_sems, recv_sems = refs[2 * n:2 * n + ns], refs[2 * n + ns:2 * n + 2 * ns]
        local_sems = refs[2 * n + 2 * ns:2 * n + 2 * ns + n]
        x, y, c = _coords()
        me_ = _me()
        for a in range(n):
            pltpu.make_async_copy(src_refs[a] if gather else src_refs[a].at[me_], land_refs[a].at[me_], local_sems[a]).wait()
        for k in range(1, NDEV):
            px, py, pc = _peer(k)
            p = 4 * px + 2 * py + pc
            for a in range(n):
                i = a * (NDEV - 1) + k - 1
                cp = pltpu.make_async_remote_copy(
                    src_ref=src_refs[a] if gather else src_refs[a].at[p], dst_ref=land_refs[a].at[p],
                    send_sem=send_sems[i], recv_sem=recv_sems[i],
                    device_id=(x, y, 1 - c), device_id_type=MESH)
                cp.wait_send()
                cp.wait_recv()

    outs = pl.pallas_call(
        body, name=name,
        out_shape=tuple(pltpu.HBM(s.shape, s.dtype) for s in list(hd["srcs"]) + list(hd["lands"])),
        in_specs=[HBM_SPEC] * (2 * n) + [SEM_SPEC] * (2 * ns + n) + [pl.BlockSpec(memory_space=pl.ANY)],
        out_specs=tuple([HBM_SPEC] * (2 * n)),
        input_output_aliases={i: i for i in range(2 * n)},
        compiler_params=pltpu.CompilerParams(has_side_effects=EFFECT),
    )(*hd["srcs"], *hd["lands"], *hd["sems"], after)
    return list(outs[n:])


def _adam_math(w_, m_, v_, r_):
    g = r_[0].astype(f32)
    for j in range(1, NDEV):
        g = g + r_[j].astype(f32)
    m2 = ADAM_B1 * m_ + (1.0 - ADAM_B1) * g
    v2 = ADAM_B2 * v_ + (1.0 - ADAM_B2) * jnp.square(g)
    m_hat = m2 / (1.0 - ADAM_B1 ** ADAM_STEP)
    v_hat = v2 / (1.0 - ADAM_B2 ** ADAM_STEP)
    delta = -ADAM_LR * (m_hat / (jnp.sqrt(v_hat) + ADAM_EPS) + ADAM_WD * w_)
    return g, delta, m2, v2


def adamw(name, w, m, v, recv):
    return _ew(name, _adam_math, [w, m, v, recv], [], [(w.shape[1], f32)] * 4, budget=ADAM_TILE_ELEMS)


def adamw_layer(name, w3, m3, v3, recv, l, prev):
    L, R, C = w3.shape
    tm = _tile(R, max(16, ADAM_TILE_ELEMS // (NDEV * C)), 16)
    n_prev = 0 if prev is None else 4

    def body(*refs):
        w_ref, m_ref, v_ref, r_ref = refs[:4]
        o_refs = refs[4 + n_prev:]
        res = _adam_math(w_ref[...], m_ref[...], v_ref[...], r_ref[...])
        for o, val in zip(o_refs, res):
            o[...] = val

    lay = pl.BlockSpec((None, tm, C), lambda i: (l, i, 0))
    return pl.pallas_call(
        body, name=name, grid=(R // tm,),
        in_specs=[lay, lay, lay, pl.BlockSpec((NDEV, tm, C), lambda i: (0, i, 0))] + [pl.BlockSpec(memory_space=pl.ANY)] * n_prev,
        out_specs=[lay] * 4, out_shape=[_sds((L, R, C), f32)] * 4,
        input_output_aliases={4 + i: i for i in range(n_prev)},
        compiler_params=_params(("parallel",)),
    )(w3, m3, v3, recv, *(prev or []))


def rmsnorm_fwd(name, x, g):
    D = x.shape[1]

    def fn(x_, g_):
        r = lax.rsqrt(jnp.mean(x_ * x_, axis=-1, keepdims=True) + EPS)
        return ((x_ * r) * g_,)

    return _ew(name, fn, [x], [g], [(D, bf16)])[0]


def rmsnorm_bwd(name, x, g, dh, dres):
    D = x.shape[1]

    def fn(x_, dh_, dres_, g_):
        r = lax.rsqrt(jnp.mean(x_ * x_, axis=-1, keepdims=True) + EPS)
        xh = x_ * r
        dxh = dh_ * g_
        dx = dres_ + r * (dxh - xh * jnp.mean(dxh * xh, axis=-1, keepdims=True))
        return dx, dx, jnp.sum(dh_ * xh, axis=0, keepdims=True)

    return _ew(name, fn, [x, dh, dres], [g], [(D, f32), (D, bf16)], [D])


def _silu_parts(a):
    sig = jax.nn.sigmoid(a)
    return sig, a * sig


def mm_cols(name, h, w3_list, epilogue=None, extra=(), outs=None):
    T, K = h.shape
    Nb = w3_list[0].shape[2]
    tm = _tile(T, 1024, 16)
    outs = outs or [f32]
    blk = pl.BlockSpec((None, tm, Nb), lambda j, i: (j, i, 0))
    return _mm(name, [h], w3_list, pl.BlockSpec((tm, K), lambda j, i: (i, 0)), pl.BlockSpec((None, K, Nb), lambda j, i: (j, 0, 0)),
               NN, (NDEV, T // tm), [_sds((NDEV, T, Nb), dt) for dt in outs], [blk] * len(outs), epilogue, extra, [blk] * len(extra),
               separate=True)


def mm_rows(name, p_list, w3_list, epilogue=None, extra=(), outs=None, after=None):
    _, T, Kb = p_list[0].shape
    N = w3_list[0].shape[2]
    tm, tn = _tile(T, 512, 16), _tile(N, 512 // len(p_list), 128)
    outs = outs or [f32]
    blk = pl.BlockSpec((tm, tn), lambda i, n: (i, n))
    return _mm(name, p_list, w3_list, pl.BlockSpec((NDEV, tm, Kb), lambda i, n: (0, i, 0)), pl.BlockSpec((NDEV, Kb, tn), lambda i, n: (0, 0, n)),
               NN, (T // tm, N // tn), [_sds((T, N), dt) for dt in outs], [blk] * len(outs), epilogue, extra, [blk] * len(extra), nsub=NDEV,
               after=after)


def mm_rows_plain(name, a, w3, epilogue=None, extra=(), extra_specs=None, outs=None):
    T, K = a.shape
    Kb, N = w3.shape[1], w3.shape[2]
    tm, tn = _tile(T, 512, 16), _tile(N, 1024, 128)
    outs = outs or [f32]
    blk = pl.BlockSpec((tm, tn), lambda i, n: (i, n))
    return _mm(name, [a], [w3], pl.BlockSpec((tm, K), lambda i, n: (i, 0)), pl.BlockSpec((NDEV, Kb, tn), lambda i, n: (0, 0, n)),
               NN, (T // tm, N // tn), [_sds((T, N), dt) for dt in outs], [blk] * len(outs), epilogue, extra,
               extra_specs or [blk] * len(extra), nsub=NDEV, merge_b=True)


def mm_t_rows(name, d, w3_list, epilogue=None, extra=(), outs=None, rows=512):
    T, K = d.shape
    Nb = w3_list[0].shape[1]
    tm = _tile(T, rows, 16)
    outs = outs or [f32]
    blk = pl.BlockSpec((None, tm, Nb), lambda j, i: (j, i, 0))
    return _mm(name, [d], w3_list, pl.BlockSpec((tm, K), lambda j, i: (i, 0)), pl.BlockSpec((None, Nb, K), lambda j, i: (j, 0, 0)),
               NT, (NDEV, T // tm), [_sds((NDEV, T, Nb), dt) for dt in outs], [blk] * len(outs), epilogue, extra, [blk] * len(extra),
               separate=True)


def mm_t_rows_plain(name, d, w3, epilogue=None, extra=(), outs=None, after=None):
    T, K = d.shape
    Nb = w3.shape[1]
    tm = _tile(T, 256, 16)
    outs = outs or [f32]
    blk = pl.BlockSpec((tm, NDEV * Nb), lambda i: (i, 0))
    return _mm(name, [d], [w3], pl.BlockSpec((tm, K), lambda i: (i, 0)), pl.BlockSpec((NDEV, Nb, K), lambda i: (0, 0, 0)),
               NT, (T // tm,), [_sds((T, NDEV * Nb), dt) for dt in outs], [blk] * len(outs), epilogue, extra, [blk] * len(extra),
               nsub=NDEV, merge_b=True, after=after)


def mm_t_cols(name, d_list, w3_list, epilogue=None, extra=(), outs=None, after=None):
    _, T, Kb = d_list[0].shape
    N = w3_list[0].shape[1]
    tm, tn = _tile(T, 512, 16), _tile(N, 256, 128)
    outs = outs or [f32]
    blk = pl.BlockSpec((tm, tn), lambda i, n: (i, n))
    return _mm(name, d_list, w3_list, pl.BlockSpec((NDEV, tm, Kb), lambda i, n: (0, i, 0)), pl.BlockSpec((NDEV, tn, Kb), lambda i, n: (0, n, 0)),
               NT, (T // tm, N // tn), [_sds((T, N), dt) for dt in outs], [blk] * len(outs), epilogue, extra, [blk] * len(extra), nsub=NDEV,
               after=after)


def mm_grad_rows(name, p, d, scale=1.0):
    _, T, Mb = p.shape
    N = d.shape[1]
    tn = _tile(N, 1024, 128)
    return _mm(name, [p], [d], pl.BlockSpec((None, T, Mb), lambda j, n: (j, 0, 0)), pl.BlockSpec((T, tn), lambda j, n: (0, n)),
               TNd, (NDEV, N // tn), [_sds((NDEV, Mb, N), bf16)], [pl.BlockSpec((None, Mb, tn), lambda j, n: (j, 0, n))],
               (lambda acc: (acc * scale,)))[0]


def mm_grad_rows_plain(name, a, d):
    T, M = a.shape
    N = d.shape[1]
    tm, tn = _tile(M, 512, 128), _tile(N, 512, 128)
    g = _mm(name, [a], [d], pl.BlockSpec((T, tm), lambda m, n: (0, m)), pl.BlockSpec((T, tn), lambda m, n: (0, n)),
            TNd, (M // tm, N // tn), [_sds((M, N), bf16)], [pl.BlockSpec((tm, tn), lambda m, n: (m, n))])[0]
    return g.reshape(NDEV, M // NDEV, N)


def mm_grad_cols(name, h, d):
    T, M = h.shape
    Nb = d.shape[2]
    tm = _tile(M, 512, 128)
    return _mm(name, [h], [d], pl.BlockSpec((T, tm), lambda j, m: (0, m)), pl.BlockSpec((None, T, Nb), lambda j, m: (j, 0, 0)),
               TNd, (NDEV, M // tm), [_sds((NDEV, M, Nb), bf16)], [pl.BlockSpec((None, tm, Nb), lambda j, m: (j, m, 0))])[0]


def ffn_fwd(tag, x, g, wg3, wu3, wd3):
    h = rmsnorm_fwd(tag + "_norm", x, g)

    def gate_up_epi(a_, b_):
        _, sl = _silu_parts(a_)
        return a_, b_, sl * b_

    a, b, p = mm_t_rows(tag + "_gateup", h, [wg3, wu3], gate_up_epi, outs=[bf16, bf16, bf16], rows=1024)
    xo = mm_rows(tag + "_down", [p], [wd3], lambda acc, x_: (x_ + FFN_RES * acc,), [x])[0]
    return xo, (x, h, a, b, p)


def ffn_bwd(tag, dxo, dxo16, saved, g, wg3, wu3, wd3, send, each=False):
    x, h, a, b, p = saved

    def dp_epi(acc, a_, b_):
        dp = FFN_RES * acc
        a_, b_ = a_.astype(f32), b_.astype(f32)
        sig, sl = _silu_parts(a_)
        return dp * b_ * (sig * (1.0 + a_ * (1.0 - sig))), dp * sl

    da, db = mm_t_rows(tag + "_dp", dxo16, [wd3], dp_epi, [a, b], [bf16, bf16], rows=1024)
    g_wd = mm_grad_rows(tag + "_gwd", p, dxo16, FFN_RES)
    if each:
        send([g_wd], [2])
    g_wg = mm_grad_rows(tag + "_gwg", da, h)
    if each:
        send([g_wg], [0])
    g_wu = mm_grad_rows(tag + "_gwu", db, h)
    tok = send([g_wu], [1]) if each else send([g_wg, g_wu, g_wd], [0, 1, 2])
    dh = mm_rows(tag + "_dh", [da, db], [wg3, wu3], after=tok)[0]
    return rmsnorm_bwd(tag + "_dnorm", x, g, dh, dxo)


def _disc(lr, li, ldt, brt, bit):
    dt = jnp.exp(ldt)
    lr = jnp.minimum(lr, -1e-4)
    mag = jnp.exp(lr * dt)
    ab_re = mag * jnp.cos(li * dt)
    ab_im = mag * jnp.sin(li * dt)
    den = lr * lr + li * li
    n_re = ab_re - 1.0
    f_re = (n_re * lr + ab_im * li) / den
    f_im = (ab_im * lr - n_re * li) / den
    bb_re = f_re[:, None, :] * brt - f_im[:, None, :] * bit
    bb_im = f_re[:, None, :] * bit + f_im[:, None, :] * brt
    return ab_re, ab_im, bb_re, bb_im


def disc_fwd(lr, li, ldt, brt, bit):
    def body(lr_ref, li_ref, ldt_ref, brt_ref, bit_ref, ar_ref, ai_ref, bbr_ref, bbi_ref):
        ar, ai, bbr, bbi = _disc(lr_ref[...], li_ref[...], ldt_ref[...], brt_ref[...], bit_ref[...])
        ar_ref[...] = ar
        ai_ref[...] = ai
        bbr_ref[...] = bbr
        bbi_ref[...] = bbi

    return pl.pallas_call(body, name="s5_disc", out_shape=[_sds(lr.shape, f32), _sds(lr.shape, f32), _sds(brt.shape, f32), _sds(brt.shape, f32)],
                          compiler_params=pltpu.CompilerParams(vmem_limit_bytes=VMEM_LIMIT_V7X))(lr, li, ldt, brt, bit)


def disc_bwd(lr, li, ldt, brt, bit, d_ar, d_ai, d_bbr, d_bbi):
    def body(lr_ref, li_ref, ldt_ref, brt_ref, bit_ref, dar_ref, dai_ref, dbbr_ref, dbbi_ref, o_lr, o_li, o_ldt, o_brt, o_bit):
        _, vjp = jax.vjp(_disc, lr_ref[...], li_ref[...], ldt_ref[...], brt_ref[...], bit_ref[...])
        g = vjp((dar_ref[...], dai_ref[...], dbbr_ref[...], dbbi_ref[...]))
        o_lr[...] = g[0]
        o_li[...] = g[1]
        o_ldt[...] = g[2]
        o_brt[...] = g[3]
        o_bit[...] = g[4]

    return pl.pallas_call(body, name="s5_disc_bwd",
                          out_shape=[_sds(lr.shape, f32), _sds(lr.shape, f32), _sds(ldt.shape, f32), _sds(brt.shape, f32), _sds(brt.shape, f32)],
                          compiler_params=pltpu.CompilerParams(vmem_limit_bytes=VMEM_LIMIT_V7X))(lr, li, ldt, brt, bit, d_ar, d_ai, d_bbr, d_bbi)


def _blockdiag(m):
    G, R, C = m.shape
    eye = jnp.eye(GB, dtype=m.dtype)
    m5 = m.reshape(G // GB, GB, R, 1, C) * eye[None, :, None, :, None]
    return m5.reshape(G // GB, GB * R, GB * C)


def _diag_extract(M, R, C):
    nb = M.shape[0]
    eye = jnp.eye(GB, dtype=M.dtype)
    m5 = M.reshape(nb, GB, R, GB, C) * eye[None, :, None, :, None]
    return m5.sum(axis=3).reshape(nb * GB, R, C)


def mm_blockdiag(name, a_list, bd_list, dims, epilogue=None, extra=(), extra_bcast=(), outs=None):
    T = a_list[0].shape[0]
    nb, r, c = bd_list[0].shape
    ra = a_list[0].shape[1] // nb
    ca = c if dims == NN else r
    n_p, n_e, n_b = len(a_list), len(extra), len(extra_bcast)
    outs = outs or [f32]
    tm = _tile(T, 128, 16)

    def body(*refs):
        a_refs, b_refs = refs[:n_p], refs[n_p:2 * n_p]
        e_refs = refs[2 * n_p:2 * n_p + n_e + n_b]
        o_refs = refs[2 * n_p + n_e + n_b:]
        for b in range(nb):
            s = None
            for ar, br in zip(a_refs, b_refs):
                d = lax.dot_general(ar[:, b * ra:(b + 1) * ra].astype(bf16), br[b], dims, preferred_element_type=f32)
                s = d if s is None else s + d
            cols = slice(b * ca, (b + 1) * ca)
            vals = (s,) if epilogue is None else epilogue(s, *[e[:, cols] for e in e_refs])
            for o, val in zip(o_refs, vals):
                o[:, cols] = val.astype(o.dtype)

    row = lambda w: pl.BlockSpec((tm, w), lambda i: (i, 0))
    return pl.pallas_call(
        body, name=name, grid=(T // tm,),
        in_specs=[row(nb * ra)] * n_p + [pl.BlockSpec((nb, r, c), lambda i: (0, 0, 0))] * n_p + [row(nb * ca)] * n_e
        + [pl.BlockSpec((1, nb * ca), lambda i: (0, 0))] * n_b,
        out_specs=[row(nb * ca)] * len(outs), out_shape=[_sds((T, nb * ca), dt) for dt in outs],
        compiler_params=_params(("parallel",)),
    )(*a_list, *bd_list, *extra, *extra_bcast)


def mm_blockdiag_grad(name, a, d, nb):
    T = a.shape[0]
    ra, rd = a.shape[1] // nb, d.shape[1] // nb
    tk = _tile(T, 256, 16)

    def body(a_ref, d_ref, o_ref):
        @pl.when(pl.program_id(0) == 0)
        def _():
            o_ref[...] = jnp.zeros_like(o_ref)

        for b in range(nb):
            o_ref[b] += lax.dot_general(a_ref[:, b * ra:(b + 1) * ra].astype(bf16), d_ref[:, b * rd:(b + 1) * rd].astype(bf16),
                                        TNd, preferred_element_type=f32)

    return pl.pallas_call(
        body, name=name, grid=(T // tk,),
        in_specs=[pl.BlockSpec((tk, nb * ra), lambda k: (k, 0)), pl.BlockSpec((tk, nb * rd), lambda k: (k, 0))],
        out_specs=pl.BlockSpec((nb, ra, rd), lambda k: (0, 0, 0)), out_shape=_sds((nb, ra, rd), f32),
        compiler_params=_params(("arbitrary",)),
    )(a, d)


SUB = 8
SCAN_LANES = 1024
SCAN_GROUPS = 64


def s5_tables(ar, ai):
    N = ar.shape[1]

    def body(ar_ref, ai_ref, f_ref, b_ref):
        r, i = ar_ref[...], ai_ref[...]
        pw = [(r, i)]
        for _ in range(SUB - 1):
            pr, pi = pw[-1]
            pw.append((pr * r - pi * i, pr * i + pi * r))
        row = lax.broadcasted_iota(jnp.int32, (SUB, N), 0)
        for t, k in enumerate((1, 2, 4)):
            pr, pi = pw[k - 1]
            f_ref[2 * t] = jnp.where(row >= k, pr, 0.0)
            f_ref[2 * t + 1] = jnp.where(row >= k, pi, 0.0)
            b_ref[2 * t] = jnp.where(row <= SUB - 1 - k, pr, 0.0)
            b_ref[2 * t + 1] = jnp.where(row <= SUB - 1 - k, -pi, 0.0)
        fr = fi = br = bi = jnp.zeros((SUB, N), f32)
        for j in range(SUB):
            fr, fi = jnp.where(row == j, pw[j][0], fr), jnp.where(row == j, pw[j][1], fi)
            br, bi = jnp.where(row == j, pw[SUB - 1 - j][0], br), jnp.where(row == j, -pw[SUB - 1 - j][1], bi)
        f_ref[6], f_ref[7] = fr, fi
        b_ref[6], b_ref[7] = br, bi

    return pl.pallas_call(body, name="s5_tables", out_shape=[_sds((8, SUB, N), f32)] * 2,
                          compiler_params=pltpu.CompilerParams(vmem_limit_bytes=VMEM_LIMIT_V7X))(ar, ai)


def _group_scan(xr, xi, m, cr, ci, up):
    for t, k in enumerate((1, 2, 4)):
        shift = SUB - k if up else k
        pr, pi = pltpu.roll(xr, shift, 0), pltpu.roll(xi, shift, 0)
        xr, xi = xr + m[2 * t] * pr - m[2 * t + 1] * pi, xi + m[2 * t] * pi + m[2 * t + 1] * pr
    return xr + m[6] * cr - m[7] * ci, xi + m[6] * ci + m[7] * cr


def s5_scan_fwd(bu_re, bu_im, tab):
    T8, _, N = bu_re.shape
    W, tb = _tile(N, SCAN_LANES, 128), _tile(T8, SCAN_GROUPS, 1)

    def body(bur_ref, bui_ref, tab_ref, sr_ref, si_ref, st):
        @pl.when(pl.program_id(1) == 0)
        def _():
            st[...] = jnp.zeros_like(st)

        def step(g, c):
            m = [tab_ref[t] for t in range(8)]
            xr, xi = _group_scan(bur_ref[g], bui_ref[g], m, c[0], c[1], False)
            sr_ref[g] = xr
            si_ref[g] = xi
            return jnp.broadcast_to(xr[SUB - 1:SUB], (SUB, W)), jnp.broadcast_to(xi[SUB - 1:SUB], (SUB, W))

        cr, ci = lax.fori_loop(0, tb, step, (st[0], st[1]))
        st[0] = cr
        st[1] = ci

    blk = pl.BlockSpec((tb, SUB, W), lambda j, i: (i, 0, j))
    tabs = pl.BlockSpec((8, SUB, W), lambda j, i: (0, 0, j))
    return pl.pallas_call(body, name="s5_scan", grid=(N // W, T8 // tb), in_specs=[blk, blk, tabs], out_specs=[blk, blk],
                          out_shape=[_sds(bu_re.shape, f32)] * 2, scratch_shapes=[pltpu.VMEM((2, SUB, W), f32)],
                          compiler_params=_params(("parallel", "arbitrary")))(bu_re, bu_im, tab)


def s5_scan_bwd(ds_re, ds_im, s_re, s_im, tab):
    T8, _, N = ds_re.shape
    W, tb = _tile(N, SCAN_LANES, 128), _tile(T8, SCAN_GROUPS, 1)
    nblk = T8 // tb

    def body(dsr_ref, dsi_ref, sr_ref, si_ref, tab_ref, qr_ref, qi_ref, dar_ref, dai_ref, st):
        @pl.when(pl.program_id(1) == 0)
        def _():
            st[...] = jnp.zeros_like(st)
            dar_ref[...] = jnp.zeros_like(dar_ref)
            dai_ref[...] = jnp.zeros_like(dai_ref)

        last_row = lax.broadcasted_iota(jnp.int32, (SUB, W), 0) == SUB - 1

        def step(i, c):
            cr, ci, dar, dai = c
            g = tb - 1 - i
            m = [tab_ref[t] for t in range(8)]
            xr, xi = _group_scan(dsr_ref[g], dsi_ref[g], m, cr, ci, True)
            qr_ref[g] = xr
            qi_ref[g] = xi
            nr = jnp.where(last_row, cr, pltpu.roll(xr, SUB - 1, 0))
            ni = jnp.where(last_row, ci, pltpu.roll(xi, SUB - 1, 0))
            sr, si = sr_ref[g], si_ref[g]
            dar = dar + nr * sr + ni * si
            dai = dai + ni * sr - nr * si
            return jnp.broadcast_to(xr[0:1], (SUB, W)), jnp.broadcast_to(xi[0:1], (SUB, W)), dar, dai

        cr, ci, dar, dai = lax.fori_loop(0, tb, step, (st[0], st[1], dar_ref[...], dai_ref[...]))
        st[0] = cr
        st[1] = ci
        dar_ref[...] = dar
        dai_ref[...] = dai

        @pl.when(pl.program_id(1) == nblk - 1)
        def _():
            dar_ref[...] = jnp.broadcast_to(jnp.sum(dar, axis=0, keepdims=True), (SUB, W))
            dai_ref[...] = jnp.broadcast_to(jnp.sum(dai, axis=0, keepdims=True), (SUB, W))

    blk = pl.BlockSpec((tb, SUB, W), lambda j, i: (nblk - 1 - i, 0, j))
    tabs = pl.BlockSpec((8, SUB, W), lambda j, i: (0, 0, j))
    acc = pl.BlockSpec((SUB, W), lambda j, i: (0, j))
    return pl.pallas_call(body, name="s5_scan_bwd", grid=(N // W, nblk), in_specs=[blk, blk, blk, blk, tabs], out_specs=[blk, blk, acc, acc],
                          out_shape=[_sds(ds_re.shape, f32)] * 2 + [_sds((SUB, N), f32)] * 2, scratch_shapes=[pltpu.VMEM((2, SUB, W), f32)],
                          compiler_params=_params(("parallel", "arbitrary")))(ds_re, ds_im, s_re, s_im, tab)


def _gelu_grad(y):
    c = math.sqrt(2.0 / math.pi)
    th = jnp.tanh(c * (y + 0.044715 * (y * y * y)))
    return 0.5 * (1.0 + th) + 0.5 * y * (1.0 - th * th) * c * (1.0 + 3.0 * 0.044715 * (y * y))


def s5_prepare(lam_re, lam_im, log_dt, b_re, b_im, c_re, c_im):
    G = lam_re.shape[0]
    ldt = log_dt.reshape(G, 1)
    brt, bit = jnp.swapaxes(b_re, 1, 2), jnp.swapaxes(b_im, 1, 2)
    ar, ai, bbr, bbi = disc_fwd(lam_re, lam_im, ldt, brt, bit)
    tab_f, tab_b = s5_tables(ar.reshape(1, -1), ai.reshape(1, -1))
    mats = dict(
        bbd_re=_blockdiag(bbr).astype(bf16), bbd_im=_blockdiag(bbi).astype(bf16),
        cd_re=_blockdiag(jnp.swapaxes(c_re, 1, 2)).astype(bf16),
        cd_imn=_blockdiag(-jnp.swapaxes(c_im, 1, 2)).astype(bf16),
        tab_f=tab_f, tab_b=tab_b)
    return (lam_re, lam_im, ldt, brt, bit), mats


def s5_fwd(x, gm, w_in3, w_glu3, w_out3, mats, d_skip, b_glu):
    T, D = x.shape
    N = mats["tab_f"].shape[2]
    h = rmsnorm_fwd("s5_norm", x, gm)
    u = mm_rows_plain("s5_in", h, w_in3)[0]
    bu_re = mm_blockdiag("s5_bu_re", [u], [mats["bbd_re"]], NN)[0]
    bu_im = mm_blockdiag("s5_bu_im", [u], [mats["bbd_im"]], NN)[0]
    grouped = (T // SUB, SUB, N)
    s_re, s_im = s5_scan_fwd(bu_re.reshape(grouped), bu_im.reshape(grouped), mats["tab_f"])
    s_re, s_im = s_re.reshape(T, N), s_im.reshape(T, N)

    def y_epi(acc, u_, d_):
        y2 = acc + d_ * u_
        return y2, jax.nn.gelu(y2)

    y2, gl = mm_blockdiag("s5_y", [s_re, s_im], [mats["cd_re"], mats["cd_imn"]], NN, y_epi, [u], [d_skip], [f32, bf16])

    def glu_epi(acc, y2_, b_):
        zg = acc + b_
        return zg, jax.nn.gelu(y2_) * jax.nn.sigmoid(zg)

    tm, tn = _tile(T, 512, 16), _tile(D, 1024, 128)
    zg, o = mm_rows_plain("s5_glu", gl, w_glu3, glu_epi, [y2, b_glu],
                          [pl.BlockSpec((tm, tn), lambda i, n: (i, n)), pl.BlockSpec((1, tn), lambda i, n: (0, n))], [f32, bf16])
    xo = mm_rows_plain("s5_out", o, w_out3, lambda acc, x_: (x_ + acc,), [x])[0]
    return xo, (x, h, u, s_re, s_im, y2, gl, zg, o)


def s5_bwd(dxo, dxo16, saved, gm, w_in3, w_glu3, w_out3, mats, d_skip, send):
    x, h, u, s_re, s_im, y2, gl, zg, o = saved
    T, D = x.shape
    N = mats["tab_b"].shape[2]
    nb = mats["cd_re"].shape[0]

    def do_epi(acc, y2_, zg_):
        sg = jax.nn.sigmoid(zg_)
        return acc * sg, acc * jax.nn.gelu(y2_) * (sg * (1.0 - sg))

    dgl_direct, dzg = mm_t_rows_plain("s5_do", dxo16, w_out3, do_epi, [y2, zg], [f32, f32])
    g_wout = mm_grad_rows_plain("s5_gwout", o, dxo16)
    g_wglu = mm_grad_rows_plain("s5_gwglu", gl, dzg)
    dy2 = mm_t_rows_plain("s5_dgl", dzg, w_glu3, lambda acc, dd_, y2_: ((acc + dd_) * _gelu_grad(y2_),), [dgl_direct, y2])[0]

    def red_fn(dy2_, u_, dzg_, d_):
        return dy2_ * d_, jnp.sum(dy2_ * u_, axis=0, keepdims=True), jnp.sum(dzg_, axis=0, keepdims=True)

    du_direct, dd, dbglu = _ew("s5_dskip", red_fn, [dy2, u, dzg], [d_skip], [(D, f32)], [D, D])
    ds_re = mm_blockdiag("s5_ds_re", [dy2], [mats["cd_re"]], NT)[0]
    ds_im = mm_blockdiag("s5_ds_im", [dy2], [mats["cd_imn"]], NT)[0]
    d_cd_re = mm_blockdiag_grad("s5_gc_re", s_re, dy2, nb)
    d_cd_imn = mm_blockdiag_grad("s5_gc_im", s_im, dy2, nb)
    sh = (T // SUB, SUB, N)
    q_re, q_im, d_ar, d_ai = s5_scan_bwd(ds_re.reshape(sh), ds_im.reshape(sh), s_re.reshape(sh), s_im.reshape(sh), mats["tab_b"])
    q_re, q_im, d_ar, d_ai = q_re.reshape(T, N), q_im.reshape(T, N), d_ar[0], d_ai[0]
    d_bbd_re = mm_blockdiag_grad("s5_gb_re", u, q_re, nb)
    d_bbd_im = mm_blockdiag_grad("s5_gb_im", u, q_im, nb)
    du = mm_blockdiag("s5_du", [q_re, q_im], [mats["bbd_re"], mats["bbd_im"]], NT, lambda acc, dd_: (acc + dd_,), [du_direct])[0]
    g_win = mm_grad_rows_plain("s5_gwin", h, du)
    tok = send([g_win, g_wglu, g_wout], [0, 1, 2])
    dh = mm_t_rows_plain("s5_dh", du, w_in3, after=tok)[0]
    dx, dx16, dgm = rmsnorm_bwd("s5_dnorm", x, gm, dh, dxo)
    small = dict(d_ar=d_ar, d_ai=d_ai, d_bbr=_diag_extract(d_bbd_re, S5H, S5P), d_bbi=_diag_extract(d_bbd_im, S5H, S5P),
                 d_c_re=jnp.swapaxes(_diag_extract(d_cd_re, S5P, S5H), 1, 2), d_c_im=-jnp.swapaxes(_diag_extract(d_cd_imn, S5P, S5H), 1, 2),
                 d_d=dd, d_bglu=dbglu)
    return dx, dx16, dgm, small


def qk_prepass(qkv, g_q, g_k, nh):
    _, T, W = qkv.shape
    cpd = W // HEAD
    tm = _tile(T, 512, 16)

    def body(x_ref, gq_ref, gk_ref, o_ref):
        j = pl.program_id(0)
        for c in range(cpd):
            which = (j * cpd + c) // nh
            x = x_ref[:, c * HEAD:(c + 1) * HEAD]
            r = lax.rsqrt(jnp.mean(x * x, axis=-1, keepdims=True) + EPS)
            g = jnp.where(which == 0, gq_ref[...], gk_ref[...])
            o_ref[c] = jnp.where(which == 2, x, (x * r) * g).astype(o_ref.dtype)

    vec = pl.BlockSpec((1, HEAD), lambda j, i: (0, 0))
    return pl.pallas_call(
        body, name="sb_qknorm", grid=(NDEV, T // tm),
        in_specs=[pl.BlockSpec((None, tm, W), lambda j, i: (j, i, 0)), vec, vec],
        out_specs=pl.BlockSpec((cpd, tm, HEAD), lambda j, i: (j, i, 0)),
        out_shape=_sds((3 * nh, T, HEAD), bf16), compiler_params=_params(("parallel", "parallel")))(qkv, g_q, g_k)


def qk_prepass_bwd(qkv, g_q, g_k, dqn, dkn, dv, nh):
    _, T, W = qkv.shape
    cpd = W // HEAD
    tm = _tile(T, 1024, 16)

    def body(x_ref, gq_ref, gk_ref, dq_ref, dk_ref, dv_ref, o_ref, dgq_ref, dgk_ref):
        ch = pl.program_id(1)
        which = ch // nh

        @pl.when((pl.program_id(0) == 0) & (ch == 0))
        def _():
            dgq_ref[...] = jnp.zeros_like(dgq_ref)
            dgk_ref[...] = jnp.zeros_like(dgk_ref)

        x = x_ref[...]
        dn = jnp.where(which == 0, dq_ref[...], dk_ref[...])
        g = jnp.where(which == 0, gq_ref[...], gk_ref[...])
        r = lax.rsqrt(jnp.mean(x * x, axis=-1, keepdims=True) + EPS)
        xh = x * r
        dxh = dn * g
        dx = r * (dxh - xh * jnp.mean(dxh * xh, axis=-1, keepdims=True))
        o_ref[...] = jnp.where(which == 2, dv_ref[...], dx).astype(o_ref.dtype)
        dg = jnp.sum(dn * xh, axis=0, keepdims=True)
        dgq_ref[...] += jnp.where(which == 0, dg, 0.0)
        dgk_ref[...] += jnp.where(which == 1, dg, 0.0)

    chunk = pl.BlockSpec((None, tm, HEAD), lambda i, ch: (ch // cpd, i, ch % cpd))
    head = lambda off: pl.BlockSpec((None, tm, HEAD), lambda i, ch: (jnp.clip(ch - off, 0, nh - 1), i, 0))
    vec = pl.BlockSpec((1, HEAD), lambda i, ch: (0, 0))
    return pl.pallas_call(
        body, name="sb_qknorm_bwd", grid=(T // tm, 3 * nh),
        in_specs=[chunk, vec, vec, head(0), head(nh), head(2 * nh)], out_specs=[chunk, vec, vec],
        out_shape=[_sds(qkv.shape, bf16), _sds((1, HEAD), f32), _sds((1, HEAD), f32)],
        compiler_params=_params(("arbitrary", "arbitrary")))(qkv, g_q, g_k, dqn, dkn, dv)


def _split_dot(a, tri):
    hi = a.astype(bf16)
    lo = (a - hi.astype(f32)).astype(bf16)
    return lax.dot_general(hi, tri, NN, preferred_element_type=f32) + lax.dot_general(lo, tri, NN, preferred_element_type=f32)


LOG2E = 1.4426950408889634


def _sb_logits(q, k, scale):
    t = lax.dot_general(q, k, NT, preferred_element_type=f32) * (scale * LOG2E)
    sp = jnp.log2(1.0 + jnp.exp2(-jnp.abs(t)))
    return jnp.minimum(t, 0.0) - sp, -(jnp.maximum(t, 0.0) + sp)


HP = 2
KB = 4


def attn_fwd(qkvn, nh, tq):
    _, T, _ = qkvn.shape
    nq = T // tq
    scale = 1.0 / math.sqrt(HEAD)

    def body(q_ref, k_ref, v_ref, o_ref):
        qi = pl.program_id(1)
        qs = [q_ref[h] for h in range(HP)]
        row = lax.broadcasted_iota(jnp.int32, (tq, tq), 0)
        col = lax.broadcasted_iota(jnp.int32, (tq, tq), 1)
        past = col < row
        tri = (row > col).astype(bf16)

        def blocks(kbs, carry, acc, diag):
            new_c, new_a = [], []
            for h in range(HP):
                parts = []
                for kb in kbs:
                    ks = pl.ds(pl.multiple_of(kb * tq, tq), tq)
                    k, v = k_ref[h, ks, :], v_ref[h, ks, :]
                    lb, lk = _sb_logits(qs[h], k, scale)
                    if diag:
                        lk = jnp.where(past, lk, 0.0)
                    parts.append((lb + _split_dot(lk, tri), jnp.sum(lk, axis=1, keepdims=True), v))
                c, a = carry[h], acc[h]
                for expo, total, v in parts:
                    w = jnp.exp2(expo + c)
                    if diag:
                        w = jnp.where(past, w, 0.0)
                    a = a + lax.dot_general(w.astype(bf16), v, NN, preferred_element_type=f32)
                    c = c + total
                new_a.append(a)
                new_c.append(c)
            return tuple(new_c), tuple(new_a)

        zero_c = tuple(jnp.zeros((tq, 1), f32) for _ in range(HP))
        zero_a = tuple(jnp.zeros((tq, HEAD), f32) for _ in range(HP))
        carry, acc = blocks([qi], zero_c, zero_a, True)
        rest = qi % KB
        carry, acc = lax.fori_loop(0, rest % 2, lambda i, c: blocks([qi - 1], c[0], c[1], False), (carry, acc))
        top = qi - 1 - rest % 2
        carry, acc = lax.fori_loop(0, rest // 2, lambda i, c: blocks([top, top - 1], c[0], c[1], False), (carry, acc))
        top = qi - 1 - rest
        carry, acc = lax.fori_loop(0, qi // KB, lambda i, c: blocks([top - KB * i - j for j in range(KB)], c[0], c[1], False), (carry, acc))
        for h in range(HP):
            o_ref[:, h * HEAD:(h + 1) * HEAD] = acc[h]

    heads = lambda off: pl.BlockSpec((HP, T, HEAD), lambda hh, i: (off // HP + hh, 0, 0))
    return pl.pallas_call(
        body, name="sb_attn", grid=(nh // HP, nq),
        in_specs=[pl.BlockSpec((HP, tq, HEAD), lambda hh, i: (hh, i, 0)), heads(nh), heads(2 * nh)],
        out_specs=pl.BlockSpec((tq, HP * HEAD), lambda hh, i: (i, hh)),
        out_shape=_sds((T, nh * HEAD), f32), compiler_params=_params(("parallel", "arbitrary")))(qkvn, qkvn, qkvn)


def attn_bwd(qkvn, do, nh, tq):
    _, T, _ = qkvn.shape
    nq = T // tq
    scale = 1.0 / math.sqrt(HEAD)

    def body(q_ref, k_ref, v_ref, do_ref, dq_ref, dk_ref, dv_ref, e_scr, sg_scr):
        qi = pl.program_id(1)

        @pl.when(qi == 0)
        def _():
            dk_ref[...] = jnp.zeros_like(dk_ref)
            dv_ref[...] = jnp.zeros_like(dv_ref)

        qs = [q_ref[h] for h in range(HP)]
        dobs = [do_ref[:, h * HEAD:(h + 1) * HEAD].astype(bf16) for h in range(HP)]
        row = lax.broadcasted_iota(jnp.int32, (tq, tq), 0)
        col = lax.broadcasted_iota(jnp.int32, (tq, tq), 1)
        past = col < row
        tri_later = (row > col).astype(bf16)
        tri_before = (row < col).astype(bf16)

        def sweep1(kbs, carry, diag):
            new_c = []
            for h in range(HP):
                parts = []
                for kb in kbs:
                    ks = pl.ds(pl.multiple_of(kb * tq, tq), tq)
                    k, v = k_ref[h, ks, :], v_ref[h, ks, :]
                    lb, lk = _sb_logits(qs[h], k, scale)
                    if diag:
                        lk = jnp.where(past, lk, 0.0)
                    sg_scr[h, kb] = jnp.exp2(lb)
                    dw = lax.dot_general(dobs[h], v, NT, preferred_element_type=f32)
                    parts.append((kb, ks, lb + _split_dot(lk, tri_later), jnp.sum(lk, axis=1, keepdims=True), dw))
                c = carry[h]
                for kb, ks, expo, total, dw in parts:
                    w = jnp.exp2(expo + c)
                    if diag:
                        w = jnp.where(past, w, 0.0)
                    e_scr[h, kb] = dw * w
                    dv_ref[h, ks, :] += lax.dot_general(w.astype(bf16), dobs[h], TNd, preferred_element_type=f32)
                    c = c + total
                new_c.append(c)
            return tuple(new_c)

        zero_c = tuple(jnp.zeros((tq, 1), f32) for _ in range(HP))
        carry = sweep1([qi], zero_c, True)
        rest = qi % KB
        carry = lax.fori_loop(0, rest % 2, lambda i, c: sweep1([qi - 1], c, False), carry)
        top2 = qi - 1 - rest % 2
        carry = lax.fori_loop(0, rest // 2, lambda i, c: sweep1([top2, top2 - 1], c, False), carry)
        top = qi - 1 - rest
        lax.fori_loop(0, qi // KB, lambda i, c: sweep1([top - KB * i - j for j in range(KB)], c, False), carry)

        def sweep2(kbs, carry, dq, diag):
            new_c, new_q = [], []
            for h in range(HP):
                parts = []
                for kb in kbs:
                    e, sg = e_scr[h, kb], sg_scr[h, kb]
                    parts.append((kb, e, sg, _split_dot(e, tri_before), jnp.sum(e, axis=1, keepdims=True)))
                c, dqh = carry[h], dq[h]
                for kb, e, sg, before, total in parts:
                    ks = pl.ds(pl.multiple_of(kb * tq, tq), tq)
                    dz = (e * (1.0 - sg) - (before + c) * sg) * scale
                    if diag:
                        dz = jnp.where(past, dz, 0.0)
                    dzb = dz.astype(bf16)
                    dqh = dqh + lax.dot_general(dzb, k_ref[h, ks, :], NN, preferred_element_type=f32)
                    dk_ref[h, ks, :] += lax.dot_general(dzb, qs[h], TNd, preferred_element_type=f32)
                    c = c + total
                new_q.append(dqh)
                new_c.append(c)
            return tuple(new_c), tuple(new_q)

        zero_q = tuple(jnp.zeros((tq, HEAD), f32) for _ in range(HP))
        carry, dq = lax.fori_loop(0, qi // KB, lambda i, c: sweep2([KB * i + j for j in range(KB)], c[0], c[1], False), (zero_c, zero_q))
        carry, dq = lax.fori_loop(0, rest // 2, lambda i, c: sweep2([qi - rest, qi - rest + 1], c[0], c[1], False), (carry, dq))
        carry, dq = lax.fori_loop(0, rest % 2, lambda i, c: sweep2([qi - 1], c[0], c[1], False), (carry, dq))
        _, dq = sweep2([qi], carry, dq, True)
        for h in range(HP):
            dq_ref[h] = dq[h]

    heads = lambda off: pl.BlockSpec((HP, T, HEAD), lambda hh, i: (off // HP + hh, 0, 0))
    return pl.pallas_call(
        body, name="sb_attn_bwd", grid=(nh // HP, nq),
        in_specs=[pl.BlockSpec((HP, tq, HEAD), lambda hh, i: (hh, i, 0)), heads(nh), heads(2 * nh),
                  pl.BlockSpec((tq, HP * HEAD), lambda hh, i: (i, hh))],
        out_specs=[pl.BlockSpec((HP, tq, HEAD), lambda hh, i: (hh, i, 0)), heads(0), heads(0)],
        out_shape=[_sds((nh, T, HEAD), f32)] * 3,
        scratch_shapes=[pltpu.VMEM((HP, nq, tq, tq), f32), pltpu.VMEM((HP, nq, tq, tq), f32)],
        compiler_params=_params(("parallel", "arbitrary")))(qkvn, qkvn, qkvn, do)


def sb_fwd(x, gm, w_qkv3, w_o3, g_q, g_k):
    T, D = x.shape
    nh = D // HEAD
    h = rmsnorm_fwd("sb_norm", x, gm)
    qkv = mm_cols("sb_qkv", h, [w_qkv3])[0]
    qkvn = qk_prepass(qkv, g_q, g_k, nh)
    o = attn_fwd(qkvn, nh, _tile(T, 256, 128))
    xo = mm_rows_plain("sb_o", o, w_o3, lambda acc, x_: (x_ + acc,), [x])[0]
    return xo, (x, h, qkv, qkvn, o)


def sb_bwd(dxo, dxo16, saved, gm, w_qkv3, w_o3, g_q, g_k, send):
    x, h, qkv, qkvn, o = saved
    T, D = x.shape
    nh = D // HEAD
    do = mm_t_rows_plain("sb_do", dxo16, w_o3)[0]
    g_wo = mm_grad_rows_plain("sb_gwo", o, dxo16)
    dqn, dkn, dv = attn_bwd(qkvn, do, nh, _tile(T, 256, 128))
    dqkv, dgq, dgk = qk_prepass_bwd(qkv, g_q, g_k, dqn, dkn, dv, nh)
    g_wqkv = mm_grad_cols("sb_gwqkv", h, dqkv)
    tok = send([g_wqkv, g_wo], [0, 1])
    dh = mm_t_cols("sb_dh", [dqkv], [w_qkv3], after=tok)[0]
    dx, dx16, dgm = rmsnorm_bwd("sb_dnorm", x, gm, dh, dxo)
    return dx, dx16, dgm, dgq, dgk


def loss_head(y, target):
    D = y.shape[1]

    def fn(y_, t_):
        err = y_ - t_
        dy = err / D
        return dy, dy, jnp.sum(err * err, axis=0, keepdims=True)

    dy, dy16, sq = _ew("loss_head", fn, [y, target], [], [(D, f32), (D, bf16)], [D])
    return 0.5 * (jnp.sum(sq) / D), dy, dy16


FFN = ["w_gate", "w_up", "w_down"]
SMALL = ["norm_ffn1", "norm_mix", "s5_lam_re", "s5_lam_im", "s5_log_dt", "s5_b_re", "s5_b_im", "s5_c_re", "s5_c_im",
         "s5_d", "s5_b_glu", "sb_g_q", "sb_g_k", "norm_ffn2"]
WEIGHTS = ["norm_ffn1", "ffn1_w_gate", "ffn1_w_up", "ffn1_w_down", "norm_mix", "s5_w_in", "s5_lam_re", "s5_lam_im", "s5_log_dt",
           "s5_b_re", "s5_b_im", "s5_c_re", "s5_c_im", "s5_d", "s5_w_glu", "s5_b_glu", "s5_w_out", "sb_w_qkv", "sb_g_q", "sb_g_k",
           "sb_w_o", "norm_ffn2", "ffn2_w_gate", "ffn2_w_up", "ffn2_w_down"]
PACK_LANES = 128
PACK_ROWS = 64


def _pack(arrs):
    flat = jnp.concatenate([a.reshape(-1).astype(f32) for a in arrs])
    pad = (-flat.shape[0]) % (PACK_LANES * PACK_ROWS)
    return jnp.pad(flat, (0, pad)).reshape(-1, PACK_LANES)


def _unpack(buf, like):
    flat = buf.reshape(-1)
    out, off = [], 0
    for a in like:
        out.append(flat[off:off + a.size].reshape(a.shape))
        off += a.size
    return out


def _after(token, g):
    return g + token[0, 0]


TRANSPOSED = ("ffn1_w_gate", "ffn1_w_up", "ffn2_w_gate", "ffn2_w_up")


def _step(w, m, v, x, target):
    x = x[0]
    target = target[0]
    w, m, v = [{n: jnp.swapaxes(a, 1, 2) if n in TRANSPOSED else a for n, a in d.items()} for d in (w, m, v)]

    def shard(n, l):
        return to_bf16("cast%d_%s" % (l, n), w[n], l)

    def gain(n, l):
        return w[n][l:l + 1]

    wf1a, tok = all_gather("ag_l0f1", [shard("ffn1_" + s, 0) for s in FFN])
    h_s5 = exchange_start("ag_s5", [shard("s5_w_in", 0), shard("s5_w_glu", 0), shard("s5_w_out", 0)], True, tok)
    h_f2a = exchange_start("ag_l0f2", [shard("ffn2_" + s, 0) for s in FFN], True, h_s5["token"])
    h_f1b = exchange_start("ag_l1f1", [shard("ffn1_" + s, 1) for s in FFN], True, h_f2a["token"])
    h_sb = exchange_start("ag_sb", [shard("sb_w_qkv", 0), shard("sb_w_o", 0)], True, h_f1b["token"])
    h_f2b = exchange_start("ag_l1f2", [shard("ffn2_" + s, 1) for s in FFN], True, h_sb["token"])
    tok = h_f2b["token"]

    s5_in, mats = s5_prepare(w["s5_lam_re"][0], w["s5_lam_im"][0], w["s5_log_dt"][0], w["s5_b_re"][0], w["s5_b_im"][0],
                             w["s5_c_re"][0], w["s5_c_im"][0])

    x1, sv_f1a = ffn_fwd("l0f1", x, _after(tok, gain("norm_ffn1", 0)), *wf1a)
    ws5 = exchange_wait("agw_s5", h_s5, x1)
    x2, sv_s5 = s5_fwd(x1, gain("norm_mix", 0), *ws5, mats, w["s5_d"], w["s5_b_glu"])
    wf2a = exchange_wait("agw_l0f2", h_f2a, x2)
    x3, sv_f2a = ffn_fwd("l0f2", x2, gain("norm_ffn2", 0), *wf2a)
    wf1b = exchange_wait("agw_l1f1", h_f1b, x3)
    x4, sv_f1b = ffn_fwd("l1f1", x3, gain("norm_ffn1", 1), *wf1b)
    wsb = exchange_wait("agw_sb", h_sb, x4)
    x5, sv_sb = sb_fwd(x4, gain("norm_mix", 1), *wsb, w["sb_g_q"], w["sb_g_k"])
    wf2b = exchange_wait("agw_l1f2", h_f2b, x5)
    x6, sv_f2b = ffn_fwd("l1f2", x5, gain("norm_ffn2", 1), *wf2b)
    loss_local, dy, dy16 = loss_head(x6, target)
    loss = lax.psum(loss_local, AXES)

    rs = {}

    def sender(key):
        def send(grads, places):
            sent = rs.setdefault(key, [])
            sent.append((exchange_start("rs_%s_%d" % (key, len(sent)), grads, False), places))
            return sent[-1][0]["token"]
        return send

    d5, d5h, dn_f2b = ffn_bwd("l1f2b", dy, dy16, sv_f2b, gain("norm_ffn2", 1), *wf2b, sender("l1f2"))
    d4, d4h, dn_sb, dgq, dgk = sb_bwd(d5, d5h, sv_sb, gain("norm_mix", 1), *wsb, w["sb_g_q"], w["sb_g_k"], sender("sb"))
    d3, d3h, dn_f1b = ffn_bwd("l1f1b", d4, d4h, sv_f1b, gain("norm_ffn1", 1), *wf1b, sender("l1f1"))
    d2, d2h, dn_f2a = ffn_bwd("l0f2b", d3, d3h, sv_f2a, gain("norm_ffn2", 0), *wf2a, sender("l0f2"))
    d1, d1h, dn_s5, s5s = s5_bwd(d2, d2h, sv_s5, gain("norm_mix", 0), *ws5, mats, w["s5_d"], sender("s5"))
    d0, _, dn_f1a = ffn_bwd("l0f1b", d1, d1h, sv_f1a, gain("norm_ffn1", 0), *wf1a, sender("l0f1"), each=True)
    d_lr, d_li, d_ldt, d_brt, d_bit = disc_bwd(*s5_in, s5s["d_ar"].reshape(s5_in[0].shape), s5s["d_ai"].reshape(s5_in[0].shape),
                                               s5s["d_bbr"], s5s["d_bbi"])

    part = {
        "norm_ffn1": jnp.concatenate([dn_f1a, dn_f1b]), "norm_mix": jnp.concatenate([dn_s5, dn_sb]),
        "norm_ffn2": jnp.concatenate([dn_f2a, dn_f2b]),
        "s5_lam_re": d_lr, "s5_lam_im": d_li, "s5_log_dt": d_ldt, "s5_b_re": jnp.swapaxes(d_brt, 1, 2), "s5_b_im": jnp.swapaxes(d_bit, 1, 2),
        "s5_c_re": s5s["d_c_re"], "s5_c_im": s5s["d_c_im"], "s5_d": s5s["d_d"], "s5_b_glu": s5s["d_bglu"], "sb_g_q": dgq, "sb_g_k": dgk,
    }
    h_small = exchange_start("ag_small", [_pack([part[n].reshape(w[n].shape) for n in SMALL])], True)

    res = {}
    late = h_small["token"]
    for key, names in (("l1f2", [("ffn2_" + s, 1) for s in FFN]), ("sb", [("sb_w_qkv", None), ("sb_w_o", None)]),
                       ("l1f1", [("ffn1_" + s, 1) for s in FFN]), ("l0f2", [("ffn2_" + s, 0) for s in FFN]),
                       ("s5", [("s5_w_in", None), ("s5_w_glu", None), ("s5_w_out", None)]), ("small", None),
                       ("l0f1", [("ffn1_" + s, 0) for s in FFN])):
        if key == "small":
            packed = exchange_wait("agw_small", h_small, late)[0]
            outs = adamw("adamw_small", _pack([w[n] for n in SMALL]), _pack([m[n] for n in SMALL]), _pack([v[n] for n in SMALL]), packed)
            un = [_unpack(o, [w[n] for n in SMALL]) for o in outs]
            for i, n in enumerate(SMALL):
                res[n] = [un[k][i] for k in range(4)]
            late = outs[0]
            continue
        for k, (handle, places) in enumerate(rs[key]):
            recv = exchange_wait("rsw_%s_%d" % (key, k), handle, late)
            for place, r in zip(places, recv):
                n, l = names[place]
                if l is None:
                    res[n] = [o[None] for o in adamw("adamw_" + n, w[n][0], m[n][0], v[n][0], r)]
                else:
                    res[n] = adamw_layer("adamw%d_%s" % (l, n), w[n], m[n], v[n], r, l, res.get(n))
                late = res[n][0]

    for n in TRANSPOSED:
        res[n] = [jnp.swapaxes(o, 1, 2) for o in res[n]]
    return (loss, d0[None], *[res[n][0] for n in WEIGHTS], *[res[n][1] for n in WEIGHTS],
            *[res[n][2] for n in WEIGHTS], *[res[n][3] for n in WEIGHTS])


def kernel(x, norm_ffn1, ffn1_w_gate, ffn1_w_up, ffn1_w_down, norm_mix, s5_w_in, s5_lam_re, s5_lam_im, s5_log_dt, s5_b_re, s5_b_im, s5_c_re, s5_c_im, s5_d, s5_w_glu, s5_b_glu, s5_w_out, sb_w_qkv, sb_g_q, sb_g_k, sb_w_o, norm_ffn2, ffn2_w_gate, ffn2_w_up, ffn2_w_down, loss_target, m_norm_ffn1, m_ffn1_w_gate, m_ffn1_w_up, m_ffn1_w_down, m_norm_mix, m_s5_w_in, m_s5_lam_re, m_s5_lam_im, m_s5_log_dt, m_s5_b_re, m_s5_b_im, m_s5_c_re, m_s5_c_im, m_s5_d, m_s5_w_glu, m_s5_b_glu, m_s5_w_out, m_sb_w_qkv, m_sb_g_q, m_sb_g_k, m_sb_w_o, m_norm_ffn2, m_ffn2_w_gate, m_ffn2_w_up, m_ffn2_w_down, v_norm_ffn1, v_ffn1_w_gate, v_ffn1_w_up, v_ffn1_w_down, v_norm_mix, v_s5_w_in, v_s5_lam_re, v_s5_lam_im, v_s5_log_dt, v_s5_b_re, v_s5_b_im, v_s5_c_re, v_s5_c_im, v_s5_d, v_s5_w_glu, v_s5_b_glu, v_s5_w_out, v_sb_w_qkv, v_sb_g_q, v_sb_g_k, v_sb_w_o, v_norm_ffn2, v_ffn2_w_gate, v_ffn2_w_up, v_ffn2_w_down):
    w = dict(norm_ffn1=norm_ffn1, ffn1_w_gate=ffn1_w_gate, ffn1_w_up=ffn1_w_up, ffn1_w_down=ffn1_w_down, norm_mix=norm_mix, s5_w_in=s5_w_in, s5_lam_re=s5_lam_re, s5_lam_im=s5_lam_im, s5_log_dt=s5_log_dt, s5_b_re=s5_b_re, s5_b_im=s5_b_im, s5_c_re=s5_c_re, s5_c_im=s5_c_im, s5_d=s5_d, s5_w_glu=s5_w_glu, s5_b_glu=s5_b_glu, s5_w_out=s5_w_out, sb_w_qkv=sb_w_qkv, sb_g_q=sb_g_q, sb_g_k=sb_g_k, sb_w_o=sb_w_o, norm_ffn2=norm_ffn2, ffn2_w_gate=ffn2_w_gate, ffn2_w_up=ffn2_w_up, ffn2_w_down=ffn2_w_down)
    m = dict(norm_ffn1=m_norm_ffn1, ffn1_w_gate=m_ffn1_w_gate, ffn1_w_up=m_ffn1_w_up, ffn1_w_down=m_ffn1_w_down, norm_mix=m_norm_mix, s5_w_in=m_s5_w_in, s5_lam_re=m_s5_lam_re, s5_lam_im=m_s5_lam_im, s5_log_dt=m_s5_log_dt, s5_b_re=m_s5_b_re, s5_b_im=m_s5_b_im, s5_c_re=m_s5_c_re, s5_c_im=m_s5_c_im, s5_d=m_s5_d, s5_w_glu=m_s5_w_glu, s5_b_glu=m_s5_b_glu, s5_w_out=m_s5_w_out, sb_w_qkv=m_sb_w_qkv, sb_g_q=m_sb_g_q, sb_g_k=m_sb_g_k, sb_w_o=m_sb_w_o, norm_ffn2=m_norm_ffn2, ffn2_w_gate=m_ffn2_w_gate, ffn2_w_up=m_ffn2_w_up, ffn2_w_down=m_ffn2_w_down)
    v = dict(norm_ffn1=v_norm_ffn1, ffn1_w_gate=v_ffn1_w_gate, ffn1_w_up=v_ffn1_w_up, ffn1_w_down=v_ffn1_w_down, norm_mix=v_norm_mix, s5_w_in=v_s5_w_in, s5_lam_re=v_s5_lam_re, s5_lam_im=v_s5_lam_im, s5_log_dt=v_s5_log_dt, s5_b_re=v_s5_b_re, s5_b_im=v_s5_b_im, s5_c_re=v_s5_c_re, s5_c_im=v_s5_c_im, s5_d=v_s5_d, s5_w_glu=v_s5_w_glu, s5_b_glu=v_s5_b_glu, s5_w_out=v_s5_w_out, sb_w_qkv=v_sb_w_qkv, sb_g_q=v_sb_g_q, sb_g_k=v_sb_g_k, sb_w_o=v_sb_w_o, norm_ffn2=v_norm_ffn2, ffn2_w_gate=v_ffn2_w_gate, ffn2_w_up=v_ffn2_w_up, ffn2_w_down=v_ffn2_w_down)
    return _step(w, m, v, x, loss_target)
```
